```python
import jax, jax.numpy as jnp
from jax import lax
import numpy as np

D_MODEL = 1024
BATCH = 2
SEQ = 8192
DEPTH = 1
DEC_BATCH = 128
DEC_SEQ = 1
PAST_LEN = 16384
PAGE_SIZE = 128

HEAD_DIM = 64
N_HEADS = 8
N_KV_HEADS = 2
GQA_GROUP = N_HEADS // N_KV_HEADS
ATTN_DIM = N_HEADS * HEAD_DIM
KV_DIM = N_KV_HEADS * HEAD_DIM
WINDOW = 128
BLOCK = WINDOW
CONV_DIM = D_MODEL - ATTN_DIM
CONV_GROUPS = 8
CONV_WIDTH = 3
MIX_DIM = ATTN_DIM + CONV_DIM
OFF_Q = 0
OFF_K = OFF_Q + ATTN_DIM
OFF_V = OFF_K + KV_DIM
OFF_B = OFF_V + KV_DIM
OFF_C = OFF_B + CONV_DIM
OFF_H = OFF_C + CONV_DIM
IN_PROJ_DIM = OFF_H + CONV_DIM
N_EXPERT_GROUPS = 4
EXPERTS_PER_GROUP = 4
N_EXPERTS = N_EXPERT_GROUPS * EXPERTS_PER_GROUP
TOP_K_FINE = 2
D_EXPERT = 256
RMS_EPS = 1e-5
NEG_INF = -1e30

kernel_name = "hymba_swa_sink_shortconv_hiermoe_step"


def rmsnorm(x, g):
    xf = x.astype(jnp.float32)
    y = xf * lax.rsqrt(jnp.mean(xf * xf, axis=-1, keepdims=True) + RMS_EPS)
    return (y * g.astype(jnp.float32)).astype(x.dtype)


def group_rmsnorm(x, g, n_groups):
    shp = x.shape
    xf = x.astype(jnp.float32).reshape(shp[:-1] + (n_groups, shp[-1] // n_groups))
    y = xf * lax.rsqrt(jnp.mean(xf * xf, axis=-1, keepdims=True) + RMS_EPS)
    return (y.reshape(shp) * g.astype(jnp.float32)).astype(x.dtype)


def sink_attention(q, k, v, valid, sinks):
    scores = jnp.einsum('...qkgd,...ckd->...kgqc', q, k).astype(jnp.float32) * (HEAD_DIM ** -0.5)
    scores = jnp.where(valid, scores, NEG_INF)
    sink = sinks.astype(jnp.float32).reshape(N_KV_HEADS, GQA_GROUP, 1, 1)
    m = jnp.maximum(scores.max(axis=-1, keepdims=True), sink)
    e = jnp.exp(scores - m)
    p = e / (e.sum(axis=-1, keepdims=True) + jnp.exp(sink - m))
    return jnp.einsum('...kgqc,...ckd->...qkgd', p.astype(v.dtype), v)


def banded_window_attention(q, k, v, sinks):
    n, s = q.shape[0], q.shape[1]
    nb = s // BLOCK
    qb = q.reshape(n, nb, BLOCK, N_KV_HEADS, GQA_GROUP, HEAD_DIM)

    def band(t):
        tb = t.reshape(n, nb, BLOCK, N_KV_HEADS, HEAD_DIM)
        prev = jnp.concatenate([jnp.zeros_like(tb[:, :1]), tb[:, :-1]], axis=1)
        return jnp.concatenate([prev, tb], axis=2)

    blk = jnp.arange(nb)[:, None, None]
    a = jnp.arange(BLOCK)[None, :, None]
    c = jnp.arange(2 * BLOCK)[None, None, :]
    diff = BLOCK + a - c
    key_pos = (blk - 1) * BLOCK + c
    valid = (diff >= 0) & (diff <= WINDOW) & (key_pos >= 0)
    out = sink_attention(qb, band(k), band(v), valid[None, :, None, None], sinks)
    return out.reshape(n, s, N_KV_HEADS, GQA_GROUP, HEAD_DIM)


def cached_window_attention(q, k, v, k_hist, v_hist, sinks):
    t = q.shape[1]
    wb = k_hist.shape[1]
    k_all = jnp.concatenate([k_hist, k], axis=1)
    v_all = jnp.concatenate([v_hist, v], axis=1)
    diff = wb + jnp.arange(t)[:, None] - jnp.arange(wb + t)[None, :]
    valid = (diff >= 0) & (diff <= WINDOW)
    out = sink_attention(q, k_all, v_all, valid[None, None, None], sinks)
    return out, k_all[:, -wb:], v_all[:, -wb:]


def causal_short_conv(u, u_hist, conv_w):
    t = u.shape[1]
    full = jnp.concatenate([u_hist, u], axis=1)
    z = sum(conv_w[j] * full[:, j:j + t] for j in range(CONV_WIDTH))
    return z, full[:, -(CONV_WIDTH - 1):]


def hierarchical_moe(xt, w_group, b_group, w_fine, b_fine, w_gate, w_up, w_down):
    nt = xt.shape[0]
    coarse = (xt @ w_group + b_group).astype(jnp.float32)
    p_group = jax.nn.softmax(coarse, axis=-1)
    g_sel = jnp.argmax(coarse, axis=-1)
    p_sel = jnp.take_along_axis(p_group, g_sel[:, None], axis=1)
    fine = (xt @ w_fine + b_fine).astype(jnp.float32).reshape(nt, N_EXPERT_GROUPS, EXPERTS_PER_GROUP)
    fine_sel = jnp.take_along_axis(fine, g_sel[:, None, None], axis=1)[:, 0]
    top_p, top_i = lax.top_k(jax.nn.softmax(fine_sel, axis=-1), TOP_K_FINE)
    top_p = top_p / top_p.sum(axis=-1, keepdims=True)
    expert_id = g_sel[:, None] * EXPERTS_PER_GROUP + top_i
    gates = jnp.einsum('nk,nke->ne', p_sel * top_p,
                       jax.nn.one_hot(expert_id, N_EXPERTS, dtype=jnp.float32)).astype(xt.dtype)
    hg = jnp.einsum('nd,edf->nef', xt, w_gate)
    hu = jnp.einsum('nd,edf->nef', xt, w_up)
    act = jax.nn.silu(hg) * hu * gates[:, :, None]
    return jnp.einsum('nef,efd->nd', act, w_down)


def layer_forward(x, k_hist, v_hist, u_hist, norm1_g, w_in, conv_w, sinks, attn_out_g, conv_out_g,
                  w_out, norm2_g, w_group, b_group, w_fine, b_fine, w_gate, w_up, w_down):
    n, t, _ = x.shape
    xn = rmsnorm(x, norm1_g)
    proj = xn @ w_in
    q = proj[..., OFF_Q:OFF_K].reshape(n, t, N_KV_HEADS, GQA_GROUP, HEAD_DIM)
    k = proj[..., OFF_K:OFF_V].reshape(n, t, N_KV_HEADS, HEAD_DIM)
    v = proj[..., OFF_V:OFF_B].reshape(n, t, N_KV_HEADS, HEAD_DIM)
    gate_b = proj[..., OFF_B:OFF_C]
    gate_c = proj[..., OFF_C:OFF_H]
    h = proj[..., OFF_H:IN_PROJ_DIM]
    if k_hist is None:
        attn = banded_window_attention(q, k, v, sinks)
        keep = min(WINDOW, t)
        new_k, new_v = k[:, t - keep:], v[:, t - keep:]
        u_hist = jnp.zeros((n, CONV_WIDTH - 1, CONV_DIM), x.dtype)
    else:
        attn, new_k, new_v = cached_window_attention(q, k, v, k_hist, v_hist, sinks)
    attn = attn.reshape(n, t, ATTN_DIM)
    z, new_u = causal_short_conv(gate_c * h, u_hist, conv_w)
    conv_out = gate_b * z
    mixed = jnp.concatenate([group_rmsnorm(attn, attn_out_g, N_HEADS),
                             group_rmsnorm(conv_out, conv_out_g, CONV_GROUPS)], axis=-1)
    x = x + mixed @ w_out
    xn2 = rmsnorm(x, norm2_g).reshape(n * t, D_MODEL)
    x = x + hierarchical_moe(xn2, w_group, b_group, w_fine, b_fine, w_gate, w_up, w_down).reshape(n, t, D_MODEL)
    return x, new_k, new_v, new_u


def setup_inputs(seed: int = 0) -> dict:
    key = jax.random.key(seed)
    ks = jax.random.split(key, 24)

    def nrm(k, shape, scale):
        return jax.random.normal(k, shape, jnp.float32) * scale

    win = min(WINDOW, PAST_LEN)
    return {
        "x_prompt": nrm(ks[0], (BATCH, SEQ, D_MODEL), 1.0),
        "x_sample": nrm(ks[1], (DEC_BATCH, DEC_SEQ, D_MODEL), 1.0),
        "cache_k_win": nrm(ks[2], (DEPTH, DEC_BATCH, win, N_KV_HEADS, HEAD_DIM), 1.0),
        "cache_v_win": nrm(ks[3], (DEPTH, DEC_BATCH, win, N_KV_HEADS, HEAD_DIM), 1.0),
        "state_conv": nrm(ks[4], (DEPTH, DEC_BATCH, CONV_WIDTH - 1, CONV_DIM), 1.0),
        "norm1_g": 1.0 + nrm(ks[5], (DEPTH, D_MODEL), 0.02),
        "w_in": nrm(ks[6], (DEPTH, D_MODEL, IN_PROJ_DIM), D_MODEL ** -0.5),
        "conv_w": nrm(ks[7], (DEPTH, CONV_WIDTH, CONV_DIM), CONV_WIDTH ** -0.5),
        "sinks": nrm(ks[8], (DEPTH, N_HEADS), 1.0),
        "attn_out_g": 1.0 + nrm(ks[9], (DEPTH, ATTN_DIM), 0.02),
        "conv_out_g": 1.0 + nrm(ks[10], (DEPTH, CONV_DIM), 0.02),
        "w_out": nrm(ks[11], (DEPTH, MIX_DIM, D_MODEL), MIX_DIM ** -0.5),
        "norm2_g": 1.0 + nrm(ks[12], (DEPTH, D_MODEL), 0.02),
        "w_group": nrm(ks[13], (DEPTH, D_MODEL, N_EXPERT_GROUPS), D_MODEL ** -0.5),
        "b_group": nrm(ks[14], (DEPTH, N_EXPERT_GROUPS), 0.01),
        "w_fine": nrm(ks[15], (DEPTH, D_MODEL, N_EXPERTS), D_MODEL ** -0.5),
        "b_fine": nrm(ks[16], (DEPTH, N_EXPERTS), 0.01),
        "w_gate": nrm(ks[17], (DEPTH, N_EXPERTS, D_MODEL, D_EXPERT), D_MODEL ** -0.5),
        "w_up": nrm(ks[18], (DEPTH, N_EXPERTS, D_MODEL, D_EXPERT), D_MODEL ** -0.5),
        "w_down": nrm(ks[19], (DEPTH, N_EXPERTS, D_EXPERT, D_MODEL), D_EXPERT ** -0.5),
        "final_g": 1.0 + nrm(ks[20], (D_MODEL,), 0.02),
    }


def reference(x_prompt, x_sample, cache_k_win, cache_v_win, state_conv, norm1_g, w_in, conv_w, sinks,
              attn_out_g, conv_out_g, w_out, norm2_g, w_group, b_group, w_fine, b_fine, w_gate, w_up,
              w_down, final_g):
    yp, ys = x_prompt, x_sample
    kp_l, vp_l, up_l, ks_l, vs_l, us_l = [], [], [], [], [], []
    for l in range(DEPTH):
        lp = (norm1_g[l], w_in[l], conv_w[l], sinks[l], attn_out_g[l], conv_out_g[l], w_out[l],
              norm2_g[l], w_group[l], b_group[l], w_fine[l], b_fine[l], w_gate[l], w_up[l], w_down[l])
        yp, kp, vp, up = layer_forward(yp, None, None, None, *lp)
        ys, ksm, vsm, usm = layer_forward(ys, cache_k_win[l], cache_v_win[l], state_conv[l], *lp)
        kp_l.append(kp); vp_l.append(vp); up_l.append(up)
        ks_l.append(ksm); vs_l.append(vsm); us_l.append(usm)
    y_prompt = rmsnorm(yp, final_g)
    y_sample = rmsnorm(ys, final_g)
    return (y_prompt, y_sample, jnp.stack(kp_l), jnp.stack(vp_l), jnp.stack(up_l),
            jnp.stack(ks_l), jnp.stack(vs_l), jnp.stack(us_l))
```

```python
import functools

import jax
import jax.numpy as jnp
from jax import lax
from jax.experimental import pallas as pl
from jax.experimental.pallas import tpu as pltpu

D_MODEL = 1024
HEAD_DIM = 64
N_HEADS = 8
ATTN_DIM = 512
KV_DIM = 128
WINDOW = 128
CONV_DIM = 512
CONV_WIDTH = 3
OFF_K = 512
OFF_V = 640
OFF_B = 768
OFF_C = 1280
OFF_H = 1792
IN_PROJ_DIM = 2304
N_GROUPS = 4
N_EXPERTS = 16
D_EXPERT = 256
RMS_EPS = 1e-5
NEG_INF = -1e30

LANES = 128
ROUTER_LANES = 128
COARSE_OFF = N_EXPERTS
T_LAYER = 512
T_MOE = 1024
DEC_CHUNK = 16
VMEM_LIMIT = 48 * 1024 * 1024

BF16 = jnp.bfloat16
F32 = jnp.float32


def _dot(a, b):
    return jnp.dot(a, b, preferred_element_type=F32)


def _rms(x, g):
    ms = jnp.mean(x * x, axis=-1, keepdims=True)
    return x * lax.rsqrt(ms + RMS_EPS) * g


def _group_norm64(y, g):
    rows, cols = y.shape
    lo = lax.broadcasted_iota(jnp.int32, (rows, LANES), 1) < HEAD_DIM
    outs = []
    for c in range(cols // LANES):
        blk = y[:, c * LANES:(c + 1) * LANES]
        sq = blk * blk
        s_lo = jnp.sum(jnp.where(lo, sq, 0.0), axis=-1, keepdims=True)
        s_hi = jnp.sum(jnp.where(lo, 0.0, sq), axis=-1, keepdims=True)
        ms = jnp.where(lo, s_lo, s_hi) * (1.0 / HEAD_DIM)
        outs.append(blk * lax.rsqrt(ms + RMS_EPS))
    return jnp.concatenate(outs, axis=-1) * g


def _router(xn2, wr_hi, wr_lo, br):
    hi = xn2.astype(BF16)
    lo = (xn2 - hi.astype(F32)).astype(BF16)
    logits = _dot(hi, wr_hi) + _dot(lo, wr_hi) + _dot(hi, wr_lo) + br
    rows = logits.shape[0]
    lane = lax.broadcasted_iota(jnp.int32, (rows, ROUTER_LANES), 1)
    lanef = lane.astype(F32)
    big = float(ROUTER_LANES)
    coarse = (lane >= COARSE_OFF) & (lane < COARSE_OFF + N_GROUPS)
    cm = jnp.max(jnp.where(coarse, logits, -jnp.inf), axis=-1, keepdims=True)
    g_sel = jnp.min(jnp.where(coarse & (logits == cm), lanef - COARSE_OFF, big),
                    axis=-1, keepdims=True)
    zc = jnp.sum(jnp.where(coarse, jnp.exp(logits - cm), 0.0), axis=-1, keepdims=True)
    p_sel = 1.0 / zc
    per_group = N_EXPERTS // N_GROUPS
    fine = (lane < N_EXPERTS) & ((lane // per_group).astype(F32) == g_sel)
    f1 = jnp.max(jnp.where(fine, logits, -jnp.inf), axis=-1, keepdims=True)
    i1 = jnp.min(jnp.where(fine & (logits == f1), lanef, big), axis=-1, keepdims=True)
    rest = fine & (lanef != i1)
    f2 = jnp.max(jnp.where(rest, logits, -jnp.inf), axis=-1, keepdims=True)
    i2 = jnp.min(jnp.where(rest & (logits == f2), lanef, big), axis=-1, keepdims=True)
    e2 = jnp.exp(f2 - f1)
    t1 = 1.0 / (1.0 + e2)
    t2 = e2 * t1
    gates = jnp.where(lanef == i1, p_sel * t1, jnp.where(lanef == i2, p_sel * t2, 0.0))
    return hi, gates


def _mix_tail(x, attn, conv_out, gattn, gconv, wout, g2, wr_hi, wr_lo, br):
    mixed = jnp.concatenate([_group_norm64(attn, gattn), _group_norm64(conv_out, gconv)],
                            axis=-1).astype(BF16)
    x1 = x + _dot(mixed, wout)
    xn2_bf, gates = _router(_rms(x1, g2), wr_hi, wr_lo, br)
    return x1, xn2_bf, gates


def _layer_prompt_body(sinks_ref, x_ref, g1_ref, win_ref, convw_ref, gattn_ref, gconv_ref,
                       wout_ref, g2_ref, wrhi_ref, wrlo_ref, br_ref,
                       x1_ref, xn2_ref, gates_ref, knew_ref, vnew_ref, unew_ref,
                       q_s, kcat_s, vcat_s, attn_s, ucarry_s):
    i = pl.program_id(1)
    t = T_LAYER
    x = x_ref[0]
    xn = _rms(x, g1_ref[...]).astype(BF16)

    @pl.when(i == 0)
    def _():
        kcat_s[0:WINDOW, :] = jnp.zeros((WINDOW, KV_DIM), BF16)
        vcat_s[0:WINDOW, :] = jnp.zeros((WINDOW, KV_DIM), BF16)
        ucarry_s[...] = jnp.zeros(ucarry_s.shape, F32)

    @pl.when(i > 0)
    def _():
        kcat_s[0:WINDOW, :] = kcat_s[t:t + WINDOW, :]
        vcat_s[0:WINDOW, :] = vcat_s[t:t + WINDOW, :]

    q_s[...] = (_dot(xn, win_ref[:, 0:OFF_K]) * (HEAD_DIM ** -0.5)).astype(BF16)
    k = _dot(xn, win_ref[:, OFF_K:OFF_V])
    v = _dot(xn, win_ref[:, OFF_V:OFF_B])
    kcat_s[WINDOW:, :] = k.astype(BF16)
    vcat_s[WINDOW:, :] = v.astype(BF16)
    knew_ref[0] = k[t - WINDOW:, :]
    vnew_ref[0] = v[t - WINDOW:, :]

    a_idx = lax.broadcasted_iota(jnp.int32, (WINDOW, 2 * WINDOW), 0)
    c_idx = lax.broadcasted_iota(jnp.int32, (WINDOW, 2 * WINDOW), 1)
    lane_lo = lax.broadcasted_iota(jnp.int32, (WINDOW, LANES), 1) < HEAD_DIM

    def block(j, carry):
        r0 = pl.multiple_of(j * WINDOW, WINDOW)
        kk = kcat_s[pl.ds(r0, 2 * WINDOW), :]
        vv = vcat_s[pl.ds(r0, 2 * WINDOW), :]
        c_min = jnp.where(jnp.logical_and(i == 0, j == 0), WINDOW, 0)
        valid = (c_idx >= jnp.maximum(a_idx, c_min)) & (c_idx <= a_idx + WINDOW)
        for m in range(N_HEADS // 2):
            q128 = q_s[pl.ds(r0, WINDOW), m * LANES:(m + 1) * LANES]
            halves = []
            for half in range(2):
                head = m + half * (N_HEADS // 2)
                keep = lane_lo if half == 0 else jnp.logical_not(lane_lo)
                qh = jnp.where(keep, q128, jnp.zeros_like(q128))
                s = lax.dot_general(qh, kk, (((1,), (1,)), ((), ())), preferred_element_type=F32)
                s = jnp.where(valid, s, NEG_INF)
                sink = sinks_ref[head]
                mx = jnp.maximum(jnp.max(s, axis=-1, keepdims=True), sink)
                e = jnp.exp(s - mx)
                den = jnp.sum(e, axis=-1, keepdims=True) + jnp.exp(sink - mx)
                halves.append(_dot(e.astype(BF16), vv) / den)
            attn_s[pl.ds(r0, WINDOW), m * LANES:(m + 1) * LANES] = jnp.where(lane_lo, halves[0], halves[1])
        return carry

    lax.fori_loop(0, t // WINDOW, block, 0)

    gate_b = _dot(xn, win_ref[:, OFF_B:OFF_C])
    u = _dot(xn, win_ref[:, OFF_C:OFF_H]) * _dot(xn, win_ref[:, OFF_H:IN_PROJ_DIM])
    prev = ucarry_s[...]
    row = lax.broadcasted_iota(jnp.int32, (t, CONV_DIM), 0)
    u1 = jnp.where(row == 0, prev[7:8, :], pltpu.roll(u, 1, axis=0))
    u2 = jnp.where(row == 0, prev[6:7, :], jnp.where(row == 1, prev[7:8, :], pltpu.roll(u, 2, axis=0)))
    cw = convw_ref[...]
    z = cw[0:1, :] * u2 + cw[1:2, :] * u1 + cw[2:3, :] * u
    ucarry_s[...] = u[t - 8:, :]
    unew_ref[0] = u[t - (CONV_WIDTH - 1):, :]

    x1, xn2_bf, gates = _mix_tail(x, attn_s[...], gate_b * z, gattn_ref[...], gconv_ref[...],
                                  wout_ref[...], g2_ref[...], wrhi_ref[...], wrlo_ref[...], br_ref[...])
    x1_ref[0] = x1
    xn2_ref[0] = xn2_bf
    gates_ref[0] = gates


def _layer_prompt(x, sinks, g1, win, convw, gattn, gconv, wout, g2, wr_hi, wr_lo, br):
    nb, seq, _ = x.shape
    t = T_LAYER
    const = lambda b, i, s: (0, 0)
    tile = lambda b, i, s: (b, i, 0)
    per_seq = lambda b, i, s: (b, 0, 0)
    grid_spec = pltpu.PrefetchScalarGridSpec(
        num_scalar_prefetch=1,
        grid=(nb, seq // t),
        in_specs=[
            pl.BlockSpec((1, t, D_MODEL), tile),
            pl.BlockSpec((1, D_MODEL), const),
            pl.BlockSpec((D_MODEL, IN_PROJ_DIM), const),
            pl.BlockSpec((CONV_WIDTH, CONV_DIM), const),
            pl.BlockSpec((1, ATTN_DIM), const),
            pl.BlockSpec((1, CONV_DIM), const),
            pl.BlockSpec((D_MODEL, D_MODEL), const),
            pl.BlockSpec((1, D_MODEL), const),
            pl.BlockSpec((D_MODEL, ROUTER_LANES), const),
            pl.BlockSpec((D_MODEL, ROUTER_LANES), const),
            pl.BlockSpec((1, ROUTER_LANES), const),
        ],
        out_specs=[
            pl.BlockSpec((1, t, D_MODEL), tile),
            pl.BlockSpec((1, t, D_MODEL), tile),
            pl.BlockSpec((1, t, ROUTER_LANES), tile),
            pl.BlockSpec((1, WINDOW, KV_DIM), per_seq),
            pl.BlockSpec((1, WINDOW, KV_DIM), per_seq),
            pl.BlockSpec((1, CONV_WIDTH - 1, CONV_DIM), per_seq),
        ],
        scratch_shapes=[
            pltpu.VMEM((t, ATTN_DIM), BF16),
            pltpu.VMEM((t + WINDOW, KV_DIM), BF16),
            pltpu.VMEM((t + WINDOW, KV_DIM), BF16),
            pltpu.VMEM((t, ATTN_DIM), F32),
            pltpu.VMEM((8, CONV_DIM), F32),
        ],
    )
    return pl.pallas_call(
        _layer_prompt_body,
        grid_spec=grid_spec,
        out_shape=[
            jax.ShapeDtypeStruct((nb, seq, D_MODEL), F32),
            jax.ShapeDtypeStruct((nb, seq, D_MODEL), BF16),
            jax.ShapeDtypeStruct((nb, seq, ROUTER_LANES), F32),
            jax.ShapeDtypeStruct((nb, WINDOW, KV_DIM), F32),
            jax.ShapeDtypeStruct((nb, WINDOW, KV_DIM), F32),
            jax.ShapeDtypeStruct((nb, CONV_WIDTH - 1, CONV_DIM), F32),
        ],
        compiler_params=pltpu.CompilerParams(
            dimension_semantics=("arbitrary", "arbitrary"), vmem_limit_bytes=VMEM_LIMIT),
        name="layer_prompt",
    )(sinks, x, g1, win, convw, gattn, gconv, wout, g2, wr_hi, wr_lo, br)


def _moe_body(x1_ref, xn2_ref, gates_ref, wgu_ref, wd_ref, gf_ref, y_ref, acc_s):
    e = pl.program_id(1)

    @pl.when(e == 0)
    def _():
        acc_s[...] = jnp.zeros(acc_s.shape, F32)

    h = _dot(xn2_ref[...], wgu_ref[0])
    hg = h[:, :D_EXPERT]
    hu = h[:, D_EXPERT:]
    gates = gates_ref[...]
    lane = lax.broadcasted_iota(jnp.int32, gates.shape, 1)
    g = jnp.sum(jnp.where(lane == e, gates, 0.0), axis=-1, keepdims=True)
    act = hg / (1.0 + jnp.exp(-hg)) * hu * g
    acc_s[...] += _dot(act.astype(BF16), wd_ref[0])

    @pl.when(e == N_EXPERTS - 1)
    def _():
        y_ref[...] = _rms(x1_ref[...] + acc_s[...], gf_ref[...])


def _moe(x1, xn2, gates, wgu, wd, gf, t):
    n = x1.shape[0]
    tile = lambda i, e: (i, 0)
    return pl.pallas_call(
        _moe_body,
        grid=(n // t, N_EXPERTS),
        in_specs=[
            pl.BlockSpec((t, D_MODEL), tile),
            pl.BlockSpec((t, D_MODEL), tile),
            pl.BlockSpec((t, ROUTER_LANES), tile),
            pl.BlockSpec((1, D_MODEL, 2 * D_EXPERT), lambda i, e: (e, 0, 0)),
            pl.BlockSpec((1, D_EXPERT, D_MODEL), lambda i, e: (e, 0, 0)),
            pl.BlockSpec((1, D_MODEL), lambda i, e: (0, 0)),
        ],
        out_specs=pl.BlockSpec((t, D_MODEL), tile),
        out_shape=jax.ShapeDtypeStruct((n, D_MODEL), F32),
        scratch_shapes=[pltpu.VMEM((t, D_MODEL), F32)],
        compiler_params=pltpu.CompilerParams(
            dimension_semantics=("arbitrary", "arbitrary"), vmem_limit_bytes=VMEM_LIMIT),
        name="moe",
    )(x1, xn2, gates, wgu, wd, gf)


def _proj_sample_body(x_ref, g1_ref, win_ref, proj_ref):
    xn = _rms(x_ref[...], g1_ref[...]).astype(BF16)
    proj_ref[...] = _dot(xn, win_ref[...])


def _proj_sample(x, g1, win):
    n = x.shape[0]
    return pl.pallas_call(
        _proj_sample_body,
        out_shape=jax.ShapeDtypeStruct((n, IN_PROJ_DIM), F32),
        compiler_params=pltpu.CompilerParams(vmem_limit_bytes=VMEM_LIMIT),
        name="proj_sample",
    )(x, g1, win)


def _attn_sample_body(q8_ref, knew_ref, vnew_ref, khist_ref, vhist_ref, sinks_ref,
                      out8_ref, kout_ref, vout_ref):
    q8 = q8_ref[...]
    kh = khist_ref[...]
    vh = vhist_ref[...]
    knew = knew_ref[...]
    vnew = vnew_ref[...]
    s = jnp.einsum('bhj,bcj->bhc', q8, kh.astype(BF16), preferred_element_type=F32)
    s_new = jnp.sum(q8.astype(F32) * knew, axis=-1, keepdims=True)
    sink = sinks_ref[...][None]
    mx = jnp.maximum(jnp.maximum(jnp.max(s, axis=-1, keepdims=True), s_new), sink)
    e = jnp.exp(s - mx)
    e_new = jnp.exp(s_new - mx)
    den = jnp.sum(e, axis=-1, keepdims=True) + e_new + jnp.exp(sink - mx)
    o = jnp.einsum('bhc,bcj->bhj', e.astype(BF16), vh.astype(BF16), preferred_element_type=F32)
    out8_ref[...] = (o + e_new * vnew) / den
    kout_ref[:, 0:WINDOW - 1, :] = khist_ref[:, 1:WINDOW, :]
    kout_ref[:, WINDOW - 1:WINDOW, :] = knew
    vout_ref[:, 0:WINDOW - 1, :] = vhist_ref[:, 1:WINDOW, :]
    vout_ref[:, WINDOW - 1:WINDOW, :] = vnew


def _attn_sample(q8, knew, vnew, khist, vhist, sinks_col):
    n = q8.shape[0]
    c = DEC_CHUNK
    blk3 = lambda i: (i, 0, 0)
    return pl.pallas_call(
        _attn_sample_body,
        grid=(n // c,),
        in_specs=[
            pl.BlockSpec((c, N_HEADS, LANES), blk3),
            pl.BlockSpec((c, 1, KV_DIM), blk3),
            pl.BlockSpec((c, 1, KV_DIM), blk3),
            pl.BlockSpec((c, WINDOW, KV_DIM), blk3),
            pl.BlockSpec((c, WINDOW, KV_DIM), blk3),
            pl.BlockSpec((N_HEADS, 1), lambda i: (0, 0)),
        ],
        out_specs=[
            pl.BlockSpec((c, N_HEADS, LANES), blk3),
            pl.BlockSpec((c, WINDOW, KV_DIM), blk3),
            pl.BlockSpec((c, WINDOW, KV_DIM), blk3),
        ],
        out_shape=[
            jax.ShapeDtypeStruct((n, N_HEADS, LANES), F32),
            jax.ShapeDtypeStruct((n, WINDOW, KV_DIM), F32),
            jax.ShapeDtypeStruct((n, WINDOW, KV_DIM), F32),
        ],
        compiler_params=pltpu.CompilerParams(
            dimension_semantics=("arbitrary",), vmem_limit_bytes=VMEM_LIMIT),
        name="attn_sample",
    )(q8, knew, vnew, khist, vhist, sinks_col)


def _tail_sample_body(x_ref, attn_ref, proj_ref, uprev2_ref, uprev1_ref, convw_ref, gattn_ref,
                      gconv_ref, wout_ref, g2_ref, wrhi_ref, wrlo_ref, br_ref,
                      x1_ref, xn2_ref, gates_ref, u_ref):
    gate_b = proj_ref[:, OFF_B:OFF_C]
    u = proj_ref[:, OFF_C:OFF_H] * proj_ref[:, OFF_H:IN_PROJ_DIM]
    cw = convw_ref[...]
    z = cw[0:1, :] * uprev2_ref[...] + cw[1:2, :] * uprev1_ref[...] + cw[2:3, :] * u
    u_ref[...] = u
    x1, xn2_bf, gates = _mix_tail(x_ref[...], attn_ref[...], gate_b * z, gattn_ref[...], gconv_ref[...],
                                  wout_ref[...], g2_ref[...], wrhi_ref[...], wrlo_ref[...], br_ref[...])
    x1_ref[...] = x1
    xn2_ref[...] = xn2_bf
    gates_ref[...] = gates


def _tail_sample(x, attn, proj, uprev2, uprev1, convw, gattn, gconv, wout, g2, wr_hi, wr_lo, br):
    n = x.shape[0]
    return pl.pallas_call(
        _tail_sample_body,
        out_shape=[
            jax.ShapeDtypeStruct((n, D_MODEL), F32),
            jax.ShapeDtypeStruct((n, D_MODEL), BF16),
            jax.ShapeDtypeStruct((n, ROUTER_LANES), F32),
            jax.ShapeDtypeStruct((n, CONV_DIM), F32),
        ],
        compiler_params=pltpu.CompilerParams(vmem_limit_bytes=VMEM_LIMIT),
        name="tail_sample",
    )(x, attn, proj, uprev2, uprev1, convw, gattn, gconv, wout, g2, wr_hi, wr_lo, br)


def _head_perm():
    idx = []
    for m in range(N_HEADS // 2):
        idx += list(range(m * HEAD_DIM, (m + 1) * HEAD_DIM))
        idx += list(range((m + N_HEADS // 2) * HEAD_DIM, (m + N_HEADS // 2 + 1) * HEAD_DIM))
    return jnp.asarray(idx, jnp.int32)


def kernel(x_prompt, x_sample, cache_k_win, cache_v_win, state_conv, norm1_g, w_in, conv_w, sinks,
           attn_out_g, conv_out_g, w_out, norm2_g, w_group, b_group, w_fine, b_fine, w_gate, w_up,
           w_down, final_g):
    nb, seq, _ = x_prompt.shape
    nd = x_sample.shape[0]
    perm = _head_perm()

    w_in0 = w_in[0]
    win = jnp.concatenate([w_in0[:, :ATTN_DIM][:, perm], w_in0[:, ATTN_DIM:]], axis=1).astype(BF16)
    w_out0 = w_out[0]
    wout = jnp.concatenate([w_out0[:ATTN_DIM][perm], w_out0[ATTN_DIM:]], axis=0).astype(BF16)
    gattn = attn_out_g[0][perm][None]
    gconv = conv_out_g[0][None]
    g1 = norm1_g[0][None]
    g2 = norm2_g[0][None]
    gf = final_g[None]
    convw = conv_w[0]
    sinks0 = sinks[0]
    pad = ROUTER_LANES - N_EXPERTS - N_GROUPS
    wr = jnp.concatenate([w_fine[0], w_group[0], jnp.zeros((D_MODEL, pad), F32)], axis=1)
    wr_hi = wr.astype(BF16)
    wr_lo = (wr - wr_hi.astype(F32)).astype(BF16)
    br = jnp.concatenate([b_fine[0], b_group[0], jnp.zeros((pad,), F32)])[None]
    wgu = jnp.concatenate([w_gate[0], w_up[0]], axis=-1).astype(BF16)
    wd = w_down[0].astype(BF16)

    x1p, xn2p, gatesp, kp, vp, up = _layer_prompt(x_prompt, sinks0, g1, win, convw, gattn, gconv,
                                                  wout, g2, wr_hi, wr_lo, br)
    n_p = nb * seq
    y_prompt = _moe(x1p.reshape(n_p, D_MODEL), xn2p.reshape(n_p, D_MODEL),
                    gatesp.reshape(n_p, ROUTER_LANES), wgu, wd, gf, T_MOE).reshape(nb, seq, D_MODEL)

    xs = x_sample.reshape(nd, D_MODEL)
    proj = _proj_sample(xs, g1, win)
    lane = jnp.arange(LANES)
    qp = (proj[:, :ATTN_DIM] * (HEAD_DIM ** -0.5)).reshape(nd, N_HEADS // 2, LANES)
    q8 = jnp.concatenate([jnp.where(lane < HEAD_DIM, qp, 0.0), jnp.where(lane >= HEAD_DIM, qp, 0.0)],
                         axis=1).astype(BF16)
    knew = proj[:, OFF_K:OFF_V].reshape(nd, 1, KV_DIM)
    vnew = proj[:, OFF_V:OFF_B].reshape(nd, 1, KV_DIM)
    khist = cache_k_win[0].reshape(nd, WINDOW, KV_DIM)
    vhist = cache_v_win[0].reshape(nd, WINDOW, KV_DIM)
    out8, ks, vs = _attn_sample(q8, knew, vnew, khist, vhist, sinks0[:, None])
    attn_s = jnp.where(lane < HEAD_DIM, out8[:, :N_HEADS // 2], out8[:, N_HEADS // 2:]).reshape(nd, ATTN_DIM)
    st = state_conv[0]
    x1s, xn2s, gatess, us = _tail_sample(xs, attn_s, proj, st[:, 0], st[:, 1], convw, gattn, gconv,
                                         wout, g2, wr_hi, wr_lo, br)
    y_sample = _moe(x1s, xn2s, gatess, wgu, wd, gf, nd).reshape(nd, 1, D_MODEL)

    n_kv = KV_DIM // HEAD_DIM
    return (y_prompt, y_sample,
            kp.reshape(1, nb, WINDOW, n_kv, HEAD_DIM), vp.reshape(1, nb, WINDOW, n_kv, HEAD_DIM),
            up[None],
            ks.reshape(1, nd, WINDOW, n_kv, HEAD_DIM), vs.reshape(1, nd, WINDOW, n_kv, HEAD_DIM),
            jnp.stack([st[:, 1], us], axis=1)[None])
```

```python
import functools

import jax
import jax.numpy as jnp
from jax import lax
from jax.experimental import pallas as pl
from jax.experimental.pallas import tpu as pltpu

D_MODEL = 1024
HEAD_DIM = 64
N_HEADS = 8
ATTN_DIM = 512
KV_DIM = 128
WINDOW = 128
CONV_DIM = 512
CONV_WIDTH = 3
OFF_K = 512
OFF_V = 640
OFF_B = 768
OFF_C = 1280
OFF_H = 1792
IN_PROJ_DIM = 2304
N_GROUPS = 4
N_EXPERTS = 16
D_EXPERT = 256
RMS_EPS = 1e-5
NEG_INF = -1e30

LANES = 128
ROUTER_LANES = 128
COARSE_OFF = N_EXPERTS
T_LAYER = 512
GID_LANE = N_EXPERTS
T_MOE = 512
MOE_CHUNK = 128
MOE_CHUNK_DEC = 64
DEC_CHUNK = 16
VMEM_LIMIT = 48 * 1024 * 1024
MOE_VMEM_LIMIT = 56 * 1024 * 1024

BF16 = jnp.bfloat16
F32 = jnp.float32


def _dot(a, b):
    return jnp.dot(a, b, preferred_element_type=F32)


def _rms(x, g):
    ms = jnp.mean(x * x, axis=-1, keepdims=True)
    return x * lax.rsqrt(ms + RMS_EPS) * g


def _group_norm64(y, g):
    rows, cols = y.shape
    lo = lax.broadcasted_iota(jnp.int32, (rows, LANES), 1) < HEAD_DIM
    outs = []
    for c in range(cols // LANES):
        blk = y[:, c * LANES:(c + 1) * LANES]
        sq = blk * blk
        s_lo = jnp.sum(jnp.where(lo, sq, 0.0), axis=-1, keepdims=True)
        s_hi = jnp.sum(jnp.where(lo, 0.0, sq), axis=-1, keepdims=True)
        ms = jnp.where(lo, s_lo, s_hi) * (1.0 / HEAD_DIM)
        outs.append(blk * lax.rsqrt(ms + RMS_EPS))
    return jnp.concatenate(outs, axis=-1) * g


def _router(xn2, wr_hi, wr_lo, br):
    hi = xn2.astype(BF16)
    lo = (xn2 - hi.astype(F32)).astype(BF16)
    logits = _dot(hi, wr_hi) + _dot(lo, wr_hi) + _dot(hi, wr_lo) + br
    rows = logits.shape[0]
    lane = lax.broadcasted_iota(jnp.int32, (rows, ROUTER_LANES), 1)
    lanef = lane.astype(F32)
    big = float(ROUTER_LANES)
    coarse = (lane >= COARSE_OFF) & (lane < COARSE_OFF + N_GROUPS)
    cm = jnp.max(jnp.where(coarse, logits, -jnp.inf), axis=-1, keepdims=True)
    g_sel = jnp.min(jnp.where(coarse & (logits == cm), lanef - COARSE_OFF, big),
                    axis=-1, keepdims=True)
    zc = jnp.sum(jnp.where(coarse, jnp.exp(logits - cm), 0.0), axis=-1, keepdims=True)
    p_sel = 1.0 / zc
    per_group = N_EXPERTS // N_GROUPS
    fine = (lane < N_EXPERTS) & ((lane // per_group).astype(F32) == g_sel)
    f1 = jnp.max(jnp.where(fine, logits, -jnp.inf), axis=-1, keepdims=True)
    i1 = jnp.min(jnp.where(fine & (logits == f1), lanef, big), axis=-1, keepdims=True)
    rest = fine & (lanef != i1)
    f2 = jnp.max(jnp.where(rest, logits, -jnp.inf), axis=-1, keepdims=True)
    i2 = jnp.min(jnp.where(rest & (logits == f2), lanef, big), axis=-1, keepdims=True)
    e2 = jnp.exp(f2 - f1)
    t1 = 1.0 / (1.0 + e2)
    t2 = e2 * t1
    gates = jnp.where(lanef == i1, p_sel * t1, jnp.where(lanef == i2, p_sel * t2, 0.0))
    gates = jnp.where(lane == GID_LANE, g_sel, gates)
    return hi, gates


def _mix_tail(x, attn, conv_out, gattn, gconv, wout, g2, wr_hi, wr_lo, br):
    mixed = jnp.concatenate([_group_norm64(attn, gattn), _group_norm64(conv_out, gconv)],
                            axis=-1).astype(BF16)
    x1 = x + _dot(mixed, wout)
    xn2_bf, gates = _router(_rms(x1, g2), wr_hi, wr_lo, br)
    return x1, xn2_bf, gates


def _layer_prompt_body(sinks_ref, x_ref, g1_ref, win_ref, convw_ref, gattn_ref, gconv_ref,
                       wout_ref, g2_ref, wrhi_ref, wrlo_ref, br_ref,
                       x1_ref, xn2_ref, gates_ref, knew_ref, vnew_ref, unew_ref,
                       q_s, kcat_s, vcat_s, attn_s, ucarry_s):
    i = pl.program_id(1)
    t = T_LAYER
    x = x_ref[0]
    xn = _rms(x, g1_ref[...]).astype(BF16)

    @pl.when(i == 0)
    def _():
        kcat_s[0:WINDOW, :] = jnp.zeros((WINDOW, KV_DIM), BF16)
        vcat_s[0:WINDOW, :] = jnp.zeros((WINDOW, KV_DIM), BF16)
        ucarry_s[...] = jnp.zeros(ucarry_s.shape, F32)

    @pl.when(i > 0)
    def _():
        kcat_s[0:WINDOW, :] = kcat_s[t:t + WINDOW, :]
        vcat_s[0:WINDOW, :] = vcat_s[t:t + WINDOW, :]

    q_s[...] = (_dot(xn, win_ref[:, 0:OFF_K]) * (HEAD_DIM ** -0.5)).astype(BF16)
    k = _dot(xn, win_ref[:, OFF_K:OFF_V])
    v = _dot(xn, win_ref[:, OFF_V:OFF_B])
    kcat_s[WINDOW:, :] = k.astype(BF16)
    vcat_s[WINDOW:, :] = v.astype(BF16)
    knew_ref[0] = k[t - WINDOW:, :]
    vnew_ref[0] = v[t - WINDOW:, :]

    a_idx = lax.broadcasted_iota(jnp.int32, (WINDOW, 2 * WINDOW), 0)
    c_idx = lax.broadcasted_iota(jnp.int32, (WINDOW, 2 * WINDOW), 1)
    lane_lo = lax.broadcasted_iota(jnp.int32, (WINDOW, LANES), 1) < HEAD_DIM

    def block(j, carry):
        r0 = pl.multiple_of(j * WINDOW, WINDOW)
        kk = kcat_s[pl.ds(r0, 2 * WINDOW), :]
        vv = vcat_s[pl.ds(r0, 2 * WINDOW), :]
        c_min = jnp.where(jnp.logical_and(i == 0, j == 0), WINDOW, 0)
        valid = (c_idx >= jnp.maximum(a_idx, c_min)) & (c_idx <= a_idx + WINDOW)
        for m in range(N_HEADS // 2):
            q128 = q_s[pl.ds(r0, WINDOW), m * LANES:(m + 1) * LANES]
            halves = []
            for half in range(2):
                head = m + half * (N_HEADS // 2)
                keep = lane_lo if half == 0 else jnp.logical_not(lane_lo)
                qh = jnp.where(keep, q128, jnp.zeros_like(q128))
                s = lax.dot_general(qh, kk, (((1,), (1,)), ((), ())), preferred_element_type=F32)
                s = jnp.where(valid, s, NEG_INF)
                sink = sinks_ref[head]
                mx = jnp.maximum(jnp.max(s, axis=-1, keepdims=True), sink)
                e = jnp.exp(s - mx)
                den = jnp.sum(e, axis=-1, keepdims=True) + jnp.exp(sink - mx)
                halves.append(_dot(e.astype(BF16), vv) / den)
            attn_s[pl.ds(r0, WINDOW), m * LANES:(m + 1) * LANES] = jnp.where(lane_lo, halves[0], halves[1])
        return carry

    lax.fori_loop(0, t // WINDOW, block, 0)

    gate_b = _dot(xn, win_ref[:, OFF_B:OFF_C])
    u = _dot(xn, win_ref[:, OFF_C:OFF_H]) * _dot(xn, win_ref[:, OFF_H:IN_PROJ_DIM])
    prev = ucarry_s[...]
    row = lax.broadcasted_iota(jnp.int32, (t, CONV_DIM), 0)
    u1 = jnp.where(row == 0, prev[7:8, :], pltpu.roll(u, 1, axis=0))
    u2 = jnp.where(row == 0, prev[6:7, :], jnp.where(row == 1, prev[7:8, :], pltpu.roll(u, 2, axis=0)))
    cw = convw_ref[...]
    z = cw[0:1, :] * u2 + cw[1:2, :] * u1 + cw[2:3, :] * u
    ucarry_s[...] = u[t - 8:, :]
    unew_ref[0] = u[t - (CONV_WIDTH - 1):, :]

    x1, xn2_bf, gates = _mix_tail(x, attn_s[...], gate_b * z, gattn_ref[...], gconv_ref[...],
                                  wout_ref[...], g2_ref[...], wrhi_ref[...], wrlo_ref[...], br_ref[...])
    x1_ref[0] = x1
    xn2_ref[0] = xn2_bf
    gates_ref[0] = gates


def _layer_prompt(x, sinks, g1, win, convw, gattn, gconv, wout, g2, wr_hi, wr_lo, br):
    nb, seq, _ = x.shape
    t = T_LAYER
    const = lambda b, i, s: (0, 0)
    tile = lambda b, i, s: (b, i, 0)
    per_seq = lambda b, i, s: (b, 0, 0)
    grid_spec = pltpu.PrefetchScalarGridSpec(
        num_scalar_prefetch=1,
        grid=(nb, seq // t),
        in_specs=[
            pl.BlockSpec((1, t, D_MODEL), tile),
            pl.BlockSpec((1, D_MODEL), const),
            pl.BlockSpec((D_MODEL, IN_PROJ_DIM), const),
            pl.BlockSpec((CONV_WIDTH, CONV_DIM), const),
            pl.BlockSpec((1, ATTN_DIM), const),
            pl.BlockSpec((1, CONV_DIM), const),
            pl.BlockSpec((D_MODEL, D_MODEL), const),
            pl.BlockSpec((1, D_MODEL), const),
            pl.BlockSpec((D_MODEL, ROUTER_LANES), const),
            pl.BlockSpec((D_MODEL, ROUTER_LANES), const),
            pl.BlockSpec((1, ROUTER_LANES), const),
        ],
        out_specs=[
            pl.BlockSpec((1, t, D_MODEL), tile),
            pl.BlockSpec((1, t, D_MODEL), tile),
            pl.BlockSpec((1, t, ROUTER_LANES), tile),
            pl.BlockSpec((1, WINDOW, KV_DIM), per_seq),
            pl.BlockSpec((1, WINDOW, KV_DIM), per_seq),
            pl.BlockSpec((1, CONV_WIDTH - 1, CONV_DIM), per_seq),
        ],
        scratch_shapes=[
            pltpu.VMEM((t, ATTN_DIM), BF16),
            pltpu.VMEM((t + WINDOW, KV_DIM), BF16),
            pltpu.VMEM((t + WINDOW, KV_DIM), BF16),
            pltpu.VMEM((t, ATTN_DIM), F32),
            pltpu.VMEM((8, CONV_DIM), F32),
        ],
    )
    return pl.pallas_call(
        _layer_prompt_body,
        grid_spec=grid_spec,
        out_shape=[
            jax.ShapeDtypeStruct((nb, seq, D_MODEL), F32),
            jax.ShapeDtypeStruct((nb, seq, D_MODEL), BF16),
            jax.ShapeDtypeStruct((nb, seq, ROUTER_LANES), F32),
            jax.ShapeDtypeStruct((nb, WINDOW, KV_DIM), F32),
            jax.ShapeDtypeStruct((nb, WINDOW, KV_DIM), F32),
            jax.ShapeDtypeStruct((nb, CONV_WIDTH - 1, CONV_DIM), F32),
        ],
        compiler_params=pltpu.CompilerParams(
            dimension_semantics=("arbitrary", "arbitrary"), vmem_limit_bytes=VMEM_LIMIT),
        name="layer_prompt",
    )(sinks, x, g1, win, convw, gattn, gconv, wout, g2, wr_hi, wr_lo, br)


def _moe_body(chunk, x1_ref, xn2_ref, gates_ref, wgu_ref, wd_ref, gf_ref, y_ref, xs_s, gs_s, ys_s):
    t = x1_ref.shape[0]
    per_group = N_EXPERTS // N_GROUPS
    gates = gates_ref[...]
    lane = lax.broadcasted_iota(jnp.int32, (t, ROUTER_LANES), 1)
    gid = jnp.sum(jnp.where(lane == GID_LANE, gates, 0.0), axis=-1, keepdims=True)
    onehot = jnp.where((lane < N_GROUPS) & (lane.astype(F32) == gid), 1.0, 0.0)

    r_idx = lax.broadcasted_iota(jnp.int32, (t, t), 0)
    c_idx = lax.broadcasted_iota(jnp.int32, (t, t), 1)
    lower = jnp.where(c_idx < r_idx, 1.0, 0.0).astype(BF16)
    prefix = _dot(lower, onehot.astype(BF16))
    rank = jnp.sum(prefix * onehot, axis=-1, keepdims=True)
    counts = jnp.sum(onehot, axis=0, keepdims=True)
    lane1 = lax.broadcasted_iota(jnp.int32, (1, ROUTER_LANES), 1)
    cnt = [jnp.sum(jnp.where(lane1 == g, counts, 0.0)).astype(jnp.int32) for g in range(N_GROUPS)]
    off = [jnp.int32(0)]
    for g in range(1, N_GROUPS):
        off.append(off[-1] + cnt[g - 1])

    pos_col = rank
    for g in range(1, N_GROUPS):
        pos_col = pos_col + jnp.where(gid == float(g), off[g].astype(F32), 0.0)
    pos_row = jnp.transpose(jnp.broadcast_to(pos_col, (t, LANES)))[0:1, :]
    perm = jnp.where(r_idx.astype(F32) == pos_row, 1.0, 0.0).astype(BF16)
    perm_t = jnp.where(c_idx.astype(F32) == pos_col, 1.0, 0.0).astype(BF16)

    xs_s[...] = _dot(perm, xn2_ref[...]).astype(BF16)
    p1 = gates.astype(BF16)
    r1 = gates - p1.astype(F32)
    p2 = r1.astype(BF16)
    p3 = (r1 - p2.astype(F32)).astype(BF16)
    gs3 = _dot(perm, jnp.concatenate([p1, p2, p3], axis=1))
    gs_s[...] = (gs3[:, :ROUTER_LANES] + gs3[:, ROUTER_LANES:2 * ROUTER_LANES]) + gs3[:, 2 * ROUTER_LANES:]
    ys_s[...] = jnp.zeros(ys_s.shape, F32)

    for g in range(N_GROUPS):
        c_lo = off[g] // chunk
        c_hi = jnp.where(cnt[g] > 0, (off[g] + cnt[g] + chunk - 1) // chunk, c_lo)

        def chunk_body(ci, carry, g=g):
            r0 = pl.multiple_of(ci * chunk, chunk)
            h = _dot(xs_s[pl.ds(r0, chunk), :], wgu_ref[g])
            gsc = gs_s[pl.ds(r0, chunk), :]
            acts = []
            for k in range(per_group):
                hg = h[:, k * D_EXPERT:(k + 1) * D_EXPERT]
                hu = h[:, (per_group + k) * D_EXPERT:(per_group + k + 1) * D_EXPERT]
                e = g * per_group + k
                acts.append((hg / (1.0 + jnp.exp(-hg)) * hu * gsc[:, e:e + 1]).astype(BF16))
            ys_s[pl.ds(r0, chunk), :] += _dot(jnp.concatenate(acts, axis=1), wd_ref[g])
            return carry

        lax.fori_loop(c_lo, c_hi, chunk_body, 0)

    moe = _dot(perm_t, ys_s[...].astype(BF16))
    y_ref[...] = _rms(x1_ref[...] + moe, gf_ref[...])


def _moe(x1, xn2, gates, wgu, wd, gf, t, chunk):
    n = x1.shape[0]
    tile = lambda i: (i, 0)
    whole = lambda i: (0, 0, 0)
    hidden = (N_EXPERTS // N_GROUPS) * D_EXPERT
    return pl.pallas_call(
        functools.partial(_moe_body, chunk),
        grid=(n // t,),
        in_specs=[
            pl.BlockSpec((t, D_MODEL), tile),
            pl.BlockSpec((t, D_MODEL), tile),
            pl.BlockSpec((t, ROUTER_LANES), tile),
            pl.BlockSpec((N_GROUPS, D_MODEL, 2 * hidden), whole, pipeline_mode=pl.Buffered(1)),
            pl.BlockSpec((N_GROUPS, hidden, D_MODEL), whole, pipeline_mode=pl.Buffered(1)),
            pl.BlockSpec((1, D_MODEL), lambda i: (0, 0)),
        ],
        out_specs=pl.BlockSpec((t, D_MODEL), tile),
        out_shape=jax.ShapeDtypeStruct((n, D_MODEL), F32),
        scratch_shapes=[
            pltpu.VMEM((t, D_MODEL), BF16),
            pltpu.VMEM((t, ROUTER_LANES), F32),
            pltpu.VMEM((t, D_MODEL), F32),
        ],
        compiler_params=pltpu.CompilerParams(
            dimension_semantics=("arbitrary",), vmem_limit_bytes=MOE_VMEM_LIMIT),
        name="moe",
    )(x1, xn2, gates, wgu, wd, gf)


def _proj_sample_body(x_ref, g1_ref, win_ref, proj_ref):
    xn = _rms(x_ref[...], g1_ref[...]).astype(BF16)
    proj_ref[...] = _dot(xn, win_ref[...])


def _proj_sample(x, g1, win):
    n = x.shape[0]
    return pl.pallas_call(
        _proj_sample_body,
        out_shape=jax.ShapeDtypeStruct((n, IN_PROJ_DIM), F32),
        compiler_params=pltpu.CompilerParams(vmem_limit_bytes=VMEM_LIMIT),
        name="proj_sample",
    )(x, g1, win)


def _attn_sample_body(q8_ref, knew_ref, vnew_ref, khist_ref, vhist_ref, sinks_ref,
                      out8_ref, kout_ref, vout_ref):
    q8 = q8_ref[...]
    kh = khist_ref[...]
    vh = vhist_ref[...]
    knew = knew_ref[...]
    vnew = vnew_ref[...]
    s = jnp.einsum('bhj,bcj->bhc', q8, kh.astype(BF16), preferred_element_type=F32)
    s_new = jnp.sum(q8.astype(F32) * knew, axis=-1, keepdims=True)
    sink = sinks_ref[...][None]
    mx = jnp.maximum(jnp.maximum(jnp.max(s, axis=-1, keepdims=True), s_new), sink)
    e = jnp.exp(s - mx)
    e_new = jnp.exp(s_new - mx)
    den = jnp.sum(e, axis=-1, keepdims=True) + e_new + jnp.exp(sink - mx)
    o = jnp.einsum('bhc,bcj->bhj', e.astype(BF16), vh.astype(BF16), preferred_element_type=F32)
    out8_ref[...] = (o + e_new * vnew) / den
    kout_ref[:, 0:WINDOW - 1, :] = khist_ref[:, 1:WINDOW, :]
    kout_ref[:, WINDOW - 1:WINDOW, :] = knew
    vout_ref[:, 0:WINDOW - 1, :] = vhist_ref[:, 1:WINDOW, :]
    vout_ref[:, WINDOW - 1:WINDOW, :] = vnew


def _attn_sample(q8, knew, vnew, khist, vhist, sinks_col):
    n = q8.shape[0]
    c = DEC_CHUNK
    blk3 = lambda i: (i, 0, 0)
    return pl.pallas_call(
        _attn_sample_body,
        grid=(n // c,),
        in_specs=[
            pl.BlockSpec((c, N_HEADS, LANES), blk3),
            pl.BlockSpec((c, 1, KV_DIM), blk3),
            pl.BlockSpec((c, 1, KV_DIM), blk3),
            pl.BlockSpec((c, WINDOW, KV_DIM), blk3),
            pl.BlockSpec((c, WINDOW, KV_DIM), blk3),
            pl.BlockSpec((N_HEADS, 1), lambda i: (0, 0)),
        ],
        out_specs=[
            pl.BlockSpec((c, N_HEADS, LANES), blk3),
            pl.BlockSpec((c, WINDOW, KV_DIM), blk3),
            pl.BlockSpec((c, WINDOW, KV_DIM), blk3),
        ],
        out_shape=[
            jax.ShapeDtypeStruct((n, N_HEADS, LANES), F32),
            jax.ShapeDtypeStruct((n, WINDOW, KV_DIM), F32),
            jax.ShapeDtypeStruct((n, WINDOW, KV_DIM), F32),
        ],
        compiler_params=pltpu.CompilerParams(
            dimension_semantics=("arbitrary",), vmem_limit_bytes=VMEM_LIMIT),
        name="attn_sample",
    )(q8, knew, vnew, khist, vhist, sinks_col)


def _tail_sample_body(x_ref, attn_ref, proj_ref, uprev2_ref, uprev1_ref, convw_ref, gattn_ref,
                      gconv_ref, wout_ref, g2_ref, wrhi_ref, wrlo_ref, br_ref,
                      x1_ref, xn2_ref, gates_ref, u_ref):
    gate_b = proj_ref[:, OFF_B:OFF_C]
    u = proj_ref[:, OFF_C:OFF_H] * proj_ref[:, OFF_H:IN_PROJ_DIM]
    cw = convw_ref[...]
    z = cw[0:1, :] * uprev2_ref[...] + cw[1:2, :] * uprev1_ref[...] + cw[2:3, :] * u
    u_ref[...] = u
    x1, xn2_bf, gates = _mix_tail(x_ref[...], attn_ref[...], gate_b * z, gattn_ref[...], gconv_ref[...],
                                  wout_ref[...], g2_ref[...], wrhi_ref[...], wrlo_ref[...], br_ref[...])
    x1_ref[...] = x1
    xn2_ref[...] = xn2_bf
    gates_ref[...] = gates


def _tail_sample(x, attn, proj, uprev2, uprev1, convw, gattn, gconv, wout, g2, wr_hi, wr_lo, br):
    n = x.shape[0]
    return pl.pallas_call(
        _tail_sample_body,
        out_shape=[
            jax.ShapeDtypeStruct((n, D_MODEL), F32),
            jax.ShapeDtypeStruct((n, D_MODEL), BF16),
            jax.ShapeDtypeStruct((n, ROUTER_LANES), F32),
            jax.ShapeDtypeStruct((n, CONV_DIM), F32),
        ],
        compiler_params=pltpu.CompilerParams(vmem_limit_bytes=VMEM_LIMIT),
        name="tail_sample",
    )(x, attn, proj, uprev2, uprev1, convw, gattn, gconv, wout, g2, wr_hi, wr_lo, br)


def _head_perm():
    idx = []
    for m in range(N_HEADS // 2):
        idx += list(range(m * HEAD_DIM, (m + 1) * HEAD_DIM))
        idx += list(range((m + N_HEADS // 2) * HEAD_DIM, (m + N_HEADS // 2 + 1) * HEAD_DIM))
    return jnp.asarray(idx, jnp.int32)


def kernel(x_prompt, x_sample, cache_k_win, cache_v_win, state_conv, norm1_g, w_in, conv_w, sinks,
           attn_out_g, conv_out_g, w_out, norm2_g, w_group, b_group, w_fine, b_fine, w_gate, w_up,
           w_down, final_g):
    nb, seq, _ = x_prompt.shape
    nd = x_sample.shape[0]
    perm = _head_perm()

    w_in0 = w_in[0]
    win = jnp.concatenate([w_in0[:, :ATTN_DIM][:, perm], w_in0[:, ATTN_DIM:]], axis=1).astype(BF16)
    w_out0 = w_out[0]
    wout = jnp.concatenate([w_out0[:ATTN_DIM][perm], w_out0[ATTN_DIM:]], axis=0).astype(BF16)
    gattn = attn_out_g[0][perm][None]
    gconv = conv_out_g[0][None]
    g1 = norm1_g[0][None]
    g2 = norm2_g[0][None]
    gf = final_g[None]
    convw = conv_w[0]
    sinks0 = sinks[0]
    pad = ROUTER_LANES - N_EXPERTS - N_GROUPS
    wr = jnp.concatenate([w_fine[0], w_group[0], jnp.zeros((D_MODEL, pad), F32)], axis=1)
    wr_hi = wr.astype(BF16)
    wr_lo = (wr - wr_hi.astype(F32)).astype(BF16)
    br = jnp.concatenate([b_fine[0], b_group[0], jnp.zeros((pad,), F32)])[None]
    per_group = N_EXPERTS // N_GROUPS
    hidden = per_group * D_EXPERT
    by_group = lambda w: (w.reshape(N_GROUPS, per_group, D_MODEL, D_EXPERT).transpose(0, 2, 1, 3)
                          .reshape(N_GROUPS, D_MODEL, hidden))
    wgu = jnp.concatenate([by_group(w_gate[0]), by_group(w_up[0])], axis=-1).astype(BF16)
    wd = w_down[0].reshape(N_GROUPS, hidden, D_MODEL).astype(BF16)

    x1p, xn2p, gatesp, kp, vp, up = _layer_prompt(x_prompt, sinks0, g1, win, convw, gattn, gconv,
                                                  wout, g2, wr_hi, wr_lo, br)
    n_p = nb * seq
    y_prompt = _moe(x1p.reshape(n_p, D_MODEL), xn2p.reshape(n_p, D_MODEL),
                    gatesp.reshape(n_p, ROUTER_LANES), wgu, wd, gf, T_MOE, MOE_CHUNK).reshape(nb, seq, D_MODEL)

    xs = x_sample.reshape(nd, D_MODEL)
    proj = _proj_sample(xs, g1, win)
    lane = jnp.arange(LANES)
    qp = (proj[:, :ATTN_DIM] * (HEAD_DIM ** -0.5)).reshape(nd, N_HEADS // 2, LANES)
    q8 = jnp.concatenate([jnp.where(lane < HEAD_DIM, qp, 0.0), jnp.where(lane >= HEAD_DIM, qp, 0.0)],
                         axis=1).astype(BF16)
    knew = proj[:, OFF_K:OFF_V].reshape(nd, 1, KV_DIM)
    vnew = proj[:, OFF_V:OFF_B].reshape(nd, 1, KV_DIM)
    khist = cache_k_win[0].reshape(nd, WINDOW, KV_DIM)
    vhist = cache_v_win[0].reshape(nd, WINDOW, KV_DIM)
    out8, ks, vs = _attn_sample(q8, knew, vnew, khist, vhist, sinks0[:, None])
    attn_s = jnp.where(lane < HEAD_DIM, out8[:, :N_HEADS // 2], out8[:, N_HEADS // 2:]).reshape(nd, ATTN_DIM)
    st = state_conv[0]
    x1s, xn2s, gatess, us = _tail_sample(xs, attn_s, proj, st[:, 0], st[:, 1], convw, gattn, gconv,
                                         wout, g2, wr_hi, wr_lo, br)
    y_sample = _moe(x1s, xn2s, gatess, wgu, wd, gf, nd, MOE_CHUNK_DEC).reshape(nd, 1, D_MODEL)

    n_kv = KV_DIM // HEAD_DIM
    return (y_prompt, y_sample,
            kp.reshape(1, nb, WINDOW, n_kv, HEAD_DIM), vp.reshape(1, nb, WINDOW, n_kv, HEAD_DIM),
            up[None],
            ks.reshape(1, nd, WINDOW, n_kv, HEAD_DIM), vs.reshape(1, nd, WINDOW, n_kv, HEAD_DIM),
            jnp.stack([st[:, 1], us], axis=1)[None])
```

```python
import functools

import jax
import jax.numpy as jnp
from jax import lax
from jax.experimental import pallas as pl
from jax.experimental.pallas import tpu as pltpu

D_MODEL = 1024
HEAD_DIM = 64
N_HEADS = 8
ATTN_DIM = 512
KV_DIM = 128
WINDOW = 128
CONV_DIM = 512
CONV_WIDTH = 3
OFF_K = 512
OFF_V = 640
OFF_B = 768
OFF_C = 1280
OFF_H = 1792
IN_PROJ_DIM = 2304
N_GROUPS = 4
N_EXPERTS = 16
D_EXPERT = 256
RMS_EPS = 1e-5
NEG_INF = -1e30

LANES = 128
ROUTER_LANES = 128
COARSE_OFF = N_EXPERTS
T_LAYER = 512
GID_LANE = N_EXPERTS
T_MOE = 512
MOE_CHUNK = 128
MOE_CHUNK_DEC = 64
DEC_CHUNK = 16
VMEM_LIMIT = 48 * 1024 * 1024
MOE_VMEM_LIMIT = 56 * 1024 * 1024

BF16 = jnp.bfloat16
F32 = jnp.float32


def _dot(a, b):
    return jnp.dot(a, b, preferred_element_type=F32)


def _rms(x, g):
    ms = jnp.mean(x * x, axis=-1, keepdims=True)
    return x * lax.rsqrt(ms + RMS_EPS) * g


def _group_norm64(y, g):
    rows, cols = y.shape
    lo = lax.broadcasted_iota(jnp.int32, (rows, LANES), 1) < HEAD_DIM
    outs = []
    for c in range(cols // LANES):
        blk = y[:, c * LANES:(c + 1) * LANES]
        sq = blk * blk
        s_lo = jnp.sum(jnp.where(lo, sq, 0.0), axis=-1, keepdims=True)
        s_hi = jnp.sum(jnp.where(lo, 0.0, sq), axis=-1, keepdims=True)
        ms = jnp.where(lo, s_lo, s_hi) * (1.0 / HEAD_DIM)
        outs.append(blk * lax.rsqrt(ms + RMS_EPS))
    return jnp.concatenate(outs, axis=-1) * g


def _router(xn2, wr, br):
    hi = xn2.astype(BF16)
    lo = (xn2 - hi.astype(F32)).astype(BF16)
    both = _dot(hi, wr)
    logits = both[:, :ROUTER_LANES] + both[:, ROUTER_LANES:] + _dot(lo, wr[:, :ROUTER_LANES]) + br
    rows = logits.shape[0]
    lane = lax.broadcasted_iota(jnp.int32, (rows, ROUTER_LANES), 1)
    lanef = lane.astype(F32)
    big = float(ROUTER_LANES)
    coarse = (lane >= COARSE_OFF) & (lane < COARSE_OFF + N_GROUPS)
    cm = jnp.max(jnp.where(coarse, logits, -jnp.inf), axis=-1, keepdims=True)
    g_sel = jnp.min(jnp.where(coarse & (logits == cm), lanef - COARSE_OFF, big),
                    axis=-1, keepdims=True)
    zc = jnp.sum(jnp.where(coarse, jnp.exp(logits - cm), 0.0), axis=-1, keepdims=True)
    p_sel = 1.0 / zc
    per_group = N_EXPERTS // N_GROUPS
    fine = (lane < N_EXPERTS) & ((lane // per_group).astype(F32) == g_sel)
    f1 = jnp.max(jnp.where(fine, logits, -jnp.inf), axis=-1, keepdims=True)
    i1 = jnp.min(jnp.where(fine & (logits == f1), lanef, big), axis=-1, keepdims=True)
    rest = fine & (lanef != i1)
    f2 = jnp.max(jnp.where(rest, logits, -jnp.inf), axis=-1, keepdims=True)
    i2 = jnp.min(jnp.where(rest & (logits == f2), lanef, big), axis=-1, keepdims=True)
    e2 = jnp.exp(f2 - f1)
    t1 = 1.0 / (1.0 + e2)
    t2 = e2 * t1
    gates = jnp.where(lanef == i1, p_sel * t1, jnp.where(lanef == i2, p_sel * t2, 0.0))
    gates = jnp.where(lane == GID_LANE, g_sel, gates)
    return hi, gates


def _mix_tail(x, attn, conv_out, gattn, gconv, wout, g2, wr, br):
    mixed = jnp.concatenate([_group_norm64(attn, gattn), _group_norm64(conv_out, gconv)],
                            axis=-1).astype(BF16)
    x1 = x + _dot(mixed, wout)
    xn2_bf, gates = _router(_rms(x1, g2), wr, br)
    return x1, xn2_bf, gates


def _layer_prompt_body(sinks_ref, x_ref, g1_ref, win_ref, convw_ref, gattn_ref, gconv_ref,
                       wout_ref, g2_ref, wr_ref, br_ref,
                       x1_ref, xn2_ref, gates_ref, knew_ref, vnew_ref, unew_ref,
                       q_s, kcat_s, vcat_s, attn_s, ucarry_s):
    i = pl.program_id(1)
    t = T_LAYER
    x = x_ref[0]
    xn = _rms(x, g1_ref[...]).astype(BF16)

    @pl.when(i == 0)
    def _():
        kcat_s[0:WINDOW, :] = jnp.zeros((WINDOW, KV_DIM), BF16)
        vcat_s[0:WINDOW, :] = jnp.zeros((WINDOW, KV_DIM), BF16)
        ucarry_s[...] = jnp.zeros(ucarry_s.shape, F32)

    @pl.when(i > 0)
    def _():
        kcat_s[0:WINDOW, :] = kcat_s[t:t + WINDOW, :]
        vcat_s[0:WINDOW, :] = vcat_s[t:t + WINDOW, :]

    q_s[...] = (_dot(xn, win_ref[:, 0:OFF_K]) * (HEAD_DIM ** -0.5)).astype(BF16)
    k = _dot(xn, win_ref[:, OFF_K:OFF_V])
    v = _dot(xn, win_ref[:, OFF_V:OFF_B])
    kcat_s[WINDOW:, :] = k.astype(BF16)
    vcat_s[WINDOW:, :] = v.astype(BF16)
    knew_ref[0] = k[t - WINDOW:, :]
    vnew_ref[0] = v[t - WINDOW:, :]

    a_idx = lax.broadcasted_iota(jnp.int32, (WINDOW, 2 * WINDOW), 0)
    c_idx = lax.broadcasted_iota(jnp.int32, (WINDOW, 2 * WINDOW), 1)
    lane_lo = lax.broadcasted_iota(jnp.int32, (WINDOW, LANES), 1) < HEAD_DIM
    n_pair = N_HEADS // 2

    def block(j):
        r0 = j * WINDOW
        kk = kcat_s[pl.ds(r0, 2 * WINDOW), :]
        vv = vcat_s[pl.ds(r0, 2 * WINDOW), :]
        c_min = jnp.where(i == 0, WINDOW, 0) if j == 0 else 0
        valid = (c_idx >= jnp.maximum(a_idx, c_min)) & (c_idx <= a_idx + WINDOW)
        qcols = [q_s[pl.ds(r0, WINDOW), m * LANES:(m + 1) * LANES] for m in range(n_pair)]
        outs = []
        for half in range(2):
            keep = lane_lo if half == 0 else jnp.logical_not(lane_lo)
            qg = jnp.concatenate([jnp.where(keep, qc, jnp.zeros_like(qc)) for qc in qcols], axis=0)
            s = lax.dot_general(qg, kk, (((1,), (1,)), ((), ())), preferred_element_type=F32)
            es, dens = [], []
            for m in range(n_pair):
                sm = jnp.where(valid, s[m * WINDOW:(m + 1) * WINDOW], NEG_INF)
                sink = sinks_ref[m + half * n_pair]
                mx = jnp.maximum(jnp.max(sm, axis=-1, keepdims=True), sink)
                e = jnp.exp(sm - mx)
                dens.append(jnp.sum(e, axis=-1, keepdims=True) + jnp.exp(sink - mx))
                es.append(e.astype(BF16))
            o = _dot(jnp.concatenate(es, axis=0), vv)
            outs.append([o[m * WINDOW:(m + 1) * WINDOW] / dens[m] for m in range(n_pair)])
        for m in range(n_pair):
            attn_s[pl.ds(r0, WINDOW), m * LANES:(m + 1) * LANES] = jnp.where(lane_lo, outs[0][m], outs[1][m])

    for j in range(t // WINDOW):
        block(j)

    gate_b = _dot(xn, win_ref[:, OFF_B:OFF_C])
    u = _dot(xn, win_ref[:, OFF_C:OFF_H]) * _dot(xn, win_ref[:, OFF_H:IN_PROJ_DIM])
    prev = ucarry_s[...]
    row = lax.broadcasted_iota(jnp.int32, (t, CONV_DIM), 0)
    u1 = jnp.where(row == 0, prev[7:8, :], pltpu.roll(u, 1, axis=0))
    u2 = jnp.where(row == 0, prev[6:7, :], jnp.where(row == 1, prev[7:8, :], pltpu.roll(u, 2, axis=0)))
    cw = convw_ref[...]
    z = cw[0:1, :] * u2 + cw[1:2, :] * u1 + cw[2:3, :] * u
    ucarry_s[...] = u[t - 8:, :]
    unew_ref[0] = u[t - (CONV_WIDTH - 1):, :]

    x1, xn2_bf, gates = _mix_tail(x, attn_s[...], gate_b * z, gattn_ref[...], gconv_ref[...],
                                  wout_ref[...], g2_ref[...], wr_ref[...], br_ref[...])
    x1_ref[0] = x1
    xn2_ref[0] = xn2_bf
    gates_ref[0] = gates


def _layer_prompt(x, sinks, g1, win, convw, gattn, gconv, wout, g2, wr, br):
    nb, seq, _ = x.shape
    t = T_LAYER
    const = lambda b, i, s: (0, 0)
    tile = lambda b, i, s: (b, i, 0)
    per_seq = lambda b, i, s: (b, 0, 0)
    grid_spec = pltpu.PrefetchScalarGridSpec(
        num_scalar_prefetch=1,
        grid=(nb, seq // t),
        in_specs=[
            pl.BlockSpec((1, t, D_MODEL), tile),
            pl.BlockSpec((1, D_MODEL), const),
            pl.BlockSpec((D_MODEL, IN_PROJ_DIM), const),
            pl.BlockSpec((CONV_WIDTH, CONV_DIM), const),
            pl.BlockSpec((1, ATTN_DIM), const),
            pl.BlockSpec((1, CONV_DIM), const),
            pl.BlockSpec((D_MODEL, D_MODEL), const),
            pl.BlockSpec((1, D_MODEL), const),
            pl.BlockSpec((D_MODEL, 2 * ROUTER_LANES), const),
            pl.BlockSpec((1, ROUTER_LANES), const),
        ],
        out_specs=[
            pl.BlockSpec((1, t, D_MODEL), tile),
            pl.BlockSpec((1, t, D_MODEL), tile),
            pl.BlockSpec((1, t, ROUTER_LANES), tile),
            pl.BlockSpec((1, WINDOW, KV_DIM), per_seq),
            pl.BlockSpec((1, WINDOW, KV_DIM), per_seq),
            pl.BlockSpec((1, CONV_WIDTH - 1, CONV_DIM), per_seq),
        ],
        scratch_shapes=[
            pltpu.VMEM((t, ATTN_DIM), BF16),
            pltpu.VMEM((t + WINDOW, KV_DIM), BF16),
            pltpu.VMEM((t + WINDOW, KV_DIM), BF16),
            pltpu.VMEM((t, ATTN_DIM), F32),
            pltpu.VMEM((8, CONV_DIM), F32),
        ],
    )
    return pl.pallas_call(
        _layer_prompt_body,
        grid_spec=grid_spec,
        out_shape=[
            jax.ShapeDtypeStruct((nb, seq, D_MODEL), F32),
            jax.ShapeDtypeStruct((nb, seq, D_MODEL), BF16),
            jax.ShapeDtypeStruct((nb, seq, ROUTER_LANES), F32),
            jax.ShapeDtypeStruct((nb, WINDOW, KV_DIM), F32),
            jax.ShapeDtypeStruct((nb, WINDOW, KV_DIM), F32),
            jax.ShapeDtypeStruct((nb, CONV_WIDTH - 1, CONV_DIM), F32),
        ],
        compiler_params=pltpu.CompilerParams(
            dimension_semantics=("arbitrary", "arbitrary"), vmem_limit_bytes=VMEM_LIMIT),
        name="layer_prompt",
    )(sinks, x, g1, win, convw, gattn, gconv, wout, g2, wr, br)


def _moe_body(chunk, x1_ref, xn2_ref, gates_ref, wg_ref, wu_ref, wd_ref, gf_ref, y_ref, xs_s, gs_s, ys_s):
    t = x1_ref.shape[0]
    per_group = N_EXPERTS // N_GROUPS
    gates = gates_ref[...]
    lane = lax.broadcasted_iota(jnp.int32, (t, ROUTER_LANES), 1)
    gid = jnp.sum(jnp.where(lane == GID_LANE, gates, 0.0), axis=-1, keepdims=True)
    onehot = jnp.where((lane < N_GROUPS) & (lane.astype(F32) == gid), 1.0, 0.0)

    r_idx = lax.broadcasted_iota(jnp.int32, (t, t), 0)
    c_idx = lax.broadcasted_iota(jnp.int32, (t, t), 1)
    lower = jnp.where(c_idx < r_idx, 1.0, 0.0).astype(BF16)
    prefix = _dot(lower, onehot.astype(BF16))
    rank = jnp.sum(prefix * onehot, axis=-1, keepdims=True)
    counts = jnp.sum(onehot, axis=0, keepdims=True)
    lane1 = lax.broadcasted_iota(jnp.int32, (1, ROUTER_LANES), 1)
    cnt = [jnp.sum(jnp.where(lane1 == g, counts, 0.0)).astype(jnp.int32) for g in range(N_GROUPS)]
    off = [jnp.int32(0)]
    for g in range(1, N_GROUPS):
        off.append(off[-1] + cnt[g - 1])

    pos_col = rank
    for g in range(1, N_GROUPS):
        pos_col = pos_col + jnp.where(gid == float(g), off[g].astype(F32), 0.0)
    pos_row = jnp.transpose(jnp.broadcast_to(pos_col, (t, LANES)))[0:1, :]
    perm = jnp.where(r_idx.astype(F32) == pos_row, 1.0, 0.0).astype(BF16)
    perm_t = jnp.where(c_idx.astype(F32) == pos_col, 1.0, 0.0).astype(BF16)

    xs_s[...] = _dot(perm, xn2_ref[...]).astype(BF16)
    p1 = gates.astype(BF16)
    r1 = gates - p1.astype(F32)
    p2 = r1.astype(BF16)
    p3 = (r1 - p2.astype(F32)).astype(BF16)
    gs3 = _dot(perm, jnp.concatenate([p1, p2, p3], axis=1))
    gs_s[...] = (gs3[:, :ROUTER_LANES] + gs3[:, ROUTER_LANES:2 * ROUTER_LANES]) + gs3[:, 2 * ROUTER_LANES:]
    ys_s[...] = jnp.zeros(ys_s.shape, F32)

    for g in range(N_GROUPS):
        c_lo = off[g] // chunk
        c_hi = jnp.where(cnt[g] > 0, (off[g] + cnt[g] + chunk - 1) // chunk, c_lo)

        def chunk_body(ci, carry, g=g):
            r0 = pl.multiple_of(ci * chunk, chunk)
            xs = xs_s[pl.ds(r0, chunk), :]
            gsc = gs_s[pl.ds(r0, chunk), :]
            acts = []
            for k in range(per_group):
                e = g * per_group + k
                hg = _dot(xs, wg_ref[e])
                hu = _dot(xs, wu_ref[e])
                acts.append((hg / (1.0 + jnp.exp(-hg)) * hu * gsc[:, e:e + 1]).astype(BF16))
            ys_s[pl.ds(r0, chunk), :] += _dot(jnp.concatenate(acts, axis=1), wd_ref[g])
            return carry

        lax.fori_loop(c_lo, c_hi, chunk_body, 0)

    moe = _dot(perm_t, ys_s[...].astype(BF16))
    y_ref[...] = _rms(x1_ref[...] + moe, gf_ref[...])


def _moe(x1, xn2, gates, wg, wu, wd, gf, t, chunk):
    n = x1.shape[0]
    tile = lambda i: (i, 0)
    whole = lambda i: (0, 0, 0)
    hidden = (N_EXPERTS // N_GROUPS) * D_EXPERT
    resident = pl.Buffered(1)
    return pl.pallas_call(
        functools.partial(_moe_body, chunk),
        grid=(n // t,),
        in_specs=[
            pl.BlockSpec((t, D_MODEL), tile),
            pl.BlockSpec((t, D_MODEL), tile),
            pl.BlockSpec((t, ROUTER_LANES), tile),
            pl.BlockSpec((N_EXPERTS, D_MODEL, D_EXPERT), whole, pipeline_mode=resident),
            pl.BlockSpec((N_EXPERTS, D_MODEL, D_EXPERT), whole, pipeline_mode=resident),
            pl.BlockSpec((N_GROUPS, hidden, D_MODEL), whole, pipeline_mode=resident),
            pl.BlockSpec((1, D_MODEL), lambda i: (0, 0)),
        ],
        out_specs=pl.BlockSpec((t, D_MODEL), tile),
        out_shape=jax.ShapeDtypeStruct((n, D_MODEL), F32),
        scratch_shapes=[
            pltpu.VMEM((t, D_MODEL), BF16),
            pltpu.VMEM((t, ROUTER_LANES), F32),
            pltpu.VMEM((t, D_MODEL), F32),
        ],
        compiler_params=pltpu.CompilerParams(
            dimension_semantics=("arbitrary",), vmem_limit_bytes=MOE_VMEM_LIMIT),
        name="moe",
    )(x1, xn2, gates, wg, wu, wd, gf)


def _proj_sample_body(x_ref, g1_ref, win_ref, proj_ref):
    xn = _rms(x_ref[...], g1_ref[...]).astype(BF16)
    proj_ref[...] = _dot(xn, win_ref[...])


def _proj_sample(x, g1, win):
    n = x.shape[0]
    return pl.pallas_call(
        _proj_sample_body,
        out_shape=jax.ShapeDtypeStruct((n, IN_PROJ_DIM), F32),
        compiler_params=pltpu.CompilerParams(vmem_limit_bytes=VMEM_LIMIT),
        name="proj_sample",
    )(x, g1, win)


def _attn_sample_body(q8_ref, knew_ref, vnew_ref, khist_ref, vhist_ref, sinks_ref,
                      out8_ref, kout_ref, vout_ref):
    q8 = q8_ref[...]
    kh = khist_ref[...]
    vh = vhist_ref[...]
    knew = knew_ref[...]
    vnew = vnew_ref[...]
    s = jnp.einsum('bhj,bcj->bhc', q8, kh.astype(BF16), preferred_element_type=F32)
    s_new = jnp.sum(q8.astype(F32) * knew, axis=-1, keepdims=True)
    sink = sinks_ref[...][None]
    mx = jnp.maximum(jnp.maximum(jnp.max(s, axis=-1, keepdims=True), s_new), sink)
    e = jnp.exp(s - mx)
    e_new = jnp.exp(s_new - mx)
    den = jnp.sum(e, axis=-1, keepdims=True) + e_new + jnp.exp(sink - mx)
    o = jnp.einsum('bhc,bcj->bhj', e.astype(BF16), vh.astype(BF16), preferred_element_type=F32)
    out8_ref[...] = (o + e_new * vnew) / den
    kout_ref[:, 0:WINDOW - 1, :] = khist_ref[:, 1:WINDOW, :]
    kout_ref[:, WINDOW - 1:WINDOW, :] = knew
    vout_ref[:, 0:WINDOW - 1, :] = vhist_ref[:, 1:WINDOW, :]
    vout_ref[:, WINDOW - 1:WINDOW, :] = vnew


def _attn_sample(q8, knew, vnew, khist, vhist, sinks_col):
    n = q8.shape[0]
    c = DEC_CHUNK
    blk3 = lambda i: (i, 0, 0)
    return pl.pallas_call(
        _attn_sample_body,
        grid=(n // c,),
        in_specs=[
            pl.BlockSpec((c, N_HEADS, LANES), blk3),
            pl.BlockSpec((c, 1, KV_DIM), blk3),
            pl.BlockSpec((c, 1, KV_DIM), blk3),
            pl.BlockSpec((c, WINDOW, KV_DIM), blk3),
            pl.BlockSpec((c, WINDOW, KV_DIM), blk3),
            pl.BlockSpec((N_HEADS, 1), lambda i: (0, 0)),
        ],
        out_specs=[
            pl.BlockSpec((c, N_HEADS, LANES), blk3),
            pl.BlockSpec((c, WINDOW, KV_DIM), blk3),
            pl.BlockSpec((c, WINDOW, KV_DIM), blk3),
        ],
        out_shape=[
            jax.ShapeDtypeStruct((n, N_HEADS, LANES), F32),
            jax.ShapeDtypeStruct((n, WINDOW, KV_DIM), F32),
            jax.ShapeDtypeStruct((n, WINDOW, KV_DIM), F32),
        ],
        compiler_params=pltpu.CompilerParams(
            dimension_semantics=("arbitrary",), vmem_limit_bytes=VMEM_LIMIT),
        name="attn_sample",
    )(q8, knew, vnew, khist, vhist, sinks_col)


def _tail_sample_body(x_ref, attn_ref, proj_ref, uprev2_ref, uprev1_ref, convw_ref, gattn_ref,
                      gconv_ref, wout_ref, g2_ref, wr_ref, br_ref,
                      x1_ref, xn2_ref, gates_ref, u_ref):
    gate_b = proj_ref[:, OFF_B:OFF_C]
    u = proj_ref[:, OFF_C:OFF_H] * proj_ref[:, OFF_H:IN_PROJ_DIM]
    cw = convw_ref[...]
    z = cw[0:1, :] * uprev2_ref[...] + cw[1:2, :] * uprev1_ref[...] + cw[2:3, :] * u
    u_ref[...] = u
    x1, xn2_bf, gates = _mix_tail(x_ref[...], attn_ref[...], gate_b * z, gattn_ref[...], gconv_ref[...],
                                  wout_ref[...], g2_ref[...], wr_ref[...], br_ref[...])
    x1_ref[...] = x1
    xn2_ref[...] = xn2_bf
    gates_ref[...] = gates


def _tail_sample(x, attn, proj, uprev2, uprev1, convw, gattn, gconv, wout, g2, wr, br):
    n = x.shape[0]
    return pl.pallas_call(
        _tail_sample_body,
        out_shape=[
            jax.ShapeDtypeStruct((n, D_MODEL), F32),
            jax.ShapeDtypeStruct((n, D_MODEL), BF16),
            jax.ShapeDtypeStruct((n, ROUTER_LANES), F32),
            jax.ShapeDtypeStruct((n, CONV_DIM), F32),
        ],
        compiler_params=pltpu.CompilerParams(vmem_limit_bytes=VMEM_LIMIT),
        name="tail_sample",
    )(x, attn, proj, uprev2, uprev1, convw, gattn, gconv, wout, g2, wr, br)


def _head_perm():
    idx = []
    for m in range(N_HEADS // 2):
        idx += list(range(m * HEAD_DIM, (m + 1) * HEAD_DIM))
        idx += list(range((m + N_HEADS // 2) * HEAD_DIM, (m + N_HEADS // 2 + 1) * HEAD_DIM))
    return jnp.asarray(idx, jnp.int32)


def kernel(x_prompt, x_sample, cache_k_win, cache_v_win, state_conv, norm1_g, w_in, conv_w, sinks,
           attn_out_g, conv_out_g, w_out, norm2_g, w_group, b_group, w_fine, b_fine, w_gate, w_up,
           w_down, final_g):
    nb, seq, _ = x_prompt.shape
    nd = x_sample.shape[0]
    perm = _head_perm()

    w_in0 = w_in[0]
    win = jnp.concatenate([w_in0[:, :ATTN_DIM][:, perm], w_in0[:, ATTN_DIM:]], axis=1).astype(BF16)
    w_out0 = w_out[0]
    wout = jnp.concatenate([w_out0[:ATTN_DIM][perm], w_out0[ATTN_DIM:]], axis=0).astype(BF16)
    gattn = attn_out_g[0][perm][None]
    gconv = conv_out_g[0][None]
    g1 = norm1_g[0][None]
    g2 = norm2_g[0][None]
    gf = final_g[None]
    convw = conv_w[0]
    sinks0 = sinks[0]
    pad = ROUTER_LANES - N_EXPERTS - N_GROUPS
    wr32 = jnp.concatenate([w_fine[0], w_group[0], jnp.zeros((D_MODEL, pad), F32)], axis=1)
    wr_hi = wr32.astype(BF16)
    wr = jnp.concatenate([wr_hi, (wr32 - wr_hi.astype(F32)).astype(BF16)], axis=1)
    br = jnp.concatenate([b_fine[0], b_group[0], jnp.zeros((pad,), F32)])[None]
    wg = w_gate[0].astype(BF16)
    wu = w_up[0].astype(BF16)
    wd = w_down[0].astype(BF16).reshape(N_GROUPS, (N_EXPERTS // N_GROUPS) * D_EXPERT, D_MODEL)

    x1p, xn2p, gatesp, kp, vp, up = _layer_prompt(x_prompt, sinks0, g1, win, convw, gattn, gconv,
                                                  wout, g2, wr, br)
    n_p = nb * seq
    y_prompt = _moe(x1p.reshape(n_p, D_MODEL), xn2p.reshape(n_p, D_MODEL),
                    gatesp.reshape(n_p, ROUTER_LANES), wg, wu, wd, gf, T_MOE, MOE_CHUNK).reshape(nb, seq, D_MODEL)

    xs = x_sample.reshape(nd, D_MODEL)
    proj = _proj_sample(xs, g1, win)
    lane = jnp.arange(LANES)
    qp = (proj[:, :ATTN_DIM] * (HEAD_DIM ** -0.5)).reshape(nd, N_HEADS // 2, LANES)
    q8 = jnp.concatenate([jnp.where(lane < HEAD_DIM, qp, 0.0), jnp.where(lane >= HEAD_DIM, qp, 0.0)],
                         axis=1).astype(BF16)
    knew = proj[:, OFF_K:OFF_V].reshape(nd, 1, KV_DIM)
    vnew = proj[:, OFF_V:OFF_B].reshape(nd, 1, KV_DIM)
    khist = cache_k_win[0].reshape(nd, WINDOW, KV_DIM)
    vhist = cache_v_win[0].reshape(nd, WINDOW, KV_DIM)
    out8, ks, vs = _attn_sample(q8, knew, vnew, khist, vhist, sinks0[:, None])
    attn_s = jnp.where(lane < HEAD_DIM, out8[:, :N_HEADS // 2], out8[:, N_HEADS // 2:]).reshape(nd, ATTN_DIM)
    st = state_conv[0]
    x1s, xn2s, gatess, us = _tail_sample(xs, attn_s, proj, st[:, 0], st[:, 1], convw, gattn, gconv,
                                         wout, g2, wr, br)
    y_sample = _moe(x1s, xn2s, gatess, wg, wu, wd, gf, nd, MOE_CHUNK_DEC).reshape(nd, 1, D_MODEL)

    n_kv = KV_DIM // HEAD_DIM
    return (y_prompt, y_sample,
            kp.reshape(1, nb, WINDOW, n_kv, HEAD_DIM), vp.reshape(1, nb, WINDOW, n_kv, HEAD_DIM),
            up[None],
            ks.reshape(1, nd, WINDOW, n_kv, HEAD_DIM), vs.reshape(1, nd, WINDOW, n_kv, HEAD_DIM),
            jnp.stack([st[:, 1], us], axis=1)[None])
```

```python
import functools

import jax
import jax.numpy as jnp
from jax import lax
from jax.experimental import pallas as pl
from jax.experimental.pallas import tpu as pltpu

D_MODEL = 1024
HEAD_DIM = 64
N_HEADS = 8
ATTN_DIM = 512
KV_DIM = 128
WINDOW = 128
CONV_DIM = 512
CONV_WIDTH = 3
OFF_K = 512
OFF_V = 640
OFF_B = 768
OFF_C = 1280
OFF_H = 1792
IN_PROJ_DIM = 2304
N_GROUPS = 4
N_EXPERTS = 16
D_EXPERT = 256
RMS_EPS = 1e-5
NEG_INF = -1e30

LANES = 128
ROW_ALIGN = 16
ROUTER_LANES = 128
COARSE_OFF = N_EXPERTS
T_LAYER = 1024
N_SUB = 2
GID_LANE = N_EXPERTS
T_MOE = 512
MOE_CHUNK = 128
MOE_CHUNK_DEC = 64
DEC_CHUNK = 16
VMEM_LIMIT = 48 * 1024 * 1024
MOE_VMEM_LIMIT = 56 * 1024 * 1024

BF16 = jnp.bfloat16
F32 = jnp.float32


def _dot(a, b):
    return jnp.dot(a, b, preferred_element_type=F32)


def _rms(x, g):
    ms = jnp.mean(x * x, axis=-1, keepdims=True)
    return x * lax.rsqrt(ms + RMS_EPS) * g


def _group_norm64(y, g):
    rows, cols = y.shape
    lo = lax.broadcasted_iota(jnp.int32, (rows, LANES), 1) < HEAD_DIM
    outs = []
    for c in range(cols // LANES):
        blk = y[:, c * LANES:(c + 1) * LANES]
        sq = blk * blk
        s_lo = jnp.sum(jnp.where(lo, sq, 0.0), axis=-1, keepdims=True)
        s_hi = jnp.sum(jnp.where(lo, 0.0, sq), axis=-1, keepdims=True)
        ms = jnp.where(lo, s_lo, s_hi) * (1.0 / HEAD_DIM)
        outs.append(blk * lax.rsqrt(ms + RMS_EPS))
    return jnp.concatenate(outs, axis=-1) * g


def _router(xn2, wr, br):
    hi = xn2.astype(BF16)
    lo = (xn2 - hi.astype(F32)).astype(BF16)
    both = _dot(hi, wr)
    logits = both[:, :ROUTER_LANES] + both[:, ROUTER_LANES:] + _dot(lo, wr[:, :ROUTER_LANES]) + br
    rows = logits.shape[0]
    lane = lax.broadcasted_iota(jnp.int32, (rows, ROUTER_LANES), 1)
    lanef = lane.astype(F32)
    big = float(ROUTER_LANES)
    coarse = (lane >= COARSE_OFF) & (lane < COARSE_OFF + N_GROUPS)
    cm = jnp.max(jnp.where(coarse, logits, -jnp.inf), axis=-1, keepdims=True)
    g_sel = jnp.min(jnp.where(coarse & (logits == cm), lanef - COARSE_OFF, big),
                    axis=-1, keepdims=True)
    zc = jnp.sum(jnp.where(coarse, jnp.exp(logits - cm), 0.0), axis=-1, keepdims=True)
    p_sel = 1.0 / zc
    per_group = N_EXPERTS // N_GROUPS
    fine = (lane < N_EXPERTS) & ((lane // per_group).astype(F32) == g_sel)
    f1 = jnp.max(jnp.where(fine, logits, -jnp.inf), axis=-1, keepdims=True)
    i1 = jnp.min(jnp.where(fine & (logits == f1), lanef, big), axis=-1, keepdims=True)
    rest = fine & (lanef != i1)
    f2 = jnp.max(jnp.where(rest, logits, -jnp.inf), axis=-1, keepdims=True)
    i2 = jnp.min(jnp.where(rest & (logits == f2), lanef, big), axis=-1, keepdims=True)
    e2 = jnp.exp(f2 - f1)
    t1 = 1.0 / (1.0 + e2)
    t2 = e2 * t1
    gates = jnp.where(lanef == i1, p_sel * t1, jnp.where(lanef == i2, p_sel * t2, 0.0))
    gates = jnp.where(lane == GID_LANE, g_sel, gates)
    return hi, gates


def _mix_tail(x, attn, conv_out, gattn, gconv, wout, g2, wr, br):
    mixed = jnp.concatenate([_group_norm64(attn, gattn), _group_norm64(conv_out, gconv)],
                            axis=-1).astype(BF16)
    x1 = x + _dot(mixed, wout)
    xn2_bf, gates = _router(_rms(x1, g2), wr, br)
    return x1, xn2_bf, gates


def _layer_prompt_body(sinks_ref, x_ref, g1_ref, win_ref, convw_ref, gattn_ref, gconv_ref,
                       wout_ref, g2_ref, wr_ref, br_ref,
                       x1_ref, xn2_ref, gates_ref, knew_ref, vnew_ref, unew_ref,
                       q_s, kcat_s, vcat_s, attn_s, ucarry_s):
    i = pl.program_id(1)
    t = T_LAYER // N_SUB
    last = N_SUB - 1

    @pl.when(i == 0)
    def _():
        kcat_s[last, t:t + WINDOW, :] = jnp.zeros((WINDOW, KV_DIM), BF16)
        vcat_s[last, t:t + WINDOW, :] = jnp.zeros((WINDOW, KV_DIM), BF16)
        ucarry_s[last] = jnp.zeros((8, CONV_DIM), F32)

    a_idx = lax.broadcasted_iota(jnp.int32, (WINDOW, 2 * WINDOW), 0)
    c_idx = lax.broadcasted_iota(jnp.int32, (WINDOW, 2 * WINDOW), 1)
    lane_lo = lax.broadcasted_iota(jnp.int32, (WINDOW, LANES), 1) < HEAD_DIM
    row = lax.broadcasted_iota(jnp.int32, (t, CONV_DIM), 0)
    n_pair = N_HEADS // 2
    cw = convw_ref[...]

    for h in range(N_SUB):
        src = (h - 1) % N_SUB
        rows = pl.ds(h * t, t)
        x = x_ref[0, rows, :]
        xn = _rms(x, g1_ref[...]).astype(BF16)

        kcat_s[h, 0:WINDOW, :] = kcat_s[src, t:t + WINDOW, :]
        vcat_s[h, 0:WINDOW, :] = vcat_s[src, t:t + WINDOW, :]
        q_s[h] = (_dot(xn, win_ref[:, 0:OFF_K]) * (HEAD_DIM ** -0.5)).astype(BF16)
        k = _dot(xn, win_ref[:, OFF_K:OFF_V])
        v = _dot(xn, win_ref[:, OFF_V:OFF_B])
        kcat_s[h, WINDOW:, :] = k.astype(BF16)
        vcat_s[h, WINDOW:, :] = v.astype(BF16)
        if h == last:
            knew_ref[0] = k[t - WINDOW:, :]
            vnew_ref[0] = v[t - WINDOW:, :]

        for j in range(t // WINDOW):
            r0 = j * WINDOW
            kk = kcat_s[h, r0:r0 + 2 * WINDOW, :]
            vv = vcat_s[h, r0:r0 + 2 * WINDOW, :]
            c_min = jnp.where(i == 0, WINDOW, 0) if (h == 0 and j == 0) else 0
            valid = (c_idx >= jnp.maximum(a_idx, c_min)) & (c_idx <= a_idx + WINDOW)
            qcols = [q_s[h, r0:r0 + WINDOW, m * LANES:(m + 1) * LANES] for m in range(n_pair)]
            outs = []
            for half in range(2):
                keep = lane_lo if half == 0 else jnp.logical_not(lane_lo)
                qg = jnp.concatenate([jnp.where(keep, qc, jnp.zeros_like(qc)) for qc in qcols], axis=0)
                s = lax.dot_general(qg, kk, (((1,), (1,)), ((), ())), preferred_element_type=F32)
                es, dens = [], []
                for m in range(n_pair):
                    sm = jnp.where(valid, s[m * WINDOW:(m + 1) * WINDOW], NEG_INF)
                    sink = sinks_ref[m + half * n_pair]
                    mx = jnp.maximum(jnp.max(sm, axis=-1, keepdims=True), sink)
                    e = jnp.exp(sm - mx)
                    dens.append(jnp.sum(e, axis=-1, keepdims=True) + jnp.exp(sink - mx))
                    es.append(e.astype(BF16))
                o = _dot(jnp.concatenate(es, axis=0), vv)
                outs.append([o[m * WINDOW:(m + 1) * WINDOW] / dens[m] for m in range(n_pair)])
            for m in range(n_pair):
                attn_s[h, r0:r0 + WINDOW, m * LANES:(m + 1) * LANES] = jnp.where(lane_lo, outs[0][m], outs[1][m])

        gate_b = _dot(xn, win_ref[:, OFF_B:OFF_C])
        u = _dot(xn, win_ref[:, OFF_C:OFF_H]) * _dot(xn, win_ref[:, OFF_H:IN_PROJ_DIM])
        prev = ucarry_s[src]
        u1 = jnp.where(row == 0, prev[7:8, :], pltpu.roll(u, 1, axis=0))
        u2 = jnp.where(row == 0, prev[6:7, :], jnp.where(row == 1, prev[7:8, :], pltpu.roll(u, 2, axis=0)))
        z = cw[0:1, :] * u2 + cw[1:2, :] * u1 + cw[2:3, :] * u
        ucarry_s[h] = u[t - 8:, :]
        if h == last:
            unew_ref[0] = u[t - (CONV_WIDTH - 1):, :]

        x1, xn2_bf, gates = _mix_tail(x, attn_s[h], gate_b * z, gattn_ref[...], gconv_ref[...],
                                      wout_ref[...], g2_ref[...], wr_ref[...], br_ref[...])
        x1_ref[0, rows, :] = x1
        xn2_ref[0, rows, :] = xn2_bf
        gates_ref[0, rows, :] = gates


def _layer_prompt(x, sinks, g1, win, convw, gattn, gconv, wout, g2, wr, br):
    nb, seq, _ = x.shape
    t = T_LAYER
    ts = T_LAYER // N_SUB
    const = lambda b, i, s: (0, 0)
    tile = lambda b, i, s: (b, i, 0)
    per_seq = lambda b, i, s: (b, 0, 0)
    resident = pl.Buffered(1)
    grid_spec = pltpu.PrefetchScalarGridSpec(
        num_scalar_prefetch=1,
        grid=(nb, seq // t),
        in_specs=[
            pl.BlockSpec((1, t, D_MODEL), tile),
            pl.BlockSpec((1, D_MODEL), const),
            pl.BlockSpec((D_MODEL, IN_PROJ_DIM), const, pipeline_mode=resident),
            pl.BlockSpec((CONV_WIDTH, CONV_DIM), const),
            pl.BlockSpec((1, ATTN_DIM), const),
            pl.BlockSpec((1, CONV_DIM), const),
            pl.BlockSpec((D_MODEL, D_MODEL), const, pipeline_mode=resident),
            pl.BlockSpec((1, D_MODEL), const),
            pl.BlockSpec((D_MODEL, 2 * ROUTER_LANES), const, pipeline_mode=resident),
            pl.BlockSpec((1, ROUTER_LANES), const),
        ],
        out_specs=[
            pl.BlockSpec((1, t, D_MODEL), tile),
            pl.BlockSpec((1, t, D_MODEL), tile),
            pl.BlockSpec((1, t, ROUTER_LANES), tile),
            pl.BlockSpec((1, WINDOW, KV_DIM), per_seq),
            pl.BlockSpec((1, WINDOW, KV_DIM), per_seq),
            pl.BlockSpec((1, CONV_WIDTH - 1, CONV_DIM), per_seq),
        ],
        scratch_shapes=[
            pltpu.VMEM((N_SUB, ts, ATTN_DIM), BF16),
            pltpu.VMEM((N_SUB, ts + WINDOW, KV_DIM), BF16),
            pltpu.VMEM((N_SUB, ts + WINDOW, KV_DIM), BF16),
            pltpu.VMEM((N_SUB, ts, ATTN_DIM), F32),
            pltpu.VMEM((N_SUB, 8, CONV_DIM), F32),
        ],
    )
    return pl.pallas_call(
        _layer_prompt_body,
        grid_spec=grid_spec,
        out_shape=[
            jax.ShapeDtypeStruct((nb, seq, D_MODEL), F32),
            jax.ShapeDtypeStruct((nb, seq, D_MODEL), BF16),
            jax.ShapeDtypeStruct((nb, seq, ROUTER_LANES), F32),
            jax.ShapeDtypeStruct((nb, WINDOW, KV_DIM), F32),
            jax.ShapeDtypeStruct((nb, WINDOW, KV_DIM), F32),
            jax.ShapeDtypeStruct((nb, CONV_WIDTH - 1, CONV_DIM), F32),
        ],
        compiler_params=pltpu.CompilerParams(
            dimension_semantics=("arbitrary", "arbitrary"), vmem_limit_bytes=VMEM_LIMIT),
        name="layer_prompt",
    )(sinks, x, g1, win, convw, gattn, gconv, wout, g2, wr, br)


def _moe_body(chunk, x1_ref, xn2_ref, gates_ref, wg_ref, wu_ref, wd_ref, gf_ref, y_ref, xs_s, gs_s, ys_s):
    t = x1_ref.shape[0]
    per_group = N_EXPERTS // N_GROUPS
    gates = gates_ref[...]
    lane = lax.broadcasted_iota(jnp.int32, (t, ROUTER_LANES), 1)
    gid = jnp.sum(jnp.where(lane == GID_LANE, gates, 0.0), axis=-1, keepdims=True)
    onehot = jnp.where((lane < N_GROUPS) & (lane.astype(F32) == gid), 1.0, 0.0)

    r_idx = lax.broadcasted_iota(jnp.int32, (t, t), 0)
    c_idx = lax.broadcasted_iota(jnp.int32, (t, t), 1)
    lower = jnp.where(c_idx < r_idx, 1.0, 0.0).astype(BF16)
    prefix = _dot(lower, onehot.astype(BF16))
    rank = jnp.sum(prefix * onehot, axis=-1, keepdims=True)
    counts = jnp.sum(onehot, axis=0, keepdims=True)
    lane1 = lax.broadcasted_iota(jnp.int32, (1, ROUTER_LANES), 1)
    cnt = [jnp.sum(jnp.where(lane1 == g, counts, 0.0)).astype(jnp.int32) for g in range(N_GROUPS)]
    off = [jnp.int32(0)]
    for g in range(1, N_GROUPS):
        off.append(off[-1] + cnt[g - 1])

    pos_col = rank
    for g in range(1, N_GROUPS):
        pos_col = pos_col + jnp.where(gid == float(g), off[g].astype(F32), 0.0)
    pos_row = jnp.transpose(jnp.broadcast_to(pos_col, (t, LANES)))[0:1, :]
    perm = jnp.where(r_idx.astype(F32) == pos_row, 1.0, 0.0).astype(BF16)
    perm_t = jnp.where(c_idx.astype(F32) == pos_col, 1.0, 0.0).astype(BF16)

    xs_s[0:t, :] = _dot(perm, xn2_ref[...]).astype(BF16)
    p1 = gates.astype(BF16)
    r1 = gates - p1.astype(F32)
    p2 = r1.astype(BF16)
    p3 = (r1 - p2.astype(F32)).astype(BF16)
    gs3 = _dot(perm, jnp.concatenate([p1, p2, p3], axis=1))
    gs_s[0:t, :] = (gs3[:, :ROUTER_LANES] + gs3[:, ROUTER_LANES:2 * ROUTER_LANES]) + gs3[:, 2 * ROUTER_LANES:]
    xs_s[t:, :] = jnp.zeros((chunk, D_MODEL), BF16)
    gs_s[t:, :] = jnp.zeros((chunk, ROUTER_LANES), F32)
    ys_s[...] = jnp.zeros(ys_s.shape, F32)

    for g in range(N_GROUPS):
        base = (off[g] // ROW_ALIGN) * ROW_ALIGN
        n_chunks = jnp.where(cnt[g] > 0, (off[g] - base + cnt[g] + chunk - 1) // chunk, 0)

        def chunk_body(ci, carry, g=g, base=base):
            r0 = pl.multiple_of(base + ci * chunk, ROW_ALIGN)
            xs = xs_s[pl.ds(r0, chunk), :]
            gsc = gs_s[pl.ds(r0, chunk), :]
            acts = []
            for k in range(per_group):
                e = g * per_group + k
                hg = _dot(xs, wg_ref[e])
                hu = _dot(xs, wu_ref[e])
                acts.append((hg / (1.0 + jnp.exp(-hg)) * hu * gsc[:, e:e + 1]).astype(BF16))
            ys_s[pl.ds(r0, chunk), :] += _dot(jnp.concatenate(acts, axis=1), wd_ref[g])
            return carry

        lax.fori_loop(0, n_chunks, chunk_body, 0)

    moe = _dot(perm_t, ys_s[0:t, :].astype(BF16))
    y_ref[...] = _rms(x1_ref[...] + moe, gf_ref[...])


def _moe(x1, xn2, gates, wg, wu, wd, gf, t, chunk):
    n = x1.shape[0]
    tile = lambda i: (i, 0)
    whole = lambda i: (0, 0, 0)
    hidden = (N_EXPERTS // N_GROUPS) * D_EXPERT
    resident = pl.Buffered(1)
    return pl.pallas_call(
        functools.partial(_moe_body, chunk),
        grid=(n // t,),
        in_specs=[
            pl.BlockSpec((t, D_MODEL), tile),
            pl.BlockSpec((t, D_MODEL), tile),
            pl.BlockSpec((t, ROUTER_LANES), tile),
            pl.BlockSpec((N_EXPERTS, D_MODEL, D_EXPERT), whole, pipeline_mode=resident),
            pl.BlockSpec((N_EXPERTS, D_MODEL, D_EXPERT), whole, pipeline_mode=resident),
            pl.BlockSpec((N_GROUPS, hidden, D_MODEL), whole, pipeline_mode=resident),
            pl.BlockSpec((1, D_MODEL), lambda i: (0, 0)),
        ],
        out_specs=pl.BlockSpec((t, D_MODEL), tile),
        out_shape=jax.ShapeDtypeStruct((n, D_MODEL), F32),
        scratch_shapes=[
            pltpu.VMEM((t + chunk, D_MODEL), BF16),
            pltpu.VMEM((t + chunk, ROUTER_LANES), F32),
            pltpu.VMEM((t + chunk, D_MODEL), F32),
        ],
        compiler_params=pltpu.CompilerParams(
            dimension_semantics=("arbitrary",), vmem_limit_bytes=MOE_VMEM_LIMIT),
        name="moe",
    )(x1, xn2, gates, wg, wu, wd, gf)


def _proj_sample_body(x_ref, g1_ref, win_ref, proj_ref):
    xn = _rms(x_ref[...], g1_ref[...]).astype(BF16)
    proj_ref[...] = _dot(xn, win_ref[...])


def _proj_sample(x, g1, win):
    n = x.shape[0]
    return pl.pallas_call(
        _proj_sample_body,
        out_shape=jax.ShapeDtypeStruct((n, IN_PROJ_DIM), F32),
        compiler_params=pltpu.CompilerParams(vmem_limit_bytes=VMEM_LIMIT),
        name="proj_sample",
    )(x, g1, win)


def _attn_sample_body(q8_ref, knew_ref, vnew_ref, khist_ref, vhist_ref, sinks_ref,
                      out8_ref, kout_ref, vout_ref):
    q8 = q8_ref[...]
    kh = khist_ref[...]
    vh = vhist_ref[...]
    knew = knew_ref[...]
    vnew = vnew_ref[...]
    s = jnp.einsum('bhj,bcj->bhc', q8, kh.astype(BF16), preferred_element_type=F32)
    s_new = jnp.sum(q8.astype(F32) * knew, axis=-1, keepdims=True)
    sink = sinks_ref[...][None]
    mx = jnp.maximum(jnp.maximum(jnp.max(s, axis=-1, keepdims=True), s_new), sink)
    e = jnp.exp(s - mx)
    e_new = jnp.exp(s_new - mx)
    den = jnp.sum(e, axis=-1, keepdims=True) + e_new + jnp.exp(sink - mx)
    o = jnp.einsum('bhc,bcj->bhj', e.astype(BF16), vh.astype(BF16), preferred_element_type=F32)
    out8_ref[...] = (o + e_new * vnew) / den
    kout_ref[:, 0:WINDOW - 1, :] = khist_ref[:, 1:WINDOW, :]
    kout_ref[:, WINDOW - 1:WINDOW, :] = knew
    vout_ref[:, 0:WINDOW - 1, :] = vhist_ref[:, 1:WINDOW, :]
    vout_ref[:, WINDOW - 1:WINDOW, :] = vnew


def _attn_sample(q8, knew, vnew, khist, vhist, sinks_col):
    n = q8.shape[0]
    c = DEC_CHUNK
    blk3 = lambda i: (i, 0, 0)
    return pl.pallas_call(
        _attn_sample_body,
        grid=(n // c,),
        in_specs=[
            pl.BlockSpec((c, N_HEADS, LANES), blk3),
            pl.BlockSpec((c, 1, KV_DIM), blk3),
            pl.BlockSpec((c, 1, KV_DIM), blk3),
            pl.BlockSpec((c, WINDOW, KV_DIM), blk3),
            pl.BlockSpec((c, WINDOW, KV_DIM), blk3),
            pl.BlockSpec((N_HEADS, 1), lambda i: (0, 0)),
        ],
        out_specs=[
            pl.BlockSpec((c, N_HEADS, LANES), blk3),
            pl.BlockSpec((c, WINDOW, KV_DIM), blk3),
            pl.BlockSpec((c, WINDOW, KV_DIM), blk3),
        ],
        out_shape=[
            jax.ShapeDtypeStruct((n, N_HEADS, LANES), F32),
            jax.ShapeDtypeStruct((n, WINDOW, KV_DIM), F32),
            jax.ShapeDtypeStruct((n, WINDOW, KV_DIM), F32),
        ],
        compiler_params=pltpu.CompilerParams(
            dimension_semantics=("arbitrary",), vmem_limit_bytes=VMEM_LIMIT),
        name="attn_sample",
    )(q8, knew, vnew, khist, vhist, sinks_col)


def _tail_sample_body(x_ref, attn_ref, proj_ref, uprev2_ref, uprev1_ref, convw_ref, gattn_ref,
                      gconv_ref, wout_ref, g2_ref, wr_ref, br_ref,
                      x1_ref, xn2_ref, gates_ref, u_ref):
    gate_b = proj_ref[:, OFF_B:OFF_C]
    u = proj_ref[:, OFF_C:OFF_H] * proj_ref[:, OFF_H:IN_PROJ_DIM]
    cw = convw_ref[...]
    z = cw[0:1, :] * uprev2_ref[...] + cw[1:2, :] * uprev1_ref[...] + cw[2:3, :] * u
    u_ref[...] = u
    x1, xn2_bf, gates = _mix_tail(x_ref[...], attn_ref[...], gate_b * z, gattn_ref[...], gconv_ref[...],
                                  wout_ref[...], g2_ref[...], wr_ref[...], br_ref[...])
    x1_ref[...] = x1
    xn2_ref[...] = xn2_bf
    gates_ref[...] = gates


def _tail_sample(x, attn, proj, uprev2, uprev1, convw, gattn, gconv, wout, g2, wr, br):
    n = x.shape[0]
    return pl.pallas_call(
        _tail_sample_body,
        out_shape=[
            jax.ShapeDtypeStruct((n, D_MODEL), F32),
            jax.ShapeDtypeStruct((n, D_MODEL), BF16),
            jax.ShapeDtypeStruct((n, ROUTER_LANES), F32),
            jax.ShapeDtypeStruct((n, CONV_DIM), F32),
        ],
        compiler_params=pltpu.CompilerParams(vmem_limit_bytes=VMEM_LIMIT),
        name="tail_sample",
    )(x, attn, proj, uprev2, uprev1, convw, gattn, gconv, wout, g2, wr, br)


def _head_perm():
    idx = []
    for m in range(N_HEADS // 2):
        idx += list(range(m * HEAD_DIM, (m + 1) * HEAD_DIM))
        idx += list(range((m + N_HEADS // 2) * HEAD_DIM, (m + N_HEADS // 2 + 1) * HEAD_DIM))
    return jnp.asarray(idx, jnp.int32)


def kernel(x_prompt, x_sample, cache_k_win, cache_v_win, state_conv, norm1_g, w_in, conv_w, sinks,
           attn_out_g, conv_out_g, w_out, norm2_g, w_group, b_group, w_fine, b_fine, w_gate, w_up,
           w_down, final_g):
    nb, seq, _ = x_prompt.shape
    nd = x_sample.shape[0]
    perm = _head_perm()

    w_in0 = w_in[0]
    win = jnp.concatenate([w_in0[:, :ATTN_DIM][:, perm], w_in0[:, ATTN_DIM:]], axis=1).astype(BF16)
    w_out0 = w_out[0]
    wout = jnp.concatenate([w_out0[:ATTN_DIM][perm], w_out0[ATTN_DIM:]], axis=0).astype(BF16)
    gattn = attn_out_g[0][perm][None]
    gconv = conv_out_g[0][None]
    g1 = norm1_g[0][None]
    g2 = norm2_g[0][None]
    gf = final_g[None]
    convw = conv_w[0]
    sinks0 = sinks[0]
    pad = ROUTER_LANES - N_EXPERTS - N_GROUPS
    wr32 = jnp.concatenate([w_fine[0], w_group[0], jnp.zeros((D_MODEL, pad), F32)], axis=1)
    wr_hi = wr32.astype(BF16)
    wr = jnp.concatenate([wr_hi, (wr32 - wr_hi.astype(F32)).astype(BF16)], axis=1)
    br = jnp.concatenate([b_fine[0], b_group[0], jnp.zeros((pad,), F32)])[None]
    wg = w_gate[0].astype(BF16)
    wu = w_up[0].astype(BF16)
    wd = w_down[0].astype(BF16).reshape(N_GROUPS, (N_EXPERTS // N_GROUPS) * D_EXPERT, D_MODEL)

    x1p, xn2p, gatesp, kp, vp, up = _layer_prompt(x_prompt, sinks0, g1, win, convw, gattn, gconv,
                                                  wout, g2, wr, br)
    n_p = nb * seq
    y_prompt = _moe(x1p.reshape(n_p, D_MODEL), xn2p.reshape(n_p, D_MODEL),
                    gatesp.reshape(n_p, ROUTER_LANES), wg, wu, wd, gf, T_MOE, MOE_CHUNK).reshape(nb, seq, D_MODEL)

    xs = x_sample.reshape(nd, D_MODEL)
    proj = _proj_sample(xs, g1, win)
    lane = jnp.arange(LANES)
    qp = (proj[:, :ATTN_DIM] * (HEAD_DIM ** -0.5)).reshape(nd, N_HEADS // 2, LANES)
    q8 = jnp.concatenate([jnp.where(lane < HEAD_DIM, qp, 0.0), jnp.where(lane >= HEAD_DIM, qp, 0.0)],
                         axis=1).astype(BF16)
    knew = proj[:, OFF_K:OFF_V].reshape(nd, 1, KV_DIM)
    vnew = proj[:, OFF_V:OFF_B].reshape(nd, 1, KV_DIM)
    khist = cache_k_win[0].reshape(nd, WINDOW, KV_DIM)
    vhist = cache_v_win[0].reshape(nd, WINDOW, KV_DIM)
    out8, ks, vs = _attn_sample(q8, knew, vnew, khist, vhist, sinks0[:, None])
    attn_s = jnp.where(lane < HEAD_DIM, out8[:, :N_HEADS // 2], out8[:, N_HEADS // 2:]).reshape(nd, ATTN_DIM)
    st = state_conv[0]
    x1s, xn2s, gatess, us = _tail_sample(xs, attn_s, proj, st[:, 0], st[:, 1], convw, gattn, gconv,
                                         wout, g2, wr, br)
    y_sample = _moe(x1s, xn2s, gatess, wg, wu, wd, gf, nd, MOE_CHUNK_DEC).reshape(nd, 1, D_MODEL)

    n_kv = KV_DIM // HEAD_DIM
    return (y_prompt, y_sample,
            kp.reshape(1, nb, WINDOW, n_kv, HEAD_DIM), vp.reshape(1, nb, WINDOW, n_kv, HEAD_DIM),
            up[None],
            ks.reshape(1, nd, WINDOW, n_kv, HEAD_DIM), vs.reshape(1, nd, WINDOW, n_kv, HEAD_DIM),
            jnp.stack([st[:, 1], us], axis=1)[None])
```

```python
import functools

import jax
import jax.numpy as jnp
from jax import lax
from jax.experimental import pallas as pl
from jax.experimental.pallas import tpu as pltpu

D_MODEL = 1024
HEAD_DIM = 64
N_HEADS = 8
ATTN_DIM = 512
KV_DIM = 128
WINDOW = 128
CONV_DIM = 512
CONV_WIDTH = 3
OFF_K = 512
OFF_V = 640
OFF_B = 768
OFF_C = 1280
OFF_H = 1792
IN_PROJ_DIM = 2304
N_GROUPS = 4
N_EXPERTS = 16
D_EXPERT = 256
RMS_EPS = 1e-5
NEG_INF = -1e30

LANES = 128
ROW_ALIGN = 16
ROUTER_LANES = 128
COARSE_OFF = N_EXPERTS
T_LAYER = 1024
N_SUB = 2
GID_LANE = N_EXPERTS
T_MOE = 512
MOE_CAP, MOE_XC = 176, 64
MOE_CAP_DEC, MOE_XC_DEC = 64, 32
DEC_CHUNK = 32
VMEM_LIMIT = 48 * 1024 * 1024
MOE_VMEM_LIMIT = 56 * 1024 * 1024

BF16 = jnp.bfloat16
F32 = jnp.float32


def _dot(a, b):
    return jnp.dot(a, b, preferred_element_type=F32)


def _rms(x, g):
    ms = jnp.mean(x * x, axis=-1, keepdims=True)
    return x * lax.rsqrt(ms + RMS_EPS) * g


def _group_norm64(y, g):
    rows, cols = y.shape
    lo = lax.broadcasted_iota(jnp.int32, (rows, LANES), 1) < HEAD_DIM
    outs = []
    for c in range(cols // LANES):
        blk = y[:, c * LANES:(c + 1) * LANES]
        sq = blk * blk
        s_lo = jnp.sum(jnp.where(lo, sq, 0.0), axis=-1, keepdims=True)
        s_hi = jnp.sum(jnp.where(lo, 0.0, sq), axis=-1, keepdims=True)
        ms = jnp.where(lo, s_lo, s_hi) * (1.0 / HEAD_DIM)
        outs.append(blk * lax.rsqrt(ms + RMS_EPS))
    return jnp.concatenate(outs, axis=-1) * g


def _router(xn2, wr, br):
    hi = xn2.astype(BF16)
    lo = (xn2 - hi.astype(F32)).astype(BF16)
    both = _dot(hi, wr)
    logits = both[:, :ROUTER_LANES] + both[:, ROUTER_LANES:] + _dot(lo, wr[:, :ROUTER_LANES]) + br
    rows = logits.shape[0]
    lane = lax.broadcasted_iota(jnp.int32, (rows, ROUTER_LANES), 1)
    lanef = lane.astype(F32)
    big = float(ROUTER_LANES)
    coarse = (lane >= COARSE_OFF) & (lane < COARSE_OFF + N_GROUPS)
    cm = jnp.max(jnp.where(coarse, logits, -jnp.inf), axis=-1, keepdims=True)
    g_sel = jnp.min(jnp.where(coarse & (logits == cm), lanef - COARSE_OFF, big),
                    axis=-1, keepdims=True)
    zc = jnp.sum(jnp.where(coarse, jnp.exp(logits - cm), 0.0), axis=-1, keepdims=True)
    p_sel = 1.0 / zc
    per_group = N_EXPERTS // N_GROUPS
    fine = (lane < N_EXPERTS) & ((lane // per_group).astype(F32) == g_sel)
    f1 = jnp.max(jnp.where(fine, logits, -jnp.inf), axis=-1, keepdims=True)
    i1 = jnp.min(jnp.where(fine & (logits == f1), lanef, big), axis=-1, keepdims=True)
    rest = fine & (lanef != i1)
    f2 = jnp.max(jnp.where(rest, logits, -jnp.inf), axis=-1, keepdims=True)
    i2 = jnp.min(jnp.where(rest & (logits == f2), lanef, big), axis=-1, keepdims=True)
    e2 = jnp.exp(f2 - f1)
    t1 = 1.0 / (1.0 + e2)
    t2 = e2 * t1
    gates = jnp.where(lanef == i1, p_sel * t1, jnp.where(lanef == i2, p_sel * t2, 0.0))
    gates = jnp.where(lane == GID_LANE, g_sel, gates)
    return hi, gates


def _mix_tail(x, attn, conv_out, gattn, gconv, wout, g2, wr, br):
    mixed = jnp.concatenate([_group_norm64(attn, gattn), _group_norm64(conv_out, gconv)],
                            axis=-1).astype(BF16)
    x1 = x + _dot(mixed, wout)
    xn2_bf, gates = _router(_rms(x1, g2), wr, br)
    return x1, xn2_bf, gates


def _layer_prompt_body(sinks_ref, x_ref, g1_ref, win_ref, convw_ref, gattn_ref, gconv_ref,
                       wout_ref, g2_ref, wr_ref, br_ref,
                       x1_ref, xn2_ref, gates_ref, knew_ref, vnew_ref, unew_ref,
                       q_s, kcat_s, vcat_s, attn_s, ucarry_s):
    i = pl.program_id(1)
    t = T_LAYER // N_SUB
    last = N_SUB - 1

    @pl.when(i == 0)
    def _():
        kcat_s[last, t:t + WINDOW, :] = jnp.zeros((WINDOW, KV_DIM), BF16)
        vcat_s[last, t:t + WINDOW, :] = jnp.zeros((WINDOW, KV_DIM), BF16)
        ucarry_s[last] = jnp.zeros((8, CONV_DIM), F32)

    a_idx = lax.broadcasted_iota(jnp.int32, (WINDOW, 2 * WINDOW), 0)
    c_idx = lax.broadcasted_iota(jnp.int32, (WINDOW, 2 * WINDOW), 1)
    lane_lo = lax.broadcasted_iota(jnp.int32, (WINDOW, LANES), 1) < HEAD_DIM
    row = lax.broadcasted_iota(jnp.int32, (t, CONV_DIM), 0)
    n_pair = N_HEADS // 2
    cw = convw_ref[...]

    for h in range(N_SUB):
        src = (h - 1) % N_SUB
        rows = pl.ds(h * t, t)
        x = x_ref[0, rows, :]
        xn = _rms(x, g1_ref[...]).astype(BF16)

        kcat_s[h, 0:WINDOW, :] = kcat_s[src, t:t + WINDOW, :]
        vcat_s[h, 0:WINDOW, :] = vcat_s[src, t:t + WINDOW, :]
        q_s[h] = (_dot(xn, win_ref[:, 0:OFF_K]) * (HEAD_DIM ** -0.5)).astype(BF16)
        k = _dot(xn, win_ref[:, OFF_K:OFF_V])
        v = _dot(xn, win_ref[:, OFF_V:OFF_B])
        kcat_s[h, WINDOW:, :] = k.astype(BF16)
        vcat_s[h, WINDOW:, :] = v.astype(BF16)
        if h == last:
            knew_ref[0] = k[t - WINDOW:, :]
            vnew_ref[0] = v[t - WINDOW:, :]

        for j in range(t // WINDOW):
            r0 = j * WINDOW
            kk = kcat_s[h, r0:r0 + 2 * WINDOW, :]
            vv = vcat_s[h, r0:r0 + 2 * WINDOW, :]
            c_min = jnp.where(i == 0, WINDOW, 0) if (h == 0 and j == 0) else 0
            valid = (c_idx >= jnp.maximum(a_idx, c_min)) & (c_idx <= a_idx + WINDOW)
            qcols = [q_s[h, r0:r0 + WINDOW, m * LANES:(m + 1) * LANES] for m in range(n_pair)]
            outs = []
            for half in range(2):
                keep = lane_lo if half == 0 else jnp.logical_not(lane_lo)
                qg = jnp.concatenate([jnp.where(keep, qc, jnp.zeros_like(qc)) for qc in qcols], axis=0)
                s = lax.dot_general(qg, kk, (((1,), (1,)), ((), ())), preferred_element_type=F32)
                es, dens = [], []
                for m in range(n_pair):
                    sm = jnp.where(valid, s[m * WINDOW:(m + 1) * WINDOW], NEG_INF)
                    sink = sinks_ref[m + half * n_pair]
                    mx = jnp.maximum(jnp.max(sm, axis=-1, keepdims=True), sink)
                    e = jnp.exp(sm - mx)
                    dens.append(jnp.sum(e, axis=-1, keepdims=True) + jnp.exp(sink - mx))
                    es.append(e.astype(BF16))
                o = _dot(jnp.concatenate(es, axis=0), vv)
                outs.append([o[m * WINDOW:(m + 1) * WINDOW] / dens[m] for m in range(n_pair)])
            for m in range(n_pair):
                attn_s[h, r0:r0 + WINDOW, m * LANES:(m + 1) * LANES] = jnp.where(lane_lo, outs[0][m], outs[1][m])

        gate_b = _dot(xn, win_ref[:, OFF_B:OFF_C])
        u = _dot(xn, win_ref[:, OFF_C:OFF_H]) * _dot(xn, win_ref[:, OFF_H:IN_PROJ_DIM])
        prev = ucarry_s[src]
        u1 = jnp.where(row == 0, prev[7:8, :], pltpu.roll(u, 1, axis=0))
        u2 = jnp.where(row == 0, prev[6:7, :], jnp.where(row == 1, prev[7:8, :], pltpu.roll(u, 2, axis=0)))
        z = cw[0:1, :] * u2 + cw[1:2, :] * u1 + cw[2:3, :] * u
        ucarry_s[h] = u[t - 8:, :]
        if h == last:
            unew_ref[0] = u[t - (CONV_WIDTH - 1):, :]

        x1, xn2_bf, gates = _mix_tail(x, attn_s[h], gate_b * z, gattn_ref[...], gconv_ref[...],
                                      wout_ref[...], g2_ref[...], wr_ref[...], br_ref[...])
        x1_ref[0, rows, :] = x1
        xn2_ref[0, rows, :] = xn2_bf
        gates_ref[0, rows, :] = gates


def _layer_prompt(x, sinks, g1, win, convw, gattn, gconv, wout, g2, wr, br):
    nb, seq, _ = x.shape
    t = T_LAYER
    ts = T_LAYER // N_SUB
    const = lambda b, i, s: (0, 0)
    tile = lambda b, i, s: (b, i, 0)
    per_seq = lambda b, i, s: (b, 0, 0)
    resident = pl.Buffered(1)
    grid_spec = pltpu.PrefetchScalarGridSpec(
        num_scalar_prefetch=1,
        grid=(nb, seq // t),
        in_specs=[
            pl.BlockSpec((1, t, D_MODEL), tile),
            pl.BlockSpec((1, D_MODEL), const),
            pl.BlockSpec((D_MODEL, IN_PROJ_DIM), const, pipeline_mode=resident),
            pl.BlockSpec((CONV_WIDTH, CONV_DIM), const),
            pl.BlockSpec((1, ATTN_DIM), const),
            pl.BlockSpec((1, CONV_DIM), const),
            pl.BlockSpec((D_MODEL, D_MODEL), const, pipeline_mode=resident),
            pl.BlockSpec((1, D_MODEL), const),
            pl.BlockSpec((D_MODEL, 2 * ROUTER_LANES), const, pipeline_mode=resident),
            pl.BlockSpec((1, ROUTER_LANES), const),
        ],
        out_specs=[
            pl.BlockSpec((1, t, D_MODEL), tile),
            pl.BlockSpec((1, t, D_MODEL), tile),
            pl.BlockSpec((1, t, ROUTER_LANES), tile),
            pl.BlockSpec((1, WINDOW, KV_DIM), per_seq),
            pl.BlockSpec((1, WINDOW, KV_DIM), per_seq),
            pl.BlockSpec((1, CONV_WIDTH - 1, CONV_DIM), per_seq),
        ],
        scratch_shapes=[
            pltpu.VMEM((N_SUB, ts, ATTN_DIM), BF16),
            pltpu.VMEM((N_SUB, ts + WINDOW, KV_DIM), BF16),
            pltpu.VMEM((N_SUB, ts + WINDOW, KV_DIM), BF16),
            pltpu.VMEM((N_SUB, ts, ATTN_DIM), F32),
            pltpu.VMEM((N_SUB, 8, CONV_DIM), F32),
        ],
    )
    return pl.pallas_call(
        _layer_prompt_body,
        grid_spec=grid_spec,
        out_shape=[
            jax.ShapeDtypeStruct((nb, seq, D_MODEL), F32),
            jax.ShapeDtypeStruct((nb, seq, D_MODEL), BF16),
            jax.ShapeDtypeStruct((nb, seq, ROUTER_LANES), F32),
            jax.ShapeDtypeStruct((nb, WINDOW, KV_DIM), F32),
            jax.ShapeDtypeStruct((nb, WINDOW, KV_DIM), F32),
            jax.ShapeDtypeStruct((nb, CONV_WIDTH - 1, CONV_DIM), F32),
        ],
        compiler_params=pltpu.CompilerParams(
            dimension_semantics=("arbitrary", "arbitrary"), vmem_limit_bytes=VMEM_LIMIT),
        name="layer_prompt",
    )(sinks, x, g1, win, convw, gattn, gconv, wout, g2, wr, br)


def _moe_body(cap, xc, x1_ref, xn2_ref, gates_ref, wg_ref, wu_ref, wd_ref, gf_ref, y_ref,
              xs_s, gs_s, ys_s):
    t = x1_ref.shape[0]
    per_group = N_EXPERTS // N_GROUPS
    gates = gates_ref[...]
    lane = lax.broadcasted_iota(jnp.int32, (t, ROUTER_LANES), 1)
    gid = jnp.sum(jnp.where(lane == GID_LANE, gates, 0.0), axis=-1, keepdims=True)
    onehot = jnp.where((lane < N_GROUPS) & (lane.astype(F32) == gid), 1.0, 0.0)

    r_idx = lax.broadcasted_iota(jnp.int32, (t, t), 0)
    c_idx = lax.broadcasted_iota(jnp.int32, (t, t), 1)
    lower = jnp.where(c_idx < r_idx, 1.0, 0.0).astype(BF16)
    prefix = _dot(lower, onehot.astype(BF16))
    rank = jnp.sum(prefix * onehot, axis=-1, keepdims=True)
    counts = jnp.sum(onehot, axis=0, keepdims=True)
    lane1 = lax.broadcasted_iota(jnp.int32, (1, ROUTER_LANES), 1)
    cnt = [jnp.sum(jnp.where(lane1 == g, counts, 0.0)).astype(jnp.int32) for g in range(N_GROUPS)]
    off = [jnp.int32(0)]
    for g in range(1, N_GROUPS):
        off.append(off[-1] + cnt[g - 1])

    pos_col = rank
    for g in range(1, N_GROUPS):
        pos_col = pos_col + jnp.where(gid == float(g), off[g].astype(F32), 0.0)
    pos_row = jnp.transpose(jnp.broadcast_to(pos_col, (t, LANES)))[0:1, :]
    perm = jnp.where(r_idx.astype(F32) == pos_row, 1.0, 0.0).astype(BF16)
    perm_t = jnp.where(c_idx.astype(F32) == pos_col, 1.0, 0.0).astype(BF16)

    xs_s[0:t, :] = _dot(perm, xn2_ref[...]).astype(BF16)
    p1 = gates.astype(BF16)
    r1 = gates - p1.astype(F32)
    p2 = r1.astype(BF16)
    p3 = (r1 - p2.astype(F32)).astype(BF16)
    gs3 = _dot(perm, jnp.concatenate([p1, p2, p3], axis=1))
    gs_s[0:t, :] = (gs3[:, :ROUTER_LANES] + gs3[:, ROUTER_LANES:2 * ROUTER_LANES]) + gs3[:, 2 * ROUTER_LANES:]
    pad = xs_s.shape[0] - t
    xs_s[t:, :] = jnp.zeros((pad, D_MODEL), BF16)
    gs_s[t:, :] = jnp.zeros((pad, ROUTER_LANES), F32)
    ys_s[...] = jnp.zeros(ys_s.shape, F32)

    def group_rows(g, r0, rows):
        xs = xs_s[pl.ds(r0, rows), :]
        gsc = gs_s[pl.ds(r0, rows), :]
        acts = []
        for k in range(per_group):
            e = g * per_group + k
            hg = _dot(xs, wg_ref[e])
            hu = _dot(xs, wu_ref[e])
            acts.append((hg / (1.0 + jnp.exp(-hg)) * hu * gsc[:, e:e + 1]).astype(BF16))
        ys_s[pl.ds(r0, rows), :] += _dot(jnp.concatenate(acts, axis=1), wd_ref[g])

    base = [(off[g] // ROW_ALIGN) * ROW_ALIGN for g in range(N_GROUPS)]
    for g in range(N_GROUPS):
        group_rows(g, pl.multiple_of(base[g], ROW_ALIGN), cap)
    for g in range(N_GROUPS):
        n_extra = jnp.maximum(0, (off[g] + cnt[g] - (base[g] + cap) + xc - 1) // xc)

        def extra(k, carry, g=g):
            group_rows(g, pl.multiple_of(base[g] + cap + k * xc, ROW_ALIGN), xc)
            return carry

        lax.fori_loop(0, n_extra, extra, 0)

    moe = _dot(perm_t, ys_s[0:t, :].astype(BF16))
    y_ref[...] = _rms(x1_ref[...] + moe, gf_ref[...])


def _moe(x1, xn2, route, wg, wu, wd, gf, t, cap, xc):
    n = x1.shape[0]
    tile = lambda i: (i, 0)
    whole = lambda i: (0, 0, 0)
    resident = pl.Buffered(1)
    rows = t + cap + xc
    hidden = (N_EXPERTS // N_GROUPS) * D_EXPERT
    return pl.pallas_call(
        functools.partial(_moe_body, cap, xc),
        grid=(n // t,),
        in_specs=[
            pl.BlockSpec((t, D_MODEL), tile),
            pl.BlockSpec((t, D_MODEL), tile),
            pl.BlockSpec((t, ROUTER_LANES), tile),
            pl.BlockSpec((N_EXPERTS, D_MODEL, D_EXPERT), whole, pipeline_mode=resident),
            pl.BlockSpec((N_EXPERTS, D_MODEL, D_EXPERT), whole, pipeline_mode=resident),
            pl.BlockSpec((N_GROUPS, hidden, D_MODEL), whole, pipeline_mode=resident),
            pl.BlockSpec((1, D_MODEL), lambda i: (0, 0)),
        ],
        out_specs=pl.BlockSpec((t, D_MODEL), tile),
        out_shape=jax.ShapeDtypeStruct((n, D_MODEL), F32),
        scratch_shapes=[
            pltpu.VMEM((rows, D_MODEL), BF16),
            pltpu.VMEM((rows, ROUTER_LANES), F32),
            pltpu.VMEM((rows, D_MODEL), F32),
        ],
        compiler_params=pltpu.CompilerParams(
            dimension_semantics=("arbitrary",), vmem_limit_bytes=MOE_VMEM_LIMIT),
        name="moe",
    )(x1, xn2, route, wg, wu, wd, gf)


def _proj_sample_body(x_ref, g1_ref, win_ref, proj_ref):
    xn = _rms(x_ref[...], g1_ref[...]).astype(BF16)
    proj_ref[...] = _dot(xn, win_ref[...])


def _proj_sample(x, g1, win):
    n = x.shape[0]
    return pl.pallas_call(
        _proj_sample_body,
        out_shape=jax.ShapeDtypeStruct((n, IN_PROJ_DIM), F32),
        compiler_params=pltpu.CompilerParams(vmem_limit_bytes=VMEM_LIMIT),
        name="proj_sample",
    )(x, g1, win)


def _attn_sample_body(q8_ref, knew_ref, vnew_ref, khist_ref, vhist_ref, sinks_ref,
                      out8_ref, kout_ref, vout_ref):
    q8 = q8_ref[...]
    kh = khist_ref[...]
    vh = vhist_ref[...]
    knew = knew_ref[...]
    vnew = vnew_ref[...]
    s = jnp.einsum('bhj,bcj->bhc', q8, kh.astype(BF16), preferred_element_type=F32)
    s_new = jnp.sum(q8.astype(F32) * knew, axis=-1, keepdims=True)
    sink = sinks_ref[...][None]
    mx = jnp.maximum(jnp.maximum(jnp.max(s, axis=-1, keepdims=True), s_new), sink)
    e = jnp.exp(s - mx)
    e_new = jnp.exp(s_new - mx)
    den = jnp.sum(e, axis=-1, keepdims=True) + e_new + jnp.exp(sink - mx)
    o = jnp.einsum('bhc,bcj->bhj', e.astype(BF16), vh.astype(BF16), preferred_element_type=F32)
    out8_ref[...] = (o + e_new * vnew) / den
    kout_ref[:, 0:WINDOW - 1, :] = khist_ref[:, 1:WINDOW, :]
    kout_ref[:, WINDOW - 1:WINDOW, :] = knew
    vout_ref[:, 0:WINDOW - 1, :] = vhist_ref[:, 1:WINDOW, :]
    vout_ref[:, WINDOW - 1:WINDOW, :] = vnew


def _attn_sample(q8, knew, vnew, khist, vhist, sinks_col):
    n = q8.shape[0]
    c = DEC_CHUNK
    blk3 = lambda i: (i, 0, 0)
    return pl.pallas_call(
        _attn_sample_body,
        grid=(n // c,),
        in_specs=[
            pl.BlockSpec((c, N_HEADS, LANES), blk3),
            pl.BlockSpec((c, 1, KV_DIM), blk3),
            pl.BlockSpec((c, 1, KV_DIM), blk3),
            pl.BlockSpec((c, WINDOW, KV_DIM), blk3),
            pl.BlockSpec((c, WINDOW, KV_DIM), blk3),
            pl.BlockSpec((N_HEADS, 1), lambda i: (0, 0)),
        ],
        out_specs=[
            pl.BlockSpec((c, N_HEADS, LANES), blk3),
            pl.BlockSpec((c, WINDOW, KV_DIM), blk3),
            pl.BlockSpec((c, WINDOW, KV_DIM), blk3),
        ],
        out_shape=[
            jax.ShapeDtypeStruct((n, N_HEADS, LANES), F32),
            jax.ShapeDtypeStruct((n, WINDOW, KV_DIM), F32),
            jax.ShapeDtypeStruct((n, WINDOW, KV_DIM), F32),
        ],
        compiler_params=pltpu.CompilerParams(
            dimension_semantics=("arbitrary",), vmem_limit_bytes=VMEM_LIMIT),
        name="attn_sample",
    )(q8, knew, vnew, khist, vhist, sinks_col)


def _tail_sample_body(x_ref, attn_ref, proj_ref, uprev2_ref, uprev1_ref, convw_ref, gattn_ref,
                      gconv_ref, wout_ref, g2_ref, wr_ref, br_ref,
                      x1_ref, xn2_ref, gates_ref, u_ref):
    gate_b = proj_ref[:, OFF_B:OFF_C]
    u = proj_ref[:, OFF_C:OFF_H] * proj_ref[:, OFF_H:IN_PROJ_DIM]
    cw = convw_ref[...]
    z = cw[0:1, :] * uprev2_ref[...] + cw[1:2, :] * uprev1_ref[...] + cw[2:3, :] * u
    u_ref[...] = u
    x1, xn2_bf, gates = _mix_tail(x_ref[...], attn_ref[...], gate_b * z, gattn_ref[...], gconv_ref[...],
                                  wout_ref[...], g2_ref[...], wr_ref[...], br_ref[...])
    x1_ref[...] = x1
    xn2_ref[...] = xn2_bf
    gates_ref[...] = gates


def _tail_sample(x, attn, proj, uprev2, uprev1, convw, gattn, gconv, wout, g2, wr, br):
    n = x.shape[0]
    return pl.pallas_call(
        _tail_sample_body,
        out_shape=[
            jax.ShapeDtypeStruct((n, D_MODEL), F32),
            jax.ShapeDtypeStruct((n, D_MODEL), BF16),
            jax.ShapeDtypeStruct((n, ROUTER_LANES), F32),
            jax.ShapeDtypeStruct((n, CONV_DIM), F32),
        ],
        compiler_params=pltpu.CompilerParams(vmem_limit_bytes=VMEM_LIMIT),
        name="tail_sample",
    )(x, attn, proj, uprev2, uprev1, convw, gattn, gconv, wout, g2, wr, br)


def _pair_heads(w, axis):
    shape = w.shape
    n_pair = N_HEADS // 2
    split = shape[:axis] + (2, n_pair, HEAD_DIM) + shape[axis + 1:]
    return jnp.swapaxes(w.reshape(split), axis, axis + 1).reshape(shape)


def kernel(x_prompt, x_sample, cache_k_win, cache_v_win, state_conv, norm1_g, w_in, conv_w, sinks,
           attn_out_g, conv_out_g, w_out, norm2_g, w_group, b_group, w_fine, b_fine, w_gate, w_up,
           w_down, final_g):
    nb, seq, _ = x_prompt.shape
    nd = x_sample.shape[0]

    w_in0 = w_in[0].astype(BF16)
    win = jnp.concatenate([_pair_heads(w_in0[:, :ATTN_DIM], 1), w_in0[:, ATTN_DIM:]], axis=1)
    w_out0 = w_out[0].astype(BF16)
    wout = jnp.concatenate([_pair_heads(w_out0[:ATTN_DIM], 0), w_out0[ATTN_DIM:]], axis=0)
    gattn = _pair_heads(attn_out_g[0], 0)[None]
    gconv = conv_out_g[0][None]
    g1 = norm1_g[0][None]
    g2 = norm2_g[0][None]
    gf = final_g[None]
    convw = conv_w[0]
    sinks0 = sinks[0]
    pad = ROUTER_LANES - N_EXPERTS - N_GROUPS
    wr32 = jnp.concatenate([w_fine[0], w_group[0], jnp.zeros((D_MODEL, pad), F32)], axis=1)
    wr_hi = wr32.astype(BF16)
    wr = jnp.concatenate([wr_hi, (wr32 - wr_hi.astype(F32)).astype(BF16)], axis=1)
    br = jnp.concatenate([b_fine[0], b_group[0], jnp.zeros((pad,), F32)])[None]
    wg = w_gate[0].astype(BF16)
    wu = w_up[0].astype(BF16)
    wd = w_down[0].astype(BF16).reshape(N_GROUPS, (N_EXPERTS // N_GROUPS) * D_EXPERT, D_MODEL)

    x1p, xn2p, gatesp, kp, vp, up = _layer_prompt(x_prompt, sinks0, g1, win, convw, gattn, gconv,
                                                  wout, g2, wr, br)
    n_p = nb * seq
    y_prompt = _moe(x1p.reshape(n_p, D_MODEL), xn2p.reshape(n_p, D_MODEL),
                    gatesp.reshape(n_p, ROUTER_LANES), wg, wu, wd, gf, T_MOE, MOE_CAP, MOE_XC).reshape(nb, seq, D_MODEL)

    xs = x_sample.reshape(nd, D_MODEL)
    proj = _proj_sample(xs, g1, win)
    lane = jnp.arange(LANES)
    qp = (proj[:, :ATTN_DIM] * (HEAD_DIM ** -0.5)).reshape(nd, N_HEADS // 2, LANES)
    q8 = jnp.concatenate([jnp.where(lane < HEAD_DIM, qp, 0.0), jnp.where(lane >= HEAD_DIM, qp, 0.0)],
                         axis=1).astype(BF16)
    knew = proj[:, OFF_K:OFF_V].reshape(nd, 1, KV_DIM)
    vnew = proj[:, OFF_V:OFF_B].reshape(nd, 1, KV_DIM)
    khist = cache_k_win[0].reshape(nd, WINDOW, KV_DIM)
    vhist = cache_v_win[0].reshape(nd, WINDOW, KV_DIM)
    out8, ks, vs = _attn_sample(q8, knew, vnew, khist, vhist, sinks0[:, None])
    attn_s = jnp.where(lane < HEAD_DIM, out8[:, :N_HEADS // 2], out8[:, N_HEADS // 2:]).reshape(nd, ATTN_DIM)
    st = state_conv[0]
    x1s, xn2s, gatess, us = _tail_sample(xs, attn_s, proj, st[:, 0], st[:, 1], convw, gattn, gconv,
                                         wout, g2, wr, br)
    y_sample = _moe(x1s, xn2s, gatess, wg, wu, wd, gf, nd, MOE_CAP_DEC, MOE_XC_DEC).reshape(nd, 1, D_MODEL)

    n_kv = KV_DIM // HEAD_DIM
    return (y_prompt, y_sample,
            kp.reshape(1, nb, WINDOW, n_kv, HEAD_DIM), vp.reshape(1, nb, WINDOW, n_kv, HEAD_DIM),
            up[None],
            ks.reshape(1, nd, WINDOW, n_kv, HEAD_DIM), vs.reshape(1, nd, WINDOW, n_kv, HEAD_DIM),
            jnp.stack([st[:, 1], us], axis=1)[None])
```

```python
import functools

import jax
import jax.numpy as jnp
from jax import lax
from jax.experimental import pallas as pl
from jax.experimental.pallas import tpu as pltpu

D_MODEL = 1024
HEAD_DIM = 64
N_HEADS = 8
ATTN_DIM = 512
KV_DIM = 128
WINDOW = 128
CONV_DIM = 512
CONV_WIDTH = 3
OFF_K = 512
OFF_V = 640
OFF_B = 768
OFF_C = 1280
OFF_H = 1792
IN_PROJ_DIM = 2304
N_GROUPS = 4
N_EXPERTS = 16
D_EXPERT = 256
RMS_EPS = 1e-5
NEG_INF = -1e30

LANES = 128
ROW_ALIGN = 16
ROUTER_LANES = 128
COARSE_OFF = N_EXPERTS
T_LAYER = 1024
N_SUB = 2
GID_LANE = N_EXPERTS
T_MOE = 512
MOE_CAP, MOE_XC = 160, 64
MOE_CAP_DEC, MOE_XC_DEC = 64, 32
DEC_CHUNK = 32
VMEM_LIMIT = 48 * 1024 * 1024
MOE_VMEM_LIMIT = 56 * 1024 * 1024

BF16 = jnp.bfloat16
F32 = jnp.float32


def _dot(a, b):
    return jnp.dot(a, b, preferred_element_type=F32)


def _rms(x, g):
    ms = jnp.mean(x * x, axis=-1, keepdims=True)
    return x * lax.rsqrt(ms + RMS_EPS) * g


def _group_norm64(y, g):
    rows, cols = y.shape
    lo = lax.broadcasted_iota(jnp.int32, (rows, LANES), 1) < HEAD_DIM
    outs = []
    for c in range(cols // LANES):
        blk = y[:, c * LANES:(c + 1) * LANES]
        sq = blk * blk
        s_lo = jnp.sum(jnp.where(lo, sq, 0.0), axis=-1, keepdims=True)
        s_hi = jnp.sum(jnp.where(lo, 0.0, sq), axis=-1, keepdims=True)
        ms = jnp.where(lo, s_lo, s_hi) * (1.0 / HEAD_DIM)
        outs.append(blk * lax.rsqrt(ms + RMS_EPS))
    return jnp.concatenate(outs, axis=-1) * g


def _router(xn2, wr, br):
    hi = xn2.astype(BF16)
    lo = (xn2 - hi.astype(F32)).astype(BF16)
    both = _dot(hi, wr)
    logits = both[:, :ROUTER_LANES] + both[:, ROUTER_LANES:] + _dot(lo, wr[:, :ROUTER_LANES]) + br
    rows = logits.shape[0]
    lane = lax.broadcasted_iota(jnp.int32, (rows, ROUTER_LANES), 1)
    lanef = lane.astype(F32)
    big = float(ROUTER_LANES)
    coarse = (lane >= COARSE_OFF) & (lane < COARSE_OFF + N_GROUPS)
    cm = jnp.max(jnp.where(coarse, logits, -jnp.inf), axis=-1, keepdims=True)
    g_sel = jnp.min(jnp.where(coarse & (logits == cm), lanef - COARSE_OFF, big),
                    axis=-1, keepdims=True)
    zc = jnp.sum(jnp.where(coarse, jnp.exp(logits - cm), 0.0), axis=-1, keepdims=True)
    p_sel = 1.0 / zc
    per_group = N_EXPERTS // N_GROUPS
    fine = (lane < N_EXPERTS) & ((lane // per_group).astype(F32) == g_sel)
    f1 = jnp.max(jnp.where(fine, logits, -jnp.inf), axis=-1, keepdims=True)
    i1 = jnp.min(jnp.where(fine & (logits == f1), lanef, big), axis=-1, keepdims=True)
    rest = fine & (lanef != i1)
    f2 = jnp.max(jnp.where(rest, logits, -jnp.inf), axis=-1, keepdims=True)
    i2 = jnp.min(jnp.where(rest & (logits == f2), lanef, big), axis=-1, keepdims=True)
    e2 = jnp.exp(f2 - f1)
    t1 = 1.0 / (1.0 + e2)
    t2 = e2 * t1
    gates = jnp.where(lanef == i1, p_sel * t1, jnp.where(lanef == i2, p_sel * t2, 0.0))
    gates = jnp.where(lane == GID_LANE, g_sel, gates)
    return hi, gates


def _mix_tail(x, attn, conv_out, gattn, gconv, wout, g2, wr, br):
    mixed = jnp.concatenate([_group_norm64(attn, gattn), _group_norm64(conv_out, gconv)],
                            axis=-1).astype(BF16)
    x1 = x + _dot(mixed, wout)
    xn2_bf, gates = _router(_rms(x1, g2), wr, br)
    return x1, xn2_bf, gates


def _layer_prompt_body(sinks_ref, x_ref, g1_ref, win_ref, convw_ref, gattn_ref, gconv_ref,
                       wout_ref, g2_ref, wr_ref, br_ref, wg32_ref, wu32_ref, wd32_ref,
                       x1_ref, xn2_ref, gates_ref, knew_ref, vnew_ref, unew_ref,
                       wg16_ref, wu16_ref, wd16_ref,
                       q_s, kcat_s, vcat_s, attn_s, ucarry_s):
    i = pl.program_id(1)
    wg16_ref[...] = wg32_ref[...].astype(BF16)
    wu16_ref[...] = wu32_ref[...].astype(BF16)
    wd16_ref[...] = wd32_ref[...].astype(BF16)
    t = T_LAYER // N_SUB
    last = N_SUB - 1

    @pl.when(i == 0)
    def _():
        kcat_s[last, t:t + WINDOW, :] = jnp.zeros((WINDOW, KV_DIM), BF16)
        vcat_s[last, t:t + WINDOW, :] = jnp.zeros((WINDOW, KV_DIM), BF16)
        ucarry_s[last] = jnp.zeros((8, CONV_DIM), F32)

    a_idx = lax.broadcasted_iota(jnp.int32, (WINDOW, 2 * WINDOW), 0)
    c_idx = lax.broadcasted_iota(jnp.int32, (WINDOW, 2 * WINDOW), 1)
    lane_lo = lax.broadcasted_iota(jnp.int32, (WINDOW, LANES), 1) < HEAD_DIM
    row = lax.broadcasted_iota(jnp.int32, (t, CONV_DIM), 0)
    n_pair = N_HEADS // 2
    cw = convw_ref[...]

    for h in range(N_SUB):
        src = (h - 1) % N_SUB
        rows = pl.ds(h * t, t)
        x = x_ref[0, rows, :]
        xn = _rms(x, g1_ref[...]).astype(BF16)

        kcat_s[h, 0:WINDOW, :] = kcat_s[src, t:t + WINDOW, :]
        vcat_s[h, 0:WINDOW, :] = vcat_s[src, t:t + WINDOW, :]
        q_s[h] = (_dot(xn, win_ref[:, 0:OFF_K]) * (HEAD_DIM ** -0.5)).astype(BF16)
        k = _dot(xn, win_ref[:, OFF_K:OFF_V])
        v = _dot(xn, win_ref[:, OFF_V:OFF_B])
        kcat_s[h, WINDOW:, :] = k.astype(BF16)
        vcat_s[h, WINDOW:, :] = v.astype(BF16)
        if h == last:
            knew_ref[0] = k[t - WINDOW:, :]
            vnew_ref[0] = v[t - WINDOW:, :]

        for j in range(t // WINDOW):
            r0 = j * WINDOW
            kk = kcat_s[h, r0:r0 + 2 * WINDOW, :]
            vv = vcat_s[h, r0:r0 + 2 * WINDOW, :]
            c_min = jnp.where(i == 0, WINDOW, 0) if (h == 0 and j == 0) else 0
            valid = (c_idx >= jnp.maximum(a_idx, c_min)) & (c_idx <= a_idx + WINDOW)
            qcols = [q_s[h, r0:r0 + WINDOW, m * LANES:(m + 1) * LANES] for m in range(n_pair)]
            outs = []
            for half in range(2):
                keep = lane_lo if half == 0 else jnp.logical_not(lane_lo)
                qg = jnp.concatenate([jnp.where(keep, qc, jnp.zeros_like(qc)) for qc in qcols], axis=0)
                s = lax.dot_general(qg, kk, (((1,), (1,)), ((), ())), preferred_element_type=F32)
                es, dens = [], []
                for m in range(n_pair):
                    sm = jnp.where(valid, s[m * WINDOW:(m + 1) * WINDOW], NEG_INF)
                    sink = sinks_ref[m + half * n_pair]
                    mx = jnp.maximum(jnp.max(sm, axis=-1, keepdims=True), sink)
                    e = jnp.exp(sm - mx)
                    dens.append(jnp.sum(e, axis=-1, keepdims=True) + jnp.exp(sink - mx))
                    es.append(e.astype(BF16))
                o = _dot(jnp.concatenate(es, axis=0), vv)
                outs.append([o[m * WINDOW:(m + 1) * WINDOW] / dens[m] for m in range(n_pair)])
            for m in range(n_pair):
                attn_s[h, r0:r0 + WINDOW, m * LANES:(m + 1) * LANES] = jnp.where(lane_lo, outs[0][m], outs[1][m])

        gate_b = _dot(xn, win_ref[:, OFF_B:OFF_C])
        u = _dot(xn, win_ref[:, OFF_C:OFF_H]) * _dot(xn, win_ref[:, OFF_H:IN_PROJ_DIM])
        prev = ucarry_s[src]
        u1 = jnp.where(row == 0, prev[7:8, :], pltpu.roll(u, 1, axis=0))
        u2 = jnp.where(row == 0, prev[6:7, :], jnp.where(row == 1, prev[7:8, :], pltpu.roll(u, 2, axis=0)))
        z = cw[0:1, :] * u2 + cw[1:2, :] * u1 + cw[2:3, :] * u
        ucarry_s[h] = u[t - 8:, :]
        if h == last:
            unew_ref[0] = u[t - (CONV_WIDTH - 1):, :]

        x1, xn2_bf, gates = _mix_tail(x, attn_s[h], gate_b * z, gattn_ref[...], gconv_ref[...],
                                      wout_ref[...], g2_ref[...], wr_ref[...], br_ref[...])
        x1_ref[0, rows, :] = x1
        xn2_ref[0, rows, :] = xn2_bf
        gates_ref[0, rows, :] = gates


def _layer_prompt(x, sinks, g1, win, convw, gattn, gconv, wout, g2, wr, br, wg32, wu32, wd32):
    nb, seq, _ = x.shape
    t = T_LAYER
    ts = T_LAYER // N_SUB
    n_i = seq // t
    assert nb * n_i == N_EXPERTS, "one expert's weights are cast per grid step"
    const = lambda b, i, s: (0, 0)
    tile = lambda b, i, s: (b, i, 0)
    per_seq = lambda b, i, s: (b, 0, 0)
    per_step = lambda b, i, s: (b * n_i + i, 0, 0)
    resident = pl.Buffered(1)
    grid_spec = pltpu.PrefetchScalarGridSpec(
        num_scalar_prefetch=1,
        grid=(nb, seq // t),
        in_specs=[
            pl.BlockSpec((1, t, D_MODEL), tile),
            pl.BlockSpec((1, D_MODEL), const),
            pl.BlockSpec((D_MODEL, IN_PROJ_DIM), const, pipeline_mode=resident),
            pl.BlockSpec((CONV_WIDTH, CONV_DIM), const),
            pl.BlockSpec((1, ATTN_DIM), const),
            pl.BlockSpec((1, CONV_DIM), const),
            pl.BlockSpec((D_MODEL, D_MODEL), const, pipeline_mode=resident),
            pl.BlockSpec((1, D_MODEL), const),
            pl.BlockSpec((D_MODEL, 2 * ROUTER_LANES), const, pipeline_mode=resident),
            pl.BlockSpec((1, ROUTER_LANES), const),
            pl.BlockSpec((1, D_MODEL, D_EXPERT), per_step),
            pl.BlockSpec((1, D_MODEL, D_EXPERT), per_step),
            pl.BlockSpec((1, D_EXPERT, D_MODEL), per_step),
        ],
        out_specs=[
            pl.BlockSpec((1, t, D_MODEL), tile),
            pl.BlockSpec((1, t, D_MODEL), tile),
            pl.BlockSpec((1, t, ROUTER_LANES), tile),
            pl.BlockSpec((1, WINDOW, KV_DIM), per_seq),
            pl.BlockSpec((1, WINDOW, KV_DIM), per_seq),
            pl.BlockSpec((1, CONV_WIDTH - 1, CONV_DIM), per_seq),
            pl.BlockSpec((1, D_MODEL, D_EXPERT), per_step),
            pl.BlockSpec((1, D_MODEL, D_EXPERT), per_step),
            pl.BlockSpec((1, D_EXPERT, D_MODEL), per_step),
        ],
        scratch_shapes=[
            pltpu.VMEM((N_SUB, ts, ATTN_DIM), BF16),
            pltpu.VMEM((N_SUB, ts + WINDOW, KV_DIM), BF16),
            pltpu.VMEM((N_SUB, ts + WINDOW, KV_DIM), BF16),
            pltpu.VMEM((N_SUB, ts, ATTN_DIM), F32),
            pltpu.VMEM((N_SUB, 8, CONV_DIM), F32),
        ],
    )
    return pl.pallas_call(
        _layer_prompt_body,
        grid_spec=grid_spec,
        out_shape=[
            jax.ShapeDtypeStruct((nb, seq, D_MODEL), F32),
            jax.ShapeDtypeStruct((nb, seq, D_MODEL), BF16),
            jax.ShapeDtypeStruct((nb, seq, ROUTER_LANES), F32),
            jax.ShapeDtypeStruct((nb, WINDOW, KV_DIM), F32),
            jax.ShapeDtypeStruct((nb, WINDOW, KV_DIM), F32),
            jax.ShapeDtypeStruct((nb, CONV_WIDTH - 1, CONV_DIM), F32),
            jax.ShapeDtypeStruct((N_EXPERTS, D_MODEL, D_EXPERT), BF16),
            jax.ShapeDtypeStruct((N_EXPERTS, D_MODEL, D_EXPERT), BF16),
            jax.ShapeDtypeStruct((N_EXPERTS, D_EXPERT, D_MODEL), BF16),
        ],
        compiler_params=pltpu.CompilerParams(
            dimension_semantics=("arbitrary", "arbitrary"), vmem_limit_bytes=MOE_VMEM_LIMIT),
        name="layer_prompt",
    )(sinks, x, g1, win, convw, gattn, gconv, wout, g2, wr, br, wg32, wu32, wd32)


def _moe_body(cap, xc, x1_ref, xn2_ref, gates_ref, wg_ref, wu_ref, wd_ref, gf_ref, y_ref,
              xs_s, gs_s, ys_s):
    t = x1_ref.shape[0]
    per_group = N_EXPERTS // N_GROUPS
    gates = gates_ref[...]
    lane = lax.broadcasted_iota(jnp.int32, (t, ROUTER_LANES), 1)
    gid = jnp.sum(jnp.where(lane == GID_LANE, gates, 0.0), axis=-1, keepdims=True)
    onehot = jnp.where((lane < N_GROUPS) & (lane.astype(F32) == gid), 1.0, 0.0)

    r_idx = lax.broadcasted_iota(jnp.int32, (t, t), 0)
    c_idx = lax.broadcasted_iota(jnp.int32, (t, t), 1)
    lower = jnp.where(c_idx < r_idx, 1.0, 0.0).astype(BF16)
    prefix = _dot(lower, onehot.astype(BF16))
    rank = jnp.sum(prefix * onehot, axis=-1, keepdims=True)
    counts = jnp.sum(onehot, axis=0, keepdims=True)
    lane1 = lax.broadcasted_iota(jnp.int32, (1, ROUTER_LANES), 1)
    cnt = [jnp.sum(jnp.where(lane1 == g, counts, 0.0)).astype(jnp.int32) for g in range(N_GROUPS)]
    off = [jnp.int32(0)]
    for g in range(1, N_GROUPS):
        off.append(off[-1] + cnt[g - 1])

    pos_col = rank
    for g in range(1, N_GROUPS):
        pos_col = pos_col + jnp.where(gid == float(g), off[g].astype(F32), 0.0)
    pos_row = jnp.transpose(jnp.broadcast_to(pos_col, (t, LANES)))[0:1, :]
    perm = jnp.where(r_idx.astype(F32) == pos_row, 1.0, 0.0).astype(BF16)
    perm_t = jnp.where(c_idx.astype(F32) == pos_col, 1.0, 0.0).astype(BF16)

    xs_s[0:t, :] = _dot(perm, xn2_ref[...]).astype(BF16)
    p1 = gates.astype(BF16)
    r1 = gates - p1.astype(F32)
    p2 = r1.astype(BF16)
    p3 = (r1 - p2.astype(F32)).astype(BF16)
    gs3 = _dot(perm, jnp.concatenate([p1, p2, p3], axis=1))
    gs_s[0:t, :] = (gs3[:, :ROUTER_LANES] + gs3[:, ROUTER_LANES:2 * ROUTER_LANES]) + gs3[:, 2 * ROUTER_LANES:]
    pad = xs_s.shape[0] - t
    xs_s[t:, :] = jnp.zeros((pad, D_MODEL), BF16)
    gs_s[t:, :] = jnp.zeros((pad, ROUTER_LANES), F32)
    ys_s[...] = jnp.zeros(ys_s.shape, F32)

    def group_rows(g, r0, rows):
        xs = xs_s[pl.ds(r0, rows), :]
        gsc = gs_s[pl.ds(r0, rows), :]
        acts = []
        for k in range(per_group):
            e = g * per_group + k
            hg = _dot(xs, wg_ref[e])
            hu = _dot(xs, wu_ref[e])
            acts.append((hg / (1.0 + jnp.exp(-hg)) * hu * gsc[:, e:e + 1]).astype(BF16))
        ys_s[pl.ds(r0, rows), :] += _dot(jnp.concatenate(acts, axis=1), wd_ref[g])

    base = [(off[g] // ROW_ALIGN) * ROW_ALIGN for g in range(N_GROUPS)]
    for g in range(N_GROUPS):
        group_rows(g, pl.multiple_of(base[g], ROW_ALIGN), cap)
    for g in range(N_GROUPS):
        n_extra = jnp.maximum(0, (off[g] + cnt[g] - (base[g] + cap) + xc - 1) // xc)

        def extra(k, carry, g=g):
            group_rows(g, pl.multiple_of(base[g] + cap + k * xc, ROW_ALIGN), xc)
            return carry

        lax.fori_loop(0, n_extra, extra, 0)

    moe = _dot(perm_t, ys_s[0:t, :].astype(BF16))
    y_ref[...] = _rms(x1_ref[...] + moe, gf_ref[...])


def _moe(x1, xn2, route, wg, wu, wd, gf, t, cap, xc):
    n = x1.shape[0]
    tile = lambda i: (i, 0)
    whole = lambda i: (0, 0, 0)
    resident = pl.Buffered(1)
    rows = t + cap + xc
    hidden = (N_EXPERTS // N_GROUPS) * D_EXPERT
    return pl.pallas_call(
        functools.partial(_moe_body, cap, xc),
        grid=(n // t,),
        in_specs=[
            pl.BlockSpec((t, D_MODEL), tile),
            pl.BlockSpec((t, D_MODEL), tile),
            pl.BlockSpec((t, ROUTER_LANES), tile),
            pl.BlockSpec((N_EXPERTS, D_MODEL, D_EXPERT), whole, pipeline_mode=resident),
            pl.BlockSpec((N_EXPERTS, D_MODEL, D_EXPERT), whole, pipeline_mode=resident),
            pl.BlockSpec((N_GROUPS, hidden, D_MODEL), whole, pipeline_mode=resident),
            pl.BlockSpec((1, D_MODEL), lambda i: (0, 0)),
        ],
        out_specs=pl.BlockSpec((t, D_MODEL), tile),
        out_shape=jax.ShapeDtypeStruct((n, D_MODEL), F32),
        scratch_shapes=[
            pltpu.VMEM((rows, D_MODEL), BF16),
            pltpu.VMEM((rows, ROUTER_LANES), F32),
            pltpu.VMEM((rows, D_MODEL), F32),
        ],
        compiler_params=pltpu.CompilerParams(
            dimension_semantics=("arbitrary",), vmem_limit_bytes=MOE_VMEM_LIMIT),
        name="moe",
    )(x1, xn2, route, wg, wu, wd, gf)


def _proj_sample_body(x_ref, g1_ref, win_ref, proj_ref):
    xn = _rms(x_ref[...], g1_ref[...]).astype(BF16)
    proj_ref[...] = _dot(xn, win_ref[...])


def _proj_sample(x, g1, win):
    n = x.shape[0]
    return pl.pallas_call(
        _proj_sample_body,
        out_shape=jax.ShapeDtypeStruct((n, IN_PROJ_DIM), F32),
        compiler_params=pltpu.CompilerParams(vmem_limit_bytes=VMEM_LIMIT),
        name="proj_sample",
    )(x, g1, win)


def _attn_sample_body(q8_ref, knew_ref, vnew_ref, khist_ref, vhist_ref, sinks_ref,
                      out8_ref, kout_ref, vout_ref):
    q8 = q8_ref[...]
    kh = khist_ref[...]
    vh = vhist_ref[...]
    knew = knew_ref[...]
    vnew = vnew_ref[...]
    s = jnp.einsum('bhj,bcj->bhc', q8, kh.astype(BF16), preferred_element_type=F32)
    s_new = jnp.sum(q8.astype(F32) * knew, axis=-1, keepdims=True)
    sink = sinks_ref[...][None]
    mx = jnp.maximum(jnp.maximum(jnp.max(s, axis=-1, keepdims=True), s_new), sink)
    e = jnp.exp(s - mx)
    e_new = jnp.exp(s_new - mx)
    den = jnp.sum(e, axis=-1, keepdims=True) + e_new + jnp.exp(sink - mx)
    o = jnp.einsum('bhc,bcj->bhj', e.astype(BF16), vh.astype(BF16), preferred_element_type=F32)
    out8_ref[...] = (o + e_new * vnew) / den
    kout_ref[:, 0:WINDOW - 1, :] = khist_ref[:, 1:WINDOW, :]
    kout_ref[:, WINDOW - 1:WINDOW, :] = knew
    vout_ref[:, 0:WINDOW - 1, :] = vhist_ref[:, 1:WINDOW, :]
    vout_ref[:, WINDOW - 1:WINDOW, :] = vnew


def _attn_sample(q8, knew, vnew, khist, vhist, sinks_col):
    n = q8.shape[0]
    c = DEC_CHUNK
    blk3 = lambda i: (i, 0, 0)
    return pl.pallas_call(
        _attn_sample_body,
        grid=(n // c,),
        in_specs=[
            pl.BlockSpec((c, N_HEADS, LANES), blk3),
            pl.BlockSpec((c, 1, KV_DIM), blk3),
            pl.BlockSpec((c, 1, KV_DIM), blk3),
            pl.BlockSpec((c, WINDOW, KV_DIM), blk3),
            pl.BlockSpec((c, WINDOW, KV_DIM), blk3),
            pl.BlockSpec((N_HEADS, 1), lambda i: (0, 0)),
        ],
        out_specs=[
            pl.BlockSpec((c, N_HEADS, LANES), blk3),
            pl.BlockSpec((c, WINDOW, KV_DIM), blk3),
            pl.BlockSpec((c, WINDOW, KV_DIM), blk3),
        ],
        out_shape=[
            jax.ShapeDtypeStruct((n, N_HEADS, LANES), F32),
            jax.ShapeDtypeStruct((n, WINDOW, KV_DIM), F32),
            jax.ShapeDtypeStruct((n, WINDOW, KV_DIM), F32),
        ],
        compiler_params=pltpu.CompilerParams(
            dimension_semantics=("arbitrary",), vmem_limit_bytes=VMEM_LIMIT),
        name="attn_sample",
    )(q8, knew, vnew, khist, vhist, sinks_col)


def _tail_sample_body(x_ref, attn_ref, proj_ref, uprev2_ref, uprev1_ref, convw_ref, gattn_ref,
                      gconv_ref, wout_ref, g2_ref, wr_ref, br_ref,
                      x1_ref, xn2_ref, gates_ref, u_ref):
    gate_b = proj_ref[:, OFF_B:OFF_C]
    u = proj_ref[:, OFF_C:OFF_H] * proj_ref[:, OFF_H:IN_PROJ_DIM]
    cw = convw_ref[...]
    z = cw[0:1, :] * uprev2_ref[...] + cw[1:2, :] * uprev1_ref[...] + cw[2:3, :] * u
    u_ref[...] = u
    x1, xn2_bf, gates = _mix_tail(x_ref[...], attn_ref[...], gate_b * z, gattn_ref[...], gconv_ref[...],
                                  wout_ref[...], g2_ref[...], wr_ref[...], br_ref[...])
    x1_ref[...] = x1
    xn2_ref[...] = xn2_bf
    gates_ref[...] = gates


def _tail_sample(x, attn, proj, uprev2, uprev1, convw, gattn, gconv, wout, g2, wr, br):
    n = x.shape[0]
    return pl.pallas_call(
        _tail_sample_body,
        out_shape=[
            jax.ShapeDtypeStruct((n, D_MODEL), F32),
            jax.ShapeDtypeStruct((n, D_MODEL), BF16),
            jax.ShapeDtypeStruct((n, ROUTER_LANES), F32),
            jax.ShapeDtypeStruct((n, CONV_DIM), F32),
        ],
        compiler_params=pltpu.CompilerParams(vmem_limit_bytes=VMEM_LIMIT),
        name="tail_sample",
    )(x, attn, proj, uprev2, uprev1, convw, gattn, gconv, wout, g2, wr, br)


def _pair_heads(w, axis):
    shape = w.shape
    n_pair = N_HEADS // 2
    split = shape[:axis] + (2, n_pair, HEAD_DIM) + shape[axis + 1:]
    return jnp.swapaxes(w.reshape(split), axis, axis + 1).reshape(shape)


def kernel(x_prompt, x_sample, cache_k_win, cache_v_win, state_conv, norm1_g, w_in, conv_w, sinks,
           attn_out_g, conv_out_g, w_out, norm2_g, w_group, b_group, w_fine, b_fine, w_gate, w_up,
           w_down, final_g):
    nb, seq, _ = x_prompt.shape
    nd = x_sample.shape[0]

    w_in0 = w_in[0].astype(BF16)
    win = jnp.concatenate([_pair_heads(w_in0[:, :ATTN_DIM], 1), w_in0[:, ATTN_DIM:]], axis=1)
    w_out0 = w_out[0].astype(BF16)
    wout = jnp.concatenate([_pair_heads(w_out0[:ATTN_DIM], 0), w_out0[ATTN_DIM:]], axis=0)
    gattn = _pair_heads(attn_out_g[0], 0)[None]
    gconv = conv_out_g[0][None]
    g1 = norm1_g[0][None]
    g2 = norm2_g[0][None]
    gf = final_g[None]
    convw = conv_w[0]
    sinks0 = sinks[0]
    pad = ROUTER_LANES - N_EXPERTS - N_GROUPS
    wr32 = jnp.concatenate([w_fine[0], w_group[0], jnp.zeros((D_MODEL, pad), F32)], axis=1)
    wr_hi = wr32.astype(BF16)
    wr = jnp.concatenate([wr_hi, (wr32 - wr_hi.astype(F32)).astype(BF16)], axis=1)
    br = jnp.concatenate([b_fine[0], b_group[0], jnp.zeros((pad,), F32)])[None]

    x1p, xn2p, gatesp, kp, vp, up, wg, wu, wd = _layer_prompt(
        x_prompt, sinks0, g1, win, convw, gattn, gconv, wout, g2, wr, br, w_gate[0], w_up[0], w_down[0])
    wd = wd.reshape(N_GROUPS, (N_EXPERTS // N_GROUPS) * D_EXPERT, D_MODEL)
    n_p = nb * seq
    y_prompt = _moe(x1p.reshape(n_p, D_MODEL), xn2p.reshape(n_p, D_MODEL),
                    gatesp.reshape(n_p, ROUTER_LANES), wg, wu, wd, gf, T_MOE, MOE_CAP, MOE_XC).reshape(nb, seq, D_MODEL)

    xs = x_sample.reshape(nd, D_MODEL)
    proj = _proj_sample(xs, g1, win)
    lane = jnp.arange(LANES)
    qp = (proj[:, :ATTN_DIM] * (HEAD_DIM ** -0.5)).reshape(nd, N_HEADS // 2, LANES)
    q8 = jnp.concatenate([jnp.where(lane < HEAD_DIM, qp, 0.0), jnp.where(lane >= HEAD_DIM, qp, 0.0)],
                         axis=1).astype(BF16)
    knew = proj[:, OFF_K:OFF_V].reshape(nd, 1, KV_DIM)
    vnew = proj[:, OFF_V:OFF_B].reshape(nd, 1, KV_DIM)
    khist = cache_k_win[0].reshape(nd, WINDOW, KV_DIM)
    vhist = cache_v_win[0].reshape(nd, WINDOW, KV_DIM)
    out8, ks, vs = _attn_sample(q8, knew, vnew, khist, vhist, sinks0[:, None])
    attn_s = jnp.where(lane < HEAD_DIM, out8[:, :N_HEADS // 2], out8[:, N_HEADS // 2:]).reshape(nd, ATTN_DIM)
    st = state_conv[0]
    x1s, xn2s, gatess, us = _tail_sample(xs, attn_s, proj, st[:, 0], st[:, 1], convw, gattn, gconv,
                                         wout, g2, wr, br)
    y_sample = _moe(x1s, xn2s, gatess, wg, wu, wd, gf, nd, MOE_CAP_DEC, MOE_XC_DEC).reshape(nd, 1, D_MODEL)

    n_kv = KV_DIM // HEAD_DIM
    return (y_prompt, y_sample,
            kp.reshape(1, nb, WINDOW, n_kv, HEAD_DIM), vp.reshape(1, nb, WINDOW, n_kv, HEAD_DIM),
            up[None],
            ks.reshape(1, nd, WINDOW, n_kv, HEAD_DIM), vs.reshape(1, nd, WINDOW, n_kv, HEAD_DIM),
            jnp.stack([st[:, 1], us], axis=1)[None])
```

```python
import functools

import jax
import jax.numpy as jnp
from jax import lax
from jax.experimental import pallas as pl
from jax.experimental.pallas import tpu as pltpu

D_MODEL = 1024
HEAD_DIM = 64
N_HEADS = 8
ATTN_DIM = 512
KV_DIM = 128
WINDOW = 128
CONV_DIM = 512
CONV_WIDTH = 3
OFF_K = 512
OFF_V = 640
OFF_B = 768
OFF_C = 1280
OFF_H = 1792
IN_PROJ_DIM = 2304
N_GROUPS = 4
N_EXPERTS = 16
D_EXPERT = 256
RMS_EPS = 1e-5
NEG_INF = -1e30

LANES = 128
ROW_ALIGN = 16
ROUTER_LANES = 128
COARSE_OFF = N_EXPERTS
T_LAYER = 1024
N_SUB = 2
GID_LANE = N_EXPERTS
T_MOE = 512
MOE_CAP, MOE_XC = 176, 64
MOE_CAP_DEC, MOE_XC_DEC = 64, 32
DEC_CHUNK = 32
VMEM_LIMIT = 48 * 1024 * 1024
MOE_VMEM_LIMIT = 56 * 1024 * 1024

BF16 = jnp.bfloat16
F32 = jnp.float32


def _dot(a, b):
    return jnp.dot(a, b, preferred_element_type=F32)


def _rms(x, g):
    ms = jnp.mean(x * x, axis=-1, keepdims=True)
    return x * lax.rsqrt(ms + RMS_EPS) * g


def _group_norm64(y, g):
    rows, cols = y.shape
    lo = lax.broadcasted_iota(jnp.int32, (rows, LANES), 1) < HEAD_DIM
    outs = []
    for c in range(cols // LANES):
        blk = y[:, c * LANES:(c + 1) * LANES]
        sq = blk * blk
        s_lo = jnp.sum(jnp.where(lo, sq, 0.0), axis=-1, keepdims=True)
        s_hi = jnp.sum(jnp.where(lo, 0.0, sq), axis=-1, keepdims=True)
        ms = jnp.where(lo, s_lo, s_hi) * (1.0 / HEAD_DIM)
        outs.append(blk * lax.rsqrt(ms + RMS_EPS))
    return jnp.concatenate(outs, axis=-1) * g


def _router(xn2, wr, br):
    hi = xn2.astype(BF16)
    lo = (xn2 - hi.astype(F32)).astype(BF16)
    both = _dot(hi, wr)
    logits = both[:, :ROUTER_LANES] + both[:, ROUTER_LANES:] + _dot(lo, wr[:, :ROUTER_LANES]) + br
    rows = logits.shape[0]
    lane = lax.broadcasted_iota(jnp.int32, (rows, ROUTER_LANES), 1)
    lanef = lane.astype(F32)
    big = float(ROUTER_LANES)
    coarse = (lane >= COARSE_OFF) & (lane < COARSE_OFF + N_GROUPS)
    cm = jnp.max(jnp.where(coarse, logits, -jnp.inf), axis=-1, keepdims=True)
    g_sel = jnp.min(jnp.where(coarse & (logits == cm), lanef - COARSE_OFF, big),
                    axis=-1, keepdims=True)
    zc = jnp.sum(jnp.where(coarse, jnp.exp(logits - cm), 0.0), axis=-1, keepdims=True)
    p_sel = 1.0 / zc
    per_group = N_EXPERTS // N_GROUPS
    fine = (lane < N_EXPERTS) & ((lane // per_group).astype(F32) == g_sel)
    f1 = jnp.max(jnp.where(fine, logits, -jnp.inf), axis=-1, keepdims=True)
    i1 = jnp.min(jnp.where(fine & (logits == f1), lanef, big), axis=-1, keepdims=True)
    rest = fine & (lanef != i1)
    f2 = jnp.max(jnp.where(rest, logits, -jnp.inf), axis=-1, keepdims=True)
    i2 = jnp.min(jnp.where(rest & (logits == f2), lanef, big), axis=-1, keepdims=True)
    e2 = jnp.exp(f2 - f1)
    t1 = 1.0 / (1.0 + e2)
    t2 = e2 * t1
    gates = jnp.where(lanef == i1, p_sel * t1, jnp.where(lanef == i2, p_sel * t2, 0.0))
    gates = jnp.where(lane == GID_LANE, g_sel, gates)
    return hi, gates


def _mix_tail(x, attn, conv_out, gattn, gconv, wout, g2, wr, br):
    mixed = jnp.concatenate([_group_norm64(attn, gattn), _group_norm64(conv_out, gconv)],
                            axis=-1).astype(BF16)
    x1 = x + _dot(mixed, wout)
    xn2_bf, gates = _router(_rms(x1, g2), wr, br)
    return x1, xn2_bf, gates


def _layer_prompt_body(sinks_ref, x_ref, g1_ref, win_ref, convw_ref, gattn_ref, gconv_ref,
                       wout_ref, g2_ref, wr_ref, br_ref, wg32_ref, wu32_ref, wd32_ref,
                       x1_ref, xn2_ref, gates_ref, knew_ref, vnew_ref, unew_ref,
                       wg16_ref, wu16_ref, wd16_ref,
                       q_s, kcat_s, vcat_s, attn_s, ucarry_s):
    i = pl.program_id(1)
    wg16_ref[...] = wg32_ref[...].astype(BF16)
    wu16_ref[...] = wu32_ref[...].astype(BF16)
    wd16_ref[...] = wd32_ref[...].astype(BF16)
    t = T_LAYER // N_SUB
    last = N_SUB - 1

    @pl.when(i == 0)
    def _():
        kcat_s[last, t:t + WINDOW, :] = jnp.zeros((WINDOW, KV_DIM), BF16)
        vcat_s[last, t:t + WINDOW, :] = jnp.zeros((WINDOW, KV_DIM), BF16)
        ucarry_s[last] = jnp.zeros((8, CONV_DIM), F32)

    a_idx = lax.broadcasted_iota(jnp.int32, (WINDOW, 2 * WINDOW), 0)
    c_idx = lax.broadcasted_iota(jnp.int32, (WINDOW, 2 * WINDOW), 1)
    lane_lo = lax.broadcasted_iota(jnp.int32, (WINDOW, LANES), 1) < HEAD_DIM
    row = lax.broadcasted_iota(jnp.int32, (t, CONV_DIM), 0)
    n_pair = N_HEADS // 2
    cw = convw_ref[...]

    for h in range(N_SUB):
        src = (h - 1) % N_SUB
        rows = pl.ds(h * t, t)
        x = x_ref[0, rows, :]
        xn = _rms(x, g1_ref[...]).astype(BF16)

        kcat_s[h, 0:WINDOW, :] = kcat_s[src, t:t + WINDOW, :]
        vcat_s[h, 0:WINDOW, :] = vcat_s[src, t:t + WINDOW, :]
        q_s[h] = (_dot(xn, win_ref[:, 0:OFF_K]) * (HEAD_DIM ** -0.5)).astype(BF16)
        k = _dot(xn, win_ref[:, OFF_K:OFF_V])
        v = _dot(xn, win_ref[:, OFF_V:OFF_B])
        kcat_s[h, WINDOW:, :] = k.astype(BF16)
        vcat_s[h, WINDOW:, :] = v.astype(BF16)
        if h == last:
            knew_ref[0] = jnp.transpose(k[t - WINDOW:, :])
            vnew_ref[0] = jnp.transpose(v[t - WINDOW:, :])

        for j in range(t // WINDOW):
            r0 = j * WINDOW
            kk = kcat_s[h, r0:r0 + 2 * WINDOW, :]
            vv = vcat_s[h, r0:r0 + 2 * WINDOW, :]
            c_min = jnp.where(i == 0, WINDOW, 0) if (h == 0 and j == 0) else 0
            valid = (c_idx >= jnp.maximum(a_idx, c_min)) & (c_idx <= a_idx + WINDOW)
            qcols = [q_s[h, r0:r0 + WINDOW, m * LANES:(m + 1) * LANES] for m in range(n_pair)]
            outs = []
            for half in range(2):
                keep = lane_lo if half == 0 else jnp.logical_not(lane_lo)
                qg = jnp.concatenate([jnp.where(keep, qc, jnp.zeros_like(qc)) for qc in qcols], axis=0)
                s = lax.dot_general(qg, kk, (((1,), (1,)), ((), ())), preferred_element_type=F32)
                es, dens = [], []
                for m in range(n_pair):
                    sm = jnp.where(valid, s[m * WINDOW:(m + 1) * WINDOW], NEG_INF)
                    sink = sinks_ref[m + half * n_pair]
                    mx = jnp.maximum(jnp.max(sm, axis=-1, keepdims=True), sink)
                    e = jnp.exp(sm - mx)
                    dens.append(jnp.sum(e, axis=-1, keepdims=True) + jnp.exp(sink - mx))
                    es.append(e.astype(BF16))
                o = _dot(jnp.concatenate(es, axis=0), vv)
                outs.append([o[m * WINDOW:(m + 1) * WINDOW] / dens[m] for m in range(n_pair)])
            for m in range(n_pair):
                attn_s[h, r0:r0 + WINDOW, m * LANES:(m + 1) * LANES] = jnp.where(lane_lo, outs[0][m], outs[1][m])

        gate_b = _dot(xn, win_ref[:, OFF_B:OFF_C])
        u = _dot(xn, win_ref[:, OFF_C:OFF_H]) * _dot(xn, win_ref[:, OFF_H:IN_PROJ_DIM])
        prev = ucarry_s[src]
        u1 = jnp.where(row == 0, prev[7:8, :], pltpu.roll(u, 1, axis=0))
        u2 = jnp.where(row == 0, prev[6:7, :], jnp.where(row == 1, prev[7:8, :], pltpu.roll(u, 2, axis=0)))
        z = cw[0:1, :] * u2 + cw[1:2, :] * u1 + cw[2:3, :] * u
        ucarry_s[h] = u[t - 8:, :]
        if h == last:
            unew_ref[0] = u[t - (CONV_WIDTH - 1):, :]

        x1, xn2_bf, gates = _mix_tail(x, attn_s[h], gate_b * z, gattn_ref[...], gconv_ref[...],
                                      wout_ref[...], g2_ref[...], wr_ref[...], br_ref[...])
        x1_ref[0, rows, :] = x1
        xn2_ref[0, rows, :] = xn2_bf
        gates_ref[0, rows, :] = gates


def _layer_prompt(x, sinks, g1, win, convw, gattn, gconv, wout, g2, wr, br, wg32, wu32, wd32):
    nb, seq, _ = x.shape
    t = T_LAYER
    ts = T_LAYER // N_SUB
    n_i = seq // t
    assert nb * n_i == N_EXPERTS, "one expert's weights are cast per grid step"
    const = lambda b, i, s: (0, 0)
    tile = lambda b, i, s: (b, i, 0)
    per_seq = lambda b, i, s: (b, 0, 0)
    per_step = lambda b, i, s: (b * n_i + i, 0, 0)
    resident = pl.Buffered(1)
    grid_spec = pltpu.PrefetchScalarGridSpec(
        num_scalar_prefetch=1,
        grid=(nb, seq // t),
        in_specs=[
            pl.BlockSpec((1, t, D_MODEL), tile),
            pl.BlockSpec((1, D_MODEL), const),
            pl.BlockSpec((D_MODEL, IN_PROJ_DIM), const, pipeline_mode=resident),
            pl.BlockSpec((CONV_WIDTH, CONV_DIM), const),
            pl.BlockSpec((1, ATTN_DIM), const),
            pl.BlockSpec((1, CONV_DIM), const),
            pl.BlockSpec((D_MODEL, D_MODEL), const, pipeline_mode=resident),
            pl.BlockSpec((1, D_MODEL), const),
            pl.BlockSpec((D_MODEL, 2 * ROUTER_LANES), const, pipeline_mode=resident),
            pl.BlockSpec((1, ROUTER_LANES), const),
            pl.BlockSpec((1, D_MODEL, D_EXPERT), per_step),
            pl.BlockSpec((1, D_MODEL, D_EXPERT), per_step),
            pl.BlockSpec((1, D_EXPERT, D_MODEL), per_step),
        ],
        out_specs=[
            pl.BlockSpec((1, t, D_MODEL), tile),
            pl.BlockSpec((1, t, D_MODEL), tile),
            pl.BlockSpec((1, t, ROUTER_LANES), tile),
            pl.BlockSpec((1, KV_DIM, WINDOW), per_seq),
            pl.BlockSpec((1, KV_DIM, WINDOW), per_seq),
            pl.BlockSpec((1, CONV_WIDTH - 1, CONV_DIM), per_seq),
            pl.BlockSpec((1, D_MODEL, D_EXPERT), per_step),
            pl.BlockSpec((1, D_MODEL, D_EXPERT), per_step),
            pl.BlockSpec((1, D_EXPERT, D_MODEL), per_step),
        ],
        scratch_shapes=[
            pltpu.VMEM((N_SUB, ts, ATTN_DIM), BF16),
            pltpu.VMEM((N_SUB, ts + WINDOW, KV_DIM), BF16),
            pltpu.VMEM((N_SUB, ts + WINDOW, KV_DIM), BF16),
            pltpu.VMEM((N_SUB, ts, ATTN_DIM), F32),
            pltpu.VMEM((N_SUB, 8, CONV_DIM), F32),
        ],
    )
    return pl.pallas_call(
        _layer_prompt_body,
        grid_spec=grid_spec,
        out_shape=[
            jax.ShapeDtypeStruct((nb, seq, D_MODEL), F32),
            jax.ShapeDtypeStruct((nb, seq, D_MODEL), BF16),
            jax.ShapeDtypeStruct((nb, seq, ROUTER_LANES), F32),
            jax.ShapeDtypeStruct((nb, KV_DIM, WINDOW), F32),
            jax.ShapeDtypeStruct((nb, KV_DIM, WINDOW), F32),
            jax.ShapeDtypeStruct((nb, CONV_WIDTH - 1, CONV_DIM), F32),
            jax.ShapeDtypeStruct((N_EXPERTS, D_MODEL, D_EXPERT), BF16),
            jax.ShapeDtypeStruct((N_EXPERTS, D_MODEL, D_EXPERT), BF16),
            jax.ShapeDtypeStruct((N_EXPERTS, D_EXPERT, D_MODEL), BF16),
        ],
        compiler_params=pltpu.CompilerParams(
            dimension_semantics=("arbitrary", "arbitrary"), vmem_limit_bytes=MOE_VMEM_LIMIT),
        name="layer_prompt",
    )(sinks, x, g1, win, convw, gattn, gconv, wout, g2, wr, br, wg32, wu32, wd32)


def _moe_body(cap, xc, x1_ref, xn2_ref, gates_ref, wg_ref, wu_ref, wd_ref, gf_ref, y_ref,
              xs_s, gs_s, ys_s):
    t = x1_ref.shape[0]
    per_group = N_EXPERTS // N_GROUPS
    gates = gates_ref[...]
    lane = lax.broadcasted_iota(jnp.int32, (t, ROUTER_LANES), 1)
    gid = jnp.sum(jnp.where(lane == GID_LANE, gates, 0.0), axis=-1, keepdims=True)
    onehot = jnp.where((lane < N_GROUPS) & (lane.astype(F32) == gid), 1.0, 0.0)

    r_idx = lax.broadcasted_iota(jnp.int32, (t, t), 0)
    c_idx = lax.broadcasted_iota(jnp.int32, (t, t), 1)
    lower = jnp.where(c_idx < r_idx, 1.0, 0.0).astype(BF16)
    prefix = _dot(lower, onehot.astype(BF16))
    rank = jnp.sum(prefix * onehot, axis=-1, keepdims=True)
    counts = jnp.sum(onehot, axis=0, keepdims=True)
    lane1 = lax.broadcasted_iota(jnp.int32, (1, ROUTER_LANES), 1)
    cnt = [jnp.sum(jnp.where(lane1 == g, counts, 0.0)).astype(jnp.int32) for g in range(N_GROUPS)]
    off = [jnp.int32(0)]
    for g in range(1, N_GROUPS):
        off.append(off[-1] + cnt[g - 1])

    pos_col = rank
    for g in range(1, N_GROUPS):
        pos_col = pos_col + jnp.where(gid == float(g), off[g].astype(F32), 0.0)
    pos_row = jnp.transpose(jnp.broadcast_to(pos_col, (t, LANES)))[0:1, :]
    perm = jnp.where(r_idx.astype(F32) == pos_row, 1.0, 0.0).astype(BF16)
    perm_t = jnp.where(c_idx.astype(F32) == pos_col, 1.0, 0.0).astype(BF16)

    xs_s[0:t, :] = _dot(perm, xn2_ref[...]).astype(BF16)
    p1 = gates.astype(BF16)
    r1 = gates - p1.astype(F32)
    p2 = r1.astype(BF16)
    p3 = (r1 - p2.astype(F32)).astype(BF16)
    gs3 = _dot(perm, jnp.concatenate([p1, p2, p3], axis=1))
    gs_s[0:t, :] = (gs3[:, :ROUTER_LANES] + gs3[:, ROUTER_LANES:2 * ROUTER_LANES]) + gs3[:, 2 * ROUTER_LANES:]
    pad = xs_s.shape[0] - t
    xs_s[t:, :] = jnp.zeros((pad, D_MODEL), BF16)
    gs_s[t:, :] = jnp.zeros((pad, ROUTER_LANES), F32)
    ys_s[...] = jnp.zeros(ys_s.shape, F32)

    def group_rows(g, r0, rows):
        xs = xs_s[pl.ds(r0, rows), :]
        gsc = gs_s[pl.ds(r0, rows), :]
        acts = []
        for k in range(per_group):
            e = g * per_group + k
            hg = _dot(xs, wg_ref[e])
            hu = _dot(xs, wu_ref[e])
            acts.append((hg / (1.0 + jnp.exp(-hg)) * hu * gsc[:, e:e + 1]).astype(BF16))
        ys_s[pl.ds(r0, rows), :] += _dot(jnp.concatenate(acts, axis=1), wd_ref[g])

    base = [(off[g] // ROW_ALIGN) * ROW_ALIGN for g in range(N_GROUPS)]
    for g in range(N_GROUPS):
        group_rows(g, pl.multiple_of(base[g], ROW_ALIGN), cap)
    for g in range(N_GROUPS):
        n_extra = jnp.maximum(0, (off[g] + cnt[g] - (base[g] + cap) + xc - 1) // xc)

        def extra(k, carry, g=g):
            group_rows(g, pl.multiple_of(base[g] + cap + k * xc, ROW_ALIGN), xc)
            return carry

        lax.fori_loop(0, n_extra, extra, 0)

    moe = _dot(perm_t, ys_s[0:t, :].astype(BF16))
    y_ref[...] = _rms(x1_ref[...] + moe, gf_ref[...])


def _moe(x1, xn2, route, wg, wu, wd, gf, t, cap, xc):
    n = x1.shape[0]
    tile = lambda i: (i, 0)
    whole = lambda i: (0, 0, 0)
    resident = pl.Buffered(1)
    rows = t + cap + xc
    hidden = (N_EXPERTS // N_GROUPS) * D_EXPERT
    return pl.pallas_call(
        functools.partial(_moe_body, cap, xc),
        grid=(n // t,),
        in_specs=[
            pl.BlockSpec((t, D_MODEL), tile),
            pl.BlockSpec((t, D_MODEL), tile),
            pl.BlockSpec((t, ROUTER_LANES), tile),
            pl.BlockSpec((N_EXPERTS, D_MODEL, D_EXPERT), whole, pipeline_mode=resident),
            pl.BlockSpec((N_EXPERTS, D_MODEL, D_EXPERT), whole, pipeline_mode=resident),
            pl.BlockSpec((N_GROUPS, hidden, D_MODEL), whole, pipeline_mode=resident),
            pl.BlockSpec((1, D_MODEL), lambda i: (0, 0)),
        ],
        out_specs=pl.BlockSpec((t, D_MODEL), tile),
        out_shape=jax.ShapeDtypeStruct((n, D_MODEL), F32),
        scratch_shapes=[
            pltpu.VMEM((rows, D_MODEL), BF16),
            pltpu.VMEM((rows, ROUTER_LANES), F32),
            pltpu.VMEM((rows, D_MODEL), F32),
        ],
        compiler_params=pltpu.CompilerParams(
            dimension_semantics=("arbitrary",), vmem_limit_bytes=MOE_VMEM_LIMIT),
        name="moe",
    )(x1, xn2, route, wg, wu, wd, gf)


def _proj_sample_body(x_ref, g1_ref, win_ref, proj_ref):
    xn = _rms(x_ref[...], g1_ref[...]).astype(BF16)
    proj_ref[...] = _dot(xn, win_ref[...])


def _proj_sample(x, g1, win):
    n = x.shape[0]
    return pl.pallas_call(
        _proj_sample_body,
        out_shape=jax.ShapeDtypeStruct((n, IN_PROJ_DIM), F32),
        compiler_params=pltpu.CompilerParams(vmem_limit_bytes=VMEM_LIMIT),
        name="proj_sample",
    )(x, g1, win)


def _attn_sample_body(q8_ref, knew_ref, vnew_ref, kcol_ref, vcol_ref, khist_ref, vhist_ref, sinks_ref,
                      out8_ref, kout_ref, vout_ref):
    q8 = q8_ref[...]
    kh = khist_ref[...]
    vh = vhist_ref[...]
    knew = knew_ref[...]
    vnew = vnew_ref[...]
    s = jnp.einsum('bhj,bjc->bhc', q8, kh.astype(BF16), preferred_element_type=F32)
    s_new = jnp.sum(q8.astype(F32) * knew, axis=-1, keepdims=True)
    sink = sinks_ref[...][None]
    mx = jnp.maximum(jnp.maximum(jnp.max(s, axis=-1, keepdims=True), s_new), sink)
    e = jnp.exp(s - mx)
    e_new = jnp.exp(s_new - mx)
    den = jnp.sum(e, axis=-1, keepdims=True) + e_new + jnp.exp(sink - mx)
    o = jnp.einsum('bhc,bjc->bhj', e.astype(BF16), vh.astype(BF16), preferred_element_type=F32)
    out8_ref[...] = (o + e_new * vnew) / den
    last = lax.broadcasted_iota(jnp.int32, kh.shape, 2) == WINDOW - 1
    kout_ref[...] = jnp.where(last, kcol_ref[...], pltpu.roll(kh, WINDOW - 1, axis=2))
    vout_ref[...] = jnp.where(last, vcol_ref[...], pltpu.roll(vh, WINDOW - 1, axis=2))


def _attn_sample(q8, knew, vnew, kcol, vcol, khist, vhist, sinks_col):
    n = q8.shape[0]
    c = DEC_CHUNK
    blk3 = lambda i: (i, 0, 0)
    return pl.pallas_call(
        _attn_sample_body,
        grid=(n // c,),
        in_specs=[
            pl.BlockSpec((c, N_HEADS, LANES), blk3),
            pl.BlockSpec((c, 1, KV_DIM), blk3),
            pl.BlockSpec((c, 1, KV_DIM), blk3),
            pl.BlockSpec((c, KV_DIM, 1), blk3),
            pl.BlockSpec((c, KV_DIM, 1), blk3),
            pl.BlockSpec((c, KV_DIM, WINDOW), blk3),
            pl.BlockSpec((c, KV_DIM, WINDOW), blk3),
            pl.BlockSpec((N_HEADS, 1), lambda i: (0, 0)),
        ],
        out_specs=[
            pl.BlockSpec((c, N_HEADS, LANES), blk3),
            pl.BlockSpec((c, KV_DIM, WINDOW), blk3),
            pl.BlockSpec((c, KV_DIM, WINDOW), blk3),
        ],
        out_shape=[
            jax.ShapeDtypeStruct((n, N_HEADS, LANES), F32),
            jax.ShapeDtypeStruct((n, KV_DIM, WINDOW), F32),
            jax.ShapeDtypeStruct((n, KV_DIM, WINDOW), F32),
        ],
        compiler_params=pltpu.CompilerParams(
            dimension_semantics=("arbitrary",), vmem_limit_bytes=VMEM_LIMIT),
        name="attn_sample",
    )(q8, knew, vnew, kcol, vcol, khist, vhist, sinks_col)


def _tail_sample_body(x_ref, attn_ref, proj_ref, uprev2_ref, uprev1_ref, convw_ref, gattn_ref,
                      gconv_ref, wout_ref, g2_ref, wr_ref, br_ref,
                      x1_ref, xn2_ref, gates_ref, u_ref):
    gate_b = proj_ref[:, OFF_B:OFF_C]
    u = proj_ref[:, OFF_C:OFF_H] * proj_ref[:, OFF_H:IN_PROJ_DIM]
    cw = convw_ref[...]
    z = cw[0:1, :] * uprev2_ref[...] + cw[1:2, :] * uprev1_ref[...] + cw[2:3, :] * u
    u_ref[...] = u
    x1, xn2_bf, gates = _mix_tail(x_ref[...], attn_ref[...], gate_b * z, gattn_ref[...], gconv_ref[...],
                                  wout_ref[...], g2_ref[...], wr_ref[...], br_ref[...])
    x1_ref[...] = x1
    xn2_ref[...] = xn2_bf
    gates_ref[...] = gates


def _tail_sample(x, attn, proj, uprev2, uprev1, convw, gattn, gconv, wout, g2, wr, br):
    n = x.shape[0]
    return pl.pallas_call(
        _tail_sample_body,
        out_shape=[
            jax.ShapeDtypeStruct((n, D_MODEL), F32),
            jax.ShapeDtypeStruct((n, D_MODEL), BF16),
            jax.ShapeDtypeStruct((n, ROUTER_LANES), F32),
            jax.ShapeDtypeStruct((n, CONV_DIM), F32),
        ],
        compiler_params=pltpu.CompilerParams(vmem_limit_bytes=VMEM_LIMIT),
        name="tail_sample",
    )(x, attn, proj, uprev2, uprev1, convw, gattn, gconv, wout, g2, wr, br)


def _pair_heads(w, axis):
    shape = w.shape
    n_pair = N_HEADS // 2
    split = shape[:axis] + (2, n_pair, HEAD_DIM) + shape[axis + 1:]
    return jnp.swapaxes(w.reshape(split), axis, axis + 1).reshape(shape)


def kernel(x_prompt, x_sample, cache_k_win, cache_v_win, state_conv, norm1_g, w_in, conv_w, sinks,
           attn_out_g, conv_out_g, w_out, norm2_g, w_group, b_group, w_fine, b_fine, w_gate, w_up,
           w_down, final_g):
    nb, seq, _ = x_prompt.shape
    nd = x_sample.shape[0]

    w_in0 = w_in[0].astype(BF16)
    win = jnp.concatenate([_pair_heads(w_in0[:, :ATTN_DIM], 1), w_in0[:, ATTN_DIM:]], axis=1)
    w_out0 = w_out[0].astype(BF16)
    wout = jnp.concatenate([_pair_heads(w_out0[:ATTN_DIM], 0), w_out0[ATTN_DIM:]], axis=0)
    gattn = _pair_heads(attn_out_g[0], 0)[None]
    gconv = conv_out_g[0][None]
    g1 = norm1_g[0][None]
    g2 = norm2_g[0][None]
    gf = final_g[None]
    convw = conv_w[0]
    sinks0 = sinks[0]
    pad = ROUTER_LANES - N_EXPERTS - N_GROUPS
    wr32 = jnp.concatenate([w_fine[0], w_group[0], jnp.zeros((D_MODEL, pad), F32)], axis=1)
    wr_hi = wr32.astype(BF16)
    wr = jnp.concatenate([wr_hi, (wr32 - wr_hi.astype(F32)).astype(BF16)], axis=1)
    br = jnp.concatenate([b_fine[0], b_group[0], jnp.zeros((pad,), F32)])[None]

    x1p, xn2p, gatesp, kp, vp, up, wg, wu, wd = _layer_prompt(
        x_prompt, sinks0, g1, win, convw, gattn, gconv, wout, g2, wr, br, w_gate[0], w_up[0], w_down[0])
    wd = wd.reshape(N_GROUPS, (N_EXPERTS // N_GROUPS) * D_EXPERT, D_MODEL)
    n_p = nb * seq
    y_prompt = _moe(x1p.reshape(n_p, D_MODEL), xn2p.reshape(n_p, D_MODEL),
                    gatesp.reshape(n_p, ROUTER_LANES), wg, wu, wd, gf, T_MOE, MOE_CAP, MOE_XC).reshape(nb, seq, D_MODEL)

    xs = x_sample.reshape(nd, D_MODEL)
    proj = _proj_sample(xs, g1, win)
    lane = jnp.arange(LANES)
    qp = (proj[:, :ATTN_DIM] * (HEAD_DIM ** -0.5)).reshape(nd, N_HEADS // 2, LANES)
    q8 = jnp.concatenate([jnp.where(lane < HEAD_DIM, qp, 0.0), jnp.where(lane >= HEAD_DIM, qp, 0.0)],
                         axis=1).astype(BF16)
    knew = proj[:, OFF_K:OFF_V].reshape(nd, 1, KV_DIM)
    vnew = proj[:, OFF_V:OFF_B].reshape(nd, 1, KV_DIM)
    to_cm = lambda c: jnp.transpose(c[0], (0, 2, 3, 1)).reshape(nd, KV_DIM, WINDOW)
    from_cm = lambda c, n: jnp.transpose(c.reshape(n, KV_DIM // HEAD_DIM, HEAD_DIM, WINDOW), (0, 3, 1, 2))[None]
    out8, ks, vs = _attn_sample(q8, knew, vnew, knew.reshape(nd, KV_DIM, 1), vnew.reshape(nd, KV_DIM, 1),
                                to_cm(cache_k_win), to_cm(cache_v_win), sinks0[:, None])
    attn_s = jnp.where(lane < HEAD_DIM, out8[:, :N_HEADS // 2], out8[:, N_HEADS // 2:]).reshape(nd, ATTN_DIM)
    st = state_conv[0]
    x1s, xn2s, gatess, us = _tail_sample(xs, attn_s, proj, st[:, 0], st[:, 1], convw, gattn, gconv,
                                         wout, g2, wr, br)
    y_sample = _moe(x1s, xn2s, gatess, wg, wu, wd, gf, nd, MOE_CAP_DEC, MOE_XC_DEC).reshape(nd, 1, D_MODEL)

    return (y_prompt, y_sample,
            from_cm(kp, nb), from_cm(vp, nb),
            up[None],
            from_cm(ks, nd), from_cm(vs, nd),
            jnp.stack([st[:, 1], us], axis=1)[None])
```

```python
import functools

import jax
import jax.numpy as jnp
from jax import lax
from jax.experimental import pallas as pl
from jax.experimental.pallas import tpu as pltpu

D_MODEL = 1024
HEAD_DIM = 64
N_HEADS = 8
ATTN_DIM = 512
KV_DIM = 128
WINDOW = 128
CONV_DIM = 512
CONV_WIDTH = 3
OFF_K = 512
OFF_V = 640
OFF_B = 768
OFF_C = 1280
OFF_H = 1792
IN_PROJ_DIM = 2304
N_GROUPS = 4
N_EXPERTS = 16
D_EXPERT = 256
RMS_EPS = 1e-5
NEG_INF = -1e30

LANES = 128
ROW_ALIGN = 16
ROUTER_LANES = 128
COARSE_OFF = N_EXPERTS
T_LAYER = 1024
N_SUB = 2
GID_LANE = N_EXPERTS
T_MOE = 512
MOE_CAP, MOE_XC = 176, 64
MOE_CAP_DEC, MOE_XC_DEC = 64, 32
DEC_CHUNK = 32
VMEM_LIMIT = 48 * 1024 * 1024
MOE_VMEM_LIMIT = 56 * 1024 * 1024

BF16 = jnp.bfloat16
F32 = jnp.float32


def _dot(a, b):
    return jnp.dot(a, b, preferred_element_type=F32)


def _rms(x, g):
    ms = jnp.mean(x * x, axis=-1, keepdims=True)
    return x * lax.rsqrt(ms + RMS_EPS) * g


def _group_norm64(y, g):
    rows, cols = y.shape
    lo = lax.broadcasted_iota(jnp.int32, (rows, LANES), 1) < HEAD_DIM
    outs = []
    for c in range(cols // LANES):
        blk = y[:, c * LANES:(c + 1) * LANES]
        sq = blk * blk
        s_lo = jnp.sum(jnp.where(lo, sq, 0.0), axis=-1, keepdims=True)
        s_hi = jnp.sum(jnp.where(lo, 0.0, sq), axis=-1, keepdims=True)
        ms = jnp.where(lo, s_lo, s_hi) * (1.0 / HEAD_DIM)
        outs.append(blk * lax.rsqrt(ms + RMS_EPS))
    return jnp.concatenate(outs, axis=-1) * g


def _router(xn2, wr, br):
    hi = xn2.astype(BF16)
    both = _dot(hi, wr)
    logits = both[:, :ROUTER_LANES] + both[:, ROUTER_LANES:] + br
    rows = logits.shape[0]
    lane = lax.broadcasted_iota(jnp.int32, (rows, ROUTER_LANES), 1)
    lanef = lane.astype(F32)
    big = float(ROUTER_LANES)
    coarse = (lane >= COARSE_OFF) & (lane < COARSE_OFF + N_GROUPS)
    cm = jnp.max(jnp.where(coarse, logits, -jnp.inf), axis=-1, keepdims=True)
    g_sel = jnp.min(jnp.where(coarse & (logits == cm), lanef - COARSE_OFF, big),
                    axis=-1, keepdims=True)
    zc = jnp.sum(jnp.where(coarse, jnp.exp(logits - cm), 0.0), axis=-1, keepdims=True)
    p_sel = 1.0 / zc
    per_group = N_EXPERTS // N_GROUPS
    fine = (lane < N_EXPERTS) & ((lane // per_group).astype(F32) == g_sel)
    f1 = jnp.max(jnp.where(fine, logits, -jnp.inf), axis=-1, keepdims=True)
    i1 = jnp.min(jnp.where(fine & (logits == f1), lanef, big), axis=-1, keepdims=True)
    rest = fine & (lanef != i1)
    f2 = jnp.max(jnp.where(rest, logits, -jnp.inf), axis=-1, keepdims=True)
    i2 = jnp.min(jnp.where(rest & (logits == f2), lanef, big), axis=-1, keepdims=True)
    e2 = jnp.exp(f2 - f1)
    t1 = 1.0 / (1.0 + e2)
    t2 = e2 * t1
    gates = jnp.where(lanef == i1, p_sel * t1, jnp.where(lanef == i2, p_sel * t2, 0.0))
    gates = jnp.where(lane == GID_LANE, g_sel, gates)
    return hi, gates


def _mix_tail(x, attn, conv_out, gattn, gconv, wout, g2, wr, br):
    mixed = jnp.concatenate([_group_norm64(attn, gattn), _group_norm64(conv_out, gconv)],
                            axis=-1).astype(BF16)
    x1 = x + _dot(mixed, wout)
    xn2_bf, gates = _router(_rms(x1, g2), wr, br)
    return x1, xn2_bf, gates


def _layer_prompt_body(sinks_ref, x_ref, g1_ref, win_ref, convw_ref, gattn_ref, gconv_ref,
                       wout_ref, g2_ref, wr_ref, br_ref, wg32_ref, wu32_ref, wd32_ref,
                       x1_ref, xn2_ref, gates_ref, knew_ref, vnew_ref, unew_ref,
                       wg16_ref, wu16_ref, wd16_ref,
                       q_s, kcat_s, vcat_s, attn_s, ucarry_s):
    i = pl.program_id(1)
    wg16_ref[...] = wg32_ref[...].astype(BF16)
    wu16_ref[...] = wu32_ref[...].astype(BF16)
    wd16_ref[...] = wd32_ref[...].astype(BF16)
    t = T_LAYER // N_SUB
    last = N_SUB - 1

    @pl.when(i == 0)
    def _():
        kcat_s[last, t:t + WINDOW, :] = jnp.zeros((WINDOW, KV_DIM), BF16)
        vcat_s[last, t:t + WINDOW, :] = jnp.zeros((WINDOW, KV_DIM), BF16)
        ucarry_s[last] = jnp.zeros((8, CONV_DIM), F32)

    a_idx = lax.broadcasted_iota(jnp.int32, (WINDOW, 2 * WINDOW), 0)
    c_idx = lax.broadcasted_iota(jnp.int32, (WINDOW, 2 * WINDOW), 1)
    lane_lo = lax.broadcasted_iota(jnp.int32, (WINDOW, LANES), 1) < HEAD_DIM
    row = lax.broadcasted_iota(jnp.int32, (t, CONV_DIM), 0)
    n_pair = N_HEADS // 2
    cw = convw_ref[...]

    for h in range(N_SUB):
        src = (h - 1) % N_SUB
        rows = pl.ds(h * t, t)
        x = x_ref[0, rows, :]
        xn = _rms(x, g1_ref[...]).astype(BF16)

        kcat_s[h, 0:WINDOW, :] = kcat_s[src, t:t + WINDOW, :]
        vcat_s[h, 0:WINDOW, :] = vcat_s[src, t:t + WINDOW, :]
        q_s[h] = (_dot(xn, win_ref[:, 0:OFF_K]) * (HEAD_DIM ** -0.5)).astype(BF16)
        kv = _dot(xn, win_ref[:, OFF_K:OFF_B])
        k = kv[:, :KV_DIM]
        v = kv[:, KV_DIM:]
        kcat_s[h, WINDOW:, :] = k.astype(BF16)
        vcat_s[h, WINDOW:, :] = v.astype(BF16)
        if h == last:
            knew_ref[0] = jnp.transpose(k[t - WINDOW:, :])
            vnew_ref[0] = jnp.transpose(v[t - WINDOW:, :])

        for j in range(t // WINDOW):
            r0 = j * WINDOW
            kk = kcat_s[h, r0:r0 + 2 * WINDOW, :]
            vv = vcat_s[h, r0:r0 + 2 * WINDOW, :]
            c_min = jnp.where(i == 0, WINDOW, 0) if (h == 0 and j == 0) else 0
            valid = (c_idx >= jnp.maximum(a_idx, c_min)) & (c_idx <= a_idx + WINDOW)
            qcols = [q_s[h, r0:r0 + WINDOW, m * LANES:(m + 1) * LANES] for m in range(n_pair)]
            outs = []
            for half in range(2):
                keep = lane_lo if half == 0 else jnp.logical_not(lane_lo)
                qg = jnp.concatenate([jnp.where(keep, qc, jnp.zeros_like(qc)) for qc in qcols], axis=0)
                s = lax.dot_general(qg, kk, (((1,), (1,)), ((), ())), preferred_element_type=F32)
                es, dens = [], []
                for m in range(n_pair):
                    sm = jnp.where(valid, s[m * WINDOW:(m + 1) * WINDOW], NEG_INF)
                    sink = sinks_ref[m + half * n_pair]
                    mx = jnp.maximum(jnp.max(sm, axis=-1, keepdims=True), sink)
                    e = jnp.exp(sm - mx)
                    dens.append(jnp.sum(e, axis=-1, keepdims=True) + jnp.exp(sink - mx))
                    es.append(e.astype(BF16))
                o = _dot(jnp.concatenate(es, axis=0), vv)
                outs.append([o[m * WINDOW:(m + 1) * WINDOW] / dens[m] for m in range(n_pair)])
            for m in range(n_pair):
                attn_s[h, r0:r0 + WINDOW, m * LANES:(m + 1) * LANES] = jnp.where(lane_lo, outs[0][m], outs[1][m])

        gate_b = _dot(xn, win_ref[:, OFF_B:OFF_C])
        u = _dot(xn, win_ref[:, OFF_C:OFF_H]) * _dot(xn, win_ref[:, OFF_H:IN_PROJ_DIM])
        prev = ucarry_s[src]
        u1 = jnp.where(row == 0, prev[7:8, :], pltpu.roll(u, 1, axis=0))
        u2 = jnp.where(row == 0, prev[6:7, :], jnp.where(row == 1, prev[7:8, :], pltpu.roll(u, 2, axis=0)))
        z = cw[0:1, :] * u2 + cw[1:2, :] * u1 + cw[2:3, :] * u
        ucarry_s[h] = u[t - 8:, :]
        if h == last:
            unew_ref[0] = u[t - (CONV_WIDTH - 1):, :]

        x1, xn2_bf, gates = _mix_tail(x, attn_s[h], gate_b * z, gattn_ref[...], gconv_ref[...],
                                      wout_ref[...], g2_ref[...], wr_ref[...], br_ref[...])
        x1_ref[0, rows, :] = x1
        xn2_ref[0, rows, :] = xn2_bf
        gates_ref[0, rows, :] = gates


def _layer_prompt(x, sinks, g1, win, convw, gattn, gconv, wout, g2, wr, br, wg32, wu32, wd32):
    nb, seq, _ = x.shape
    t = T_LAYER
    ts = T_LAYER // N_SUB
    n_i = seq // t
    assert nb * n_i == N_EXPERTS, "one expert's weights are cast per grid step"
    const = lambda b, i, s: (0, 0)
    tile = lambda b, i, s: (b, i, 0)
    per_seq = lambda b, i, s: (b, 0, 0)
    per_step = lambda b, i, s: (b * n_i + i, 0, 0)
    resident = pl.Buffered(1)
    grid_spec = pltpu.PrefetchScalarGridSpec(
        num_scalar_prefetch=1,
        grid=(nb, seq // t),
        in_specs=[
            pl.BlockSpec((1, t, D_MODEL), tile),
            pl.BlockSpec((1, D_MODEL), const),
            pl.BlockSpec((D_MODEL, IN_PROJ_DIM), const, pipeline_mode=resident),
            pl.BlockSpec((CONV_WIDTH, CONV_DIM), const),
            pl.BlockSpec((1, ATTN_DIM), const),
            pl.BlockSpec((1, CONV_DIM), const),
            pl.BlockSpec((D_MODEL, D_MODEL), const, pipeline_mode=resident),
            pl.BlockSpec((1, D_MODEL), const),
            pl.BlockSpec((D_MODEL, 2 * ROUTER_LANES), const, pipeline_mode=resident),
            pl.BlockSpec((1, ROUTER_LANES), const),
            pl.BlockSpec((1, D_MODEL, D_EXPERT), per_step),
            pl.BlockSpec((1, D_MODEL, D_EXPERT), per_step),
            pl.BlockSpec((1, D_EXPERT, D_MODEL), per_step),
        ],
        out_specs=[
            pl.BlockSpec((1, t, D_MODEL), tile),
            pl.BlockSpec((1, t, D_MODEL), tile),
            pl.BlockSpec((1, t, ROUTER_LANES), tile),
            pl.BlockSpec((1, KV_DIM, WINDOW), per_seq),
            pl.BlockSpec((1, KV_DIM, WINDOW), per_seq),
            pl.BlockSpec((1, CONV_WIDTH - 1, CONV_DIM), per_seq),
            pl.BlockSpec((1, D_MODEL, D_EXPERT), per_step),
            pl.BlockSpec((1, D_MODEL, D_EXPERT), per_step),
            pl.BlockSpec((1, D_EXPERT, D_MODEL), per_step),
        ],
        scratch_shapes=[
            pltpu.VMEM((N_SUB, ts, ATTN_DIM), BF16),
            pltpu.VMEM((N_SUB, ts + WINDOW, KV_DIM), BF16),
            pltpu.VMEM((N_SUB, ts + WINDOW, KV_DIM), BF16),
            pltpu.VMEM((N_SUB, ts, ATTN_DIM), F32),
            pltpu.VMEM((N_SUB, 8, CONV_DIM), F32),
        ],
    )
    return pl.pallas_call(
        _layer_prompt_body,
        grid_spec=grid_spec,
        out_shape=[
            jax.ShapeDtypeStruct((nb, seq, D_MODEL), F32),
            jax.ShapeDtypeStruct((nb, seq, D_MODEL), BF16),
            jax.ShapeDtypeStruct((nb, seq, ROUTER_LANES), F32),
            jax.ShapeDtypeStruct((nb, KV_DIM, WINDOW), F32),
            jax.ShapeDtypeStruct((nb, KV_DIM, WINDOW), F32),
            jax.ShapeDtypeStruct((nb, CONV_WIDTH - 1, CONV_DIM), F32),
            jax.ShapeDtypeStruct((N_EXPERTS, D_MODEL, D_EXPERT), BF16),
            jax.ShapeDtypeStruct((N_EXPERTS, D_MODEL, D_EXPERT), BF16),
            jax.ShapeDtypeStruct((N_EXPERTS, D_EXPERT, D_MODEL), BF16),
        ],
        compiler_params=pltpu.CompilerParams(
            dimension_semantics=("arbitrary", "arbitrary"), vmem_limit_bytes=MOE_VMEM_LIMIT),
        name="layer_prompt",
    )(sinks, x, g1, win, convw, gattn, gconv, wout, g2, wr, br, wg32, wu32, wd32)


def _moe_body(cap, xc, x1_ref, xn2_ref, gates_ref, wg_ref, wu_ref, wd_ref, gf_ref, y_ref,
              xs_s, gs_s, ys_s):
    t = x1_ref.shape[0]
    per_group = N_EXPERTS // N_GROUPS
    gates = gates_ref[...]
    lane = lax.broadcasted_iota(jnp.int32, (t, ROUTER_LANES), 1)
    gid = jnp.sum(jnp.where(lane == GID_LANE, gates, 0.0), axis=-1, keepdims=True)
    onehot = jnp.where((lane < N_GROUPS) & (lane.astype(F32) == gid), 1.0, 0.0)

    r_idx = lax.broadcasted_iota(jnp.int32, (t, t), 0)
    c_idx = lax.broadcasted_iota(jnp.int32, (t, t), 1)
    lower = jnp.where(c_idx < r_idx, 1.0, 0.0).astype(BF16)
    prefix = _dot(lower, onehot.astype(BF16))
    rank = jnp.sum(prefix * onehot, axis=-1, keepdims=True)
    counts = jnp.sum(onehot, axis=0, keepdims=True)
    lane1 = lax.broadcasted_iota(jnp.int32, (1, ROUTER_LANES), 1)
    cnt = [jnp.sum(jnp.where(lane1 == g, counts, 0.0)).astype(jnp.int32) for g in range(N_GROUPS)]
    off = [jnp.int32(0)]
    for g in range(1, N_GROUPS):
        off.append(off[-1] + cnt[g - 1])

    pos_col = rank
    for g in range(1, N_GROUPS):
        pos_col = pos_col + jnp.where(gid == float(g), off[g].astype(F32), 0.0)
    pos_row = jnp.transpose(jnp.broadcast_to(pos_col, (t, LANES)))[0:1, :]
    perm = jnp.where(r_idx.astype(F32) == pos_row, 1.0, 0.0).astype(BF16)
    perm_t = jnp.where(c_idx.astype(F32) == pos_col, 1.0, 0.0).astype(BF16)

    xs_s[0:t, :] = _dot(perm, xn2_ref[...]).astype(BF16)
    p1 = gates.astype(BF16)
    r1 = gates - p1.astype(F32)
    p2 = r1.astype(BF16)
    p3 = (r1 - p2.astype(F32)).astype(BF16)
    gs3 = _dot(perm, jnp.concatenate([p1, p2, p3], axis=1))
    gs_s[0:t, :] = (gs3[:, :ROUTER_LANES] + gs3[:, ROUTER_LANES:2 * ROUTER_LANES]) + gs3[:, 2 * ROUTER_LANES:]
    pad = xs_s.shape[0] - t
    xs_s[t:, :] = jnp.zeros((pad, D_MODEL), BF16)
    gs_s[t:, :] = jnp.zeros((pad, ROUTER_LANES), F32)
    ys_s[...] = jnp.zeros(ys_s.shape, F32)

    def group_rows(g, r0, rows):
        xs = xs_s[pl.ds(r0, rows), :]
        gsc = gs_s[pl.ds(r0, rows), :]
        acts = []
        for k in range(per_group):
            e = g * per_group + k
            hg = _dot(xs, wg_ref[e])
            hu = _dot(xs, wu_ref[e])
            acts.append((hg / (1.0 + jnp.exp(-hg)) * hu * gsc[:, e:e + 1]).astype(BF16))
        ys_s[pl.ds(r0, rows), :] += _dot(jnp.concatenate(acts, axis=1), wd_ref[g])

    base = [(off[g] // ROW_ALIGN) * ROW_ALIGN for g in range(N_GROUPS)]
    for g in range(N_GROUPS):
        group_rows(g, pl.multiple_of(base[g], ROW_ALIGN), cap)
    for g in range(N_GROUPS):
        n_extra = jnp.maximum(0, (off[g] + cnt[g] - (base[g] + cap) + xc - 1) // xc)

        def extra(k, carry, g=g):
            group_rows(g, pl.multiple_of(base[g] + cap + k * xc, ROW_ALIGN), xc)
            return carry

        lax.fori_loop(0, n_extra, extra, 0)

    moe = _dot(perm_t, ys_s[0:t, :].astype(BF16))
    y_ref[...] = _rms(x1_ref[...] + moe, gf_ref[...])


def _moe(x1, xn2, route, wg, wu, wd, gf, t, cap, xc):
    n = x1.shape[0]
    tile = lambda i: (i, 0)
    whole = lambda i: (0, 0, 0)
    resident = pl.Buffered(1)
    rows = t + cap + xc
    hidden = (N_EXPERTS // N_GROUPS) * D_EXPERT
    return pl.pallas_call(
        functools.partial(_moe_body, cap, xc),
        grid=(n // t,),
        in_specs=[
            pl.BlockSpec((t, D_MODEL), tile),
            pl.BlockSpec((t, D_MODEL), tile),
            pl.BlockSpec((t, ROUTER_LANES), tile),
            pl.BlockSpec((N_EXPERTS, D_MODEL, D_EXPERT), whole, pipeline_mode=resident),
            pl.BlockSpec((N_EXPERTS, D_MODEL, D_EXPERT), whole, pipeline_mode=resident),
            pl.BlockSpec((N_GROUPS, hidden, D_MODEL), whole, pipeline_mode=resident),
            pl.BlockSpec((1, D_MODEL), lambda i: (0, 0)),
        ],
        out_specs=pl.BlockSpec((t, D_MODEL), tile),
        out_shape=jax.ShapeDtypeStruct((n, D_MODEL), F32),
        scratch_shapes=[
            pltpu.VMEM((rows, D_MODEL), BF16),
            pltpu.VMEM((rows, ROUTER_LANES), F32),
            pltpu.VMEM((rows, D_MODEL), F32),
        ],
        compiler_params=pltpu.CompilerParams(
            dimension_semantics=("arbitrary",), vmem_limit_bytes=MOE_VMEM_LIMIT),
        name="moe",
    )(x1, xn2, route, wg, wu, wd, gf)


def _proj_sample_body(x_ref, g1_ref, win_ref, proj_ref):
    xn = _rms(x_ref[...], g1_ref[...]).astype(BF16)
    proj_ref[...] = _dot(xn, win_ref[...])


def _proj_sample(x, g1, win):
    n = x.shape[0]
    return pl.pallas_call(
        _proj_sample_body,
        out_shape=jax.ShapeDtypeStruct((n, IN_PROJ_DIM), F32),
        compiler_params=pltpu.CompilerParams(vmem_limit_bytes=VMEM_LIMIT),
        name="proj_sample",
    )(x, g1, win)


def _attn_sample_body(q8_ref, knew_ref, vnew_ref, kcol_ref, vcol_ref, khist_ref, vhist_ref, sinks_ref,
                      out8_ref, kout_ref, vout_ref):
    q8 = q8_ref[...]
    kh = khist_ref[...]
    vh = vhist_ref[...]
    knew = knew_ref[...]
    vnew = vnew_ref[...]
    s = jnp.einsum('bhj,bjc->bhc', q8, kh.astype(BF16), preferred_element_type=F32)
    s_new = jnp.sum(q8.astype(F32) * knew, axis=-1, keepdims=True)
    sink = sinks_ref[...][None]
    mx = jnp.maximum(jnp.maximum(jnp.max(s, axis=-1, keepdims=True), s_new), sink)
    e = jnp.exp(s - mx)
    e_new = jnp.exp(s_new - mx)
    den = jnp.sum(e, axis=-1, keepdims=True) + e_new + jnp.exp(sink - mx)
    o = jnp.einsum('bhc,bjc->bhj', e.astype(BF16), vh.astype(BF16), preferred_element_type=F32)
    out8_ref[...] = (o + e_new * vnew) / den
    last = lax.broadcasted_iota(jnp.int32, kh.shape, 2) == WINDOW - 1
    kout_ref[...] = jnp.where(last, kcol_ref[...], pltpu.roll(kh, WINDOW - 1, axis=2))
    vout_ref[...] = jnp.where(last, vcol_ref[...], pltpu.roll(vh, WINDOW - 1, axis=2))


def _attn_sample(q8, knew, vnew, kcol, vcol, khist, vhist, sinks_col):
    n = q8.shape[0]
    c = DEC_CHUNK
    blk3 = lambda i: (i, 0, 0)
    return pl.pallas_call(
        _attn_sample_body,
        grid=(n // c,),
        in_specs=[
            pl.BlockSpec((c, N_HEADS, LANES), blk3),
            pl.BlockSpec((c, 1, KV_DIM), blk3),
            pl.BlockSpec((c, 1, KV_DIM), blk3),
            pl.BlockSpec((c, KV_DIM, 1), blk3),
            pl.BlockSpec((c, KV_DIM, 1), blk3),
            pl.BlockSpec((c, KV_DIM, WINDOW), blk3),
            pl.BlockSpec((c, KV_DIM, WINDOW), blk3),
            pl.BlockSpec((N_HEADS, 1), lambda i: (0, 0)),
        ],
        out_specs=[
            pl.BlockSpec((c, N_HEADS, LANES), blk3),
            pl.BlockSpec((c, KV_DIM, WINDOW), blk3),
            pl.BlockSpec((c, KV_DIM, WINDOW), blk3),
        ],
        out_shape=[
            jax.ShapeDtypeStruct((n, N_HEADS, LANES), F32),
            jax.ShapeDtypeStruct((n, KV_DIM, WINDOW), F32),
            jax.ShapeDtypeStruct((n, KV_DIM, WINDOW), F32),
        ],
        compiler_params=pltpu.CompilerParams(
            dimension_semantics=("arbitrary",), vmem_limit_bytes=VMEM_LIMIT),
        name="attn_sample",
    )(q8, knew, vnew, kcol, vcol, khist, vhist, sinks_col)


def _tail_sample_body(x_ref, attn_ref, proj_ref, uprev2_ref, uprev1_ref, convw_ref, gattn_ref,
                      gconv_ref, wout_ref, g2_ref, wr_ref, br_ref,
                      x1_ref, xn2_ref, gates_ref, u_ref):
    gate_b = proj_ref[:, OFF_B:OFF_C]
    u = proj_ref[:, OFF_C:OFF_H] * proj_ref[:, OFF_H:IN_PROJ_DIM]
    cw = convw_ref[...]
    z = cw[0:1, :] * uprev2_ref[...] + cw[1:2, :] * uprev1_ref[...] + cw[2:3, :] * u
    u_ref[...] = u
    x1, xn2_bf, gates = _mix_tail(x_ref[...], attn_ref[...], gate_b * z, gattn_ref[...], gconv_ref[...],
                                  wout_ref[...], g2_ref[...], wr_ref[...], br_ref[...])
    x1_ref[...] = x1
    xn2_ref[...] = xn2_bf
    gates_ref[...] = gates


def _tail_sample(x, attn, proj, uprev2, uprev1, convw, gattn, gconv, wout, g2, wr, br):
    n = x.shape[0]
    return pl.pallas_call(
        _tail_sample_body,
        out_shape=[
            jax.ShapeDtypeStruct((n, D_MODEL), F32),
            jax.ShapeDtypeStruct((n, D_MODEL), BF16),
            jax.ShapeDtypeStruct((n, ROUTER_LANES), F32),
            jax.ShapeDtypeStruct((n, CONV_DIM), F32),
        ],
        compiler_params=pltpu.CompilerParams(vmem_limit_bytes=VMEM_LIMIT),
        name="tail_sample",
    )(x, attn, proj, uprev2, uprev1, convw, gattn, gconv, wout, g2, wr, br)


def _pair_heads(w, axis):
    shape = w.shape
    n_pair = N_HEADS // 2
    split = shape[:axis] + (2, n_pair, HEAD_DIM) + shape[axis + 1:]
    return jnp.swapaxes(w.reshape(split), axis, axis + 1).reshape(shape)


def kernel(x_prompt, x_sample, cache_k_win, cache_v_win, state_conv, norm1_g, w_in, conv_w, sinks,
           attn_out_g, conv_out_g, w_out, norm2_g, w_group, b_group, w_fine, b_fine, w_gate, w_up,
           w_down, final_g):
    nb, seq, _ = x_prompt.shape
    nd = x_sample.shape[0]

    w_in0 = w_in[0].astype(BF16)
    win = jnp.concatenate([_pair_heads(w_in0[:, :ATTN_DIM], 1), w_in0[:, ATTN_DIM:]], axis=1)
    w_out0 = w_out[0].astype(BF16)
    wout = jnp.concatenate([_pair_heads(w_out0[:ATTN_DIM], 0), w_out0[ATTN_DIM:]], axis=0)
    gattn = _pair_heads(attn_out_g[0], 0)[None]
    gconv = conv_out_g[0][None]
    g1 = norm1_g[0][None]
    g2 = norm2_g[0][None]
    gf = final_g[None]
    convw = conv_w[0]
    sinks0 = sinks[0]
    pad = ROUTER_LANES - N_EXPERTS - N_GROUPS
    wr32 = jnp.concatenate([w_fine[0], w_group[0], jnp.zeros((D_MODEL, pad), F32)], axis=1)
    wr_hi = wr32.astype(BF16)
    wr = jnp.concatenate([wr_hi, (wr32 - wr_hi.astype(F32)).astype(BF16)], axis=1)
    br = jnp.concatenate([b_fine[0], b_group[0], jnp.zeros((pad,), F32)])[None]

    x1p, xn2p, gatesp, kp, vp, up, wg, wu, wd = _layer_prompt(
        x_prompt, sinks0, g1, win, convw, gattn, gconv, wout, g2, wr, br, w_gate[0], w_up[0], w_down[0])
    wd = wd.reshape(N_GROUPS, (N_EXPERTS // N_GROUPS) * D_EXPERT, D_MODEL)
    n_p = nb * seq
    y_prompt = _moe(x1p.reshape(n_p, D_MODEL), xn2p.reshape(n_p, D_MODEL),
                    gatesp.reshape(n_p, ROUTER_LANES), wg, wu, wd, gf, T_MOE, MOE_CAP, MOE_XC).reshape(nb, seq, D_MODEL)

    xs = x_sample.reshape(nd, D_MODEL)
    proj = _proj_sample(xs, g1, win)
    lane = jnp.arange(LANES)
    qp = (proj[:, :ATTN_DIM] * (HEAD_DIM ** -0.5)).reshape(nd, N_HEADS // 2, LANES)
    q8 = jnp.concatenate([jnp.where(lane < HEAD_DIM, qp, 0.0), jnp.where(lane >= HEAD_DIM, qp, 0.0)],
                         axis=1).astype(BF16)
    knew = proj[:, OFF_K:OFF_V].reshape(nd, 1, KV_DIM)
    vnew = proj[:, OFF_V:OFF_B].reshape(nd, 1, KV_DIM)
    to_cm = lambda c: jnp.transpose(c[0], (0, 2, 3, 1)).reshape(nd, KV_DIM, WINDOW)
    from_cm = lambda c, n: jnp.transpose(c.reshape(n, KV_DIM // HEAD_DIM, HEAD_DIM, WINDOW), (0, 3, 1, 2))[None]
    out8, ks, vs = _attn_sample(q8, knew, vnew, knew.reshape(nd, KV_DIM, 1), vnew.reshape(nd, KV_DIM, 1),
                                to_cm(cache_k_win), to_cm(cache_v_win), sinks0[:, None])
    attn_s = jnp.where(lane < HEAD_DIM, out8[:, :N_HEADS // 2], out8[:, N_HEADS // 2:]).reshape(nd, ATTN_DIM)
    st = state_conv[0]
    x1s, xn2s, gatess, us = _tail_sample(xs, attn_s, proj, st[:, 0], st[:, 1], convw, gattn, gconv,
                                         wout, g2, wr, br)
    y_sample = _moe(x1s, xn2s, gatess, wg, wu, wd, gf, nd, MOE_CAP_DEC, MOE_XC_DEC).reshape(nd, 1, D_MODEL)

    return (y_prompt, y_sample,
            from_cm(kp, nb), from_cm(vp, nb),
            up[None],
            from_cm(ks, nd), from_cm(vs, nd),
            jnp.stack([st[:, 1], us], axis=1)[None])
```

```python
import functools

import jax
import jax.numpy as jnp
from jax import lax
from jax.experimental import pallas as pl
from jax.experimental.pallas import tpu as pltpu

D_MODEL = 1024
HEAD_DIM = 64
N_HEADS = 8
ATTN_DIM = 512
KV_DIM = 128
WINDOW = 128
CONV_DIM = 512
CONV_WIDTH = 3
OFF_K = 512
OFF_V = 640
OFF_B = 768
OFF_C = 1280
OFF_H = 1792
IN_PROJ_DIM = 2304
N_GROUPS = 4
N_EXPERTS = 16
D_EXPERT = 256
RMS_EPS = 1e-5
NEG_INF = -1e30

LANES = 128
ROW_ALIGN = 16
ROUTER_LANES = 128
ROUTER_ROWS = 32
COARSE_OFF = N_EXPERTS
T_LAYER = 1024
N_SUB = 2
GID_LANE = N_EXPERTS
T_MOE = 512
MOE_CAP, MOE_XC = 176, 64
MOE_CAP_DEC, MOE_XC_DEC = 64, 32
DEC_CHUNK = 32
VMEM_LIMIT = 48 * 1024 * 1024
MOE_VMEM_LIMIT = 56 * 1024 * 1024

BF16 = jnp.bfloat16
F32 = jnp.float32


def _dot(a, b):
    return jnp.dot(a, b, preferred_element_type=F32)


def _rms(x, g):
    ms = jnp.mean(x * x, axis=-1, keepdims=True)
    return x * lax.rsqrt(ms + RMS_EPS) * g


def _group_norm64(y, g):
    rows, cols = y.shape
    lo = lax.broadcasted_iota(jnp.int32, (rows, LANES), 1) < HEAD_DIM
    outs = []
    for c in range(cols // LANES):
        blk = y[:, c * LANES:(c + 1) * LANES]
        sq = blk * blk
        s_lo = jnp.sum(jnp.where(lo, sq, 0.0), axis=-1, keepdims=True)
        s_hi = jnp.sum(jnp.where(lo, 0.0, sq), axis=-1, keepdims=True)
        ms = jnp.where(lo, s_lo, s_hi) * (1.0 / HEAD_DIM)
        outs.append(blk * lax.rsqrt(ms + RMS_EPS))
    return jnp.concatenate(outs, axis=-1) * g


def _router(xn2, wr, br):
    hi = xn2.astype(BF16)
    both = _dot(hi, wr)
    logits = jnp.transpose(both[:, :ROUTER_LANES] + both[:, ROUTER_LANES:])[:ROUTER_ROWS] + br
    n = logits.shape[1]
    row = lax.broadcasted_iota(jnp.int32, (ROUTER_ROWS, n), 0)
    rowf = row.astype(F32)
    big = float(ROUTER_ROWS)
    coarse = (row >= COARSE_OFF) & (row < COARSE_OFF + N_GROUPS)
    cm = jnp.max(jnp.where(coarse, logits, -jnp.inf), axis=0, keepdims=True)
    g_sel = jnp.min(jnp.where(coarse & (logits == cm), rowf - COARSE_OFF, big), axis=0, keepdims=True)
    zc = jnp.sum(jnp.where(coarse, jnp.exp(logits - cm), 0.0), axis=0, keepdims=True)
    p_sel = 1.0 / zc
    per_group = N_EXPERTS // N_GROUPS
    fine = (row < N_EXPERTS) & ((row // per_group).astype(F32) == g_sel)
    f1 = jnp.max(jnp.where(fine, logits, -jnp.inf), axis=0, keepdims=True)
    i1 = jnp.min(jnp.where(fine & (logits == f1), rowf, big), axis=0, keepdims=True)
    rest = fine & (rowf != i1)
    f2 = jnp.max(jnp.where(rest, logits, -jnp.inf), axis=0, keepdims=True)
    i2 = jnp.min(jnp.where(rest & (logits == f2), rowf, big), axis=0, keepdims=True)
    e2 = jnp.exp(f2 - f1)
    t1 = 1.0 / (1.0 + e2)
    t2 = e2 * t1
    gates_t = jnp.where(rowf == i1, p_sel * t1, jnp.where(rowf == i2, p_sel * t2, 0.0))
    gates_t = jnp.where(row == GID_LANE, g_sel, gates_t)
    gates_t = jnp.concatenate([gates_t, jnp.zeros((ROUTER_LANES - ROUTER_ROWS, n), F32)], axis=0)
    return hi, jnp.transpose(gates_t)


def _mix_tail(x, attn, conv_out, gattn, gconv, wout, g2, wr, br):
    mixed = jnp.concatenate([_group_norm64(attn, gattn), _group_norm64(conv_out, gconv)],
                            axis=-1).astype(BF16)
    x1 = x + _dot(mixed, wout)
    xn2_bf, gates = _router(_rms(x1, g2), wr, br)
    return x1, xn2_bf, gates


def _layer_prompt_body(sinks_ref, x_ref, g1_ref, win_ref, convw_ref, gattn_ref, gconv_ref,
                       wout_ref, g2_ref, wr_ref, br_ref, wg32_ref, wu32_ref, wd32_ref,
                       x1_ref, xn2_ref, gates_ref, knew_ref, vnew_ref, unew_ref,
                       wg16_ref, wu16_ref, wd16_ref,
                       q_s, kcat_s, vcat_s, attn_s, ucarry_s):
    i = pl.program_id(1)
    wg16_ref[...] = wg32_ref[...].astype(BF16)
    wu16_ref[...] = wu32_ref[...].astype(BF16)
    wd16_ref[...] = wd32_ref[...].astype(BF16)
    t = T_LAYER // N_SUB
    last = N_SUB - 1

    @pl.when(i == 0)
    def _():
        kcat_s[last, t:t + WINDOW, :] = jnp.zeros((WINDOW, KV_DIM), BF16)
        vcat_s[last, t:t + WINDOW, :] = jnp.zeros((WINDOW, KV_DIM), BF16)
        ucarry_s[last] = jnp.zeros((8, CONV_DIM), F32)

    a_idx = lax.broadcasted_iota(jnp.int32, (WINDOW, 2 * WINDOW), 0)
    c_idx = lax.broadcasted_iota(jnp.int32, (WINDOW, 2 * WINDOW), 1)
    lane_lo = lax.broadcasted_iota(jnp.int32, (WINDOW, LANES), 1) < HEAD_DIM
    row = lax.broadcasted_iota(jnp.int32, (t, CONV_DIM), 0)
    n_pair = N_HEADS // 2
    cw = convw_ref[...]

    for h in range(N_SUB):
        src = (h - 1) % N_SUB
        rows = pl.ds(h * t, t)
        x = x_ref[0, rows, :]
        xn = _rms(x, g1_ref[...]).astype(BF16)

        kcat_s[h, 0:WINDOW, :] = kcat_s[src, t:t + WINDOW, :]
        vcat_s[h, 0:WINDOW, :] = vcat_s[src, t:t + WINDOW, :]
        q_s[h] = (_dot(xn, win_ref[:, 0:OFF_K]) * (HEAD_DIM ** -0.5)).astype(BF16)
        kv = _dot(xn, win_ref[:, OFF_K:OFF_B])
        k = kv[:, :KV_DIM]
        v = kv[:, KV_DIM:]
        kcat_s[h, WINDOW:, :] = k.astype(BF16)
        vcat_s[h, WINDOW:, :] = v.astype(BF16)
        if h == last:
            knew_ref[0] = jnp.transpose(k[t - WINDOW:, :])
            vnew_ref[0] = jnp.transpose(v[t - WINDOW:, :])

        for j in range(t // WINDOW):
            r0 = j * WINDOW
            kk = kcat_s[h, r0:r0 + 2 * WINDOW, :]
            vv = vcat_s[h, r0:r0 + 2 * WINDOW, :]
            c_min = jnp.where(i == 0, WINDOW, 0) if (h == 0 and j == 0) else 0
            valid = (c_idx >= jnp.maximum(a_idx, c_min)) & (c_idx <= a_idx + WINDOW)
            qcols = [q_s[h, r0:r0 + WINDOW, m * LANES:(m + 1) * LANES] for m in range(n_pair)]
            outs = []
            for half in range(2):
                keep = lane_lo if half == 0 else jnp.logical_not(lane_lo)
                qg = jnp.concatenate([jnp.where(keep, qc, jnp.zeros_like(qc)) for qc in qcols], axis=0)
                s = lax.dot_general(qg, kk, (((1,), (1,)), ((), ())), preferred_element_type=F32)
                es, dens = [], []
                for m in range(n_pair):
                    sm = jnp.where(valid, s[m * WINDOW:(m + 1) * WINDOW], NEG_INF)
                    sink = sinks_ref[m + half * n_pair]
                    mx = jnp.maximum(jnp.max(sm, axis=-1, keepdims=True), sink)
                    e = jnp.exp(sm - mx)
                    dens.append(jnp.sum(e, axis=-1, keepdims=True) + jnp.exp(sink - mx))
                    es.append(e.astype(BF16))
                o = _dot(jnp.concatenate(es, axis=0), vv)
                outs.append([o[m * WINDOW:(m + 1) * WINDOW] / dens[m] for m in range(n_pair)])
            for m in range(n_pair):
                attn_s[h, r0:r0 + WINDOW, m * LANES:(m + 1) * LANES] = jnp.where(lane_lo, outs[0][m], outs[1][m])

        gate_b = _dot(xn, win_ref[:, OFF_B:OFF_C])
        u = _dot(xn, win_ref[:, OFF_C:OFF_H]) * _dot(xn, win_ref[:, OFF_H:IN_PROJ_DIM])
        prev = ucarry_s[src]
        u1 = jnp.where(row == 0, prev[7:8, :], pltpu.roll(u, 1, axis=0))
        u2 = jnp.where(row == 0, prev[6:7, :], jnp.where(row == 1, prev[7:8, :], pltpu.roll(u, 2, axis=0)))
        z = cw[0:1, :] * u2 + cw[1:2, :] * u1 + cw[2:3, :] * u
        ucarry_s[h] = u[t - 8:, :]
        if h == last:
            unew_ref[0] = u[t - (CONV_WIDTH - 1):, :]

        x1, xn2_bf, gates = _mix_tail(x, attn_s[h], gate_b * z, gattn_ref[...], gconv_ref[...],
                                      wout_ref[...], g2_ref[...], wr_ref[...], br_ref[...])
        x1_ref[0, rows, :] = x1
        xn2_ref[0, rows, :] = xn2_bf
        gates_ref[0, rows, :] = gates


def _layer_prompt(x, sinks, g1, win, convw, gattn, gconv, wout, g2, wr, br, wg32, wu32, wd32):
    nb, seq, _ = x.shape
    t = T_LAYER
    ts = T_LAYER // N_SUB
    n_i = seq // t
    assert nb * n_i == N_EXPERTS, "one expert's weights are cast per grid step"
    const = lambda b, i, s: (0, 0)
    tile = lambda b, i, s: (b, i, 0)
    per_seq = lambda b, i, s: (b, 0, 0)
    per_step = lambda b, i, s: (b * n_i + i, 0, 0)
    resident = pl.Buffered(1)
    grid_spec = pltpu.PrefetchScalarGridSpec(
        num_scalar_prefetch=1,
        grid=(nb, seq // t),
        in_specs=[
            pl.BlockSpec((1, t, D_MODEL), tile),
            pl.BlockSpec((1, D_MODEL), const),
            pl.BlockSpec((D_MODEL, IN_PROJ_DIM), const, pipeline_mode=resident),
            pl.BlockSpec((CONV_WIDTH, CONV_DIM), const),
            pl.BlockSpec((1, ATTN_DIM), const),
            pl.BlockSpec((1, CONV_DIM), const),
            pl.BlockSpec((D_MODEL, D_MODEL), const, pipeline_mode=resident),
            pl.BlockSpec((1, D_MODEL), const),
            pl.BlockSpec((D_MODEL, 2 * ROUTER_LANES), const, pipeline_mode=resident),
            pl.BlockSpec((ROUTER_ROWS, 1), const),
            pl.BlockSpec((1, D_MODEL, D_EXPERT), per_step),
            pl.BlockSpec((1, D_MODEL, D_EXPERT), per_step),
            pl.BlockSpec((1, D_EXPERT, D_MODEL), per_step),
        ],
        out_specs=[
            pl.BlockSpec((1, t, D_MODEL), tile),
            pl.BlockSpec((1, t, D_MODEL), tile),
            pl.BlockSpec((1, t, ROUTER_LANES), tile),
            pl.BlockSpec((1, KV_DIM, WINDOW), per_seq),
            pl.BlockSpec((1, KV_DIM, WINDOW), per_seq),
            pl.BlockSpec((1, CONV_WIDTH - 1, CONV_DIM), per_seq),
            pl.BlockSpec((1, D_MODEL, D_EXPERT), per_step),
            pl.BlockSpec((1, D_MODEL, D_EXPERT), per_step),
            pl.BlockSpec((1, D_EXPERT, D_MODEL), per_step),
        ],
        scratch_shapes=[
            pltpu.VMEM((N_SUB, ts, ATTN_DIM), BF16),
            pltpu.VMEM((N_SUB, ts + WINDOW, KV_DIM), BF16),
            pltpu.VMEM((N_SUB, ts + WINDOW, KV_DIM), BF16),
            pltpu.VMEM((N_SUB, ts, ATTN_DIM), F32),
            pltpu.VMEM((N_SUB, 8, CONV_DIM), F32),
        ],
    )
    return pl.pallas_call(
        _layer_prompt_body,
        grid_spec=grid_spec,
        out_shape=[
            jax.ShapeDtypeStruct((nb, seq, D_MODEL), F32),
            jax.ShapeDtypeStruct((nb, seq, D_MODEL), BF16),
            jax.ShapeDtypeStruct((nb, seq, ROUTER_LANES), F32),
            jax.ShapeDtypeStruct((nb, KV_DIM, WINDOW), F32),
            jax.ShapeDtypeStruct((nb, KV_DIM, WINDOW), F32),
            jax.ShapeDtypeStruct((nb, CONV_WIDTH - 1, CONV_DIM), F32),
            jax.ShapeDtypeStruct((N_EXPERTS, D_MODEL, D_EXPERT), BF16),
            jax.ShapeDtypeStruct((N_EXPERTS, D_MODEL, D_EXPERT), BF16),
            jax.ShapeDtypeStruct((N_EXPERTS, D_EXPERT, D_MODEL), BF16),
        ],
        compiler_params=pltpu.CompilerParams(
            dimension_semantics=("arbitrary", "arbitrary"), vmem_limit_bytes=MOE_VMEM_LIMIT),
        name="layer_prompt",
    )(sinks, x, g1, win, convw, gattn, gconv, wout, g2, wr, br, wg32, wu32, wd32)


def _moe_body(cap, xc, x1_ref, xn2_ref, gates_ref, wg_ref, wu_ref, wd_ref, gf_ref, y_ref,
              xs_s, gs_s, ys_s):
    t = x1_ref.shape[0]
    per_group = N_EXPERTS // N_GROUPS
    gates = gates_ref[...]
    lane = lax.broadcasted_iota(jnp.int32, (t, ROUTER_LANES), 1)
    gid = jnp.sum(jnp.where(lane == GID_LANE, gates, 0.0), axis=-1, keepdims=True)
    onehot = jnp.where((lane < N_GROUPS) & (lane.astype(F32) == gid), 1.0, 0.0)

    r_idx = lax.broadcasted_iota(jnp.int32, (t, t), 0)
    c_idx = lax.broadcasted_iota(jnp.int32, (t, t), 1)
    lower = jnp.where(c_idx < r_idx, 1.0, 0.0).astype(BF16)
    prefix = _dot(lower, onehot.astype(BF16))
    rank = jnp.sum(prefix * onehot, axis=-1, keepdims=True)
    counts = jnp.sum(onehot, axis=0, keepdims=True)
    lane1 = lax.broadcasted_iota(jnp.int32, (1, ROUTER_LANES), 1)
    cnt = [jnp.sum(jnp.where(lane1 == g, counts, 0.0)).astype(jnp.int32) for g in range(N_GROUPS)]
    off = [jnp.int32(0)]
    for g in range(1, N_GROUPS):
        off.append(off[-1] + cnt[g - 1])

    pos_col = rank
    for g in range(1, N_GROUPS):
        pos_col = pos_col + jnp.where(gid == float(g), off[g].astype(F32), 0.0)
    pos_row = jnp.transpose(jnp.broadcast_to(pos_col, (t, LANES)))[0:1, :]
    perm = jnp.where(r_idx.astype(F32) == pos_row, 1.0, 0.0).astype(BF16)
    perm_t = jnp.where(c_idx.astype(F32) == pos_col, 1.0, 0.0).astype(BF16)

    xs_s[0:t, :] = _dot(perm, xn2_ref[...]).astype(BF16)
    p1 = gates.astype(BF16)
    r1 = gates - p1.astype(F32)
    p2 = r1.astype(BF16)
    p3 = (r1 - p2.astype(F32)).astype(BF16)
    gs3 = _dot(perm, jnp.concatenate([p1, p2, p3], axis=1))
    gs_s[0:t, :] = (gs3[:, :ROUTER_LANES] + gs3[:, ROUTER_LANES:2 * ROUTER_LANES]) + gs3[:, 2 * ROUTER_LANES:]
    pad = xs_s.shape[0] - t
    xs_s[t:, :] = jnp.zeros((pad, D_MODEL), BF16)
    gs_s[t:, :] = jnp.zeros((pad, ROUTER_LANES), F32)
    ys_s[...] = jnp.zeros(ys_s.shape, F32)

    def group_rows(g, r0, rows):
        xs = xs_s[pl.ds(r0, rows), :]
        gsc = gs_s[pl.ds(r0, rows), :]
        acts = []
        for k in range(per_group):
            e = g * per_group + k
            hg = _dot(xs, wg_ref[e])
            hu = _dot(xs, wu_ref[e])
            acts.append((hg / (1.0 + jnp.exp(-hg)) * hu * gsc[:, e:e + 1]).astype(BF16))
        ys_s[pl.ds(r0, rows), :] += _dot(jnp.concatenate(acts, axis=1), wd_ref[g])

    base = [(off[g] // ROW_ALIGN) * ROW_ALIGN for g in range(N_GROUPS)]
    for g in range(N_GROUPS):
        group_rows(g, pl.multiple_of(base[g], ROW_ALIGN), cap)
    for g in range(N_GROUPS):
        n_extra = jnp.maximum(0, (off[g] + cnt[g] - (base[g] + cap) + xc - 1) // xc)

        def extra(k, carry, g=g):
            group_rows(g, pl.multiple_of(base[g] + cap + k * xc, ROW_ALIGN), xc)
            return carry

        lax.fori_loop(0, n_extra, extra, 0)

    moe = _dot(perm_t, ys_s[0:t, :].astype(BF16))
    y_ref[...] = _rms(x1_ref[...] + moe, gf_ref[...])


def _moe(x1, xn2, route, wg, wu, wd, gf, t, cap, xc):
    n = x1.shape[0]
    tile = lambda i: (i, 0)
    whole = lambda i: (0, 0, 0)
    resident = pl.Buffered(1)
    rows = t + cap + xc
    hidden = (N_EXPERTS // N_GROUPS) * D_EXPERT
    return pl.pallas_call(
        functools.partial(_moe_body, cap, xc),
        grid=(n // t,),
        in_specs=[
            pl.BlockSpec((t, D_MODEL), tile),
            pl.BlockSpec((t, D_MODEL), tile),
            pl.BlockSpec((t, ROUTER_LANES), tile),
            pl.BlockSpec((N_EXPERTS, D_MODEL, D_EXPERT), whole, pipeline_mode=resident),
            pl.BlockSpec((N_EXPERTS, D_MODEL, D_EXPERT), whole, pipeline_mode=resident),
            pl.BlockSpec((N_GROUPS, hidden, D_MODEL), whole, pipeline_mode=resident),
            pl.BlockSpec((1, D_MODEL), lambda i: (0, 0)),
        ],
        out_specs=pl.BlockSpec((t, D_MODEL), tile),
        out_shape=jax.ShapeDtypeStruct((n, D_MODEL), F32),
        scratch_shapes=[
            pltpu.VMEM((rows, D_MODEL), BF16),
            pltpu.VMEM((rows, ROUTER_LANES), F32),
            pltpu.VMEM((rows, D_MODEL), F32),
        ],
        compiler_params=pltpu.CompilerParams(
            dimension_semantics=("arbitrary",), vmem_limit_bytes=MOE_VMEM_LIMIT),
        name="moe",
    )(x1, xn2, route, wg, wu, wd, gf)


def _proj_sample_body(x_ref, g1_ref, win_ref, proj_ref, kvt_ref):
    xn = _rms(x_ref[...], g1_ref[...]).astype(BF16)
    proj_ref[...] = _dot(xn, win_ref[...])
    kvt_ref[...] = lax.dot_general(win_ref[:, OFF_K:OFF_B], xn, (((0,), (1,)), ((), ())),
                                   preferred_element_type=F32)


def _proj_sample(x, g1, win):
    n = x.shape[0]
    return pl.pallas_call(
        _proj_sample_body,
        out_shape=[jax.ShapeDtypeStruct((n, IN_PROJ_DIM), F32),
                   jax.ShapeDtypeStruct((2 * KV_DIM, n), F32)],
        compiler_params=pltpu.CompilerParams(vmem_limit_bytes=VMEM_LIMIT),
        name="proj_sample",
    )(x, g1, win)


def _attn_sample_body(q8_ref, knew_ref, vnew_ref, kvt_ref, khist_ref, vhist_ref, sinks_ref,
                      out8_ref, kout_ref, vout_ref):
    chunk = q8_ref.shape[0]
    first = pl.program_id(0) * chunk
    q8 = q8_ref[...]
    kh = khist_ref[...]
    vh = vhist_ref[...]
    knew = knew_ref[...]
    vnew = vnew_ref[...]
    s = jnp.einsum('bhj,bjc->bhc', q8, kh.astype(BF16), preferred_element_type=F32)
    s_new = jnp.sum(q8.astype(F32) * knew, axis=-1, keepdims=True)
    sink = sinks_ref[...][None]
    mx = jnp.maximum(jnp.maximum(jnp.max(s, axis=-1, keepdims=True), s_new), sink)
    e = jnp.exp(s - mx)
    e_new = jnp.exp(s_new - mx)
    den = jnp.sum(e, axis=-1, keepdims=True) + e_new + jnp.exp(sink - mx)
    o = jnp.einsum('bhc,bjc->bhj', e.astype(BF16), vh.astype(BF16), preferred_element_type=F32)
    out8_ref[...] = (o + e_new * vnew) / den
    last = lax.broadcasted_iota(jnp.int32, (KV_DIM, WINDOW), 1) == WINDOW - 1
    knew_t = kvt_ref[0:KV_DIM, :]
    vnew_t = kvt_ref[KV_DIM:, :]
    for bb in range(chunk):
        shift = lax.rem(2 * WINDOW - 1 - (first + bb), WINDOW)
        kout_ref[bb] = jnp.where(last, pltpu.roll(knew_t, shift, axis=1), pltpu.roll(kh[bb], WINDOW - 1, axis=1))
        vout_ref[bb] = jnp.where(last, pltpu.roll(vnew_t, shift, axis=1), pltpu.roll(vh[bb], WINDOW - 1, axis=1))


def _attn_sample(q8, knew, vnew, kvt, khist, vhist, sinks_col):
    assert q8.shape[0] == WINDOW == LANES, "one lane per sequence in the channel-major projection"
    n = q8.shape[0]
    c = DEC_CHUNK
    blk3 = lambda i: (i, 0, 0)
    return pl.pallas_call(
        _attn_sample_body,
        grid=(n // c,),
        in_specs=[
            pl.BlockSpec((c, N_HEADS, LANES), blk3),
            pl.BlockSpec((c, 1, KV_DIM), blk3),
            pl.BlockSpec((c, 1, KV_DIM), blk3),
            pl.BlockSpec((2 * KV_DIM, n), lambda i: (0, 0)),
            pl.BlockSpec((c, KV_DIM, WINDOW), blk3),
            pl.BlockSpec((c, KV_DIM, WINDOW), blk3),
            pl.BlockSpec((N_HEADS, 1), lambda i: (0, 0)),
        ],
        out_specs=[
            pl.BlockSpec((c, N_HEADS, LANES), blk3),
            pl.BlockSpec((c, KV_DIM, WINDOW), blk3),
            pl.BlockSpec((c, KV_DIM, WINDOW), blk3),
        ],
        out_shape=[
            jax.ShapeDtypeStruct((n, N_HEADS, LANES), F32),
            jax.ShapeDtypeStruct((n, KV_DIM, WINDOW), F32),
            jax.ShapeDtypeStruct((n, KV_DIM, WINDOW), F32),
        ],
        compiler_params=pltpu.CompilerParams(
            dimension_semantics=("arbitrary",), vmem_limit_bytes=VMEM_LIMIT),
        name="attn_sample",
    )(q8, knew, vnew, kvt, khist, vhist, sinks_col)


def _tail_sample_body(x_ref, attn_ref, proj_ref, uprev2_ref, uprev1_ref, convw_ref, gattn_ref,
                      gconv_ref, wout_ref, g2_ref, wr_ref, br_ref,
                      x1_ref, xn2_ref, gates_ref, u_ref):
    gate_b = proj_ref[:, OFF_B:OFF_C]
    u = proj_ref[:, OFF_C:OFF_H] * proj_ref[:, OFF_H:IN_PROJ_DIM]
    cw = convw_ref[...]
    z = cw[0:1, :] * uprev2_ref[...] + cw[1:2, :] * uprev1_ref[...] + cw[2:3, :] * u
    u_ref[...] = u
    x1, xn2_bf, gates = _mix_tail(x_ref[...], attn_ref[...], gate_b * z, gattn_ref[...], gconv_ref[...],
                                  wout_ref[...], g2_ref[...], wr_ref[...], br_ref[...])
    x1_ref[...] = x1
    xn2_ref[...] = xn2_bf
    gates_ref[...] = gates


def _tail_sample(x, attn, proj, uprev2, uprev1, convw, gattn, gconv, wout, g2, wr, br):
    n = x.shape[0]
    return pl.pallas_call(
        _tail_sample_body,
        out_shape=[
            jax.ShapeDtypeStruct((n, D_MODEL), F32),
            jax.ShapeDtypeStruct((n, D_MODEL), BF16),
            jax.ShapeDtypeStruct((n, ROUTER_LANES), F32),
            jax.ShapeDtypeStruct((n, CONV_DIM), F32),
        ],
        compiler_params=pltpu.CompilerParams(vmem_limit_bytes=VMEM_LIMIT),
        name="tail_sample",
    )(x, attn, proj, uprev2, uprev1, convw, gattn, gconv, wout, g2, wr, br)


def _pair_heads(w, axis):
    shape = w.shape
    n_pair = N_HEADS // 2
    split = shape[:axis] + (2, n_pair, HEAD_DIM) + shape[axis + 1:]
    return jnp.swapaxes(w.reshape(split), axis, axis + 1).reshape(shape)


def kernel(x_prompt, x_sample, cache_k_win, cache_v_win, state_conv, norm1_g, w_in, conv_w, sinks,
           attn_out_g, conv_out_g, w_out, norm2_g, w_group, b_group, w_fine, b_fine, w_gate, w_up,
           w_down, final_g):
    nb, seq, _ = x_prompt.shape
    nd = x_sample.shape[0]

    w_in0 = w_in[0].astype(BF16)
    win = jnp.concatenate([_pair_heads(w_in0[:, :ATTN_DIM], 1), w_in0[:, ATTN_DIM:]], axis=1)
    w_out0 = w_out[0].astype(BF16)
    wout = jnp.concatenate([_pair_heads(w_out0[:ATTN_DIM], 0), w_out0[ATTN_DIM:]], axis=0)
    gattn = _pair_heads(attn_out_g[0], 0)[None]
    gconv = conv_out_g[0][None]
    g1 = norm1_g[0][None]
    g2 = norm2_g[0][None]
    gf = final_g[None]
    convw = conv_w[0]
    sinks0 = sinks[0]
    pad = ROUTER_LANES - N_EXPERTS - N_GROUPS
    wr32 = jnp.concatenate([w_fine[0], w_group[0], jnp.zeros((D_MODEL, pad), F32)], axis=1)
    wr_hi = wr32.astype(BF16)
    wr = jnp.concatenate([wr_hi, (wr32 - wr_hi.astype(F32)).astype(BF16)], axis=1)
    br = jnp.concatenate([b_fine[0], b_group[0], jnp.zeros((pad,), F32)])[:ROUTER_ROWS, None]

    x1p, xn2p, gatesp, kp, vp, up, wg, wu, wd = _layer_prompt(
        x_prompt, sinks0, g1, win, convw, gattn, gconv, wout, g2, wr, br, w_gate[0], w_up[0], w_down[0])
    wd = wd.reshape(N_GROUPS, (N_EXPERTS // N_GROUPS) * D_EXPERT, D_MODEL)
    n_p = nb * seq
    y_prompt = _moe(x1p.reshape(n_p, D_MODEL), xn2p.reshape(n_p, D_MODEL),
                    gatesp.reshape(n_p, ROUTER_LANES), wg, wu, wd, gf, T_MOE, MOE_CAP, MOE_XC).reshape(nb, seq, D_MODEL)

    xs = x_sample.reshape(nd, D_MODEL)
    proj, kvt = _proj_sample(xs, g1, win)
    lane = jnp.arange(LANES)
    qp = (proj[:, :ATTN_DIM] * (HEAD_DIM ** -0.5)).reshape(nd, N_HEADS // 2, LANES)
    q8 = jnp.concatenate([jnp.where(lane < HEAD_DIM, qp, 0.0), jnp.where(lane >= HEAD_DIM, qp, 0.0)],
                         axis=1).astype(BF16)
    knew = proj[:, OFF_K:OFF_V].reshape(nd, 1, KV_DIM)
    vnew = proj[:, OFF_V:OFF_B].reshape(nd, 1, KV_DIM)
    to_cm = lambda c: jnp.transpose(c[0], (0, 2, 3, 1)).reshape(nd, KV_DIM, WINDOW)
    from_cm = lambda c, n: jnp.transpose(c.reshape(n, KV_DIM // HEAD_DIM, HEAD_DIM, WINDOW), (0, 3, 1, 2))[None]
    out8, ks, vs = _attn_sample(q8, knew, vnew, kvt, to_cm(cache_k_win), to_cm(cache_v_win), sinks0[:, None])
    attn_s = jnp.where(lane < HEAD_DIM, out8[:, :N_HEADS // 2], out8[:, N_HEADS // 2:]).reshape(nd, ATTN_DIM)
    st = state_conv[0]
    x1s, xn2s, gatess, us = _tail_sample(xs, attn_s, proj, st[:, 0], st[:, 1], convw, gattn, gconv,
                                         wout, g2, wr, br)
    y_sample = _moe(x1s, xn2s, gatess, wg, wu, wd, gf, nd, MOE_CAP_DEC, MOE_XC_DEC).reshape(nd, 1, D_MODEL)

    return (y_prompt, y_sample,
            from_cm(kp, nb), from_cm(vp, nb),
            up[None],
            from_cm(ks, nd), from_cm(vs, nd),
            jnp.stack([st[:, 1], us], axis=1)[None])
```

```python
import functools

import jax
import jax.numpy as jnp
from jax import lax
from jax.experimental import pallas as pl
from jax.experimental.pallas import tpu as pltpu

D_MODEL = 1024
HEAD_DIM = 64
N_HEADS = 8
ATTN_DIM = 512
KV_DIM = 128
WINDOW = 128
CONV_DIM = 512
CONV_WIDTH = 3
OFF_K = 512
OFF_V = 640
OFF_B = 768
OFF_C = 1280
OFF_H = 1792
IN_PROJ_DIM = 2304
N_GROUPS = 4
N_EXPERTS = 16
D_EXPERT = 256
RMS_EPS = 1e-5
NEG_INF = -1e30

LANES = 128
ROW_ALIGN = 16
ROUTER_LANES = 128
ROUTER_ROWS = 32
COARSE_OFF = N_EXPERTS
T_LAYER = 1024
N_SUB = 2
GID_LANE = N_EXPERTS
T_MOE = 512
MOE_CAP, MOE_XC = 176, 64
MOE_CAP_DEC, MOE_XC_DEC = 64, 32
DEC_CHUNK = 32
VMEM_LIMIT = 48 * 1024 * 1024
MOE_VMEM_LIMIT = 56 * 1024 * 1024

BF16 = jnp.bfloat16
F32 = jnp.float32


def _dot(a, b):
    return jnp.dot(a, b, preferred_element_type=F32)


def _rms(x, g):
    ms = jnp.mean(x * x, axis=-1, keepdims=True)
    return x * lax.rsqrt(ms + RMS_EPS) * g


def _group_norm64(y, g):
    rows, cols = y.shape
    lo = lax.broadcasted_iota(jnp.int32, (rows, LANES), 1) < HEAD_DIM
    outs = []
    for c in range(cols // LANES):
        blk = y[:, c * LANES:(c + 1) * LANES]
        sq = blk * blk
        s_lo = jnp.sum(jnp.where(lo, sq, 0.0), axis=-1, keepdims=True)
        s_hi = jnp.sum(jnp.where(lo, 0.0, sq), axis=-1, keepdims=True)
        ms = jnp.where(lo, s_lo, s_hi) * (1.0 / HEAD_DIM)
        outs.append(blk * lax.rsqrt(ms + RMS_EPS))
    return jnp.concatenate(outs, axis=-1) * g


def _router(xn2, wr, br):
    hi = xn2.astype(BF16)
    both = _dot(hi, wr)
    logits = jnp.transpose(both[:, :ROUTER_LANES] + both[:, ROUTER_LANES:])[:ROUTER_ROWS] + br
    n = logits.shape[1]
    row = lax.broadcasted_iota(jnp.int32, (ROUTER_ROWS, n), 0)
    rowf = row.astype(F32)
    big = float(ROUTER_ROWS)
    coarse = (row >= COARSE_OFF) & (row < COARSE_OFF + N_GROUPS)
    cm = jnp.max(jnp.where(coarse, logits, -jnp.inf), axis=0, keepdims=True)
    g_sel = jnp.min(jnp.where(coarse & (logits == cm), rowf - COARSE_OFF, big), axis=0, keepdims=True)
    zc = jnp.sum(jnp.where(coarse, jnp.exp(logits - cm), 0.0), axis=0, keepdims=True)
    p_sel = 1.0 / zc
    per_group = N_EXPERTS // N_GROUPS
    fine = (row < N_EXPERTS) & ((row // per_group).astype(F32) == g_sel)
    f1 = jnp.max(jnp.where(fine, logits, -jnp.inf), axis=0, keepdims=True)
    i1 = jnp.min(jnp.where(fine & (logits == f1), rowf, big), axis=0, keepdims=True)
    rest = fine & (rowf != i1)
    f2 = jnp.max(jnp.where(rest, logits, -jnp.inf), axis=0, keepdims=True)
    i2 = jnp.min(jnp.where(rest & (logits == f2), rowf, big), axis=0, keepdims=True)
    e2 = jnp.exp(f2 - f1)
    t1 = 1.0 / (1.0 + e2)
    t2 = e2 * t1
    gates_t = jnp.where(rowf == i1, p_sel * t1, jnp.where(rowf == i2, p_sel * t2, 0.0))
    gates_t = jnp.where(row == GID_LANE, g_sel, gates_t)
    gates_t = jnp.concatenate([gates_t, jnp.zeros((ROUTER_LANES - ROUTER_ROWS, n), F32)], axis=0)
    return hi, jnp.transpose(gates_t)


def _mix_tail(x, attn, conv_out, gattn, gconv, wout, g2, wr, br):
    mixed = jnp.concatenate([_group_norm64(attn, gattn), _group_norm64(conv_out, gconv)],
                            axis=-1).astype(BF16)
    x1 = x + _dot(mixed, wout)
    xn2_bf, gates = _router(_rms(x1, g2), wr, br)
    return x1, xn2_bf, gates


def _layer_prompt_body(sinks_ref, x_ref, g1_ref, win_ref, convw_ref, gattn_ref, gconv_ref,
                       wout_ref, g2_ref, wr_ref, br_ref, wg32_ref, wu32_ref, wd32_ref,
                       x1_ref, xn2_ref, gates_ref, knew_ref, vnew_ref, unew_ref,
                       wg16_ref, wu16_ref, wd16_ref,
                       q_s, kcat_s, vcat_s, attn_s, ucarry_s):
    i = pl.program_id(1)
    wg16_ref[...] = wg32_ref[...].astype(BF16)
    wu16_ref[...] = wu32_ref[...].astype(BF16)
    wd16_ref[...] = wd32_ref[...].astype(BF16)
    t = T_LAYER // N_SUB
    last = N_SUB - 1

    @pl.when(i == 0)
    def _():
        kcat_s[last, t:t + WINDOW, :] = jnp.zeros((WINDOW, KV_DIM), BF16)
        vcat_s[last, t:t + WINDOW, :] = jnp.zeros((WINDOW, KV_DIM), BF16)
        ucarry_s[last] = jnp.zeros((8, CONV_DIM), F32)

    a_idx = lax.broadcasted_iota(jnp.int32, (WINDOW, 2 * WINDOW), 0)
    c_idx = lax.broadcasted_iota(jnp.int32, (WINDOW, 2 * WINDOW), 1)
    lane_lo = lax.broadcasted_iota(jnp.int32, (WINDOW, LANES), 1) < HEAD_DIM
    row = lax.broadcasted_iota(jnp.int32, (t, CONV_DIM), 0)
    n_pair = N_HEADS // 2
    cw = convw_ref[...]

    for h in range(N_SUB):
        src = (h - 1) % N_SUB
        rows = pl.ds(h * t, t)
        x = x_ref[0, rows, :]
        xn = _rms(x, g1_ref[...]).astype(BF16)

        kcat_s[h, 0:WINDOW, :] = kcat_s[src, t:t + WINDOW, :]
        vcat_s[h, 0:WINDOW, :] = vcat_s[src, t:t + WINDOW, :]
        q_s[h] = (_dot(xn, win_ref[:, 0:OFF_K]) * (HEAD_DIM ** -0.5)).astype(BF16)
        kv = _dot(xn, win_ref[:, OFF_K:OFF_B])
        k = kv[:, :KV_DIM]
        v = kv[:, KV_DIM:]
        kcat_s[h, WINDOW:, :] = k.astype(BF16)
        vcat_s[h, WINDOW:, :] = v.astype(BF16)
        if h == last:
            knew_ref[0] = jnp.transpose(k[t - WINDOW:, :])
            vnew_ref[0] = jnp.transpose(v[t - WINDOW:, :])

        for j in range(t // WINDOW):
            r0 = j * WINDOW
            kk = kcat_s[h, r0:r0 + 2 * WINDOW, :]
            vv = vcat_s[h, r0:r0 + 2 * WINDOW, :]
            c_min = jnp.where(i == 0, WINDOW, 0) if (h == 0 and j == 0) else 0
            valid = (c_idx >= jnp.maximum(a_idx, c_min)) & (c_idx <= a_idx + WINDOW)
            qcols = [q_s[h, r0:r0 + WINDOW, m * LANES:(m + 1) * LANES] for m in range(n_pair)]
            outs = []
            for half in range(2):
                keep = lane_lo if half == 0 else jnp.logical_not(lane_lo)
                qg = jnp.concatenate([jnp.where(keep, qc, jnp.zeros_like(qc)) for qc in qcols], axis=0)
                s = lax.dot_general(qg, kk, (((1,), (1,)), ((), ())), preferred_element_type=F32)
                es, dens = [], []
                for m in range(n_pair):
                    sm = jnp.where(valid, s[m * WINDOW:(m + 1) * WINDOW], NEG_INF)
                    sink = sinks_ref[m + half * n_pair]
                    mx = jnp.maximum(jnp.max(sm, axis=-1, keepdims=True), sink)
                    e = jnp.exp(sm - mx)
                    dens.append(jnp.sum(e, axis=-1, keepdims=True) + jnp.exp(sink - mx))
                    es.append(e.astype(BF16))
                o = _dot(jnp.concatenate(es, axis=0), vv)
                outs.append([o[m * WINDOW:(m + 1) * WINDOW] / dens[m] for m in range(n_pair)])
            for m in range(n_pair):
                attn_s[h, r0:r0 + WINDOW, m * LANES:(m + 1) * LANES] = jnp.where(lane_lo, outs[0][m], outs[1][m])

        gate_b = _dot(xn, win_ref[:, OFF_B:OFF_C])
        u = _dot(xn, win_ref[:, OFF_C:OFF_H]) * _dot(xn, win_ref[:, OFF_H:IN_PROJ_DIM])
        prev = ucarry_s[src]
        u1 = jnp.where(row == 0, prev[7:8, :], pltpu.roll(u, 1, axis=0))
        u2 = jnp.where(row == 0, prev[6:7, :], jnp.where(row == 1, prev[7:8, :], pltpu.roll(u, 2, axis=0)))
        z = cw[0:1, :] * u2 + cw[1:2, :] * u1 + cw[2:3, :] * u
        ucarry_s[h] = u[t - 8:, :]
        if h == last:
            unew_ref[0] = u[t - (CONV_WIDTH - 1):, :]

        x1, xn2_bf, gates = _mix_tail(x, attn_s[h], gate_b * z, gattn_ref[...], gconv_ref[...],
                                      wout_ref[...], g2_ref[...], wr_ref[...], br_ref[...])
        x1_ref[0, rows, :] = x1
        xn2_ref[0, rows, :] = xn2_bf
        gates_ref[0, rows, :] = gates


def _layer_prompt(x, sinks, g1, win, convw, gattn, gconv, wout, g2, wr, br, wg32, wu32, wd32):
    nb, seq, _ = x.shape
    t = T_LAYER
    ts = T_LAYER // N_SUB
    n_i = seq // t
    assert nb * n_i == N_EXPERTS, "one expert's weights are cast per grid step"
    const = lambda b, i, s: (0, 0)
    tile = lambda b, i, s: (b, i, 0)
    per_seq = lambda b, i, s: (b, 0, 0)
    per_step = lambda b, i, s: (b * n_i + i, 0, 0)
    resident = pl.Buffered(1)
    grid_spec = pltpu.PrefetchScalarGridSpec(
        num_scalar_prefetch=1,
        grid=(nb, seq // t),
        in_specs=[
            pl.BlockSpec((1, t, D_MODEL), tile),
            pl.BlockSpec((1, D_MODEL), const),
            pl.BlockSpec((D_MODEL, IN_PROJ_DIM), const, pipeline_mode=resident),
            pl.BlockSpec((CONV_WIDTH, CONV_DIM), const),
            pl.BlockSpec((1, ATTN_DIM), const),
            pl.BlockSpec((1, CONV_DIM), const),
            pl.BlockSpec((D_MODEL, D_MODEL), const, pipeline_mode=resident),
            pl.BlockSpec((1, D_MODEL), const),
            pl.BlockSpec((D_MODEL, 2 * ROUTER_LANES), const, pipeline_mode=resident),
            pl.BlockSpec((ROUTER_ROWS, 1), const),
            pl.BlockSpec((1, D_MODEL, D_EXPERT), per_step),
            pl.BlockSpec((1, D_MODEL, D_EXPERT), per_step),
            pl.BlockSpec((1, D_EXPERT, D_MODEL), per_step),
        ],
        out_specs=[
            pl.BlockSpec((1, t, D_MODEL), tile),
            pl.BlockSpec((1, t, D_MODEL), tile),
            pl.BlockSpec((1, t, ROUTER_LANES), tile),
            pl.BlockSpec((1, KV_DIM, WINDOW), per_seq),
            pl.BlockSpec((1, KV_DIM, WINDOW), per_seq),
            pl.BlockSpec((1, CONV_WIDTH - 1, CONV_DIM), per_seq),
            pl.BlockSpec((1, D_MODEL, D_EXPERT), per_step),
            pl.BlockSpec((1, D_MODEL, D_EXPERT), per_step),
            pl.BlockSpec((1, D_EXPERT, D_MODEL), per_step),
        ],
        scratch_shapes=[
            pltpu.VMEM((N_SUB, ts, ATTN_DIM), BF16),
            pltpu.VMEM((N_SUB, ts + WINDOW, KV_DIM), BF16),
            pltpu.VMEM((N_SUB, ts + WINDOW, KV_DIM), BF16),
            pltpu.VMEM((N_SUB, ts, ATTN_DIM), F32),
            pltpu.VMEM((N_SUB, 8, CONV_DIM), F32),
        ],
    )
    return pl.pallas_call(
        _layer_prompt_body,
        grid_spec=grid_spec,
        out_shape=[
            jax.ShapeDtypeStruct((nb, seq, D_MODEL), F32),
            jax.ShapeDtypeStruct((nb, seq, D_MODEL), BF16),
            jax.ShapeDtypeStruct((nb, seq, ROUTER_LANES), F32),
            jax.ShapeDtypeStruct((nb, KV_DIM, WINDOW), F32),
            jax.ShapeDtypeStruct((nb, KV_DIM, WINDOW), F32),
            jax.ShapeDtypeStruct((nb, CONV_WIDTH - 1, CONV_DIM), F32),
            jax.ShapeDtypeStruct((N_EXPERTS, D_MODEL, D_EXPERT), BF16),
            jax.ShapeDtypeStruct((N_EXPERTS, D_MODEL, D_EXPERT), BF16),
            jax.ShapeDtypeStruct((N_EXPERTS, D_EXPERT, D_MODEL), BF16),
        ],
        compiler_params=pltpu.CompilerParams(
            dimension_semantics=("arbitrary", "arbitrary"), vmem_limit_bytes=MOE_VMEM_LIMIT),
        name="layer_prompt",
    )(sinks, x, g1, win, convw, gattn, gconv, wout, g2, wr, br, wg32, wu32, wd32)


def _moe_body(cap, xc, x1_ref, xn2_ref, gates_ref, wg_ref, wu_ref, wd_ref, gf_ref, y_ref,
              xs_s, gs_s, ys_s):
    t = x1_ref.shape[0]
    per_group = N_EXPERTS // N_GROUPS
    gates = gates_ref[...]
    gid_row = jnp.transpose(gates)[GID_LANE:GID_LANE + 1, :]
    grp = lax.broadcasted_iota(jnp.int32, (8, t), 0).astype(F32)
    onehot = jnp.where(grp == gid_row, 1.0, 0.0)

    tok = lax.broadcasted_iota(jnp.int32, (8, t), 1)
    csum = onehot
    shift = 1
    while shift < t:
        csum = csum + jnp.where(tok >= shift, pltpu.roll(csum, shift, axis=1), 0.0)
        shift *= 2
    rank_row = jnp.sum(onehot * (csum - 1.0), axis=0, keepdims=True)
    cnt = [csum[g, t - 1].astype(jnp.int32) for g in range(N_GROUPS)]
    off = [jnp.int32(0)]
    for g in range(1, N_GROUPS):
        off.append(off[-1] + cnt[g - 1])

    pos_row = rank_row
    for g in range(1, N_GROUPS):
        pos_row = pos_row + jnp.where(gid_row == float(g), off[g].astype(F32), 0.0)
    pos_col = jnp.transpose(jnp.broadcast_to(pos_row, (LANES, t)))
    pos_col = jnp.concatenate([pos_col] * (t // LANES), axis=1)
    r_idx = lax.broadcasted_iota(jnp.int32, (t, t), 0)
    c_idx = lax.broadcasted_iota(jnp.int32, (t, t), 1)
    perm = jnp.where(r_idx.astype(F32) == pos_row, 1.0, 0.0).astype(BF16)
    perm_t = jnp.where(c_idx.astype(F32) == pos_col, 1.0, 0.0).astype(BF16)

    p1 = gates.astype(BF16)
    p2 = (gates - p1.astype(F32)).astype(BF16)
    moved = _dot(perm, jnp.concatenate([xn2_ref[...], p1, p2], axis=1))
    xs_s[0:t, :] = moved[:, :D_MODEL].astype(BF16)
    gs_s[0:t, :] = moved[:, D_MODEL:D_MODEL + ROUTER_LANES] + moved[:, D_MODEL + ROUTER_LANES:]
    pad = xs_s.shape[0] - t
    xs_s[t:, :] = jnp.zeros((pad, D_MODEL), BF16)
    gs_s[t:, :] = jnp.zeros((pad, ROUTER_LANES), F32)
    ys_s[...] = jnp.zeros(ys_s.shape, F32)

    def group_rows(g, r0, rows):
        xs = xs_s[pl.ds(r0, rows), :]
        gsc = gs_s[pl.ds(r0, rows), :]
        acts = []
        for k in range(per_group):
            e = g * per_group + k
            hg = _dot(xs, wg_ref[e])
            hu = _dot(xs, wu_ref[e])
            acts.append((hg / (1.0 + jnp.exp(-hg)) * hu * gsc[:, e:e + 1]).astype(BF16))
        ys_s[pl.ds(r0, rows), :] += _dot(jnp.concatenate(acts, axis=1), wd_ref[g])

    base = [(off[g] // ROW_ALIGN) * ROW_ALIGN for g in range(N_GROUPS)]
    for g in range(N_GROUPS):
        group_rows(g, pl.multiple_of(base[g], ROW_ALIGN), cap)
    for g in range(N_GROUPS):
        n_extra = jnp.maximum(0, (off[g] + cnt[g] - (base[g] + cap) + xc - 1) // xc)

        def extra(k, carry, g=g):
            group_rows(g, pl.multiple_of(base[g] + cap + k * xc, ROW_ALIGN), xc)
            return carry

        lax.fori_loop(0, n_extra, extra, 0)

    moe = _dot(perm_t, ys_s[0:t, :].astype(BF16))
    y_ref[...] = _rms(x1_ref[...] + moe, gf_ref[...])


def _moe(x1, xn2, route, wg, wu, wd, gf, t, cap, xc):
    n = x1.shape[0]
    tile = lambda i: (i, 0)
    whole = lambda i: (0, 0, 0)
    resident = pl.Buffered(1)
    rows = t + cap + xc
    hidden = (N_EXPERTS // N_GROUPS) * D_EXPERT
    return pl.pallas_call(
        functools.partial(_moe_body, cap, xc),
        grid=(n // t,),
        in_specs=[
            pl.BlockSpec((t, D_MODEL), tile),
            pl.BlockSpec((t, D_MODEL), tile),
            pl.BlockSpec((t, ROUTER_LANES), tile),
            pl.BlockSpec((N_EXPERTS, D_MODEL, D_EXPERT), whole, pipeline_mode=resident),
            pl.BlockSpec((N_EXPERTS, D_MODEL, D_EXPERT), whole, pipeline_mode=resident),
            pl.BlockSpec((N_GROUPS, hidden, D_MODEL), whole, pipeline_mode=resident),
            pl.BlockSpec((1, D_MODEL), lambda i: (0, 0)),
        ],
        out_specs=pl.BlockSpec((t, D_MODEL), tile),
        out_shape=jax.ShapeDtypeStruct((n, D_MODEL), F32),
        scratch_shapes=[
            pltpu.VMEM((rows, D_MODEL), BF16),
            pltpu.VMEM((rows, ROUTER_LANES), F32),
            pltpu.VMEM((rows, D_MODEL), F32),
        ],
        compiler_params=pltpu.CompilerParams(
            dimension_semantics=("arbitrary",), vmem_limit_bytes=MOE_VMEM_LIMIT),
        name="moe",
    )(x1, xn2, route, wg, wu, wd, gf)


def _proj_sample_body(x_ref, g1_ref, win_ref, proj_ref, kvt_ref):
    xn = _rms(x_ref[...], g1_ref[...]).astype(BF16)
    proj_ref[...] = _dot(xn, win_ref[...])
    kvt_ref[...] = lax.dot_general(win_ref[:, OFF_K:OFF_B], xn, (((0,), (1,)), ((), ())),
                                   preferred_element_type=F32)


def _proj_sample(x, g1, win):
    n = x.shape[0]
    return pl.pallas_call(
        _proj_sample_body,
        out_shape=[jax.ShapeDtypeStruct((n, IN_PROJ_DIM), F32),
                   jax.ShapeDtypeStruct((2 * KV_DIM, n), F32)],
        compiler_params=pltpu.CompilerParams(vmem_limit_bytes=VMEM_LIMIT),
        name="proj_sample",
    )(x, g1, win)


def _attn_sample_body(q8_ref, knew_ref, vnew_ref, kvt_ref, khist_ref, vhist_ref, sinks_ref,
                      out8_ref, kout_ref, vout_ref):
    chunk = q8_ref.shape[0]
    first = pl.program_id(0) * chunk
    q8 = q8_ref[...]
    kh = khist_ref[...]
    vh = vhist_ref[...]
    knew = knew_ref[...]
    vnew = vnew_ref[...]
    s = jnp.einsum('bhj,bjc->bhc', q8, kh.astype(BF16), preferred_element_type=F32)
    s_new = jnp.sum(q8.astype(F32) * knew, axis=-1, keepdims=True)
    sink = sinks_ref[...][None]
    mx = jnp.maximum(jnp.maximum(jnp.max(s, axis=-1, keepdims=True), s_new), sink)
    e = jnp.exp(s - mx)
    e_new = jnp.exp(s_new - mx)
    den = jnp.sum(e, axis=-1, keepdims=True) + e_new + jnp.exp(sink - mx)
    o = jnp.einsum('bhc,bjc->bhj', e.astype(BF16), vh.astype(BF16), preferred_element_type=F32)
    out8_ref[...] = (o + e_new * vnew) / den
    last = lax.broadcasted_iota(jnp.int32, (KV_DIM, WINDOW), 1) == WINDOW - 1
    knew_t = kvt_ref[0:KV_DIM, :]
    vnew_t = kvt_ref[KV_DIM:, :]
    for bb in range(chunk):
        shift = lax.rem(2 * WINDOW - 1 - (first + bb), WINDOW)
        kout_ref[bb] = jnp.where(last, pltpu.roll(knew_t, shift, axis=1), pltpu.roll(kh[bb], WINDOW - 1, axis=1))
        vout_ref[bb] = jnp.where(last, pltpu.roll(vnew_t, shift, axis=1), pltpu.roll(vh[bb], WINDOW - 1, axis=1))


def _attn_sample(q8, knew, vnew, kvt, khist, vhist, sinks_col):
    assert q8.shape[0] == WINDOW == LANES, "one lane per sequence in the channel-major projection"
    n = q8.shape[0]
    c = DEC_CHUNK
    blk3 = lambda i: (i, 0, 0)
    return pl.pallas_call(
        _attn_sample_body,
        grid=(n // c,),
        in_specs=[
            pl.BlockSpec((c, N_HEADS, LANES), blk3),
            pl.BlockSpec((c, 1, KV_DIM), blk3),
            pl.BlockSpec((c, 1, KV_DIM), blk3),
            pl.BlockSpec((2 * KV_DIM, n), lambda i: (0, 0)),
            pl.BlockSpec((c, KV_DIM, WINDOW), blk3),
            pl.BlockSpec((c, KV_DIM, WINDOW), blk3),
            pl.BlockSpec((N_HEADS, 1), lambda i: (0, 0)),
        ],
        out_specs=[
            pl.BlockSpec((c, N_HEADS, LANES), blk3),
            pl.BlockSpec((c, KV_DIM, WINDOW), blk3),
            pl.BlockSpec((c, KV_DIM, WINDOW), blk3),
        ],
        out_shape=[
            jax.ShapeDtypeStruct((n, N_HEADS, LANES), F32),
            jax.ShapeDtypeStruct((n, KV_DIM, WINDOW), F32),
            jax.ShapeDtypeStruct((n, KV_DIM, WINDOW), F32),
        ],
        compiler_params=pltpu.CompilerParams(
            dimension_semantics=("arbitrary",), vmem_limit_bytes=VMEM_LIMIT),
        name="attn_sample",
    )(q8, knew, vnew, kvt, khist, vhist, sinks_col)


def _tail_sample_body(x_ref, attn_ref, proj_ref, uprev2_ref, uprev1_ref, convw_ref, gattn_ref,
                      gconv_ref, wout_ref, g2_ref, wr_ref, br_ref,
                      x1_ref, xn2_ref, gates_ref, u_ref):
    gate_b = proj_ref[:, OFF_B:OFF_C]
    u = proj_ref[:, OFF_C:OFF_H] * proj_ref[:, OFF_H:IN_PROJ_DIM]
    cw = convw_ref[...]
    z = cw[0:1, :] * uprev2_ref[...] + cw[1:2, :] * uprev1_ref[...] + cw[2:3, :] * u
    u_ref[...] = u
    x1, xn2_bf, gates = _mix_tail(x_ref[...], attn_ref[...], gate_b * z, gattn_ref[...], gconv_ref[...],
                                  wout_ref[...], g2_ref[...], wr_ref[...], br_ref[...])
    x1_ref[...] = x1
    xn2_ref[...] = xn2_bf
    gates_ref[...] = gates


def _tail_sample(x, attn, proj, uprev2, uprev1, convw, gattn, gconv, wout, g2, wr, br):
    n = x.shape[0]
    return pl.pallas_call(
        _tail_sample_body,
        out_shape=[
            jax.ShapeDtypeStruct((n, D_MODEL), F32),
            jax.ShapeDtypeStruct((n, D_MODEL), BF16),
            jax.ShapeDtypeStruct((n, ROUTER_LANES), F32),
            jax.ShapeDtypeStruct((n, CONV_DIM), F32),
        ],
        compiler_params=pltpu.CompilerParams(vmem_limit_bytes=VMEM_LIMIT),
        name="tail_sample",
    )(x, attn, proj, uprev2, uprev1, convw, gattn, gconv, wout, g2, wr, br)


def _pair_heads(w, axis):
    shape = w.shape
    n_pair = N_HEADS // 2
    split = shape[:axis] + (2, n_pair, HEAD_DIM) + shape[axis + 1:]
    return jnp.swapaxes(w.reshape(split), axis, axis + 1).reshape(shape)


def kernel(x_prompt, x_sample, cache_k_win, cache_v_win, state_conv, norm1_g, w_in, conv_w, sinks,
           attn_out_g, conv_out_g, w_out, norm2_g, w_group, b_group, w_fine, b_fine, w_gate, w_up,
           w_down, final_g):
    nb, seq, _ = x_prompt.shape
    nd = x_sample.shape[0]

    w_in0 = w_in[0].astype(BF16)
    win = jnp.concatenate([_pair_heads(w_in0[:, :ATTN_DIM], 1), w_in0[:, ATTN_DIM:]], axis=1)
    w_out0 = w_out[0].astype(BF16)
    wout = jnp.concatenate([_pair_heads(w_out0[:ATTN_DIM], 0), w_out0[ATTN_DIM:]], axis=0)
    gattn = _pair_heads(attn_out_g[0], 0)[None]
    gconv = conv_out_g[0][None]
    g1 = norm1_g[0][None]
    g2 = norm2_g[0][None]
    gf = final_g[None]
    convw = conv_w[0]
    sinks0 = sinks[0]
    pad = ROUTER_LANES - N_EXPERTS - N_GROUPS
    wr32 = jnp.concatenate([w_fine[0], w_group[0], jnp.zeros((D_MODEL, pad), F32)], axis=1)
    wr_hi = wr32.astype(BF16)
    wr = jnp.concatenate([wr_hi, (wr32 - wr_hi.astype(F32)).astype(BF16)], axis=1)
    br = jnp.concatenate([b_fine[0], b_group[0], jnp.zeros((pad,), F32)])[:ROUTER_ROWS, None]

    x1p, xn2p, gatesp, kp, vp, up, wg, wu, wd = _layer_prompt(
        x_prompt, sinks0, g1, win, convw, gattn, gconv, wout, g2, wr, br, w_gate[0], w_up[0], w_down[0])
    wd = wd.reshape(N_GROUPS, (N_EXPERTS // N_GROUPS) * D_EXPERT, D_MODEL)
    n_p = nb * seq
    y_prompt = _moe(x1p.reshape(n_p, D_MODEL), xn2p.reshape(n_p, D_MODEL),
                    gatesp.reshape(n_p, ROUTER_LANES), wg, wu, wd, gf, T_MOE, MOE_CAP, MOE_XC).reshape(nb, seq, D_MODEL)

    xs = x_sample.reshape(nd, D_MODEL)
    proj, kvt = _proj_sample(xs, g1, win)
    lane = jnp.arange(LANES)
    qp = (proj[:, :ATTN_DIM] * (HEAD_DIM ** -0.5)).reshape(nd, N_HEADS // 2, LANES)
    q8 = jnp.concatenate([jnp.where(lane < HEAD_DIM, qp, 0.0), jnp.where(lane >= HEAD_DIM, qp, 0.0)],
                         axis=1).astype(BF16)
    knew = proj[:, OFF_K:OFF_V].reshape(nd, 1, KV_DIM)
    vnew = proj[:, OFF_V:OFF_B].reshape(nd, 1, KV_DIM)
    to_cm = lambda c: jnp.transpose(c[0], (0, 2, 3, 1)).reshape(nd, KV_DIM, WINDOW)
    from_cm = lambda c, n: jnp.transpose(c.reshape(n, KV_DIM // HEAD_DIM, HEAD_DIM, WINDOW), (0, 3, 1, 2))[None]
    out8, ks, vs = _attn_sample(q8, knew, vnew, kvt, to_cm(cache_k_win), to_cm(cache_v_win), sinks0[:, None])
    attn_s = jnp.where(lane < HEAD_DIM, out8[:, :N_HEADS // 2], out8[:, N_HEADS // 2:]).reshape(nd, ATTN_DIM)
    st = state_conv[0]
    x1s, xn2s, gatess, us = _tail_sample(xs, attn_s, proj, st[:, 0], st[:, 1], convw, gattn, gconv,
                                         wout, g2, wr, br)
    y_sample = _moe(x1s, xn2s, gatess, wg, wu, wd, gf, nd, MOE_CAP_DEC, MOE_XC_DEC).reshape(nd, 1, D_MODEL)

    return (y_prompt, y_sample,
            from_cm(kp, nb), from_cm(vp, nb),
            up[None],
            from_cm(ks, nd), from_cm(vs, nd),
            jnp.stack([st[:, 1], us], axis=1)[None])
```

```python
import functools

import jax
import jax.numpy as jnp
from jax import lax
from jax.experimental import pallas as pl
from jax.experimental.pallas import tpu as pltpu

D_MODEL = 1024
HEAD_DIM = 64
N_HEADS = 8
ATTN_DIM = 512
KV_DIM = 128
WINDOW = 128
CONV_DIM = 512
CONV_WIDTH = 3
OFF_K = 512
OFF_V = 640
OFF_B = 768
OFF_C = 1280
OFF_H = 1792
IN_PROJ_DIM = 2304
N_GROUPS = 4
N_EXPERTS = 16
D_EXPERT = 256
RMS_EPS = 1e-5
NEG_INF = -1e30

LANES = 128
ROW_ALIGN = 16
ROUTER_LANES = 128
ROUTER_ROWS = 32
COARSE_OFF = N_EXPERTS
T_LAYER = 1024
N_SUB = 2
GID_LANE = N_EXPERTS
T_MOE = 512
MOE_CAP, MOE_XC = 176, 64
MOE_CAP_DEC, MOE_XC_DEC = 64, 32
DEC_CHUNK = 32
VMEM_LIMIT = 48 * 1024 * 1024
MOE_VMEM_LIMIT = 56 * 1024 * 1024

BF16 = jnp.bfloat16
F32 = jnp.float32


def _dot(a, b):
    return jnp.dot(a, b, preferred_element_type=F32)


def _rms(x, g):
    ms = jnp.mean(x * x, axis=-1, keepdims=True)
    return x * lax.rsqrt(ms + RMS_EPS) * g


def _group_norm64(y, g):
    rows, cols = y.shape
    lo = lax.broadcasted_iota(jnp.int32, (rows, LANES), 1) < HEAD_DIM
    outs = []
    for c in range(cols // LANES):
        blk = y[:, c * LANES:(c + 1) * LANES]
        sq = blk * blk
        s_lo = jnp.sum(jnp.where(lo, sq, 0.0), axis=-1, keepdims=True)
        s_hi = jnp.sum(jnp.where(lo, 0.0, sq), axis=-1, keepdims=True)
        ms = jnp.where(lo, s_lo, s_hi) * (1.0 / HEAD_DIM)
        outs.append(blk * lax.rsqrt(ms + RMS_EPS))
    return jnp.concatenate(outs, axis=-1) * g


def _router(xn2, wr, br):
    hi = xn2.astype(BF16)
    both = _dot(hi, wr)
    logits = jnp.transpose(both[:, :ROUTER_LANES] + both[:, ROUTER_LANES:])[:ROUTER_ROWS] + br
    n = logits.shape[1]
    row = lax.broadcasted_iota(jnp.int32, (ROUTER_ROWS, n), 0)
    rowf = row.astype(F32)
    big = float(ROUTER_ROWS)
    coarse = (row >= COARSE_OFF) & (row < COARSE_OFF + N_GROUPS)
    cm = jnp.max(jnp.where(coarse, logits, -jnp.inf), axis=0, keepdims=True)
    g_sel = jnp.min(jnp.where(coarse & (logits == cm), rowf - COARSE_OFF, big), axis=0, keepdims=True)
    zc = jnp.sum(jnp.where(coarse, jnp.exp(logits - cm), 0.0), axis=0, keepdims=True)
    p_sel = 1.0 / zc
    per_group = N_EXPERTS // N_GROUPS
    fine = (row < N_EXPERTS) & ((row // per_group).astype(F32) == g_sel)
    f1 = jnp.max(jnp.where(fine, logits, -jnp.inf), axis=0, keepdims=True)
    i1 = jnp.min(jnp.where(fine & (logits == f1), rowf, big), axis=0, keepdims=True)
    rest = fine & (rowf != i1)
    f2 = jnp.max(jnp.where(rest, logits, -jnp.inf), axis=0, keepdims=True)
    i2 = jnp.min(jnp.where(rest & (logits == f2), rowf, big), axis=0, keepdims=True)
    e2 = jnp.exp(f2 - f1)
    t1 = 1.0 / (1.0 + e2)
    t2 = e2 * t1
    gates_t = jnp.where(rowf == i1, p_sel * t1, jnp.where(rowf == i2, p_sel * t2, 0.0))
    gates_t = jnp.where(row == GID_LANE, g_sel, gates_t)
    gates_t = jnp.concatenate([gates_t, jnp.zeros((ROUTER_LANES - ROUTER_ROWS, n), F32)], axis=0)
    return hi, jnp.transpose(gates_t)


def _mix_tail(x, attn, conv_out, gattn, gconv, wout, g2, wr, br):
    mixed = jnp.concatenate([_group_norm64(attn, gattn), _group_norm64(conv_out, gconv)],
                            axis=-1).astype(BF16)
    x1 = x + _dot(mixed, wout)
    xn2_bf, gates = _router(_rms(x1, g2), wr, br)
    return x1, xn2_bf, gates


def _layer_prompt_body(sinks_ref, x_ref, g1_ref, win_ref, convw_ref, gattn_ref, gconv_ref,
                       wout_ref, g2_ref, wr_ref, br_ref, wg32_ref, wu32_ref, wd32_ref,
                       x1_ref, xn2_ref, gates_ref, knew_ref, vnew_ref, unew_ref,
                       wg16_ref, wu16_ref, wd16_ref,
                       q_s, kcat_s, vcat_s, attn_s, ucarry_s):
    i = pl.program_id(1)
    wg16_ref[...] = wg32_ref[...].astype(BF16)
    wu16_ref[...] = wu32_ref[...].astype(BF16)
    wd16_ref[...] = wd32_ref[...].astype(BF16)
    t = T_LAYER // N_SUB
    last = N_SUB - 1

    @pl.when(i == 0)
    def _():
        kcat_s[last, t:t + WINDOW, :] = jnp.zeros((WINDOW, KV_DIM), BF16)
        vcat_s[last, t:t + WINDOW, :] = jnp.zeros((WINDOW, KV_DIM), BF16)
        ucarry_s[last] = jnp.zeros((8, CONV_DIM), F32)

    a_idx = lax.broadcasted_iota(jnp.int32, (WINDOW, 2 * WINDOW), 0)
    c_idx = lax.broadcasted_iota(jnp.int32, (WINDOW, 2 * WINDOW), 1)
    lane_lo = lax.broadcasted_iota(jnp.int32, (WINDOW, LANES), 1) < HEAD_DIM
    row = lax.broadcasted_iota(jnp.int32, (t, CONV_DIM), 0)
    n_pair = N_HEADS // 2
    cw = convw_ref[...]

    for h in range(N_SUB):
        src = (h - 1) % N_SUB
        rows = pl.ds(h * t, t)
        x = x_ref[0, rows, :]
        xn = _rms(x, g1_ref[...]).astype(BF16)

        kcat_s[h, 0:WINDOW, :] = kcat_s[src, t:t + WINDOW, :]
        vcat_s[h, 0:WINDOW, :] = vcat_s[src, t:t + WINDOW, :]
        q_s[h] = (_dot(xn, win_ref[:, 0:OFF_K]) * (HEAD_DIM ** -0.5)).astype(BF16)
        kv = _dot(xn, win_ref[:, OFF_K:OFF_B])
        k = kv[:, :KV_DIM]
        v = kv[:, KV_DIM:]
        kcat_s[h, WINDOW:, :] = k.astype(BF16)
        vcat_s[h, WINDOW:, :] = v.astype(BF16)
        if h == last:
            knew_ref[0] = jnp.transpose(k[t - WINDOW:, :])
            vnew_ref[0] = jnp.transpose(v[t - WINDOW:, :])

        for j in range(t // WINDOW):
            r0 = j * WINDOW
            kk = kcat_s[h, r0:r0 + 2 * WINDOW, :]
            vv = vcat_s[h, r0:r0 + 2 * WINDOW, :]
            c_min = jnp.where(i == 0, WINDOW, 0) if (h == 0 and j == 0) else 0
            valid = (c_idx >= jnp.maximum(a_idx, c_min)) & (c_idx <= a_idx + WINDOW)
            qcols = [q_s[h, r0:r0 + WINDOW, m * LANES:(m + 1) * LANES] for m in range(n_pair)]
            outs = []
            for half in range(2):
                keep = lane_lo if half == 0 else jnp.logical_not(lane_lo)
                qg = jnp.concatenate([jnp.where(keep, qc, jnp.zeros_like(qc)) for qc in qcols], axis=0)
                s = lax.dot_general(qg, kk, (((1,), (1,)), ((), ())), preferred_element_type=F32)
                es, dens = [], []
                for m in range(n_pair):
                    sm = jnp.where(valid, s[m * WINDOW:(m + 1) * WINDOW], NEG_INF)
                    sink = sinks_ref[m + half * n_pair]
                    mx = jnp.maximum(jnp.max(sm, axis=-1, keepdims=True), sink)
                    e = jnp.exp(sm - mx)
                    dens.append(jnp.sum(e, axis=-1, keepdims=True) + jnp.exp(sink - mx))
                    es.append(e.astype(BF16))
                o = _dot(jnp.concatenate(es, axis=0), vv)
                outs.append([o[m * WINDOW:(m + 1) * WINDOW] / dens[m] for m in range(n_pair)])
            for m in range(n_pair):
                attn_s[h, r0:r0 + WINDOW, m * LANES:(m + 1) * LANES] = jnp.where(lane_lo, outs[0][m], outs[1][m])

        gate_b = _dot(xn, win_ref[:, OFF_B:OFF_C])
        u = _dot(xn, win_ref[:, OFF_C:OFF_H]) * _dot(xn, win_ref[:, OFF_H:IN_PROJ_DIM])
        prev = ucarry_s[src]
        u1 = jnp.where(row == 0, prev[7:8, :], pltpu.roll(u, 1, axis=0))
        u2 = jnp.where(row == 0, prev[6:7, :], jnp.where(row == 1, prev[7:8, :], pltpu.roll(u, 2, axis=0)))
        z = cw[0:1, :] * u2 + cw[1:2, :] * u1 + cw[2:3, :] * u
        ucarry_s[h] = u[t - 8:, :]
        if h == last:
            unew_ref[0] = u[t - (CONV_WIDTH - 1):, :]

        x1, xn2_bf, gates = _mix_tail(x, attn_s[h], gate_b * z, gattn_ref[...], gconv_ref[...],
                                      wout_ref[...], g2_ref[...], wr_ref[...], br_ref[...])
        x1_ref[0, rows, :] = x1
        xn2_ref[0, rows, :] = xn2_bf
        gates_ref[0, rows, :] = gates


def _layer_prompt(x, sinks, g1, win, convw, gattn, gconv, wout, g2, wr, br, wg32, wu32, wd32):
    nb, seq, _ = x.shape
    t = T_LAYER
    ts = T_LAYER // N_SUB
    n_i = seq // t
    assert nb * n_i == N_EXPERTS, "one expert's weights are cast per grid step"
    const = lambda b, i, s: (0, 0)
    tile = lambda b, i, s: (b, i, 0)
    per_seq = lambda b, i, s: (b, 0, 0)
    per_step = lambda b, i, s: (b * n_i + i, 0, 0)
    resident = pl.Buffered(1)
    grid_spec = pltpu.PrefetchScalarGridSpec(
        num_scalar_prefetch=1,
        grid=(nb, seq // t),
        in_specs=[
            pl.BlockSpec((1, t, D_MODEL), tile),
            pl.BlockSpec((1, D_MODEL), const),
            pl.BlockSpec((D_MODEL, IN_PROJ_DIM), const, pipeline_mode=resident),
            pl.BlockSpec((CONV_WIDTH, CONV_DIM), const),
            pl.BlockSpec((1, ATTN_DIM), const),
            pl.BlockSpec((1, CONV_DIM), const),
            pl.BlockSpec((D_MODEL, D_MODEL), const, pipeline_mode=resident),
            pl.BlockSpec((1, D_MODEL), const),
            pl.BlockSpec((D_MODEL, 2 * ROUTER_LANES), const, pipeline_mode=resident),
            pl.BlockSpec((ROUTER_ROWS, 1), const),
            pl.BlockSpec((1, D_MODEL, D_EXPERT), per_step),
            pl.BlockSpec((1, D_MODEL, D_EXPERT), per_step),
            pl.BlockSpec((1, D_EXPERT, D_MODEL), per_step),
        ],
        out_specs=[
            pl.BlockSpec((1, t, D_MODEL), tile),
            pl.BlockSpec((1, t, D_MODEL), tile),
            pl.BlockSpec((1, t, ROUTER_LANES), tile),
            pl.BlockSpec((1, KV_DIM, WINDOW), per_seq),
            pl.BlockSpec((1, KV_DIM, WINDOW), per_seq),
            pl.BlockSpec((1, CONV_WIDTH - 1, CONV_DIM), per_seq),
            pl.BlockSpec((1, D_MODEL, D_EXPERT), per_step),
            pl.BlockSpec((1, D_MODEL, D_EXPERT), per_step),
            pl.BlockSpec((1, D_EXPERT, D_MODEL), per_step),
        ],
        scratch_shapes=[
            pltpu.VMEM((N_SUB, ts, ATTN_DIM), BF16),
            pltpu.VMEM((N_SUB, ts + WINDOW, KV_DIM), BF16),
            pltpu.VMEM((N_SUB, ts + WINDOW, KV_DIM), BF16),
            pltpu.VMEM((N_SUB, ts, ATTN_DIM), F32),
            pltpu.VMEM((N_SUB, 8, CONV_DIM), F32),
        ],
    )
    return pl.pallas_call(
        _layer_prompt_body,
        grid_spec=grid_spec,
        out_shape=[
            jax.ShapeDtypeStruct((nb, seq, D_MODEL), F32),
            jax.ShapeDtypeStruct((nb, seq, D_MODEL), BF16),
            jax.ShapeDtypeStruct((nb, seq, ROUTER_LANES), F32),
            jax.ShapeDtypeStruct((nb, KV_DIM, WINDOW), F32),
            jax.ShapeDtypeStruct((nb, KV_DIM, WINDOW), F32),
            jax.ShapeDtypeStruct((nb, CONV_WIDTH - 1, CONV_DIM), F32),
            jax.ShapeDtypeStruct((N_EXPERTS, D_MODEL, D_EXPERT), BF16),
            jax.ShapeDtypeStruct((N_EXPERTS, D_MODEL, D_EXPERT), BF16),
            jax.ShapeDtypeStruct((N_EXPERTS, D_EXPERT, D_MODEL), BF16),
        ],
        compiler_params=pltpu.CompilerParams(
            dimension_semantics=("arbitrary", "arbitrary"), vmem_limit_bytes=MOE_VMEM_LIMIT),
        name="layer_prompt",
    )(sinks, x, g1, win, convw, gattn, gconv, wout, g2, wr, br, wg32, wu32, wd32)


def _moe_body(cap, xc, x1_ref, xn2_ref, gates_ref, wg_ref, wu_ref, wd_ref, gf_ref, y_ref,
              xs_s, gs_s, ys_s):
    t = x1_ref.shape[0]
    per_group = N_EXPERTS // N_GROUPS
    gates = gates_ref[...]
    gid_row = jnp.transpose(gates)[GID_LANE:GID_LANE + 1, :]
    grp = lax.broadcasted_iota(jnp.int32, (8, t), 0).astype(F32)
    onehot = jnp.where(grp == gid_row, 1.0, 0.0)

    r_idx = lax.broadcasted_iota(jnp.int32, (t, t), 0)
    c_idx = lax.broadcasted_iota(jnp.int32, (t, t), 1)
    upper = jnp.where(r_idx <= c_idx, 1.0, 0.0).astype(BF16)
    csum = _dot(onehot.astype(BF16), upper)
    rank_row = jnp.sum(onehot * (csum - 1.0), axis=0, keepdims=True)
    cnt = [csum[g, t - 1].astype(jnp.int32) for g in range(N_GROUPS)]
    off = [jnp.int32(0)]
    for g in range(1, N_GROUPS):
        off.append(off[-1] + cnt[g - 1])

    pos_row = rank_row
    for g in range(1, N_GROUPS):
        pos_row = pos_row + jnp.where(gid_row == float(g), off[g].astype(F32), 0.0)
    pos_col = jnp.transpose(jnp.broadcast_to(pos_row, (LANES, t)))
    pos_col = jnp.concatenate([pos_col] * (t // LANES), axis=1)
    perm = jnp.where(r_idx.astype(F32) == pos_row, 1.0, 0.0).astype(BF16)
    perm_t = jnp.where(c_idx.astype(F32) == pos_col, 1.0, 0.0).astype(BF16)

    p1 = gates.astype(BF16)
    p2 = (gates - p1.astype(F32)).astype(BF16)
    moved = _dot(perm, jnp.concatenate([xn2_ref[...], p1, p2], axis=1))
    xs_s[0:t, :] = moved[:, :D_MODEL].astype(BF16)
    gs_s[0:t, :] = moved[:, D_MODEL:D_MODEL + ROUTER_LANES] + moved[:, D_MODEL + ROUTER_LANES:]
    pad = xs_s.shape[0] - t
    xs_s[t:, :] = jnp.zeros((pad, D_MODEL), BF16)
    gs_s[t:, :] = jnp.zeros((pad, ROUTER_LANES), F32)
    ys_s[...] = jnp.zeros(ys_s.shape, F32)

    def group_rows(g, r0, rows):
        xs = xs_s[pl.ds(r0, rows), :]
        gsc = gs_s[pl.ds(r0, rows), :]
        acts = []
        for k in range(per_group):
            e = g * per_group + k
            hg = _dot(xs, wg_ref[e])
            hu = _dot(xs, wu_ref[e])
            acts.append((hg / (1.0 + jnp.exp(-hg)) * hu * gsc[:, e:e + 1]).astype(BF16))
        ys_s[pl.ds(r0, rows), :] += _dot(jnp.concatenate(acts, axis=1), wd_ref[g])

    base = [(off[g] // ROW_ALIGN) * ROW_ALIGN for g in range(N_GROUPS)]
    for g in range(N_GROUPS):
        group_rows(g, pl.multiple_of(base[g], ROW_ALIGN), cap)
    for g in range(N_GROUPS):
        n_extra = jnp.maximum(0, (off[g] + cnt[g] - (base[g] + cap) + xc - 1) // xc)

        def extra(k, carry, g=g):
            group_rows(g, pl.multiple_of(base[g] + cap + k * xc, ROW_ALIGN), xc)
            return carry

        lax.fori_loop(0, n_extra, extra, 0)

    moe = _dot(perm_t, ys_s[0:t, :].astype(BF16))
    y_ref[...] = _rms(x1_ref[...] + moe, gf_ref[...])


def _moe(x1, xn2, route, wg, wu, wd, gf, t, cap, xc):
    n = x1.shape[0]
    tile = lambda i: (i, 0)
    whole = lambda i: (0, 0, 0)
    resident = pl.Buffered(1)
    rows = t + cap + xc
    hidden = (N_EXPERTS // N_GROUPS) * D_EXPERT
    return pl.pallas_call(
        functools.partial(_moe_body, cap, xc),
        grid=(n // t,),
        in_specs=[
            pl.BlockSpec((t, D_MODEL), tile),
            pl.BlockSpec((t, D_MODEL), tile),
            pl.BlockSpec((t, ROUTER_LANES), tile),
            pl.BlockSpec((N_EXPERTS, D_MODEL, D_EXPERT), whole, pipeline_mode=resident),
            pl.BlockSpec((N_EXPERTS, D_MODEL, D_EXPERT), whole, pipeline_mode=resident),
            pl.BlockSpec((N_GROUPS, hidden, D_MODEL), whole, pipeline_mode=resident),
            pl.BlockSpec((1, D_MODEL), lambda i: (0, 0)),
        ],
        out_specs=pl.BlockSpec((t, D_MODEL), tile),
        out_shape=jax.ShapeDtypeStruct((n, D_MODEL), F32),
        scratch_shapes=[
            pltpu.VMEM((rows, D_MODEL), BF16),
            pltpu.VMEM((rows, ROUTER_LANES), F32),
            pltpu.VMEM((rows, D_MODEL), F32),
        ],
        compiler_params=pltpu.CompilerParams(
            dimension_semantics=("arbitrary",), vmem_limit_bytes=MOE_VMEM_LIMIT),
        name="moe",
    )(x1, xn2, route, wg, wu, wd, gf)


def _proj_sample_body(x_ref, g1_ref, win_ref, proj_ref, kvt_ref):
    xn = _rms(x_ref[...], g1_ref[...]).astype(BF16)
    proj_ref[...] = _dot(xn, win_ref[...])
    kvt_ref[...] = lax.dot_general(win_ref[:, OFF_K:OFF_B], xn, (((0,), (1,)), ((), ())),
                                   preferred_element_type=F32)


def _proj_sample(x, g1, win):
    n = x.shape[0]
    return pl.pallas_call(
        _proj_sample_body,
        out_shape=[jax.ShapeDtypeStruct((n, IN_PROJ_DIM), F32),
                   jax.ShapeDtypeStruct((2 * KV_DIM, n), F32)],
        compiler_params=pltpu.CompilerParams(vmem_limit_bytes=VMEM_LIMIT),
        name="proj_sample",
    )(x, g1, win)


def _attn_sample_body(q8_ref, knew_ref, vnew_ref, kvt_ref, khist_ref, vhist_ref, sinks_ref,
                      out8_ref, kout_ref, vout_ref):
    chunk = q8_ref.shape[0]
    first = pl.program_id(0) * chunk
    q8 = q8_ref[...]
    kh = khist_ref[...]
    vh = vhist_ref[...]
    knew = knew_ref[...]
    vnew = vnew_ref[...]
    s = jnp.einsum('bhj,bjc->bhc', q8, kh.astype(BF16), preferred_element_type=F32)
    s_new = jnp.sum(q8.astype(F32) * knew, axis=-1, keepdims=True)
    sink = sinks_ref[...][None]
    mx = jnp.maximum(jnp.maximum(jnp.max(s, axis=-1, keepdims=True), s_new), sink)
    e = jnp.exp(s - mx)
    e_new = jnp.exp(s_new - mx)
    den = jnp.sum(e, axis=-1, keepdims=True) + e_new + jnp.exp(sink - mx)
    o = jnp.einsum('bhc,bjc->bhj', e.astype(BF16), vh.astype(BF16), preferred_element_type=F32)
    out8_ref[...] = (o + e_new * vnew) / den
    last = lax.broadcasted_iota(jnp.int32, (KV_DIM, WINDOW), 1) == WINDOW - 1
    knew_t = kvt_ref[0:KV_DIM, :]
    vnew_t = kvt_ref[KV_DIM:, :]
    for bb in range(chunk):
        shift = lax.rem(2 * WINDOW - 1 - (first + bb), WINDOW)
        kout_ref[bb] = jnp.where(last, pltpu.roll(knew_t, shift, axis=1), pltpu.roll(kh[bb], WINDOW - 1, axis=1))
        vout_ref[bb] = jnp.where(last, pltpu.roll(vnew_t, shift, axis=1), pltpu.roll(vh[bb], WINDOW - 1, axis=1))


def _attn_sample(q8, knew, vnew, kvt, khist, vhist, sinks_col):
    assert q8.shape[0] == WINDOW == LANES, "one lane per sequence in the channel-major projection"
    n = q8.shape[0]
    c = DEC_CHUNK
    blk3 = lambda i: (i, 0, 0)
    return pl.pallas_call(
        _attn_sample_body,
        grid=(n // c,),
        in_specs=[
            pl.BlockSpec((c, N_HEADS, LANES), blk3),
            pl.BlockSpec((c, 1, KV_DIM), blk3),
            pl.BlockSpec((c, 1, KV_DIM), blk3),
            pl.BlockSpec((2 * KV_DIM, n), lambda i: (0, 0)),
            pl.BlockSpec((c, KV_DIM, WINDOW), blk3),
            pl.BlockSpec((c, KV_DIM, WINDOW), blk3),
            pl.BlockSpec((N_HEADS, 1), lambda i: (0, 0)),
        ],
        out_specs=[
            pl.BlockSpec((c, N_HEADS, LANES), blk3),
            pl.BlockSpec((c, KV_DIM, WINDOW), blk3),
            pl.BlockSpec((c, KV_DIM, WINDOW), blk3),
        ],
        out_shape=[
            jax.ShapeDtypeStruct((n, N_HEADS, LANES), F32),
            jax.ShapeDtypeStruct((n, KV_DIM, WINDOW), F32),
            jax.ShapeDtypeStruct((n, KV_DIM, WINDOW), F32),
        ],
        compiler_params=pltpu.CompilerParams(
            dimension_semantics=("arbitrary",), vmem_limit_bytes=VMEM_LIMIT),
        name="attn_sample",
    )(q8, knew, vnew, kvt, khist, vhist, sinks_col)


def _tail_sample_body(x_ref, attn_ref, proj_ref, uprev2_ref, uprev1_ref, convw_ref, gattn_ref,
                      gconv_ref, wout_ref, g2_ref, wr_ref, br_ref,
                      x1_ref, xn2_ref, gates_ref, u_ref):
    gate_b = proj_ref[:, OFF_B:OFF_C]
    u = proj_ref[:, OFF_C:OFF_H] * proj_ref[:, OFF_H:IN_PROJ_DIM]
    cw = convw_ref[...]
    z = cw[0:1, :] * uprev2_ref[...] + cw[1:2, :] * uprev1_ref[...] + cw[2:3, :] * u
    u_ref[...] = u
    x1, xn2_bf, gates = _mix_tail(x_ref[...], attn_ref[...], gate_b * z, gattn_ref[...], gconv_ref[...],
                                  wout_ref[...], g2_ref[...], wr_ref[...], br_ref[...])
    x1_ref[...] = x1
    xn2_ref[...] = xn2_bf
    gates_ref[...] = gates


def _tail_sample(x, attn, proj, uprev2, uprev1, convw, gattn, gconv, wout, g2, wr, br):
    n = x.shape[0]
    return pl.pallas_call(
        _tail_sample_body,
        out_shape=[
            jax.ShapeDtypeStruct((n, D_MODEL), F32),
            jax.ShapeDtypeStruct((n, D_MODEL), BF16),
            jax.ShapeDtypeStruct((n, ROUTER_LANES), F32),
            jax.ShapeDtypeStruct((n, CONV_DIM), F32),
        ],
        compiler_params=pltpu.CompilerParams(vmem_limit_bytes=VMEM_LIMIT),
        name="tail_sample",
    )(x, attn, proj, uprev2, uprev1, convw, gattn, gconv, wout, g2, wr, br)


def _pair_heads(w, axis):
    shape = w.shape
    n_pair = N_HEADS // 2
    split = shape[:axis] + (2, n_pair, HEAD_DIM) + shape[axis + 1:]
    return jnp.swapaxes(w.reshape(split), axis, axis + 1).reshape(shape)


def kernel(x_prompt, x_sample, cache_k_win, cache_v_win, state_conv, norm1_g, w_in, conv_w, sinks,
           attn_out_g, conv_out_g, w_out, norm2_g, w_group, b_group, w_fine, b_fine, w_gate, w_up,
           w_down, final_g):
    nb, seq, _ = x_prompt.shape
    nd = x_sample.shape[0]

    w_in0 = w_in[0].astype(BF16)
    win = jnp.concatenate([_pair_heads(w_in0[:, :ATTN_DIM], 1), w_in0[:, ATTN_DIM:]], axis=1)
    w_out0 = w_out[0].astype(BF16)
    wout = jnp.concatenate([_pair_heads(w_out0[:ATTN_DIM], 0), w_out0[ATTN_DIM:]], axis=0)
    gattn = _pair_heads(attn_out_g[0], 0)[None]
    gconv = conv_out_g[0][None]
    g1 = norm1_g[0][None]
    g2 = norm2_g[0][None]
    gf = final_g[None]
    convw = conv_w[0]
    sinks0 = sinks[0]
    pad = ROUTER_LANES - N_EXPERTS - N_GROUPS
    wr32 = jnp.concatenate([w_fine[0], w_group[0], jnp.zeros((D_MODEL, pad), F32)], axis=1)
    wr_hi = wr32.astype(BF16)
    wr = jnp.concatenate([wr_hi, (wr32 - wr_hi.astype(F32)).astype(BF16)], axis=1)
    br = jnp.concatenate([b_fine[0], b_group[0], jnp.zeros((pad,), F32)])[:ROUTER_ROWS, None]

    x1p, xn2p, gatesp, kp, vp, up, wg, wu, wd = _layer_prompt(
        x_prompt, sinks0, g1, win, convw, gattn, gconv, wout, g2, wr, br, w_gate[0], w_up[0], w_down[0])
    wd = wd.reshape(N_GROUPS, (N_EXPERTS // N_GROUPS) * D_EXPERT, D_MODEL)
    n_p = nb * seq
    y_prompt = _moe(x1p.reshape(n_p, D_MODEL), xn2p.reshape(n_p, D_MODEL),
                    gatesp.reshape(n_p, ROUTER_LANES), wg, wu, wd, gf, T_MOE, MOE_CAP, MOE_XC).reshape(nb, seq, D_MODEL)

    xs = x_sample.reshape(nd, D_MODEL)
    proj, kvt = _proj_sample(xs, g1, win)
    lane = jnp.arange(LANES)
    qp = (proj[:, :ATTN_DIM] * (HEAD_DIM ** -0.5)).reshape(nd, N_HEADS // 2, LANES)
    q8 = jnp.concatenate([jnp.where(lane < HEAD_DIM, qp, 0.0), jnp.where(lane >= HEAD_DIM, qp, 0.0)],
                         axis=1).astype(BF16)
    knew = proj[:, OFF_K:OFF_V].reshape(nd, 1, KV_DIM)
    vnew = proj[:, OFF_V:OFF_B].reshape(nd, 1, KV_DIM)
    to_cm = lambda c: jnp.transpose(c[0], (0, 2, 3, 1)).reshape(nd, KV_DIM, WINDOW)
    from_cm = lambda c, n: jnp.transpose(c.reshape(n, KV_DIM // HEAD_DIM, HEAD_DIM, WINDOW), (0, 3, 1, 2))[None]
    out8, ks, vs = _attn_sample(q8, knew, vnew, kvt, to_cm(cache_k_win), to_cm(cache_v_win), sinks0[:, None])
    attn_s = jnp.where(lane < HEAD_DIM, out8[:, :N_HEADS // 2], out8[:, N_HEADS // 2:]).reshape(nd, ATTN_DIM)
    st = state_conv[0]
    x1s, xn2s, gatess, us = _tail_sample(xs, attn_s, proj, st[:, 0], st[:, 1], convw, gattn, gconv,
                                         wout, g2, wr, br)
    y_sample = _moe(x1s, xn2s, gatess, wg, wu, wd, gf, nd, MOE_CAP_DEC, MOE_XC_DEC).reshape(nd, 1, D_MODEL)

    return (y_prompt, y_sample,
            from_cm(kp, nb), from_cm(vp, nb),
            up[None],
            from_cm(ks, nd), from_cm(vs, nd),
            jnp.stack([st[:, 1], us], axis=1)[None])
```

```python
import functools

import jax
import jax.numpy as jnp
from jax import lax
from jax.experimental import pallas as pl
from jax.experimental.pallas import tpu as pltpu

D_MODEL = 1024
HEAD_DIM = 64
N_HEADS = 8
ATTN_DIM = 512
KV_DIM = 128
WINDOW = 128
CONV_DIM = 512
CONV_WIDTH = 3
OFF_K = 512
OFF_V = 640
OFF_B = 768
OFF_C = 1280
OFF_H = 1792
IN_PROJ_DIM = 2304
N_GROUPS = 4
N_EXPERTS = 16
D_EXPERT = 256
RMS_EPS = 1e-5
NEG_INF = -1e30

LANES = 128
ROW_ALIGN = 16
ROUTER_LANES = 128
ROUTER_ROWS = 32
COARSE_OFF = N_EXPERTS
T_LAYER = 1024
N_SUB = 2
GID_LANE = N_EXPERTS
T_MOE = 512
MOE_CAP, MOE_XC = 160, 64
MOE_CAP_DEC, MOE_XC_DEC = 64, 32
DEC_CHUNK = 32
VMEM_LIMIT = 48 * 1024 * 1024
MOE_VMEM_LIMIT = 56 * 1024 * 1024

BF16 = jnp.bfloat16
F32 = jnp.float32


def _dot(a, b):
    return jnp.dot(a, b, preferred_element_type=F32)


def _rms(x, g):
    ms = jnp.mean(x * x, axis=-1, keepdims=True)
    return x * lax.rsqrt(ms + RMS_EPS) * g


def _group_norm64(y, g):
    rows, cols = y.shape
    lo = lax.broadcasted_iota(jnp.int32, (rows, LANES), 1) < HEAD_DIM
    outs = []
    for c in range(cols // LANES):
        blk = y[:, c * LANES:(c + 1) * LANES]
        sq = blk * blk
        s_lo = jnp.sum(jnp.where(lo, sq, 0.0), axis=-1, keepdims=True)
        s_hi = jnp.sum(jnp.where(lo, 0.0, sq), axis=-1, keepdims=True)
        ms = jnp.where(lo, s_lo, s_hi) * (1.0 / HEAD_DIM)
        outs.append(blk * lax.rsqrt(ms + RMS_EPS))
    return jnp.concatenate(outs, axis=-1) * g


def _router(xn2, wr, br):
    hi = xn2.astype(BF16)
    both = _dot(hi, wr)
    logits = jnp.transpose(both[:, :ROUTER_LANES] + both[:, ROUTER_LANES:])[:ROUTER_ROWS] + br
    n = logits.shape[1]
    row = lax.broadcasted_iota(jnp.int32, (ROUTER_ROWS, n), 0)
    rowf = row.astype(F32)
    big = float(ROUTER_ROWS)
    coarse = (row >= COARSE_OFF) & (row < COARSE_OFF + N_GROUPS)
    cm = jnp.max(jnp.where(coarse, logits, -jnp.inf), axis=0, keepdims=True)
    g_sel = jnp.min(jnp.where(coarse & (logits == cm), rowf - COARSE_OFF, big), axis=0, keepdims=True)
    zc = jnp.sum(jnp.where(coarse, jnp.exp(logits - cm), 0.0), axis=0, keepdims=True)
    p_sel = 1.0 / zc
    per_group = N_EXPERTS // N_GROUPS
    fine = (row < N_EXPERTS) & ((row // per_group).astype(F32) == g_sel)
    f1 = jnp.max(jnp.where(fine, logits, -jnp.inf), axis=0, keepdims=True)
    i1 = jnp.min(jnp.where(fine & (logits == f1), rowf, big), axis=0, keepdims=True)
    rest = fine & (rowf != i1)
    f2 = jnp.max(jnp.where(rest, logits, -jnp.inf), axis=0, keepdims=True)
    i2 = jnp.min(jnp.where(rest & (logits == f2), rowf, big), axis=0, keepdims=True)
    e2 = jnp.exp(f2 - f1)
    t1 = 1.0 / (1.0 + e2)
    t2 = e2 * t1
    gates_t = jnp.where(rowf == i1, p_sel * t1, jnp.where(rowf == i2, p_sel * t2, 0.0))
    gates_t = jnp.where(row == GID_LANE, g_sel, gates_t)
    gates_t = jnp.concatenate([gates_t, jnp.zeros((ROUTER_LANES - ROUTER_ROWS, n), F32)], axis=0)
    return hi, jnp.transpose(gates_t)


def _mix_tail(x, attn, conv_out, gattn, gconv, wout, g2, wr, br):
    mixed = jnp.concatenate([_group_norm64(attn, gattn), _group_norm64(conv_out, gconv)],
                            axis=-1).astype(BF16)
    x1 = x + _dot(mixed, wout)
    xn2_bf, gates = _router(_rms(x1, g2), wr, br)
    return x1, xn2_bf, gates


def _layer_prompt_body(sinks_ref, x_ref, g1_ref, win_ref, convw_ref, gattn_ref, gconv_ref,
                       wout_ref, g2_ref, wr_ref, br_ref, wg32_ref, wu32_ref, wd32_ref,
                       x1_ref, xn2_ref, gates_ref, knew_ref, vnew_ref, unew_ref,
                       wg16_ref, wu16_ref, wd16_ref,
                       q_s, kcat_s, vcat_s, attn_s, ucarry_s):
    i = pl.program_id(1)
    wg16_ref[...] = wg32_ref[...].astype(BF16)
    wu16_ref[...] = wu32_ref[...].astype(BF16)
    wd16_ref[...] = wd32_ref[...].astype(BF16)
    t = T_LAYER // N_SUB
    last = N_SUB - 1

    @pl.when(i == 0)
    def _():
        kcat_s[last, t:t + WINDOW, :] = jnp.zeros((WINDOW, KV_DIM), BF16)
        vcat_s[last, t:t + WINDOW, :] = jnp.zeros((WINDOW, KV_DIM), BF16)
        ucarry_s[last] = jnp.zeros((8, CONV_DIM), F32)

    a_idx = lax.broadcasted_iota(jnp.int32, (WINDOW, 2 * WINDOW), 0)
    c_idx = lax.broadcasted_iota(jnp.int32, (WINDOW, 2 * WINDOW), 1)
    lane_lo = lax.broadcasted_iota(jnp.int32, (WINDOW, LANES), 1) < HEAD_DIM
    row = lax.broadcasted_iota(jnp.int32, (t, CONV_DIM), 0)
    n_pair = N_HEADS // 2
    cw = convw_ref[...]

    for h in range(N_SUB):
        src = (h - 1) % N_SUB
        rows = pl.ds(h * t, t)
        x = x_ref[0, rows, :]
        xn = _rms(x, g1_ref[...]).astype(BF16)

        kcat_s[h, 0:WINDOW, :] = kcat_s[src, t:t + WINDOW, :]
        vcat_s[h, 0:WINDOW, :] = vcat_s[src, t:t + WINDOW, :]
        q_s[h] = (_dot(xn, win_ref[:, 0:OFF_K]) * (HEAD_DIM ** -0.5)).astype(BF16)
        kv = _dot(xn, win_ref[:, OFF_K:OFF_B])
        k = kv[:, :KV_DIM]
        v = kv[:, KV_DIM:]
        kcat_s[h, WINDOW:, :] = k.astype(BF16)
        vcat_s[h, WINDOW:, :] = v.astype(BF16)
        if h == last:
            knew_ref[0] = jnp.transpose(k[t - WINDOW:, :])
            vnew_ref[0] = jnp.transpose(v[t - WINDOW:, :])

        for j in range(t // WINDOW):
            r0 = j * WINDOW
            kk = kcat_s[h, r0:r0 + 2 * WINDOW, :]
            vv = vcat_s[h, r0:r0 + 2 * WINDOW, :]
            c_min = jnp.where(i == 0, WINDOW, 0) if (h == 0 and j == 0) else 0
            valid = (c_idx >= jnp.maximum(a_idx, c_min)) & (c_idx <= a_idx + WINDOW)
            qcols = [q_s[h, r0:r0 + WINDOW, m * LANES:(m + 1) * LANES] for m in range(n_pair)]
            outs = []
            for half in range(2):
                keep = lane_lo if half == 0 else jnp.logical_not(lane_lo)
                qg = jnp.concatenate([jnp.where(keep, qc, jnp.zeros_like(qc)) for qc in qcols], axis=0)
                s = lax.dot_general(qg, kk, (((1,), (1,)), ((), ())), preferred_element_type=F32)
                es, dens = [], []
                for m in range(n_pair):
                    sm = jnp.where(valid, s[m * WINDOW:(m + 1) * WINDOW], NEG_INF)
                    sink = sinks_ref[m + half * n_pair]
                    mx = jnp.maximum(jnp.max(sm, axis=-1, keepdims=True), sink)
                    e = jnp.exp(sm - mx)
                    dens.append(jnp.sum(e, axis=-1, keepdims=True) + jnp.exp(sink - mx))
                    es.append(e.astype(BF16))
                o = _dot(jnp.concatenate(es, axis=0), vv)
                outs.append([o[m * WINDOW:(m + 1) * WINDOW] / dens[m] for m in range(n_pair)])
            for m in range(n_pair):
                attn_s[h, r0:r0 + WINDOW, m * LANES:(m + 1) * LANES] = jnp.where(lane_lo, outs[0][m], outs[1][m])

        gate_b = _dot(xn, win_ref[:, OFF_B:OFF_C])
        u = _dot(xn, win_ref[:, OFF_C:OFF_H]) * _dot(xn, win_ref[:, OFF_H:IN_PROJ_DIM])
        prev = ucarry_s[src]
        u1 = jnp.where(row == 0, prev[7:8, :], pltpu.roll(u, 1, axis=0))
        u2 = jnp.where(row == 0, prev[6:7, :], jnp.where(row == 1, prev[7:8, :], pltpu.roll(u, 2, axis=0)))
        z = cw[0:1, :] * u2 + cw[1:2, :] * u1 + cw[2:3, :] * u
        ucarry_s[h] = u[t - 8:, :]
        if h == last:
            unew_ref[0] = u[t - (CONV_WIDTH - 1):, :]

        x1, xn2_bf, gates = _mix_tail(x, attn_s[h], gate_b * z, gattn_ref[...], gconv_ref[...],
                                      wout_ref[...], g2_ref[...], wr_ref[...], br_ref[...])
        x1_ref[0, rows, :] = x1
        xn2_ref[0, rows, :] = xn2_bf
        gates_ref[0, rows, :] = gates


def _layer_prompt(x, sinks, g1, win, convw, gattn, gconv, wout, g2, wr, br, wg32, wu32, wd32):
    nb, seq, _ = x.shape
    t = T_LAYER
    ts = T_LAYER // N_SUB
    n_i = seq // t
    assert nb * n_i == N_EXPERTS, "one expert's weights are cast per grid step"
    const = lambda b, i, s: (0, 0)
    tile = lambda b, i, s: (b, i, 0)
    per_seq = lambda b, i, s: (b, 0, 0)
    per_step = lambda b, i, s: (b * n_i + i, 0, 0)
    resident = pl.Buffered(1)
    grid_spec = pltpu.PrefetchScalarGridSpec(
        num_scalar_prefetch=1,
        grid=(nb, seq // t),
        in_specs=[
            pl.BlockSpec((1, t, D_MODEL), tile),
            pl.BlockSpec((1, D_MODEL), const),
            pl.BlockSpec((D_MODEL, IN_PROJ_DIM), const, pipeline_mode=resident),
            pl.BlockSpec((CONV_WIDTH, CONV_DIM), const),
            pl.BlockSpec((1, ATTN_DIM), const),
            pl.BlockSpec((1, CONV_DIM), const),
            pl.BlockSpec((D_MODEL, D_MODEL), const, pipeline_mode=resident),
            pl.BlockSpec((1, D_MODEL), const),
            pl.BlockSpec((D_MODEL, 2 * ROUTER_LANES), const, pipeline_mode=resident),
            pl.BlockSpec((ROUTER_ROWS, 1), const),
            pl.BlockSpec((1, D_MODEL, D_EXPERT), per_step),
            pl.BlockSpec((1, D_MODEL, D_EXPERT), per_step),
            pl.BlockSpec((1, D_EXPERT, D_MODEL), per_step),
        ],
        out_specs=[
            pl.BlockSpec((1, t, D_MODEL), tile),
            pl.BlockSpec((1, t, D_MODEL), tile),
            pl.BlockSpec((1, t, ROUTER_LANES), tile),
            pl.BlockSpec((1, KV_DIM, WINDOW), per_seq),
            pl.BlockSpec((1, KV_DIM, WINDOW), per_seq),
            pl.BlockSpec((1, CONV_WIDTH - 1, CONV_DIM), per_seq),
            pl.BlockSpec((1, D_MODEL, D_EXPERT), per_step),
            pl.BlockSpec((1, D_MODEL, D_EXPERT), per_step),
            pl.BlockSpec((1, D_EXPERT, D_MODEL), per_step),
        ],
        scratch_shapes=[
            pltpu.VMEM((N_SUB, ts, ATTN_DIM), BF16),
            pltpu.VMEM((N_SUB, ts + WINDOW, KV_DIM), BF16),
            pltpu.VMEM((N_SUB, ts + WINDOW, KV_DIM), BF16),
            pltpu.VMEM((N_SUB, ts, ATTN_DIM), F32),
            pltpu.VMEM((N_SUB, 8, CONV_DIM), F32),
        ],
    )
    return pl.pallas_call(
        _layer_prompt_body,
        grid_spec=grid_spec,
        out_shape=[
            jax.ShapeDtypeStruct((nb, seq, D_MODEL), F32),
            jax.ShapeDtypeStruct((nb, seq, D_MODEL), BF16),
            jax.ShapeDtypeStruct((nb, seq, ROUTER_LANES), F32),
            jax.ShapeDtypeStruct((nb, KV_DIM, WINDOW), F32),
            jax.ShapeDtypeStruct((nb, KV_DIM, WINDOW), F32),
            jax.ShapeDtypeStruct((nb, CONV_WIDTH - 1, CONV_DIM), F32),
            jax.ShapeDtypeStruct((N_EXPERTS, D_MODEL, D_EXPERT), BF16),
            jax.ShapeDtypeStruct((N_EXPERTS, D_MODEL, D_EXPERT), BF16),
            jax.ShapeDtypeStruct((N_EXPERTS, D_EXPERT, D_MODEL), BF16),
        ],
        compiler_params=pltpu.CompilerParams(
            dimension_semantics=("arbitrary", "arbitrary"), vmem_limit_bytes=MOE_VMEM_LIMIT),
        name="layer_prompt",
    )(sinks, x, g1, win, convw, gattn, gconv, wout, g2, wr, br, wg32, wu32, wd32)


def _moe_body(cap, xc, x1_ref, xn2_ref, gates_ref, wg_ref, wu_ref, wd_ref, gf_ref, y_ref,
              xs_s, gs_s, ys_s):
    t = x1_ref.shape[0]
    per_group = N_EXPERTS // N_GROUPS
    gates = gates_ref[...]
    gid_row = jnp.transpose(gates)[GID_LANE:GID_LANE + 1, :]
    grp = lax.broadcasted_iota(jnp.int32, (8, t), 0).astype(F32)
    onehot = jnp.where(grp == gid_row, 1.0, 0.0)

    r_idx = lax.broadcasted_iota(jnp.int32, (t, t), 0)
    c_idx = lax.broadcasted_iota(jnp.int32, (t, t), 1)
    upper = jnp.where(r_idx <= c_idx, 1.0, 0.0).astype(BF16)
    csum = _dot(onehot.astype(BF16), upper)
    rank_row = jnp.sum(onehot * (csum - 1.0), axis=0, keepdims=True)
    cnt = [csum[g, t - 1].astype(jnp.int32) for g in range(N_GROUPS)]
    off = [jnp.int32(0)]
    for g in range(1, N_GROUPS):
        off.append(off[-1] + cnt[g - 1])

    pos_row = rank_row
    for g in range(1, N_GROUPS):
        pos_row = pos_row + jnp.where(gid_row == float(g), off[g].astype(F32), 0.0)
    pos_col = jnp.transpose(jnp.broadcast_to(pos_row, (LANES, t)))
    pos_col = jnp.concatenate([pos_col] * (t // LANES), axis=1)
    perm = jnp.where(r_idx.astype(F32) == pos_row, 1.0, 0.0).astype(BF16)
    perm_t = jnp.where(c_idx.astype(F32) == pos_col, 1.0, 0.0).astype(BF16)

    p1 = gates.astype(BF16)
    p2 = (gates - p1.astype(F32)).astype(BF16)
    moved = _dot(perm, jnp.concatenate([xn2_ref[...], p1, p2], axis=1))
    xs_s[0:t, :] = moved[:, :D_MODEL].astype(BF16)
    gs_s[0:t, :] = moved[:, D_MODEL:D_MODEL + ROUTER_LANES] + moved[:, D_MODEL + ROUTER_LANES:]
    pad = xs_s.shape[0] - t
    xs_s[t:, :] = jnp.zeros((pad, D_MODEL), BF16)
    gs_s[t:, :] = jnp.zeros((pad, ROUTER_LANES), F32)
    ys_s[...] = jnp.zeros(ys_s.shape, F32)

    def group_rows(g, r0, rows):
        xs = xs_s[pl.ds(r0, rows), :]
        gsc = gs_s[pl.ds(r0, rows), :]
        acts = []
        for k in range(per_group):
            e = g * per_group + k
            hg = _dot(xs, wg_ref[e])
            hu = _dot(xs, wu_ref[e])
            acts.append((hg / (1.0 + jnp.exp(-hg)) * hu * gsc[:, e:e + 1]).astype(BF16))
        ys_s[pl.ds(r0, rows), :] += _dot(jnp.concatenate(acts, axis=1), wd_ref[g])

    base = [(off[g] // ROW_ALIGN) * ROW_ALIGN for g in range(N_GROUPS)]
    for g in range(N_GROUPS):
        group_rows(g, pl.multiple_of(base[g], ROW_ALIGN), cap)
    for g in range(N_GROUPS):
        n_extra = jnp.maximum(0, (off[g] + cnt[g] - (base[g] + cap) + xc - 1) // xc)

        def extra(k, carry, g=g):
            group_rows(g, pl.multiple_of(base[g] + cap + k * xc, ROW_ALIGN), xc)
            return carry

        lax.fori_loop(0, n_extra, extra, 0)

    moe = _dot(perm_t, ys_s[0:t, :].astype(BF16))
    y_ref[...] = _rms(x1_ref[...] + moe, gf_ref[...])


def _moe(x1, xn2, route, wg, wu, wd, gf, t, cap, xc):
    n = x1.shape[0]
    tile = lambda i: (i, 0)
    whole = lambda i: (0, 0, 0)
    resident = pl.Buffered(1)
    rows = t + cap + xc
    hidden = (N_EXPERTS // N_GROUPS) * D_EXPERT
    return pl.pallas_call(
        functools.partial(_moe_body, cap, xc),
        grid=(n // t,),
        in_specs=[
            pl.BlockSpec((t, D_MODEL), tile),
            pl.BlockSpec((t, D_MODEL), tile),
            pl.BlockSpec((t, ROUTER_LANES), tile),
            pl.BlockSpec((N_EXPERTS, D_MODEL, D_EXPERT), whole, pipeline_mode=resident),
            pl.BlockSpec((N_EXPERTS, D_MODEL, D_EXPERT), whole, pipeline_mode=resident),
            pl.BlockSpec((N_GROUPS, hidden, D_MODEL), whole, pipeline_mode=resident),
            pl.BlockSpec((1, D_MODEL), lambda i: (0, 0)),
        ],
        out_specs=pl.BlockSpec((t, D_MODEL), tile),
        out_shape=jax.ShapeDtypeStruct((n, D_MODEL), F32),
        scratch_shapes=[
            pltpu.VMEM((rows, D_MODEL), BF16),
            pltpu.VMEM((rows, ROUTER_LANES), F32),
            pltpu.VMEM((rows, D_MODEL), F32),
        ],
        compiler_params=pltpu.CompilerParams(
            dimension_semantics=("arbitrary",), vmem_limit_bytes=MOE_VMEM_LIMIT),
        name="moe",
    )(x1, xn2, route, wg, wu, wd, gf)


def _proj_sample_body(x_ref, g1_ref, win_ref, proj_ref, kvt_ref):
    xn = _rms(x_ref[...], g1_ref[...]).astype(BF16)
    proj_ref[...] = _dot(xn, win_ref[...])
    kvt_ref[...] = lax.dot_general(win_ref[:, OFF_K:OFF_B], xn, (((0,), (1,)), ((), ())),
                                   preferred_element_type=F32)


def _proj_sample(x, g1, win):
    n = x.shape[0]
    return pl.pallas_call(
        _proj_sample_body,
        out_shape=[jax.ShapeDtypeStruct((n, IN_PROJ_DIM), F32),
                   jax.ShapeDtypeStruct((2 * KV_DIM, n), F32)],
        compiler_params=pltpu.CompilerParams(vmem_limit_bytes=VMEM_LIMIT),
        name="proj_sample",
    )(x, g1, win)


def _attn_sample_body(q8_ref, knew_ref, vnew_ref, kvt_ref, khist_ref, vhist_ref, sinks_ref,
                      out8_ref, kout_ref, vout_ref):
    chunk = q8_ref.shape[0]
    first = pl.program_id(0) * chunk
    q8 = q8_ref[...]
    kh = khist_ref[...]
    vh = vhist_ref[...]
    knew = knew_ref[...]
    vnew = vnew_ref[...]
    s = jnp.einsum('bhj,bjc->bhc', q8, kh.astype(BF16), preferred_element_type=F32)
    s_new = jnp.sum(q8.astype(F32) * knew, axis=-1, keepdims=True)
    sink = sinks_ref[...][None]
    mx = jnp.maximum(jnp.maximum(jnp.max(s, axis=-1, keepdims=True), s_new), sink)
    e = jnp.exp(s - mx)
    e_new = jnp.exp(s_new - mx)
    den = jnp.sum(e, axis=-1, keepdims=True) + e_new + jnp.exp(sink - mx)
    o = jnp.einsum('bhc,bjc->bhj', e.astype(BF16), vh.astype(BF16), preferred_element_type=F32)
    out8_ref[...] = (o + e_new * vnew) / den
    last = lax.broadcasted_iota(jnp.int32, (KV_DIM, WINDOW), 1) == WINDOW - 1
    knew_t = kvt_ref[0:KV_DIM, :]
    vnew_t = kvt_ref[KV_DIM:, :]
    for bb in range(chunk):
        shift = lax.rem(2 * WINDOW - 1 - (first + bb), WINDOW)
        kout_ref[bb] = jnp.where(last, pltpu.roll(knew_t, shift, axis=1), pltpu.roll(kh[bb], WINDOW - 1, axis=1))
        vout_ref[bb] = jnp.where(last, pltpu.roll(vnew_t, shift, axis=1), pltpu.roll(vh[bb], WINDOW - 1, axis=1))


def _attn_sample(q8, knew, vnew, kvt, khist, vhist, sinks_col):
    assert q8.shape[0] == WINDOW == LANES, "one lane per sequence in the channel-major projection"
    n = q8.shape[0]
    c = DEC_CHUNK
    blk3 = lambda i: (i, 0, 0)
    return pl.pallas_call(
        _attn_sample_body,
        grid=(n // c,),
        in_specs=[
            pl.BlockSpec((c, N_HEADS, LANES), blk3),
            pl.BlockSpec((c, 1, KV_DIM), blk3),
            pl.BlockSpec((c, 1, KV_DIM), blk3),
            pl.BlockSpec((2 * KV_DIM, n), lambda i: (0, 0)),
            pl.BlockSpec((c, KV_DIM, WINDOW), blk3),
            pl.BlockSpec((c, KV_DIM, WINDOW), blk3),
            pl.BlockSpec((N_HEADS, 1), lambda i: (0, 0)),
        ],
        out_specs=[
            pl.BlockSpec((c, N_HEADS, LANES), blk3),
            pl.BlockSpec((c, KV_DIM, WINDOW), blk3),
            pl.BlockSpec((c, KV_DIM, WINDOW), blk3),
        ],
        out_shape=[
            jax.ShapeDtypeStruct((n, N_HEADS, LANES), F32),
            jax.ShapeDtypeStruct((n, KV_DIM, WINDOW), F32),
            jax.ShapeDtypeStruct((n, KV_DIM, WINDOW), F32),
        ],
        compiler_params=pltpu.CompilerParams(
            dimension_semantics=("arbitrary",), vmem_limit_bytes=VMEM_LIMIT),
        name="attn_sample",
    )(q8, knew, vnew, kvt, khist, vhist, sinks_col)


def _tail_sample_body(x_ref, attn_ref, proj_ref, uprev2_ref, uprev1_ref, convw_ref, gattn_ref,
                      gconv_ref, wout_ref, g2_ref, wr_ref, br_ref,
                      x1_ref, xn2_ref, gates_ref, u_ref):
    gate_b = proj_ref[:, OFF_B:OFF_C]
    u = proj_ref[:, OFF_C:OFF_H] * proj_ref[:, OFF_H:IN_PROJ_DIM]
    cw = convw_ref[...]
    z = cw[0:1, :] * uprev2_ref[...] + cw[1:2, :] * uprev1_ref[...] + cw[2:3, :] * u
    u_ref[...] = u
    x1, xn2_bf, gates = _mix_tail(x_ref[...], attn_ref[...], gate_b * z, gattn_ref[...], gconv_ref[...],
                                  wout_ref[...], g2_ref[...], wr_ref[...], br_ref[...])
    x1_ref[...] = x1
    xn2_ref[...] = xn2_bf
    gates_ref[...] = gates


def _tail_sample(x, attn, proj, uprev2, uprev1, convw, gattn, gconv, wout, g2, wr, br):
    n = x.shape[0]
    return pl.pallas_call(
        _tail_sample_body,
        out_shape=[
            jax.ShapeDtypeStruct((n, D_MODEL), F32),
            jax.ShapeDtypeStruct((n, D_MODEL), BF16),
            jax.ShapeDtypeStruct((n, ROUTER_LANES), F32),
            jax.ShapeDtypeStruct((n, CONV_DIM), F32),
        ],
        compiler_params=pltpu.CompilerParams(vmem_limit_bytes=VMEM_LIMIT),
        name="tail_sample",
    )(x, attn, proj, uprev2, uprev1, convw, gattn, gconv, wout, g2, wr, br)


def _pair_heads(w, axis):
    shape = w.shape
    n_pair = N_HEADS // 2
    split = shape[:axis] + (2, n_pair, HEAD_DIM) + shape[axis + 1:]
    return jnp.swapaxes(w.reshape(split), axis, axis + 1).reshape(shape)


def kernel(x_prompt, x_sample, cache_k_win, cache_v_win, state_conv, norm1_g, w_in, conv_w, sinks,
           attn_out_g, conv_out_g, w_out, norm2_g, w_group, b_group, w_fine, b_fine, w_gate, w_up,
           w_down, final_g):
    nb, seq, _ = x_prompt.shape
    nd = x_sample.shape[0]

    w_in0 = w_in[0].astype(BF16)
    win = jnp.concatenate([_pair_heads(w_in0[:, :ATTN_DIM], 1), w_in0[:, ATTN_DIM:]], axis=1)
    w_out0 = w_out[0].astype(BF16)
    wout = jnp.concatenate([_pair_heads(w_out0[:ATTN_DIM], 0), w_out0[ATTN_DIM:]], axis=0)
    gattn = _pair_heads(attn_out_g[0], 0)[None]
    gconv = conv_out_g[0][None]
    g1 = norm1_g[0][None]
    g2 = norm2_g[0][None]
    gf = final_g[None]
    convw = conv_w[0]
    sinks0 = sinks[0]
    pad = ROUTER_LANES - N_EXPERTS - N_GROUPS
    wr32 = jnp.concatenate([w_fine[0], w_group[0], jnp.zeros((D_MODEL, pad), F32)], axis=1)
    wr_hi = wr32.astype(BF16)
    wr = jnp.concatenate([wr_hi, (wr32 - wr_hi.astype(F32)).astype(BF16)], axis=1)
    br = jnp.concatenate([b_fine[0], b_group[0], jnp.zeros((pad,), F32)])[:ROUTER_ROWS, None]

    x1p, xn2p, gatesp, kp, vp, up, wg, wu, wd = _layer_prompt(
        x_prompt, sinks0, g1, win, convw, gattn, gconv, wout, g2, wr, br, w_gate[0], w_up[0], w_down[0])
    wd = wd.reshape(N_GROUPS, (N_EXPERTS // N_GROUPS) * D_EXPERT, D_MODEL)
    n_p = nb * seq
    y_prompt = _moe(x1p.reshape(n_p, D_MODEL), xn2p.reshape(n_p, D_MODEL),
                    gatesp.reshape(n_p, ROUTER_LANES), wg, wu, wd, gf, T_MOE, MOE_CAP, MOE_XC).reshape(nb, seq, D_MODEL)

    xs = x_sample.reshape(nd, D_MODEL)
    proj, kvt = _proj_sample(xs, g1, win)
    lane = jnp.arange(LANES)
    qp = (proj[:, :ATTN_DIM] * (HEAD_DIM ** -0.5)).reshape(nd, N_HEADS // 2, LANES)
    q8 = jnp.concatenate([jnp.where(lane < HEAD_DIM, qp, 0.0), jnp.where(lane >= HEAD_DIM, qp, 0.0)],
                         axis=1).astype(BF16)
    knew = proj[:, OFF_K:OFF_V].reshape(nd, 1, KV_DIM)
    vnew = proj[:, OFF_V:OFF_B].reshape(nd, 1, KV_DIM)
    to_cm = lambda c: jnp.transpose(c[0], (0, 2, 3, 1)).reshape(nd, KV_DIM, WINDOW)
    from_cm = lambda c, n: jnp.transpose(c.reshape(n, KV_DIM // HEAD_DIM, HEAD_DIM, WINDOW), (0, 3, 1, 2))[None]
    out8, ks, vs = _attn_sample(q8, knew, vnew, kvt, to_cm(cache_k_win), to_cm(cache_v_win), sinks0[:, None])
    attn_s = jnp.where(lane < HEAD_DIM, out8[:, :N_HEADS // 2], out8[:, N_HEADS // 2:]).reshape(nd, ATTN_DIM)
    st = state_conv[0]
    x1s, xn2s, gatess, us = _tail_sample(xs, attn_s, proj, st[:, 0], st[:, 1], convw, gattn, gconv,
                                         wout, g2, wr, br)
    y_sample = _moe(x1s, xn2s, gatess, wg, wu, wd, gf, nd, MOE_CAP_DEC, MOE_XC_DEC).reshape(nd, 1, D_MODEL)

    return (y_prompt, y_sample,
            from_cm(kp, nb), from_cm(vp, nb),
            up[None],
            from_cm(ks, nd), from_cm(vs, nd),
            jnp.stack([st[:, 1], us], axis=1)[None])
```

```python
import functools

import jax
import jax.numpy as jnp
from jax import lax
from jax.experimental import pallas as pl
from jax.experimental.pallas import tpu as pltpu

D_MODEL = 1024
HEAD_DIM = 64
N_HEADS = 8
ATTN_DIM = 512
KV_DIM = 128
WINDOW = 128
CONV_DIM = 512
CONV_WIDTH = 3
OFF_K = 512
OFF_V = 640
OFF_B = 768
OFF_C = 1280
OFF_H = 1792
IN_PROJ_DIM = 2304
N_GROUPS = 4
N_EXPERTS = 16
D_EXPERT = 256
RMS_EPS = 1e-5
NEG_INF = -1e30

LANES = 128
ROW_ALIGN = 16
ROUTER_LANES = 128
ROUTER_ROWS = 32
COARSE_OFF = N_EXPERTS
T_LAYER = 1024
N_SUB = 2
GID_LANE = N_EXPERTS
T_MOE = 512
MOE_CAP, MOE_XC = 160, 64
MOE_CAP_DEC, MOE_XC_DEC = 64, 32
DEC_CHUNK = 32
VMEM_LIMIT = 48 * 1024 * 1024
MOE_VMEM_LIMIT = 56 * 1024 * 1024

BF16 = jnp.bfloat16
F32 = jnp.float32


def _dot(a, b):
    return jnp.dot(a, b, preferred_element_type=F32)


def _rms(x, g):
    ms = jnp.mean(x * x, axis=-1, keepdims=True)
    return x * lax.rsqrt(ms + RMS_EPS) * g


def _group_norm64(y, g):
    rows, cols = y.shape
    lo = lax.broadcasted_iota(jnp.int32, (rows, LANES), 1) < HEAD_DIM
    outs = []
    for c in range(cols // LANES):
        blk = y[:, c * LANES:(c + 1) * LANES]
        sq = blk * blk
        s_lo = jnp.sum(jnp.where(lo, sq, 0.0), axis=-1, keepdims=True)
        s_hi = jnp.sum(jnp.where(lo, 0.0, sq), axis=-1, keepdims=True)
        ms = jnp.where(lo, s_lo, s_hi) * (1.0 / HEAD_DIM)
        outs.append(blk * lax.rsqrt(ms + RMS_EPS))
    return jnp.concatenate(outs, axis=-1) * g


def _router(xn2, wr, br):
    hi = xn2.astype(BF16)
    both = _dot(hi, wr)
    logits = jnp.transpose(both[:, :ROUTER_LANES] + both[:, ROUTER_LANES:])[:ROUTER_ROWS] + br
    n = logits.shape[1]
    row = lax.broadcasted_iota(jnp.int32, (ROUTER_ROWS, n), 0)
    rowf = row.astype(F32)
    big = float(ROUTER_ROWS)
    coarse = (row >= COARSE_OFF) & (row < COARSE_OFF + N_GROUPS)
    cm = jnp.max(jnp.where(coarse, logits, -jnp.inf), axis=0, keepdims=True)
    g_sel = jnp.min(jnp.where(coarse & (logits == cm), rowf - COARSE_OFF, big), axis=0, keepdims=True)
    zc = jnp.sum(jnp.where(coarse, jnp.exp(logits - cm), 0.0), axis=0, keepdims=True)
    p_sel = 1.0 / zc
    per_group = N_EXPERTS // N_GROUPS
    fine = (row < N_EXPERTS) & ((row // per_group).astype(F32) == g_sel)
    f1 = jnp.max(jnp.where(fine, logits, -jnp.inf), axis=0, keepdims=True)
    i1 = jnp.min(jnp.where(fine & (logits == f1), rowf, big), axis=0, keepdims=True)
    rest = fine & (rowf != i1)
    f2 = jnp.max(jnp.where(rest, logits, -jnp.inf), axis=0, keepdims=True)
    i2 = jnp.min(jnp.where(rest & (logits == f2), rowf, big), axis=0, keepdims=True)
    e2 = jnp.exp(f2 - f1)
    t1 = 1.0 / (1.0 + e2)
    t2 = e2 * t1
    gates_t = jnp.where(rowf == i1, p_sel * t1, jnp.where(rowf == i2, p_sel * t2, 0.0))
    gates_t = jnp.where(row == GID_LANE, g_sel, gates_t)
    gates_t = jnp.concatenate([gates_t, jnp.zeros((ROUTER_LANES - ROUTER_ROWS, n), F32)], axis=0)
    return hi, jnp.transpose(gates_t)


def _mix_tail(x, attn, conv_out, gattn, gconv, wout, g2, wr, br):
    mixed = jnp.concatenate([_group_norm64(attn, gattn), _group_norm64(conv_out, gconv)],
                            axis=-1).astype(BF16)
    x1 = x + _dot(mixed, wout)
    xn2_bf, gates = _router(_rms(x1, g2), wr, br)
    return x1, xn2_bf, gates


def _layer_prompt_body(sinks_ref, x_ref, g1_ref, win_ref, convw_ref, gattn_ref, gconv_ref,
                       wout_ref, g2_ref, wr_ref, br_ref, wg32_ref, wu32_ref, wd32_ref,
                       x1_ref, xn2_ref, gates_ref, knew_ref, vnew_ref, unew_ref,
                       wg16_ref, wu16_ref, wd16_ref,
                       q_s, kcat_s, vcat_s, attn_s, ucarry_s):
    i = pl.program_id(1)
    wg16_ref[...] = wg32_ref[...].astype(BF16)
    wu16_ref[...] = wu32_ref[...].astype(BF16)
    wd16_ref[...] = wd32_ref[...].astype(BF16)
    t = T_LAYER // N_SUB
    last = N_SUB - 1

    @pl.when(i == 0)
    def _():
        kcat_s[last, t:t + WINDOW, :] = jnp.zeros((WINDOW, KV_DIM), BF16)
        vcat_s[last, t:t + WINDOW, :] = jnp.zeros((WINDOW, KV_DIM), BF16)
        ucarry_s[last] = jnp.zeros((8, CONV_DIM), F32)

    a_idx = lax.broadcasted_iota(jnp.int32, (WINDOW, 2 * WINDOW), 0)
    c_idx = lax.broadcasted_iota(jnp.int32, (WINDOW, 2 * WINDOW), 1)
    lane_lo = lax.broadcasted_iota(jnp.int32, (WINDOW, LANES), 1) < HEAD_DIM
    row = lax.broadcasted_iota(jnp.int32, (t, CONV_DIM), 0)
    n_pair = N_HEADS // 2
    cw = convw_ref[...]

    for h in range(N_SUB):
        src = (h - 1) % N_SUB
        rows = pl.ds(h * t, t)
        x = x_ref[0, rows, :]
        xn = _rms(x, g1_ref[...]).astype(BF16)

        kcat_s[h, 0:WINDOW, :] = kcat_s[src, t:t + WINDOW, :]
        vcat_s[h, 0:WINDOW, :] = vcat_s[src, t:t + WINDOW, :]
        q_s[h] = (_dot(xn, win_ref[:, 0:OFF_K]) * (HEAD_DIM ** -0.5)).astype(BF16)
        kv = _dot(xn, win_ref[:, OFF_K:OFF_B])
        k = kv[:, :KV_DIM]
        v = kv[:, KV_DIM:]
        kcat_s[h, WINDOW:, :] = k.astype(BF16)
        vcat_s[h, WINDOW:, :] = v.astype(BF16)
        if h == last:
            knew_ref[0] = jnp.transpose(k[t - WINDOW:, :])
            vnew_ref[0] = jnp.transpose(v[t - WINDOW:, :])

        for j in range(t // WINDOW):
            r0 = j * WINDOW
            kk = kcat_s[h, r0:r0 + 2 * WINDOW, :]
            vv = vcat_s[h, r0:r0 + 2 * WINDOW, :]
            c_min = jnp.where(i == 0, WINDOW, 0) if (h == 0 and j == 0) else 0
            valid = (c_idx >= jnp.maximum(a_idx, c_min)) & (c_idx <= a_idx + WINDOW)
            qcols = [q_s[h, r0:r0 + WINDOW, m * LANES:(m + 1) * LANES] for m in range(n_pair)]
            outs = []
            for half in range(2):
                keep = lane_lo if half == 0 else jnp.logical_not(lane_lo)
                qg = jnp.concatenate([jnp.where(keep, qc, jnp.zeros_like(qc)) for qc in qcols], axis=0)
                s = lax.dot_general(qg, kk, (((1,), (1,)), ((), ())), preferred_element_type=F32)
                es, dens = [], []
                for m in range(n_pair):
                    sm = jnp.where(valid, s[m * WINDOW:(m + 1) * WINDOW], NEG_INF)
                    sink = sinks_ref[m + half * n_pair]
                    mx = jnp.maximum(jnp.max(sm, axis=-1, keepdims=True), sink)
                    e = jnp.exp(sm - mx)
                    dens.append(jnp.sum(e, axis=-1, keepdims=True) + jnp.exp(sink - mx))
                    es.append(e.astype(BF16))
                o = _dot(jnp.concatenate(es, axis=0), vv)
                outs.append([o[m * WINDOW:(m + 1) * WINDOW] / dens[m] for m in range(n_pair)])
            for m in range(n_pair):
                attn_s[h, r0:r0 + WINDOW, m * LANES:(m + 1) * LANES] = jnp.where(lane_lo, outs[0][m], outs[1][m])

        gate_b = _dot(xn, win_ref[:, OFF_B:OFF_C])
        u = _dot(xn, win_ref[:, OFF_C:OFF_H]) * _dot(xn, win_ref[:, OFF_H:IN_PROJ_DIM])
        prev = ucarry_s[src]
        u1 = jnp.where(row == 0, prev[7:8, :], pltpu.roll(u, 1, axis=0))
        u2 = jnp.where(row == 0, prev[6:7, :], jnp.where(row == 1, prev[7:8, :], pltpu.roll(u, 2, axis=0)))
        z = cw[0:1, :] * u2 + cw[1:2, :] * u1 + cw[2:3, :] * u
        ucarry_s[h] = u[t - 8:, :]
        if h == last:
            unew_ref[0] = u[t - (CONV_WIDTH - 1):, :]

        x1, xn2_bf, gates = _mix_tail(x, attn_s[h], gate_b * z, gattn_ref[...], gconv_ref[...],
                                      wout_ref[...], g2_ref[...], wr_ref[...], br_ref[...])
        x1_ref[0, rows, :] = x1
        xn2_ref[0, rows, :] = xn2_bf
        gates_ref[0, rows, :] = gates


def _layer_prompt(x, sinks, g1, win, convw, gattn, gconv, wout, g2, wr, br, wg32, wu32, wd32):
    nb, seq, _ = x.shape
    t = T_LAYER
    ts = T_LAYER // N_SUB
    n_i = seq // t
    assert nb * n_i == N_EXPERTS, "one expert's weights are cast per grid step"
    const = lambda b, i, s: (0, 0)
    tile = lambda b, i, s: (b, i, 0)
    per_seq = lambda b, i, s: (b, 0, 0)
    per_step = lambda b, i, s: (b * n_i + i, 0, 0)
    resident = pl.Buffered(1)
    grid_spec = pltpu.PrefetchScalarGridSpec(
        num_scalar_prefetch=1,
        grid=(nb, seq // t),
        in_specs=[
            pl.BlockSpec((1, t, D_MODEL), tile),
            pl.BlockSpec((1, D_MODEL), const),
            pl.BlockSpec((D_MODEL, IN_PROJ_DIM), const, pipeline_mode=resident),
            pl.BlockSpec((CONV_WIDTH, CONV_DIM), const),
            pl.BlockSpec((1, ATTN_DIM), const),
            pl.BlockSpec((1, CONV_DIM), const),
            pl.BlockSpec((D_MODEL, D_MODEL), const, pipeline_mode=resident),
            pl.BlockSpec((1, D_MODEL), const),
            pl.BlockSpec((D_MODEL, 2 * ROUTER_LANES), const, pipeline_mode=resident),
            pl.BlockSpec((ROUTER_ROWS, 1), const),
            pl.BlockSpec((1, D_MODEL, D_EXPERT), per_step),
            pl.BlockSpec((1, D_MODEL, D_EXPERT), per_step),
            pl.BlockSpec((1, D_EXPERT, D_MODEL), per_step),
        ],
        out_specs=[
            pl.BlockSpec((1, t, D_MODEL), tile),
            pl.BlockSpec((1, t, D_MODEL), tile),
            pl.BlockSpec((1, t, ROUTER_LANES), tile),
            pl.BlockSpec((1, KV_DIM, WINDOW), per_seq),
            pl.BlockSpec((1, KV_DIM, WINDOW), per_seq),
            pl.BlockSpec((1, CONV_WIDTH - 1, CONV_DIM), per_seq),
            pl.BlockSpec((1, D_MODEL, D_EXPERT), per_step),
            pl.BlockSpec((1, D_MODEL, D_EXPERT), per_step),
            pl.BlockSpec((1, D_EXPERT, D_MODEL), per_step),
        ],
        scratch_shapes=[
            pltpu.VMEM((N_SUB, ts, ATTN_DIM), BF16),
            pltpu.VMEM((N_SUB, ts + WINDOW, KV_DIM), BF16),
            pltpu.VMEM((N_SUB, ts + WINDOW, KV_DIM), BF16),
            pltpu.VMEM((N_SUB, ts, ATTN_DIM), F32),
            pltpu.VMEM((N_SUB, 8, CONV_DIM), F32),
        ],
    )
    return pl.pallas_call(
        _layer_prompt_body,
        grid_spec=grid_spec,
        out_shape=[
            jax.ShapeDtypeStruct((nb, seq, D_MODEL), F32),
            jax.ShapeDtypeStruct((nb, seq, D_MODEL), BF16),
            jax.ShapeDtypeStruct((nb, seq, ROUTER_LANES), F32),
            jax.ShapeDtypeStruct((nb, KV_DIM, WINDOW), F32),
            jax.ShapeDtypeStruct((nb, KV_DIM, WINDOW), F32),
            jax.ShapeDtypeStruct((nb, CONV_WIDTH - 1, CONV_DIM), F32),
            jax.ShapeDtypeStruct((N_EXPERTS, D_MODEL, D_EXPERT), BF16),
            jax.ShapeDtypeStruct((N_EXPERTS, D_MODEL, D_EXPERT), BF16),
            jax.ShapeDtypeStruct((N_EXPERTS, D_EXPERT, D_MODEL), BF16),
        ],
        compiler_params=pltpu.CompilerParams(
            dimension_semantics=("arbitrary", "arbitrary"), vmem_limit_bytes=MOE_VMEM_LIMIT),
        name="layer_prompt",
    )(sinks, x, g1, win, convw, gattn, gconv, wout, g2, wr, br, wg32, wu32, wd32)


def _moe_body(cap, xc, x1_ref, xn2_ref, gates_ref, wg_ref, wu_ref, wd_ref, gf_ref, y_ref,
              xs_s, gs_s, ys_s):
    t = x1_ref.shape[0]
    per_group = N_EXPERTS // N_GROUPS
    gates = gates_ref[...]
    gid_row = jnp.transpose(gates)[GID_LANE:GID_LANE + 1, :]
    grp = lax.broadcasted_iota(jnp.int32, (8, t), 0).astype(F32)
    onehot = jnp.where(grp == gid_row, 1.0, 0.0)

    r_idx = lax.broadcasted_iota(jnp.int32, (t, t), 0)
    c_idx = lax.broadcasted_iota(jnp.int32, (t, t), 1)
    upper = jnp.where(r_idx <= c_idx, 1.0, 0.0).astype(BF16)
    csum = _dot(onehot.astype(BF16), upper)
    rank_row = jnp.sum(onehot * (csum - 1.0), axis=0, keepdims=True)
    cnt = [csum[g, t - 1].astype(jnp.int32) for g in range(N_GROUPS)]
    off = [jnp.int32(0)]
    for g in range(1, N_GROUPS):
        off.append(off[-1] + cnt[g - 1])

    pos_row = rank_row
    for g in range(1, N_GROUPS):
        pos_row = pos_row + jnp.where(gid_row == float(g), off[g].astype(F32), 0.0)
    pos_col = jnp.transpose(jnp.broadcast_to(pos_row, (LANES, t)))
    pos_col = jnp.concatenate([pos_col] * (t // LANES), axis=1)
    perm = jnp.where(r_idx.astype(F32) == pos_row, 1.0, 0.0).astype(BF16)
    perm_t = jnp.where(c_idx.astype(F32) == pos_col, 1.0, 0.0).astype(BF16)

    p1 = gates.astype(BF16)
    p2 = (gates - p1.astype(F32)).astype(BF16)
    moved = _dot(perm, jnp.concatenate([xn2_ref[...], p1, p2], axis=1))
    xs_s[0:t, :] = moved[:, :D_MODEL].astype(BF16)
    gs_s[0:t, :] = moved[:, D_MODEL:D_MODEL + ROUTER_LANES] + moved[:, D_MODEL + ROUTER_LANES:]
    pad = xs_s.shape[0] - t
    xs_s[t:, :] = jnp.zeros((pad, D_MODEL), BF16)
    gs_s[t:, :] = jnp.zeros((pad, ROUTER_LANES), F32)
    ys_s[...] = jnp.zeros(ys_s.shape, F32)

    def group_rows(g, r0, rows):
        xs = xs_s[pl.ds(r0, rows), :]
        gsc = gs_s[pl.ds(r0, rows), :]
        acts = []
        for k in range(per_group):
            e = g * per_group + k
            hg = _dot(xs, wg_ref[e])
            hu = _dot(xs, wu_ref[e])
            acts.append((hg / (1.0 + jnp.exp(-hg)) * hu * gsc[:, e:e + 1]).astype(BF16))
        ys_s[pl.ds(r0, rows), :] += _dot(jnp.concatenate(acts, axis=1), wd_ref[g])

    base = [(off[g] // ROW_ALIGN) * ROW_ALIGN for g in range(N_GROUPS)]
    for g in range(N_GROUPS):
        group_rows(g, pl.multiple_of(base[g], ROW_ALIGN), cap)
    for g in range(N_GROUPS):
        n_extra = jnp.maximum(0, (off[g] + cnt[g] - (base[g] + cap) + xc - 1) // xc)

        def extra(k, carry, g=g):
            group_rows(g, pl.multiple_of(base[g] + cap + k * xc, ROW_ALIGN), xc)
            return carry

        lax.fori_loop(0, n_extra, extra, 0)

    moe = _dot(perm_t, ys_s[0:t, :].astype(BF16))
    y_ref[...] = _rms(x1_ref[...] + moe, gf_ref[...])


def _moe(x1, xn2, gates, wg, wu, wd, gf, t, cap, xc):
    n = x1.shape[0]
    tile = lambda i: (i, 0)
    whole = lambda i: (0, 0, 0)
    resident = pl.Buffered(1)
    rows = t + cap + xc
    hidden = (N_EXPERTS // N_GROUPS) * D_EXPERT
    return pl.pallas_call(
        functools.partial(_moe_body, cap, xc),
        grid=(n // t,),
        in_specs=[
            pl.BlockSpec((t, D_MODEL), tile),
            pl.BlockSpec((t, D_MODEL), tile),
            pl.BlockSpec((t, ROUTER_LANES), tile),
            pl.BlockSpec((N_EXPERTS, D_MODEL, D_EXPERT), whole, pipeline_mode=resident),
            pl.BlockSpec((N_EXPERTS, D_MODEL, D_EXPERT), whole, pipeline_mode=resident),
            pl.BlockSpec((N_GROUPS, hidden, D_MODEL), whole, pipeline_mode=resident),
            pl.BlockSpec((1, D_MODEL), lambda i: (0, 0)),
        ],
        out_specs=pl.BlockSpec((t, D_MODEL), tile),
        out_shape=jax.ShapeDtypeStruct((n, D_MODEL), F32),
        scratch_shapes=[
            pltpu.VMEM((rows, D_MODEL), BF16),
            pltpu.VMEM((rows, ROUTER_LANES), F32),
            pltpu.VMEM((rows, D_MODEL), F32),
        ],
        compiler_params=pltpu.CompilerParams(
            dimension_semantics=("arbitrary",), vmem_limit_bytes=MOE_VMEM_LIMIT),
        name="moe",
    )(x1, xn2, gates, wg, wu, wd, gf)


def _proj_sample_body(x_ref, g1_ref, win_ref, proj_ref, kvt_ref):
    xn = _rms(x_ref[...], g1_ref[...]).astype(BF16)
    proj_ref[...] = _dot(xn, win_ref[...])
    kvt_ref[...] = lax.dot_general(win_ref[:, OFF_K:OFF_B], xn, (((0,), (1,)), ((), ())),
                                   preferred_element_type=F32)


def _proj_sample(x, g1, win):
    n = x.shape[0]
    return pl.pallas_call(
        _proj_sample_body,
        out_shape=[jax.ShapeDtypeStruct((n, IN_PROJ_DIM), F32),
                   jax.ShapeDtypeStruct((2 * KV_DIM, n), F32)],
        compiler_params=pltpu.CompilerParams(vmem_limit_bytes=VMEM_LIMIT),
        name="proj_sample",
    )(x, g1, win)


def _attn_sample_body(q8_ref, knew_ref, vnew_ref, kvt_ref, khist_ref, vhist_ref, sinks_ref,
                      out8_ref, kout_ref, vout_ref):
    chunk = q8_ref.shape[0]
    first = pl.program_id(0) * chunk
    q8 = q8_ref[...]
    kh = khist_ref[...]
    vh = vhist_ref[...]
    knew = knew_ref[...]
    vnew = vnew_ref[...]
    s = jnp.einsum('bhj,bjc->bhc', q8, kh.astype(BF16), preferred_element_type=F32)
    s_new = jnp.sum(q8.astype(F32) * knew, axis=-1, keepdims=True)
    sink = sinks_ref[...][None]
    mx = jnp.maximum(jnp.maximum(jnp.max(s, axis=-1, keepdims=True), s_new), sink)
    e = jnp.exp(s - mx)
    e_new = jnp.exp(s_new - mx)
    den = jnp.sum(e, axis=-1, keepdims=True) + e_new + jnp.exp(sink - mx)
    o = jnp.einsum('bhc,bjc->bhj', e.astype(BF16), vh.astype(BF16), preferred_element_type=F32)
    out8_ref[...] = (o + e_new * vnew) / den
    last = lax.broadcasted_iota(jnp.int32, (KV_DIM, WINDOW), 1) == WINDOW - 1
    knew_t = kvt_ref[0:KV_DIM, :]
    vnew_t = kvt_ref[KV_DIM:, :]
    for bb in range(chunk):
        shift = lax.rem(2 * WINDOW - 1 - (first + bb), WINDOW)
        kout_ref[bb] = jnp.where(last, pltpu.roll(knew_t, shift, axis=1), pltpu.roll(kh[bb], WINDOW - 1, axis=1))
        vout_ref[bb] = jnp.where(last, pltpu.roll(vnew_t, shift, axis=1), pltpu.roll(vh[bb], WINDOW - 1, axis=1))


def _attn_sample(q8, knew, vnew, kvt, khist, vhist, sinks_col):
    assert q8.shape[0] == WINDOW == LANES, "one lane per sequence in the channel-major projection"
    n = q8.shape[0]
    c = DEC_CHUNK
    blk3 = lambda i: (i, 0, 0)
    return pl.pallas_call(
        _attn_sample_body,
        grid=(n // c,),
        in_specs=[
            pl.BlockSpec((c, N_HEADS, LANES), blk3),
            pl.BlockSpec((c, 1, KV_DIM), blk3),
            pl.BlockSpec((c, 1, KV_DIM), blk3),
            pl.BlockSpec((2 * KV_DIM, n), lambda i: (0, 0)),
            pl.BlockSpec((c, KV_DIM, WINDOW), blk3),
            pl.BlockSpec((c, KV_DIM, WINDOW), blk3),
            pl.BlockSpec((N_HEADS, 1), lambda i: (0, 0)),
        ],
        out_specs=[
            pl.BlockSpec((c, N_HEADS, LANES), blk3),
            pl.BlockSpec((c, KV_DIM, WINDOW), blk3),
            pl.BlockSpec((c, KV_DIM, WINDOW), blk3),
        ],
        out_shape=[
            jax.ShapeDtypeStruct((n, N_HEADS, LANES), F32),
            jax.ShapeDtypeStruct((n, KV_DIM, WINDOW), F32),
            jax.ShapeDtypeStruct((n, KV_DIM, WINDOW), F32),
        ],
        compiler_params=pltpu.CompilerParams(
            dimension_semantics=("arbitrary",), vmem_limit_bytes=VMEM_LIMIT),
        name="attn_sample",
    )(q8, knew, vnew, kvt, khist, vhist, sinks_col)


def _tail_sample_body(x_ref, attn_ref, proj_ref, uprev2_ref, uprev1_ref, convw_ref, gattn_ref,
                      gconv_ref, wout_ref, g2_ref, wr_ref, br_ref,
                      x1_ref, xn2_ref, gates_ref, u_ref):
    gate_b = proj_ref[:, OFF_B:OFF_C]
    u = proj_ref[:, OFF_C:OFF_H] * proj_ref[:, OFF_H:IN_PROJ_DIM]
    cw = convw_ref[...]
    z = cw[0:1, :] * uprev2_ref[...] + cw[1:2, :] * uprev1_ref[...] + cw[2:3, :] * u
    u_ref[...] = u
    x1, xn2_bf, gates = _mix_tail(x_ref[...], attn_ref[...], gate_b * z, gattn_ref[...], gconv_ref[...],
                                  wout_ref[...], g2_ref[...], wr_ref[...], br_ref[...])
    x1_ref[...] = x1
    xn2_ref[...] = xn2_bf
    gates_ref[...] = gates


def _tail_sample(x, attn, proj, uprev2, uprev1, convw, gattn, gconv, wout, g2, wr, br):
    n = x.shape[0]
    return pl.pallas_call(
        _tail_sample_body,
        out_shape=[
            jax.ShapeDtypeStruct((n, D_MODEL), F32),
            jax.ShapeDtypeStruct((n, D_MODEL), BF16),
            jax.ShapeDtypeStruct((n, ROUTER_LANES), F32),
            jax.ShapeDtypeStruct((n, CONV_DIM), F32),
        ],
        compiler_params=pltpu.CompilerParams(vmem_limit_bytes=VMEM_LIMIT),
        name="tail_sample",
    )(x, attn, proj, uprev2, uprev1, convw, gattn, gconv, wout, g2, wr, br)


def _pair_heads(w, axis):
    shape = w.shape
    n_pair = N_HEADS // 2
    split = shape[:axis] + (2, n_pair, HEAD_DIM) + shape[axis + 1:]
    return jnp.swapaxes(w.reshape(split), axis, axis + 1).reshape(shape)


def kernel(x_prompt, x_sample, cache_k_win, cache_v_win, state_conv, norm1_g, w_in, conv_w, sinks,
           attn_out_g, conv_out_g, w_out, norm2_g, w_group, b_group, w_fine, b_fine, w_gate, w_up,
           w_down, final_g):
    nb, seq, _ = x_prompt.shape
    nd = x_sample.shape[0]

    w_in0 = w_in[0].astype(BF16)
    win = jnp.concatenate([_pair_heads(w_in0[:, :ATTN_DIM], 1), w_in0[:, ATTN_DIM:]], axis=1)
    w_out0 = w_out[0].astype(BF16)
    wout = jnp.concatenate([_pair_heads(w_out0[:ATTN_DIM], 0), w_out0[ATTN_DIM:]], axis=0)
    gattn = _pair_heads(attn_out_g[0], 0)[None]
    gconv = conv_out_g[0][None]
    g1 = norm1_g[0][None]
    g2 = norm2_g[0][None]
    gf = final_g[None]
    convw = conv_w[0]
    sinks0 = sinks[0]
    pad = ROUTER_LANES - N_EXPERTS - N_GROUPS
    wr32 = jnp.concatenate([w_fine[0], w_group[0], jnp.zeros((D_MODEL, pad), F32)], axis=1)
    wr_hi = wr32.astype(BF16)
    wr = jnp.concatenate([wr_hi, (wr32 - wr_hi.astype(F32)).astype(BF16)], axis=1)
    br = jnp.concatenate([b_fine[0], b_group[0], jnp.zeros((pad,), F32)])[:ROUTER_ROWS, None]

    x1p, xn2p, gatesp, kp, vp, up, wg, wu, wd = _layer_prompt(
        x_prompt, sinks0, g1, win, convw, gattn, gconv, wout, g2, wr, br, w_gate[0], w_up[0], w_down[0])
    wd = wd.reshape(N_GROUPS, (N_EXPERTS // N_GROUPS) * D_EXPERT, D_MODEL)
    n_p = nb * seq
    y_prompt = _moe(x1p.reshape(n_p, D_MODEL), xn2p.reshape(n_p, D_MODEL),
                    gatesp.reshape(n_p, ROUTER_LANES), wg, wu, wd, gf, T_MOE, MOE_CAP, MOE_XC).reshape(nb, seq, D_MODEL)

    xs = x_sample.reshape(nd, D_MODEL)
    proj, kvt = _proj_sample(xs, g1, win)
    lane = jnp.arange(LANES)
    qp = (proj[:, :ATTN_DIM] * (HEAD_DIM ** -0.5)).reshape(nd, N_HEADS // 2, LANES)
    q8 = jnp.concatenate([jnp.where(lane < HEAD_DIM, qp, 0.0), jnp.where(lane >= HEAD_DIM, qp, 0.0)],
                         axis=1).astype(BF16)
    knew = proj[:, OFF_K:OFF_V].reshape(nd, 1, KV_DIM)
    vnew = proj[:, OFF_V:OFF_B].reshape(nd, 1, KV_DIM)
    to_cm = lambda c: jnp.transpose(c[0], (0, 2, 3, 1)).reshape(nd, KV_DIM, WINDOW)
    from_cm = lambda c, n: jnp.transpose(c.reshape(n, KV_DIM // HEAD_DIM, HEAD_DIM, WINDOW), (0, 3, 1, 2))[None]
    out8, ks, vs = _attn_sample(q8, knew, vnew, kvt, to_cm(cache_k_win), to_cm(cache_v_win), sinks0[:, None])
    attn_s = jnp.where(lane < HEAD_DIM, out8[:, :N_HEADS // 2], out8[:, N_HEADS // 2:]).reshape(nd, ATTN_DIM)
    st = state_conv[0]
    x1s, xn2s, gatess, us = _tail_sample(xs, attn_s, proj, st[:, 0], st[:, 1], convw, gattn, gconv,
                                         wout, g2, wr, br)
    y_sample = _moe(x1s, xn2s, gatess, wg, wu, wd, gf, nd, MOE_CAP_DEC, MOE_XC_DEC).reshape(nd, 1, D_MODEL)

    return (y_prompt, y_sample,
            from_cm(kp, nb), from_cm(vp, nb),
            up[None],
            from_cm(ks, nd), from_cm(vs, nd),
            jnp.stack([st[:, 1], us], axis=1)[None])
```

```python
import functools

import jax
import jax.numpy as jnp
from jax import lax
from jax.experimental import pallas as pl
from jax.experimental.pallas import tpu as pltpu

D_MODEL = 1024
HEAD_DIM = 64
N_HEADS = 8
ATTN_DIM = 512
KV_DIM = 128
WINDOW = 128
CONV_DIM = 512
CONV_WIDTH = 3
OFF_K = 512
OFF_V = 640
OFF_B = 768
OFF_C = 1280
OFF_H = 1792
IN_PROJ_DIM = 2304
N_GROUPS = 4
N_EXPERTS = 16
D_EXPERT = 256
RMS_EPS = 1e-5
NEG_INF = -1e30

LANES = 128
ROW_ALIGN = 16
ROUTER_LANES = 128
ROUTER_ROWS = 32
COARSE_OFF = N_EXPERTS
T_LAYER = 1024
N_SUB = 2
GID_LANE = N_EXPERTS
T_MOE = 512
MOE_CAP, MOE_XC = 160, 64
MOE_CAP_DEC, MOE_XC_DEC = 64, 32
DEC_CHUNK = 64
VMEM_LIMIT = 48 * 1024 * 1024
MOE_VMEM_LIMIT = 56 * 1024 * 1024

BF16 = jnp.bfloat16
F32 = jnp.float32


def _dot(a, b):
    return jnp.dot(a, b, preferred_element_type=F32)


def _rms(x, g):
    ms = jnp.mean(x * x, axis=-1, keepdims=True)
    return x * lax.rsqrt(ms + RMS_EPS) * g


def _group_norm64(y, g):
    rows, cols = y.shape
    lo = lax.broadcasted_iota(jnp.int32, (rows, LANES), 1) < HEAD_DIM
    outs = []
    for c in range(cols // LANES):
        blk = y[:, c * LANES:(c + 1) * LANES]
        sq = blk * blk
        s_lo = jnp.sum(jnp.where(lo, sq, 0.0), axis=-1, keepdims=True)
        s_hi = jnp.sum(jnp.where(lo, 0.0, sq), axis=-1, keepdims=True)
        ms = jnp.where(lo, s_lo, s_hi) * (1.0 / HEAD_DIM)
        outs.append(blk * lax.rsqrt(ms + RMS_EPS))
    return jnp.concatenate(outs, axis=-1) * g


def _router(xn2, wr, br):
    hi = xn2.astype(BF16)
    both = _dot(hi, wr)
    logits = jnp.transpose(both[:, :ROUTER_LANES] + both[:, ROUTER_LANES:])[:ROUTER_ROWS] + br
    n = logits.shape[1]
    row = lax.broadcasted_iota(jnp.int32, (ROUTER_ROWS, n), 0)
    rowf = row.astype(F32)
    big = float(ROUTER_ROWS)
    coarse = (row >= COARSE_OFF) & (row < COARSE_OFF + N_GROUPS)
    cm = jnp.max(jnp.where(coarse, logits, -jnp.inf), axis=0, keepdims=True)
    g_sel = jnp.min(jnp.where(coarse & (logits == cm), rowf - COARSE_OFF, big), axis=0, keepdims=True)
    zc = jnp.sum(jnp.where(coarse, jnp.exp(logits - cm), 0.0), axis=0, keepdims=True)
    p_sel = 1.0 / zc
    per_group = N_EXPERTS // N_GROUPS
    fine = (row < N_EXPERTS) & ((row // per_group).astype(F32) == g_sel)
    f1 = jnp.max(jnp.where(fine, logits, -jnp.inf), axis=0, keepdims=True)
    i1 = jnp.min(jnp.where(fine & (logits == f1), rowf, big), axis=0, keepdims=True)
    rest = fine & (rowf != i1)
    f2 = jnp.max(jnp.where(rest, logits, -jnp.inf), axis=0, keepdims=True)
    i2 = jnp.min(jnp.where(rest & (logits == f2), rowf, big), axis=0, keepdims=True)
    e2 = jnp.exp(f2 - f1)
    t1 = 1.0 / (1.0 + e2)
    t2 = e2 * t1
    gates_t = jnp.where(rowf == i1, p_sel * t1, jnp.where(rowf == i2, p_sel * t2, 0.0))
    gates_t = jnp.where(row == GID_LANE, g_sel, gates_t)
    gates_t = jnp.concatenate([gates_t, jnp.zeros((ROUTER_LANES - ROUTER_ROWS, n), F32)], axis=0)
    return hi, jnp.transpose(gates_t)


def _mix_tail(x, attn, conv_out, gattn, gconv, wout, g2, wr, br):
    mixed = jnp.concatenate([_group_norm64(attn, gattn), _group_norm64(conv_out, gconv)],
                            axis=-1).astype(BF16)
    x1 = x + _dot(mixed, wout)
    xn2_bf, gates = _router(_rms(x1, g2), wr, br)
    return x1, xn2_bf, gates


def _layer_prompt_body(sinks_ref, x_ref, g1_ref, win_ref, convw_ref, gattn_ref, gconv_ref,
                       wout_ref, g2_ref, wr_ref, br_ref, wg32_ref, wu32_ref, wd32_ref,
                       x1_ref, xn2_ref, gates_ref, knew_ref, vnew_ref, unew_ref,
                       wg16_ref, wu16_ref, wd16_ref,
                       q_s, kcat_s, vcat_s, attn_s, ucarry_s):
    i = pl.program_id(1)
    wg16_ref[...] = wg32_ref[...].astype(BF16)
    wu16_ref[...] = wu32_ref[...].astype(BF16)
    wd16_ref[...] = wd32_ref[...].astype(BF16)
    t = T_LAYER // N_SUB
    last = N_SUB - 1

    @pl.when(i == 0)
    def _():
        kcat_s[last, t:t + WINDOW, :] = jnp.zeros((WINDOW, KV_DIM), BF16)
        vcat_s[last, t:t + WINDOW, :] = jnp.zeros((WINDOW, KV_DIM), BF16)
        ucarry_s[last] = jnp.zeros((8, CONV_DIM), F32)

    a_idx = lax.broadcasted_iota(jnp.int32, (WINDOW, 2 * WINDOW), 0)
    c_idx = lax.broadcasted_iota(jnp.int32, (WINDOW, 2 * WINDOW), 1)
    lane_lo = lax.broadcasted_iota(jnp.int32, (WINDOW, LANES), 1) < HEAD_DIM
    row = lax.broadcasted_iota(jnp.int32, (t, CONV_DIM), 0)
    n_pair = N_HEADS // 2
    cw = convw_ref[...]

    for h in range(N_SUB):
        src = (h - 1) % N_SUB
        rows = pl.ds(h * t, t)
        x = x_ref[0, rows, :]
        xn = _rms(x, g1_ref[...]).astype(BF16)

        kcat_s[h, 0:WINDOW, :] = kcat_s[src, t:t + WINDOW, :]
        vcat_s[h, 0:WINDOW, :] = vcat_s[src, t:t + WINDOW, :]
        q_s[h] = (_dot(xn, win_ref[:, 0:OFF_K]) * (HEAD_DIM ** -0.5)).astype(BF16)
        kv = _dot(xn, win_ref[:, OFF_K:OFF_B])
        k = kv[:, :KV_DIM]
        v = kv[:, KV_DIM:]
        kcat_s[h, WINDOW:, :] = k.astype(BF16)
        vcat_s[h, WINDOW:, :] = v.astype(BF16)
        if h == last:
            knew_ref[0] = jnp.transpose(k[t - WINDOW:, :])
            vnew_ref[0] = jnp.transpose(v[t - WINDOW:, :])

        for j in range(t // WINDOW):
            r0 = j * WINDOW
            kk = kcat_s[h, r0:r0 + 2 * WINDOW, :]
            vv = vcat_s[h, r0:r0 + 2 * WINDOW, :]
            c_min = jnp.where(i == 0, WINDOW, 0) if (h == 0 and j == 0) else 0
            valid = (c_idx >= jnp.maximum(a_idx, c_min)) & (c_idx <= a_idx + WINDOW)
            qcols = [q_s[h, r0:r0 + WINDOW, m * LANES:(m + 1) * LANES] for m in range(n_pair)]
            outs = []
            for half in range(2):
                keep = lane_lo if half == 0 else jnp.logical_not(lane_lo)
                qg = jnp.concatenate([jnp.where(keep, qc, jnp.zeros_like(qc)) for qc in qcols], axis=0)
                s = lax.dot_general(qg, kk, (((1,), (1,)), ((), ())), preferred_element_type=F32)
                es, dens = [], []
                for m in range(n_pair):
                    sm = jnp.where(valid, s[m * WINDOW:(m + 1) * WINDOW], NEG_INF)
                    sink = sinks_ref[m + half * n_pair]
                    mx = jnp.maximum(jnp.max(sm, axis=-1, keepdims=True), sink)
                    e = jnp.exp(sm - mx)
                    dens.append(jnp.sum(e, axis=-1, keepdims=True) + jnp.exp(sink - mx))
                    es.append(e.astype(BF16))
                o = _dot(jnp.concatenate(es, axis=0), vv)
                outs.append([o[m * WINDOW:(m + 1) * WINDOW] / dens[m] for m in range(n_pair)])
            for m in range(n_pair):
                attn_s[h, r0:r0 + WINDOW, m * LANES:(m + 1) * LANES] = jnp.where(lane_lo, outs[0][m], outs[1][m])

        gate_b = _dot(xn, win_ref[:, OFF_B:OFF_C])
        u = _dot(xn, win_ref[:, OFF_C:OFF_H]) * _dot(xn, win_ref[:, OFF_H:IN_PROJ_DIM])
        prev = ucarry_s[src]
        u1 = jnp.where(row == 0, prev[7:8, :], pltpu.roll(u, 1, axis=0))
        u2 = jnp.where(row == 0, prev[6:7, :], jnp.where(row == 1, prev[7:8, :], pltpu.roll(u, 2, axis=0)))
        z = cw[0:1, :] * u2 + cw[1:2, :] * u1 + cw[2:3, :] * u
        ucarry_s[h] = u[t - 8:, :]
        if h == last:
            unew_ref[0] = u[t - (CONV_WIDTH - 1):, :]

        x1, xn2_bf, gates = _mix_tail(x, attn_s[h], gate_b * z, gattn_ref[...], gconv_ref[...],
                                      wout_ref[...], g2_ref[...], wr_ref[...], br_ref[...])
        x1_ref[0, rows, :] = x1
        xn2_ref[0, rows, :] = xn2_bf
        gates_ref[0, rows, :] = gates


def _layer_prompt(x, sinks, g1, win, convw, gattn, gconv, wout, g2, wr, br, wg32, wu32, wd32):
    nb, seq, _ = x.shape
    t = T_LAYER
    ts = T_LAYER // N_SUB
    n_i = seq // t
    assert nb * n_i == N_EXPERTS, "one expert's weights are cast per grid step"
    const = lambda b, i, s: (0, 0)
    tile = lambda b, i, s: (b, i, 0)
    per_seq = lambda b, i, s: (b, 0, 0)
    per_step = lambda b, i, s: (b * n_i + i, 0, 0)
    resident = pl.Buffered(1)
    grid_spec = pltpu.PrefetchScalarGridSpec(
        num_scalar_prefetch=1,
        grid=(nb, seq // t),
        in_specs=[
            pl.BlockSpec((1, t, D_MODEL), tile),
            pl.BlockSpec((1, D_MODEL), const),
            pl.BlockSpec((D_MODEL, IN_PROJ_DIM), const, pipeline_mode=resident),
            pl.BlockSpec((CONV_WIDTH, CONV_DIM), const),
            pl.BlockSpec((1, ATTN_DIM), const),
            pl.BlockSpec((1, CONV_DIM), const),
            pl.BlockSpec((D_MODEL, D_MODEL), const, pipeline_mode=resident),
            pl.BlockSpec((1, D_MODEL), const),
            pl.BlockSpec((D_MODEL, 2 * ROUTER_LANES), const, pipeline_mode=resident),
            pl.BlockSpec((ROUTER_ROWS, 1), const),
            pl.BlockSpec((1, D_MODEL, D_EXPERT), per_step),
            pl.BlockSpec((1, D_MODEL, D_EXPERT), per_step),
            pl.BlockSpec((1, D_EXPERT, D_MODEL), per_step),
        ],
        out_specs=[
            pl.BlockSpec((1, t, D_MODEL), tile),
            pl.BlockSpec((1, t, D_MODEL), tile),
            pl.BlockSpec((1, t, ROUTER_LANES), tile),
            pl.BlockSpec((1, KV_DIM, WINDOW), per_seq),
            pl.BlockSpec((1, KV_DIM, WINDOW), per_seq),
            pl.BlockSpec((1, CONV_WIDTH - 1, CONV_DIM), per_seq),
            pl.BlockSpec((1, D_MODEL, D_EXPERT), per_step),
            pl.BlockSpec((1, D_MODEL, D_EXPERT), per_step),
            pl.BlockSpec((1, D_EXPERT, D_MODEL), per_step),
        ],
        scratch_shapes=[
            pltpu.VMEM((N_SUB, ts, ATTN_DIM), BF16),
            pltpu.VMEM((N_SUB, ts + WINDOW, KV_DIM), BF16),
            pltpu.VMEM((N_SUB, ts + WINDOW, KV_DIM), BF16),
            pltpu.VMEM((N_SUB, ts, ATTN_DIM), F32),
            pltpu.VMEM((N_SUB, 8, CONV_DIM), F32),
        ],
    )
    return pl.pallas_call(
        _layer_prompt_body,
        grid_spec=grid_spec,
        out_shape=[
            jax.ShapeDtypeStruct((nb, seq, D_MODEL), F32),
            jax.ShapeDtypeStruct((nb, seq, D_MODEL), BF16),
            jax.ShapeDtypeStruct((nb, seq, ROUTER_LANES), F32),
            jax.ShapeDtypeStruct((nb, KV_DIM, WINDOW), F32),
            jax.ShapeDtypeStruct((nb, KV_DIM, WINDOW), F32),
            jax.ShapeDtypeStruct((nb, CONV_WIDTH - 1, CONV_DIM), F32),
            jax.ShapeDtypeStruct((N_EXPERTS, D_MODEL, D_EXPERT), BF16),
            jax.ShapeDtypeStruct((N_EXPERTS, D_MODEL, D_EXPERT), BF16),
            jax.ShapeDtypeStruct((N_EXPERTS, D_EXPERT, D_MODEL), BF16),
        ],
        compiler_params=pltpu.CompilerParams(
            dimension_semantics=("arbitrary", "arbitrary"), vmem_limit_bytes=MOE_VMEM_LIMIT),
        name="layer_prompt",
    )(sinks, x, g1, win, convw, gattn, gconv, wout, g2, wr, br, wg32, wu32, wd32)


def _moe_body(cap, xc, x1_ref, xn2_ref, gates_ref, wg_ref, wu_ref, wd_ref, gf_ref, y_ref,
              xs_s, gs_s, ys_s):
    t = x1_ref.shape[0]
    per_group = N_EXPERTS // N_GROUPS
    gates = gates_ref[...]
    gid_row = jnp.transpose(gates)[GID_LANE:GID_LANE + 1, :]
    grp = lax.broadcasted_iota(jnp.int32, (8, t), 0).astype(F32)
    onehot = jnp.where(grp == gid_row, 1.0, 0.0)

    r_idx = lax.broadcasted_iota(jnp.int32, (t, t), 0)
    c_idx = lax.broadcasted_iota(jnp.int32, (t, t), 1)
    upper = jnp.where(r_idx <= c_idx, 1.0, 0.0).astype(BF16)
    csum = _dot(onehot.astype(BF16), upper)
    rank_row = jnp.sum(onehot * (csum - 1.0), axis=0, keepdims=True)
    cnt = [csum[g, t - 1].astype(jnp.int32) for g in range(N_GROUPS)]
    off = [jnp.int32(0)]
    for g in range(1, N_GROUPS):
        off.append(off[-1] + cnt[g - 1])

    pos_row = rank_row
    for g in range(1, N_GROUPS):
        pos_row = pos_row + jnp.where(gid_row == float(g), off[g].astype(F32), 0.0)
    pos_col = jnp.transpose(jnp.broadcast_to(pos_row, (LANES, t)))
    pos_col = jnp.concatenate([pos_col] * (t // LANES), axis=1)
    perm = jnp.where(r_idx.astype(F32) == pos_row, 1.0, 0.0).astype(BF16)
    perm_t = jnp.where(c_idx.astype(F32) == pos_col, 1.0, 0.0).astype(BF16)

    p1 = gates.astype(BF16)
    p2 = (gates - p1.astype(F32)).astype(BF16)
    moved = _dot(perm, jnp.concatenate([xn2_ref[...], p1, p2], axis=1))
    xs_s[0:t, :] = moved[:, :D_MODEL].astype(BF16)
    gs_s[0:t, :] = moved[:, D_MODEL:D_MODEL + ROUTER_LANES] + moved[:, D_MODEL + ROUTER_LANES:]
    pad = xs_s.shape[0] - t
    xs_s[t:, :] = jnp.zeros((pad, D_MODEL), BF16)
    gs_s[t:, :] = jnp.zeros((pad, ROUTER_LANES), F32)
    ys_s[...] = jnp.zeros(ys_s.shape, F32)

    def group_rows(g, r0, rows):
        xs = xs_s[pl.ds(r0, rows), :]
        gsc = gs_s[pl.ds(r0, rows), :]
        acts = []
        for k in range(per_group):
            e = g * per_group + k
            hg = _dot(xs, wg_ref[e])
            hu = _dot(xs, wu_ref[e])
            acts.append((hg / (1.0 + jnp.exp(-hg)) * hu * gsc[:, e:e + 1]).astype(BF16))
        ys_s[pl.ds(r0, rows), :] += _dot(jnp.concatenate(acts, axis=1), wd_ref[g])

    base = [(off[g] // ROW_ALIGN) * ROW_ALIGN for g in range(N_GROUPS)]
    for g in range(N_GROUPS):
        group_rows(g, pl.multiple_of(base[g], ROW_ALIGN), cap)
    for g in range(N_GROUPS):
        n_extra = jnp.maximum(0, (off[g] + cnt[g] - (base[g] + cap) + xc - 1) // xc)

        def extra(k, carry, g=g):
            group_rows(g, pl.multiple_of(base[g] + cap + k * xc, ROW_ALIGN), xc)
            return carry

        lax.fori_loop(0, n_extra, extra, 0)

    moe = _dot(perm_t, ys_s[0:t, :].astype(BF16))
    y_ref[...] = _rms(x1_ref[...] + moe, gf_ref[...])


def _moe(x1, xn2, gates, wg, wu, wd, gf, t, cap, xc):
    n = x1.shape[0]
    tile = lambda i: (i, 0)
    whole = lambda i: (0, 0, 0)
    resident = pl.Buffered(1)
    rows = t + cap + xc
    hidden = (N_EXPERTS // N_GROUPS) * D_EXPERT
    return pl.pallas_call(
        functools.partial(_moe_body, cap, xc),
        grid=(n // t,),
        in_specs=[
            pl.BlockSpec((t, D_MODEL), tile),
            pl.BlockSpec((t, D_MODEL), tile),
            pl.BlockSpec((t, ROUTER_LANES), tile),
            pl.BlockSpec((N_EXPERTS, D_MODEL, D_EXPERT), whole, pipeline_mode=resident),
            pl.BlockSpec((N_EXPERTS, D_MODEL, D_EXPERT), whole, pipeline_mode=resident),
            pl.BlockSpec((N_GROUPS, hidden, D_MODEL), whole, pipeline_mode=resident),
            pl.BlockSpec((1, D_MODEL), lambda i: (0, 0)),
        ],
        out_specs=pl.BlockSpec((t, D_MODEL), tile),
        out_shape=jax.ShapeDtypeStruct((n, D_MODEL), F32),
        scratch_shapes=[
            pltpu.VMEM((rows, D_MODEL), BF16),
            pltpu.VMEM((rows, ROUTER_LANES), F32),
            pltpu.VMEM((rows, D_MODEL), F32),
        ],
        compiler_params=pltpu.CompilerParams(
            dimension_semantics=("arbitrary",), vmem_limit_bytes=MOE_VMEM_LIMIT),
        name="moe",
    )(x1, xn2, gates, wg, wu, wd, gf)


def _proj_sample_body(x_ref, g1_ref, win_ref, proj_ref, kvt_ref):
    xn = _rms(x_ref[...], g1_ref[...]).astype(BF16)
    proj_ref[...] = _dot(xn, win_ref[...])
    kvt_ref[...] = lax.dot_general(win_ref[:, OFF_K:OFF_B], xn, (((0,), (1,)), ((), ())),
                                   preferred_element_type=F32)


def _proj_sample(x, g1, win):
    n = x.shape[0]
    return pl.pallas_call(
        _proj_sample_body,
        out_shape=[jax.ShapeDtypeStruct((n, IN_PROJ_DIM), F32),
                   jax.ShapeDtypeStruct((2 * KV_DIM, n), F32)],
        compiler_params=pltpu.CompilerParams(vmem_limit_bytes=VMEM_LIMIT),
        name="proj_sample",
    )(x, g1, win)


def _attn_sample_body(q8_ref, knew_ref, vnew_ref, kvt_ref, khist_ref, vhist_ref, sinks_ref,
                      out8_ref, kout_ref, vout_ref):
    chunk = q8_ref.shape[0]
    first = pl.program_id(0) * chunk
    q8 = q8_ref[...]
    kh = khist_ref[...]
    vh = vhist_ref[...]
    knew = knew_ref[...]
    vnew = vnew_ref[...]
    s = jnp.einsum('bhj,bjc->bhc', q8, kh.astype(BF16), preferred_element_type=F32)
    s_new = jnp.sum(q8.astype(F32) * knew, axis=-1, keepdims=True)
    sink = sinks_ref[...][None]
    mx = jnp.maximum(jnp.maximum(jnp.max(s, axis=-1, keepdims=True), s_new), sink)
    e = jnp.exp(s - mx)
    e_new = jnp.exp(s_new - mx)
    den = jnp.sum(e, axis=-1, keepdims=True) + e_new + jnp.exp(sink - mx)
    o = jnp.einsum('bhc,bjc->bhj', e.astype(BF16), vh.astype(BF16), preferred_element_type=F32)
    out8_ref[...] = (o + e_new * vnew) / den
    last = lax.broadcasted_iota(jnp.int32, (KV_DIM, WINDOW), 1) == WINDOW - 1
    knew_t = kvt_ref[0:KV_DIM, :]
    vnew_t = kvt_ref[KV_DIM:, :]
    for bb in range(chunk):
        shift = lax.rem(2 * WINDOW - 1 - (first + bb), WINDOW)
        kout_ref[bb] = jnp.where(last, pltpu.roll(knew_t, shift, axis=1), pltpu.roll(kh[bb], WINDOW - 1, axis=1))
        vout_ref[bb] = jnp.where(last, pltpu.roll(vnew_t, shift, axis=1), pltpu.roll(vh[bb], WINDOW - 1, axis=1))


def _attn_sample(q8, knew, vnew, kvt, khist, vhist, sinks_col):
    assert q8.shape[0] == WINDOW == LANES, "one lane per sequence in the channel-major projection"
    n = q8.shape[0]
    c = DEC_CHUNK
    blk3 = lambda i: (i, 0, 0)
    return pl.pallas_call(
        _attn_sample_body,
        grid=(n // c,),
        in_specs=[
            pl.BlockSpec((c, N_HEADS, LANES), blk3),
            pl.BlockSpec((c, 1, KV_DIM), blk3),
            pl.BlockSpec((c, 1, KV_DIM), blk3),
            pl.BlockSpec((2 * KV_DIM, n), lambda i: (0, 0)),
            pl.BlockSpec((c, KV_DIM, WINDOW), blk3),
            pl.BlockSpec((c, KV_DIM, WINDOW), blk3),
            pl.BlockSpec((N_HEADS, 1), lambda i: (0, 0)),
        ],
        out_specs=[
            pl.BlockSpec((c, N_HEADS, LANES), blk3),
            pl.BlockSpec((c, KV_DIM, WINDOW), blk3),
            pl.BlockSpec((c, KV_DIM, WINDOW), blk3),
        ],
        out_shape=[
            jax.ShapeDtypeStruct((n, N_HEADS, LANES), F32),
            jax.ShapeDtypeStruct((n, KV_DIM, WINDOW), F32),
            jax.ShapeDtypeStruct((n, KV_DIM, WINDOW), F32),
        ],
        compiler_params=pltpu.CompilerParams(
            dimension_semantics=("arbitrary",), vmem_limit_bytes=VMEM_LIMIT),
        name="attn_sample",
    )(q8, knew, vnew, kvt, khist, vhist, sinks_col)


def _tail_sample_body(x_ref, attn_ref, proj_ref, uprev2_ref, uprev1_ref, convw_ref, gattn_ref,
                      gconv_ref, wout_ref, g2_ref, wr_ref, br_ref,
                      x1_ref, xn2_ref, gates_ref, u_ref):
    gate_b = proj_ref[:, OFF_B:OFF_C]
    u = proj_ref[:, OFF_C:OFF_H] * proj_ref[:, OFF_H:IN_PROJ_DIM]
    cw = convw_ref[...]
    z = cw[0:1, :] * uprev2_ref[...] + cw[1:2, :] * uprev1_ref[...] + cw[2:3, :] * u
    u_ref[...] = u
    x1, xn2_bf, gates = _mix_tail(x_ref[...], attn_ref[...], gate_b * z, gattn_ref[...], gconv_ref[...],
                                  wout_ref[...], g2_ref[...], wr_ref[...], br_ref[...])
    x1_ref[...] = x1
    xn2_ref[...] = xn2_bf
    gates_ref[...] = gates


def _tail_sample(x, attn, proj, uprev2, uprev1, convw, gattn, gconv, wout, g2, wr, br):
    n = x.shape[0]
    return pl.pallas_call(
        _tail_sample_body,
        out_shape=[
            jax.ShapeDtypeStruct((n, D_MODEL), F32),
            jax.ShapeDtypeStruct((n, D_MODEL), BF16),
            jax.ShapeDtypeStruct((n, ROUTER_LANES), F32),
            jax.ShapeDtypeStruct((n, CONV_DIM), F32),
        ],
        compiler_params=pltpu.CompilerParams(vmem_limit_bytes=VMEM_LIMIT),
        name="tail_sample",
    )(x, attn, proj, uprev2, uprev1, convw, gattn, gconv, wout, g2, wr, br)


def _pair_heads(w, axis):
    shape = w.shape
    n_pair = N_HEADS // 2
    split = shape[:axis] + (2, n_pair, HEAD_DIM) + shape[axis + 1:]
    return jnp.swapaxes(w.reshape(split), axis, axis + 1).reshape(shape)


def kernel(x_prompt, x_sample, cache_k_win, cache_v_win, state_conv, norm1_g, w_in, conv_w, sinks,
           attn_out_g, conv_out_g, w_out, norm2_g, w_group, b_group, w_fine, b_fine, w_gate, w_up,
           w_down, final_g):
    nb, seq, _ = x_prompt.shape
    nd = x_sample.shape[0]

    w_in0 = w_in[0].astype(BF16)
    win = jnp.concatenate([_pair_heads(w_in0[:, :ATTN_DIM], 1), w_in0[:, ATTN_DIM:]], axis=1)
    w_out0 = w_out[0].astype(BF16)
    wout = jnp.concatenate([_pair_heads(w_out0[:ATTN_DIM], 0), w_out0[ATTN_DIM:]], axis=0)
    gattn = _pair_heads(attn_out_g[0], 0)[None]
    gconv = conv_out_g[0][None]
    g1 = norm1_g[0][None]
    g2 = norm2_g[0][None]
    gf = final_g[None]
    convw = conv_w[0]
    sinks0 = sinks[0]
    pad = ROUTER_LANES - N_EXPERTS - N_GROUPS
    wr32 = jnp.concatenate([w_fine[0], w_group[0], jnp.zeros((D_MODEL, pad), F32)], axis=1)
    wr_hi = wr32.astype(BF16)
    wr = jnp.concatenate([wr_hi, (wr32 - wr_hi.astype(F32)).astype(BF16)], axis=1)
    br = jnp.concatenate([b_fine[0], b_group[0], jnp.zeros((pad,), F32)])[:ROUTER_ROWS, None]

    x1p, xn2p, gatesp, kp, vp, up, wg, wu, wd = _layer_prompt(
        x_prompt, sinks0, g1, win, convw, gattn, gconv, wout, g2, wr, br, w_gate[0], w_up[0], w_down[0])
    wd = wd.reshape(N_GROUPS, (N_EXPERTS // N_GROUPS) * D_EXPERT, D_MODEL)
    n_p = nb * seq
    y_prompt = _moe(x1p.reshape(n_p, D_MODEL), xn2p.reshape(n_p, D_MODEL),
                    gatesp.reshape(n_p, ROUTER_LANES), wg, wu, wd, gf, T_MOE, MOE_CAP, MOE_XC).reshape(nb, seq, D_MODEL)

    xs = x_sample.reshape(nd, D_MODEL)
    proj, kvt = _proj_sample(xs, g1, win)
    lane = jnp.arange(LANES)
    qp = (proj[:, :ATTN_DIM] * (HEAD_DIM ** -0.5)).reshape(nd, N_HEADS // 2, LANES)
    q8 = jnp.concatenate([jnp.where(lane < HEAD_DIM, qp, 0.0), jnp.where(lane >= HEAD_DIM, qp, 0.0)],
                         axis=1).astype(BF16)
    knew = proj[:, OFF_K:OFF_V].reshape(nd, 1, KV_DIM)
    vnew = proj[:, OFF_V:OFF_B].reshape(nd, 1, KV_DIM)
    to_cm = lambda c: jnp.transpose(c[0], (0, 2, 3, 1)).reshape(nd, KV_DIM, WINDOW)
    from_cm = lambda c, n: jnp.transpose(c.reshape(n, KV_DIM // HEAD_DIM, HEAD_DIM, WINDOW), (0, 3, 1, 2))[None]
    out8, ks, vs = _attn_sample(q8, knew, vnew, kvt, to_cm(cache_k_win), to_cm(cache_v_win), sinks0[:, None])
    attn_s = jnp.where(lane < HEAD_DIM, out8[:, :N_HEADS // 2], out8[:, N_HEADS // 2:]).reshape(nd, ATTN_DIM)
    st = state_conv[0]
    x1s, xn2s, gatess, us = _tail_sample(xs, attn_s, proj, st[:, 0], st[:, 1], convw, gattn, gconv,
                                         wout, g2, wr, br)
    y_sample = _moe(x1s, xn2s, gatess, wg, wu, wd, gf, nd, MOE_CAP_DEC, MOE_XC_DEC).reshape(nd, 1, D_MODEL)

    return (y_prompt, y_sample,
            from_cm(kp, nb), from_cm(vp, nb),
            up[None],
            from_cm(ks, nd), from_cm(vs, nd),
            jnp.stack([st[:, 1], us], axis=1)[None])
```

```python
import functools

import jax
import jax.numpy as jnp
from jax import lax
from jax.experimental import pallas as pl
from jax.experimental.pallas import tpu as pltpu

D_MODEL = 1024
HEAD_DIM = 64
N_HEADS = 8
ATTN_DIM = 512
KV_DIM = 128
WINDOW = 128
CONV_DIM = 512
CONV_WIDTH = 3
OFF_K = 512
OFF_V = 640
OFF_B = 768
OFF_C = 1280
OFF_H = 1792
IN_PROJ_DIM = 2304
N_GROUPS = 4
N_EXPERTS = 16
D_EXPERT = 256
RMS_EPS = 1e-5
NEG_INF = -1e30

LANES = 128
ROW_ALIGN = 16
ROUTER_LANES = 128
ROUTER_ROWS = 32
COARSE_OFF = N_EXPERTS
T_LAYER = 1024
N_SUB = 2
GID_LANE = N_EXPERTS
T_MOE = 512
MOE_CAP, MOE_XC = 160, 32
MOE_CAP_DEC, MOE_XC_DEC = 64, 32
DEC_CHUNK = 32
VMEM_LIMIT = 48 * 1024 * 1024
MOE_VMEM_LIMIT = 56 * 1024 * 1024

BF16 = jnp.bfloat16
F32 = jnp.float32


def _dot(a, b):
    return jnp.dot(a, b, preferred_element_type=F32)


def _rms(x, g):
    ms = jnp.mean(x * x, axis=-1, keepdims=True)
    return x * lax.rsqrt(ms + RMS_EPS) * g


def _group_norm64(y, g):
    rows, cols = y.shape
    lo = lax.broadcasted_iota(jnp.int32, (rows, LANES), 1) < HEAD_DIM
    outs = []
    for c in range(cols // LANES):
        blk = y[:, c * LANES:(c + 1) * LANES]
        sq = blk * blk
        s_lo = jnp.sum(jnp.where(lo, sq, 0.0), axis=-1, keepdims=True)
        s_hi = jnp.sum(jnp.where(lo, 0.0, sq), axis=-1, keepdims=True)
        ms = jnp.where(lo, s_lo, s_hi) * (1.0 / HEAD_DIM)
        outs.append(blk * lax.rsqrt(ms + RMS_EPS))
    return jnp.concatenate(outs, axis=-1) * g


def _router(xn2, wr, br):
    hi = xn2.astype(BF16)
    both = _dot(hi, wr)
    logits = jnp.transpose(both[:, :ROUTER_LANES] + both[:, ROUTER_LANES:])[:ROUTER_ROWS] + br
    n = logits.shape[1]
    row = lax.broadcasted_iota(jnp.int32, (ROUTER_ROWS, n), 0)
    rowf = row.astype(F32)
    big = float(ROUTER_ROWS)
    coarse = (row >= COARSE_OFF) & (row < COARSE_OFF + N_GROUPS)
    cm = jnp.max(jnp.where(coarse, logits, -jnp.inf), axis=0, keepdims=True)
    g_sel = jnp.min(jnp.where(coarse & (logits == cm), rowf - COARSE_OFF, big), axis=0, keepdims=True)
    zc = jnp.sum(jnp.where(coarse, jnp.exp(logits - cm), 0.0), axis=0, keepdims=True)
    p_sel = 1.0 / zc
    per_group = N_EXPERTS // N_GROUPS
    fine = (row < N_EXPERTS) & ((row // per_group).astype(F32) == g_sel)
    f1 = jnp.max(jnp.where(fine, logits, -jnp.inf), axis=0, keepdims=True)
    i1 = jnp.min(jnp.where(fine & (logits == f1), rowf, big), axis=0, keepdims=True)
    rest = fine & (rowf != i1)
    f2 = jnp.max(jnp.where(rest, logits, -jnp.inf), axis=0, keepdims=True)
    i2 = jnp.min(jnp.where(rest & (logits == f2), rowf, big), axis=0, keepdims=True)
    e2 = jnp.exp(f2 - f1)
    t1 = 1.0 / (1.0 + e2)
    t2 = e2 * t1
    gates_t = jnp.where(rowf == i1, p_sel * t1, jnp.where(rowf == i2, p_sel * t2, 0.0))
    gates_t = jnp.where(row == GID_LANE, g_sel, gates_t)
    gates_t = jnp.concatenate([gates_t, jnp.zeros((ROUTER_LANES - ROUTER_ROWS, n), F32)], axis=0)
    return hi, jnp.transpose(gates_t)


def _mix_tail(x, attn, conv_out, gattn, gconv, wout, g2, wr, br):
    mixed = jnp.concatenate([_group_norm64(attn, gattn), _group_norm64(conv_out, gconv)],
                            axis=-1).astype(BF16)
    x1 = x + _dot(mixed, wout)
    xn2_bf, gates = _router(_rms(x1, g2), wr, br)
    return x1, xn2_bf, gates


def _layer_prompt_body(sinks_ref, x_ref, g1_ref, win_ref, convw_ref, gattn_ref, gconv_ref,
                       wout_ref, g2_ref, wr_ref, br_ref, wg32_ref, wu32_ref, wd32_ref,
                       x1_ref, xn2_ref, gates_ref, knew_ref, vnew_ref, unew_ref,
                       wg16_ref, wu16_ref, wd16_ref,
                       q_s, kcat_s, vcat_s, attn_s, ucarry_s):
    i = pl.program_id(1)
    wg16_ref[...] = wg32_ref[...].astype(BF16)
    wu16_ref[...] = wu32_ref[...].astype(BF16)
    wd16_ref[...] = wd32_ref[...].astype(BF16)
    t = T_LAYER // N_SUB
    last = N_SUB - 1

    @pl.when(i == 0)
    def _():
        kcat_s[last, t:t + WINDOW, :] = jnp.zeros((WINDOW, KV_DIM), BF16)
        vcat_s[last, t:t + WINDOW, :] = jnp.zeros((WINDOW, KV_DIM), BF16)
        ucarry_s[last] = jnp.zeros((8, CONV_DIM), F32)

    a_idx = lax.broadcasted_iota(jnp.int32, (WINDOW, 2 * WINDOW), 0)
    c_idx = lax.broadcasted_iota(jnp.int32, (WINDOW, 2 * WINDOW), 1)
    lane_lo = lax.broadcasted_iota(jnp.int32, (WINDOW, LANES), 1) < HEAD_DIM
    row = lax.broadcasted_iota(jnp.int32, (t, CONV_DIM), 0)
    n_pair = N_HEADS // 2
    cw = convw_ref[...]

    for h in range(N_SUB):
        src = (h - 1) % N_SUB
        rows = pl.ds(h * t, t)
        x = x_ref[0, rows, :]
        xn = _rms(x, g1_ref[...]).astype(BF16)

        kcat_s[h, 0:WINDOW, :] = kcat_s[src, t:t + WINDOW, :]
        vcat_s[h, 0:WINDOW, :] = vcat_s[src, t:t + WINDOW, :]
        q_s[h] = (_dot(xn, win_ref[:, 0:OFF_K]) * (HEAD_DIM ** -0.5)).astype(BF16)
        kv = _dot(xn, win_ref[:, OFF_K:OFF_B])
        k = kv[:, :KV_DIM]
        v = kv[:, KV_DIM:]
        kcat_s[h, WINDOW:, :] = k.astype(BF16)
        vcat_s[h, WINDOW:, :] = v.astype(BF16)
        if h == last:
            knew_ref[0] = jnp.transpose(k[t - WINDOW:, :])
            vnew_ref[0] = jnp.transpose(v[t - WINDOW:, :])

        for j in range(t // WINDOW):
            r0 = j * WINDOW
            kk = kcat_s[h, r0:r0 + 2 * WINDOW, :]
            vv = vcat_s[h, r0:r0 + 2 * WINDOW, :]
            c_min = jnp.where(i == 0, WINDOW, 0) if (h == 0 and j == 0) else 0
            valid = (c_idx >= jnp.maximum(a_idx, c_min)) & (c_idx <= a_idx + WINDOW)
            qcols = [q_s[h, r0:r0 + WINDOW, m * LANES:(m + 1) * LANES] for m in range(n_pair)]
            outs = []
            for half in range(2):
                keep = lane_lo if half == 0 else jnp.logical_not(lane_lo)
                qg = jnp.concatenate([jnp.where(keep, qc, jnp.zeros_like(qc)) for qc in qcols], axis=0)
                s = lax.dot_general(qg, kk, (((1,), (1,)), ((), ())), preferred_element_type=F32)
                es, dens = [], []
                for m in range(n_pair):
                    sm = jnp.where(valid, s[m * WINDOW:(m + 1) * WINDOW], NEG_INF)
                    sink = sinks_ref[m + half * n_pair]
                    mx = jnp.maximum(jnp.max(sm, axis=-1, keepdims=True), sink)
                    e = jnp.exp(sm - mx)
                    dens.append(jnp.sum(e, axis=-1, keepdims=True) + jnp.exp(sink - mx))
                    es.append(e.astype(BF16))
                o = _dot(jnp.concatenate(es, axis=0), vv)
                outs.append([o[m * WINDOW:(m + 1) * WINDOW] / dens[m] for m in range(n_pair)])
            for m in range(n_pair):
                attn_s[h, r0:r0 + WINDOW, m * LANES:(m + 1) * LANES] = jnp.where(lane_lo, outs[0][m], outs[1][m])

        gate_b = _dot(xn, win_ref[:, OFF_B:OFF_C])
        u = _dot(xn, win_ref[:, OFF_C:OFF_H]) * _dot(xn, win_ref[:, OFF_H:IN_PROJ_DIM])
        prev = ucarry_s[src]
        u1 = jnp.where(row == 0, prev[7:8, :], pltpu.roll(u, 1, axis=0))
        u2 = jnp.where(row == 0, prev[6:7, :], jnp.where(row == 1, prev[7:8, :], pltpu.roll(u, 2, axis=0)))
        z = cw[0:1, :] * u2 + cw[1:2, :] * u1 + cw[2:3, :] * u
        ucarry_s[h] = u[t - 8:, :]
        if h == last:
            unew_ref[0] = u[t - (CONV_WIDTH - 1):, :]

        x1, xn2_bf, gates = _mix_tail(x, attn_s[h], gate_b * z, gattn_ref[...], gconv_ref[...],
                                      wout_ref[...], g2_ref[...], wr_ref[...], br_ref[...])
        x1_ref[0, rows, :] = x1
        xn2_ref[0, rows, :] = xn2_bf
        gates_ref[0, rows, :] = gates


def _layer_prompt(x, sinks, g1, win, convw, gattn, gconv, wout, g2, wr, br, wg32, wu32, wd32):
    nb, seq, _ = x.shape
    t = T_LAYER
    ts = T_LAYER // N_SUB
    n_i = seq // t
    assert nb * n_i == N_EXPERTS, "one expert's weights are cast per grid step"
    const = lambda b, i, s: (0, 0)
    tile = lambda b, i, s: (b, i, 0)
    per_seq = lambda b, i, s: (b, 0, 0)
    per_step = lambda b, i, s: (b * n_i + i, 0, 0)
    resident = pl.Buffered(1)
    grid_spec = pltpu.PrefetchScalarGridSpec(
        num_scalar_prefetch=1,
        grid=(nb, seq // t),
        in_specs=[
            pl.BlockSpec((1, t, D_MODEL), tile),
            pl.BlockSpec((1, D_MODEL), const),
            pl.BlockSpec((D_MODEL, IN_PROJ_DIM), const, pipeline_mode=resident),
            pl.BlockSpec((CONV_WIDTH, CONV_DIM), const),
            pl.BlockSpec((1, ATTN_DIM), const),
            pl.BlockSpec((1, CONV_DIM), const),
            pl.BlockSpec((D_MODEL, D_MODEL), const, pipeline_mode=resident),
            pl.BlockSpec((1, D_MODEL), const),
            pl.BlockSpec((D_MODEL, 2 * ROUTER_LANES), const, pipeline_mode=resident),
            pl.BlockSpec((ROUTER_ROWS, 1), const),
            pl.BlockSpec((1, D_MODEL, D_EXPERT), per_step),
            pl.BlockSpec((1, D_MODEL, D_EXPERT), per_step),
            pl.BlockSpec((1, D_EXPERT, D_MODEL), per_step),
        ],
        out_specs=[
            pl.BlockSpec((1, t, D_MODEL), tile),
            pl.BlockSpec((1, t, D_MODEL), tile),
            pl.BlockSpec((1, t, ROUTER_LANES), tile),
            pl.BlockSpec((1, KV_DIM, WINDOW), per_seq),
            pl.BlockSpec((1, KV_DIM, WINDOW), per_seq),
            pl.BlockSpec((1, CONV_WIDTH - 1, CONV_DIM), per_seq),
            pl.BlockSpec((1, D_MODEL, D_EXPERT), per_step),
            pl.BlockSpec((1, D_MODEL, D_EXPERT), per_step),
            pl.BlockSpec((1, D_EXPERT, D_MODEL), per_step),
        ],
        scratch_shapes=[
            pltpu.VMEM((N_SUB, ts, ATTN_DIM), BF16),
            pltpu.VMEM((N_SUB, ts + WINDOW, KV_DIM), BF16),
            pltpu.VMEM((N_SUB, ts + WINDOW, KV_DIM), BF16),
            pltpu.VMEM((N_SUB, ts, ATTN_DIM), F32),
            pltpu.VMEM((N_SUB, 8, CONV_DIM), F32),
        ],
    )
    return pl.pallas_call(
        _layer_prompt_body,
        grid_spec=grid_spec,
        out_shape=[
            jax.ShapeDtypeStruct((nb, seq, D_MODEL), F32),
            jax.ShapeDtypeStruct((nb, seq, D_MODEL), BF16),
            jax.ShapeDtypeStruct((nb, seq, ROUTER_LANES), F32),
            jax.ShapeDtypeStruct((nb, KV_DIM, WINDOW), F32),
            jax.ShapeDtypeStruct((nb, KV_DIM, WINDOW), F32),
            jax.ShapeDtypeStruct((nb, CONV_WIDTH - 1, CONV_DIM), F32),
            jax.ShapeDtypeStruct((N_EXPERTS, D_MODEL, D_EXPERT), BF16),
            jax.ShapeDtypeStruct((N_EXPERTS, D_MODEL, D_EXPERT), BF16),
            jax.ShapeDtypeStruct((N_EXPERTS, D_EXPERT, D_MODEL), BF16),
        ],
        compiler_params=pltpu.CompilerParams(
            dimension_semantics=("arbitrary", "arbitrary"), vmem_limit_bytes=MOE_VMEM_LIMIT),
        name="layer_prompt",
    )(sinks, x, g1, win, convw, gattn, gconv, wout, g2, wr, br, wg32, wu32, wd32)


def _moe_body(cap, xc, x1_ref, xn2_ref, gates_ref, wg_ref, wu_ref, wd_ref, gf_ref, y_ref,
              xs_s, gs_s, ys_s):
    t = x1_ref.shape[0]
    per_group = N_EXPERTS // N_GROUPS
    gates = gates_ref[...]
    gid_row = jnp.transpose(gates)[GID_LANE:GID_LANE + 1, :]
    grp = lax.broadcasted_iota(jnp.int32, (8, t), 0).astype(F32)
    onehot = jnp.where(grp == gid_row, 1.0, 0.0)

    r_idx = lax.broadcasted_iota(jnp.int32, (t, t), 0)
    c_idx = lax.broadcasted_iota(jnp.int32, (t, t), 1)
    upper = jnp.where(r_idx <= c_idx, 1.0, 0.0).astype(BF16)
    csum = _dot(onehot.astype(BF16), upper)
    rank_row = jnp.sum(onehot * (csum - 1.0), axis=0, keepdims=True)
    cnt = [csum[g, t - 1].astype(jnp.int32) for g in range(N_GROUPS)]
    off = [jnp.int32(0)]
    for g in range(1, N_GROUPS):
        off.append(off[-1] + cnt[g - 1])

    pos_row = rank_row
    for g in range(1, N_GROUPS):
        pos_row = pos_row + jnp.where(gid_row == float(g), off[g].astype(F32), 0.0)
    pos_col = jnp.transpose(jnp.broadcast_to(pos_row, (LANES, t)))
    pos_col = jnp.concatenate([pos_col] * (t // LANES), axis=1)
    perm = jnp.where(r_idx.astype(F32) == pos_row, 1.0, 0.0).astype(BF16)
    perm_t = jnp.where(c_idx.astype(F32) == pos_col, 1.0, 0.0).astype(BF16)

    p1 = gates.astype(BF16)
    p2 = (gates - p1.astype(F32)).astype(BF16)
    moved = _dot(perm, jnp.concatenate([xn2_ref[...], p1, p2], axis=1))
    xs_s[0:t, :] = moved[:, :D_MODEL].astype(BF16)
    gs_s[0:t, :] = moved[:, D_MODEL:D_MODEL + ROUTER_LANES] + moved[:, D_MODEL + ROUTER_LANES:]
    pad = xs_s.shape[0] - t
    xs_s[t:, :] = jnp.zeros((pad, D_MODEL), BF16)
    gs_s[t:, :] = jnp.zeros((pad, ROUTER_LANES), F32)
    ys_s[...] = jnp.zeros(ys_s.shape, F32)

    def group_rows(g, r0, rows):
        xs = xs_s[pl.ds(r0, rows), :]
        gsc = gs_s[pl.ds(r0, rows), :]
        acts = []
        for k in range(per_group):
            e = g * per_group + k
            hg = _dot(xs, wg_ref[e])
            hu = _dot(xs, wu_ref[e])
            acts.append((hg / (1.0 + jnp.exp(-hg)) * hu * gsc[:, e:e + 1]).astype(BF16))
        ys_s[pl.ds(r0, rows), :] += _dot(jnp.concatenate(acts, axis=1), wd_ref[g])

    base = [(off[g] // ROW_ALIGN) * ROW_ALIGN for g in range(N_GROUPS)]
    for g in range(N_GROUPS):
        group_rows(g, pl.multiple_of(base[g], ROW_ALIGN), cap)
    for g in range(N_GROUPS):
        n_extra = jnp.maximum(0, (off[g] + cnt[g] - (base[g] + cap) + xc - 1) // xc)

        def extra(k, carry, g=g):
            group_rows(g, pl.multiple_of(base[g] + cap + k * xc, ROW_ALIGN), xc)
            return carry

        lax.fori_loop(0, n_extra, extra, 0)

    moe = _dot(perm_t, ys_s[0:t, :].astype(BF16))
    y_ref[...] = _rms(x1_ref[...] + moe, gf_ref[...])


def _moe(x1, xn2, gates, wg, wu, wd, gf, t, cap, xc):
    n = x1.shape[0]
    tile = lambda i: (i, 0)
    whole = lambda i: (0, 0, 0)
    resident = pl.Buffered(1)
    rows = t + cap + xc
    hidden = (N_EXPERTS // N_GROUPS) * D_EXPERT
    return pl.pallas_call(
        functools.partial(_moe_body, cap, xc),
        grid=(n // t,),
        in_specs=[
            pl.BlockSpec((t, D_MODEL), tile),
            pl.BlockSpec((t, D_MODEL), tile),
            pl.BlockSpec((t, ROUTER_LANES), tile),
            pl.BlockSpec((N_EXPERTS, D_MODEL, D_EXPERT), whole, pipeline_mode=resident),
            pl.BlockSpec((N_EXPERTS, D_MODEL, D_EXPERT), whole, pipeline_mode=resident),
            pl.BlockSpec((N_GROUPS, hidden, D_MODEL), whole, pipeline_mode=resident),
            pl.BlockSpec((1, D_MODEL), lambda i: (0, 0)),
        ],
        out_specs=pl.BlockSpec((t, D_MODEL), tile),
        out_shape=jax.ShapeDtypeStruct((n, D_MODEL), F32),
        scratch_shapes=[
            pltpu.VMEM((rows, D_MODEL), BF16),
            pltpu.VMEM((rows, ROUTER_LANES), F32),
            pltpu.VMEM((rows, D_MODEL), F32),
        ],
        compiler_params=pltpu.CompilerParams(
            dimension_semantics=("arbitrary",), vmem_limit_bytes=MOE_VMEM_LIMIT),
        name="moe",
    )(x1, xn2, gates, wg, wu, wd, gf)


def _proj_sample_body(x_ref, g1_ref, win_ref, proj_ref, kvt_ref):
    xn = _rms(x_ref[...], g1_ref[...]).astype(BF16)
    proj_ref[...] = _dot(xn, win_ref[...])
    kvt_ref[...] = lax.dot_general(win_ref[:, OFF_K:OFF_B], xn, (((0,), (1,)), ((), ())),
                                   preferred_element_type=F32)


def _proj_sample(x, g1, win):
    n = x.shape[0]
    return pl.pallas_call(
        _proj_sample_body,
        out_shape=[jax.ShapeDtypeStruct((n, IN_PROJ_DIM), F32),
                   jax.ShapeDtypeStruct((2 * KV_DIM, n), F32)],
        compiler_params=pltpu.CompilerParams(vmem_limit_bytes=VMEM_LIMIT),
        name="proj_sample",
    )(x, g1, win)


def _attn_sample_body(q8_ref, knew_ref, vnew_ref, kvt_ref, khist_ref, vhist_ref, sinks_ref,
                      out8_ref, kout_ref, vout_ref):
    chunk = q8_ref.shape[0]
    first = pl.program_id(0) * chunk
    q8 = q8_ref[...]
    kh = khist_ref[...]
    vh = vhist_ref[...]
    knew = knew_ref[...]
    vnew = vnew_ref[...]
    s = jnp.einsum('bhj,bjc->bhc', q8, kh.astype(BF16), preferred_element_type=F32)
    s_new = jnp.sum(q8.astype(F32) * knew, axis=-1, keepdims=True)
    sink = sinks_ref[...][None]
    mx = jnp.maximum(jnp.maximum(jnp.max(s, axis=-1, keepdims=True), s_new), sink)
    e = jnp.exp(s - mx)
    e_new = jnp.exp(s_new - mx)
    den = jnp.sum(e, axis=-1, keepdims=True) + e_new + jnp.exp(sink - mx)
    o = jnp.einsum('bhc,bjc->bhj', e.astype(BF16), vh.astype(BF16), preferred_element_type=F32)
    out8_ref[...] = (o + e_new * vnew) / den
    last = lax.broadcasted_iota(jnp.int32, (KV_DIM, WINDOW), 1) == WINDOW - 1
    knew_t = kvt_ref[0:KV_DIM, :]
    vnew_t = kvt_ref[KV_DIM:, :]
    for bb in range(chunk):
        shift = lax.rem(2 * WINDOW - 1 - (first + bb), WINDOW)
        kout_ref[bb] = jnp.where(last, pltpu.roll(knew_t, shift, axis=1), pltpu.roll(kh[bb], WINDOW - 1, axis=1))
        vout_ref[bb] = jnp.where(last, pltpu.roll(vnew_t, shift, axis=1), pltpu.roll(vh[bb], WINDOW - 1, axis=1))


def _attn_sample(q8, knew, vnew, kvt, khist, vhist, sinks_col):
    assert q8.shape[0] == WINDOW == LANES, "one lane per sequence in the channel-major projection"
    n = q8.shape[0]
    c = DEC_CHUNK
    blk3 = lambda i: (i, 0, 0)
    return pl.pallas_call(
        _attn_sample_body,
        grid=(n // c,),
        in_specs=[
            pl.BlockSpec((c, N_HEADS, LANES), blk3),
            pl.BlockSpec((c, 1, KV_DIM), blk3),
            pl.BlockSpec((c, 1, KV_DIM), blk3),
            pl.BlockSpec((2 * KV_DIM, n), lambda i: (0, 0)),
            pl.BlockSpec((c, KV_DIM, WINDOW), blk3),
            pl.BlockSpec((c, KV_DIM, WINDOW), blk3),
            pl.BlockSpec((N_HEADS, 1), lambda i: (0, 0)),
        ],
        out_specs=[
            pl.BlockSpec((c, N_HEADS, LANES), blk3),
            pl.BlockSpec((c, KV_DIM, WINDOW), blk3),
            pl.BlockSpec((c, KV_DIM, WINDOW), blk3),
        ],
        out_shape=[
            jax.ShapeDtypeStruct((n, N_HEADS, LANES), F32),
            jax.ShapeDtypeStruct((n, KV_DIM, WINDOW), F32),
            jax.ShapeDtypeStruct((n, KV_DIM, WINDOW), F32),
        ],
        compiler_params=pltpu.CompilerParams(
            dimension_semantics=("arbitrary",), vmem_limit_bytes=VMEM_LIMIT),
        name="attn_sample",
    )(q8, knew, vnew, kvt, khist, vhist, sinks_col)


def _tail_sample_body(x_ref, attn_ref, proj_ref, uprev2_ref, uprev1_ref, convw_ref, gattn_ref,
                      gconv_ref, wout_ref, g2_ref, wr_ref, br_ref,
                      x1_ref, xn2_ref, gates_ref, u_ref):
    gate_b = proj_ref[:, OFF_B:OFF_C]
    u = proj_ref[:, OFF_C:OFF_H] * proj_ref[:, OFF_H:IN_PROJ_DIM]
    cw = convw_ref[...]
    z = cw[0:1, :] * uprev2_ref[...] + cw[1:2, :] * uprev1_ref[...] + cw[2:3, :] * u
    u_ref[...] = u
    x1, xn2_bf, gates = _mix_tail(x_ref[...], attn_ref[...], gate_b * z, gattn_ref[...], gconv_ref[...],
                                  wout_ref[...], g2_ref[...], wr_ref[...], br_ref[...])
    x1_ref[...] = x1
    xn2_ref[...] = xn2_bf
    gates_ref[...] = gates


def _tail_sample(x, attn, proj, uprev2, uprev1, convw, gattn, gconv, wout, g2, wr, br):
    n = x.shape[0]
    return pl.pallas_call(
        _tail_sample_body,
        out_shape=[
            jax.ShapeDtypeStruct((n, D_MODEL), F32),
            jax.ShapeDtypeStruct((n, D_MODEL), BF16),
            jax.ShapeDtypeStruct((n, ROUTER_LANES), F32),
            jax.ShapeDtypeStruct((n, CONV_DIM), F32),
        ],
        compiler_params=pltpu.CompilerParams(vmem_limit_bytes=VMEM_LIMIT),
        name="tail_sample",
    )(x, attn, proj, uprev2, uprev1, convw, gattn, gconv, wout, g2, wr, br)


def _pair_heads(w, axis):
    shape = w.shape
    n_pair = N_HEADS // 2
    split = shape[:axis] + (2, n_pair, HEAD_DIM) + shape[axis + 1:]
    return jnp.swapaxes(w.reshape(split), axis, axis + 1).reshape(shape)


def kernel(x_prompt, x_sample, cache_k_win, cache_v_win, state_conv, norm1_g, w_in, conv_w, sinks,
           attn_out_g, conv_out_g, w_out, norm2_g, w_group, b_group, w_fine, b_fine, w_gate, w_up,
           w_down, final_g):
    nb, seq, _ = x_prompt.shape
    nd = x_sample.shape[0]

    w_in0 = w_in[0].astype(BF16)
    win = jnp.concatenate([_pair_heads(w_in0[:, :ATTN_DIM], 1), w_in0[:, ATTN_DIM:]], axis=1)
    w_out0 = w_out[0].astype(BF16)
    wout = jnp.concatenate([_pair_heads(w_out0[:ATTN_DIM], 0), w_out0[ATTN_DIM:]], axis=0)
    gattn = _pair_heads(attn_out_g[0], 0)[None]
    gconv = conv_out_g[0][None]
    g1 = norm1_g[0][None]
    g2 = norm2_g[0][None]
    gf = final_g[None]
    convw = conv_w[0]
    sinks0 = sinks[0]
    pad = ROUTER_LANES - N_EXPERTS - N_GROUPS
    wr32 = jnp.concatenate([w_fine[0], w_group[0], jnp.zeros((D_MODEL, pad), F32)], axis=1)
    wr_hi = wr32.astype(BF16)
    wr = jnp.concatenate([wr_hi, (wr32 - wr_hi.astype(F32)).astype(BF16)], axis=1)
    br = jnp.concatenate([b_fine[0], b_group[0], jnp.zeros((pad,), F32)])[:ROUTER_ROWS, None]

    x1p, xn2p, gatesp, kp, vp, up, wg, wu, wd = _layer_prompt(
        x_prompt, sinks0, g1, win, convw, gattn, gconv, wout, g2, wr, br, w_gate[0], w_up[0], w_down[0])
    wd = wd.reshape(N_GROUPS, (N_EXPERTS // N_GROUPS) * D_EXPERT, D_MODEL)
    n_p = nb * seq
    y_prompt = _moe(x1p.reshape(n_p, D_MODEL), xn2p.reshape(n_p, D_MODEL),
                    gatesp.reshape(n_p, ROUTER_LANES), wg, wu, wd, gf, T_MOE, MOE_CAP, MOE_XC).reshape(nb, seq, D_MODEL)

    xs = x_sample.reshape(nd, D_MODEL)
    proj, kvt = _proj_sample(xs, g1, win)
    lane = jnp.arange(LANES)
    qp = (proj[:, :ATTN_DIM] * (HEAD_DIM ** -0.5)).reshape(nd, N_HEADS // 2, LANES)
    q8 = jnp.concatenate([jnp.where(lane < HEAD_DIM, qp, 0.0), jnp.where(lane >= HEAD_DIM, qp, 0.0)],
                         axis=1).astype(BF16)
    knew = proj[:, OFF_K:OFF_V].reshape(nd, 1, KV_DIM)
    vnew = proj[:, OFF_V:OFF_B].reshape(nd, 1, KV_DIM)
    to_cm = lambda c: jnp.transpose(c[0], (0, 2, 3, 1)).reshape(nd, KV_DIM, WINDOW)
    from_cm = lambda c, n: jnp.transpose(c.reshape(n, KV_DIM // HEAD_DIM, HEAD_DIM, WINDOW), (0, 3, 1, 2))[None]
    out8, ks, vs = _attn_sample(q8, knew, vnew, kvt, to_cm(cache_k_win), to_cm(cache_v_win), sinks0[:, None])
    attn_s = jnp.where(lane < HEAD_DIM, out8[:, :N_HEADS // 2], out8[:, N_HEADS // 2:]).reshape(nd, ATTN_DIM)
    st = state_conv[0]
    x1s, xn2s, gatess, us = _tail_sample(xs, attn_s, proj, st[:, 0], st[:, 1], convw, gattn, gconv,
                                         wout, g2, wr, br)
    y_sample = _moe(x1s, xn2s, gatess, wg, wu, wd, gf, nd, MOE_CAP_DEC, MOE_XC_DEC).reshape(nd, 1, D_MODEL)

    return (y_prompt, y_sample,
            from_cm(kp, nb), from_cm(vp, nb),
            up[None],
            from_cm(ks, nd), from_cm(vs, nd),
            jnp.stack([st[:, 1], us], axis=1)[None])
```

```python
import functools

import jax
import jax.numpy as jnp
from jax import lax
from jax.experimental import pallas as pl
from jax.experimental.pallas import tpu as pltpu

D_MODEL = 1024
HEAD_DIM = 64
N_HEADS = 8
ATTN_DIM = 512
KV_DIM = 128
WINDOW = 128
CONV_DIM = 512
CONV_WIDTH = 3
OFF_K = 512
OFF_V = 640
OFF_B = 768
OFF_C = 1280
OFF_H = 1792
IN_PROJ_DIM = 2304
N_GROUPS = 4
N_EXPERTS = 16
D_EXPERT = 256
RMS_EPS = 1e-5
NEG_INF = -1e30

LANES = 128
ROW_ALIGN = 16
ROUTER_LANES = 128
ROUTER_ROWS = 32
COARSE_OFF = N_EXPERTS
T_LAYER = 1024
N_SUB = 2
GID_LANE = N_EXPERTS
T_MOE = 512
TOKEN_CHUNK = 16
MOE_CAP, MOE_XC = 160, 64
MOE_CAP_DEC, MOE_XC_DEC = 64, 32
DEC_CHUNK = 32
VMEM_LIMIT = 48 * 1024 * 1024
MOE_VMEM_LIMIT = 56 * 1024 * 1024

BF16 = jnp.bfloat16
F32 = jnp.float32


def _dot(a, b):
    return jnp.dot(a, b, preferred_element_type=F32)


def _rms(x, g):
    ms = jnp.mean(x * x, axis=-1, keepdims=True)
    return x * lax.rsqrt(ms + RMS_EPS) * g


def _group_norm64(y, g):
    rows, cols = y.shape
    lo = lax.broadcasted_iota(jnp.int32, (rows, LANES), 1) < HEAD_DIM
    outs = []
    for c in range(cols // LANES):
        blk = y[:, c * LANES:(c + 1) * LANES]
        sq = blk * blk
        s_lo = jnp.sum(jnp.where(lo, sq, 0.0), axis=-1, keepdims=True)
        s_hi = jnp.sum(jnp.where(lo, 0.0, sq), axis=-1, keepdims=True)
        ms = jnp.where(lo, s_lo, s_hi) * (1.0 / HEAD_DIM)
        outs.append(blk * lax.rsqrt(ms + RMS_EPS))
    return jnp.concatenate(outs, axis=-1) * g


def _router(xn2, wr, br):
    hi = xn2.astype(BF16)
    both = _dot(hi, wr)
    logits = jnp.transpose(both[:, :ROUTER_LANES] + both[:, ROUTER_LANES:])[:ROUTER_ROWS] + br
    n = logits.shape[1]
    row = lax.broadcasted_iota(jnp.int32, (ROUTER_ROWS, n), 0)
    rowf = row.astype(F32)
    big = float(ROUTER_ROWS)
    coarse = (row >= COARSE_OFF) & (row < COARSE_OFF + N_GROUPS)
    cm = jnp.max(jnp.where(coarse, logits, -jnp.inf), axis=0, keepdims=True)
    g_sel = jnp.min(jnp.where(coarse & (logits == cm), rowf - COARSE_OFF, big), axis=0, keepdims=True)
    zc = jnp.sum(jnp.where(coarse, jnp.exp(logits - cm), 0.0), axis=0, keepdims=True)
    p_sel = 1.0 / zc
    per_group = N_EXPERTS // N_GROUPS
    fine = (row < N_EXPERTS) & ((row // per_group).astype(F32) == g_sel)
    f1 = jnp.max(jnp.where(fine, logits, -jnp.inf), axis=0, keepdims=True)
    i1 = jnp.min(jnp.where(fine & (logits == f1), rowf, big), axis=0, keepdims=True)
    rest = fine & (rowf != i1)
    f2 = jnp.max(jnp.where(rest, logits, -jnp.inf), axis=0, keepdims=True)
    i2 = jnp.min(jnp.where(rest & (logits == f2), rowf, big), axis=0, keepdims=True)
    e2 = jnp.exp(f2 - f1)
    t1 = 1.0 / (1.0 + e2)
    t2 = e2 * t1
    gates_t = jnp.where(rowf == i1, p_sel * t1, jnp.where(rowf == i2, p_sel * t2, 0.0))
    gates_t = jnp.where(row == GID_LANE, g_sel, gates_t)
    gates_t = jnp.concatenate([gates_t, jnp.zeros((ROUTER_LANES - ROUTER_ROWS, n), F32)], axis=0)
    return hi, jnp.transpose(gates_t)


def _mix_tail(x, attn, conv_out, gattn, gconv, wout, g2, wr, br):
    mixed = jnp.concatenate([_group_norm64(attn, gattn), _group_norm64(conv_out, gconv)],
                            axis=-1).astype(BF16)
    x1 = x + _dot(mixed, wout)
    xn2_bf, gates = _router(_rms(x1, g2), wr, br)
    return x1, xn2_bf, gates


def _layer_prompt_body(sinks_ref, x_ref, g1_ref, win_ref, convw_ref, gattn_ref, gconv_ref,
                       wout_ref, g2_ref, wr_ref, br_ref, wg32_ref, wu32_ref, wd32_ref,
                       x1_ref, xn2_ref, gates_ref, knew_ref, vnew_ref, unew_ref,
                       wg16_ref, wu16_ref, wd16_ref,
                       q_s, kcat_s, vcat_s, attn_s, ucarry_s):
    i = pl.program_id(1)
    wg16_ref[...] = wg32_ref[...].astype(BF16)
    wu16_ref[...] = wu32_ref[...].astype(BF16)
    wd16_ref[...] = wd32_ref[...].astype(BF16)
    t = T_LAYER // N_SUB
    last = N_SUB - 1

    @pl.when(i == 0)
    def _():
        kcat_s[last, t:t + WINDOW, :] = jnp.zeros((WINDOW, KV_DIM), BF16)
        vcat_s[last, t:t + WINDOW, :] = jnp.zeros((WINDOW, KV_DIM), BF16)
        ucarry_s[last] = jnp.zeros((8, CONV_DIM), F32)

    a_idx = lax.broadcasted_iota(jnp.int32, (WINDOW, 2 * WINDOW), 0)
    c_idx = lax.broadcasted_iota(jnp.int32, (WINDOW, 2 * WINDOW), 1)
    lane_lo = lax.broadcasted_iota(jnp.int32, (WINDOW, LANES), 1) < HEAD_DIM
    row = lax.broadcasted_iota(jnp.int32, (t, CONV_DIM), 0)
    n_pair = N_HEADS // 2
    cw = convw_ref[...]

    for h in range(N_SUB):
        src = (h - 1) % N_SUB
        rows = pl.ds(h * t, t)
        x = x_ref[0, rows, :]
        xn = _rms(x, g1_ref[...]).astype(BF16)

        kcat_s[h, 0:WINDOW, :] = kcat_s[src, t:t + WINDOW, :]
        vcat_s[h, 0:WINDOW, :] = vcat_s[src, t:t + WINDOW, :]
        q_s[h] = (_dot(xn, win_ref[:, 0:OFF_K]) * (HEAD_DIM ** -0.5)).astype(BF16)
        kv = _dot(xn, win_ref[:, OFF_K:OFF_B])
        k = kv[:, :KV_DIM]
        v = kv[:, KV_DIM:]
        kcat_s[h, WINDOW:, :] = k.astype(BF16)
        vcat_s[h, WINDOW:, :] = v.astype(BF16)
        if h == last:
            knew_ref[0] = jnp.transpose(k[t - WINDOW:, :])
            vnew_ref[0] = jnp.transpose(v[t - WINDOW:, :])

        for j in range(t // WINDOW):
            r0 = j * WINDOW
            kk = kcat_s[h, r0:r0 + 2 * WINDOW, :]
            vv = vcat_s[h, r0:r0 + 2 * WINDOW, :]
            c_min = jnp.where(i == 0, WINDOW, 0) if (h == 0 and j == 0) else 0
            valid = (c_idx >= jnp.maximum(a_idx, c_min)) & (c_idx <= a_idx + WINDOW)
            qcols = [q_s[h, r0:r0 + WINDOW, m * LANES:(m + 1) * LANES] for m in range(n_pair)]
            outs = []
            for half in range(2):
                keep = lane_lo if half == 0 else jnp.logical_not(lane_lo)
                qg = jnp.concatenate([jnp.where(keep, qc, jnp.zeros_like(qc)) for qc in qcols], axis=0)
                s = lax.dot_general(qg, kk, (((1,), (1,)), ((), ())), preferred_element_type=F32)
                es, dens = [], []
                for m in range(n_pair):
                    sm = jnp.where(valid, s[m * WINDOW:(m + 1) * WINDOW], NEG_INF)
                    sink = sinks_ref[m + half * n_pair]
                    mx = jnp.maximum(jnp.max(sm, axis=-1, keepdims=True), sink)
                    e = jnp.exp(sm - mx)
                    dens.append(jnp.sum(e, axis=-1, keepdims=True) + jnp.exp(sink - mx))
                    es.append(e.astype(BF16))
                o = _dot(jnp.concatenate(es, axis=0), vv)
                outs.append([o[m * WINDOW:(m + 1) * WINDOW] / dens[m] for m in range(n_pair)])
            for m in range(n_pair):
                attn_s[h, r0:r0 + WINDOW, m * LANES:(m + 1) * LANES] = jnp.where(lane_lo, outs[0][m], outs[1][m])

        gate_b = _dot(xn, win_ref[:, OFF_B:OFF_C])
        u = _dot(xn, win_ref[:, OFF_C:OFF_H]) * _dot(xn, win_ref[:, OFF_H:IN_PROJ_DIM])
        prev = ucarry_s[src]
        u1 = jnp.where(row == 0, prev[7:8, :], pltpu.roll(u, 1, axis=0))
        u2 = jnp.where(row == 0, prev[6:7, :], jnp.where(row == 1, prev[7:8, :], pltpu.roll(u, 2, axis=0)))
        z = cw[0:1, :] * u2 + cw[1:2, :] * u1 + cw[2:3, :] * u
        ucarry_s[h] = u[t - 8:, :]
        if h == last:
            unew_ref[0] = u[t - (CONV_WIDTH - 1):, :]

        x1, xn2_bf, gates = _mix_tail(x, attn_s[h], gate_b * z, gattn_ref[...], gconv_ref[...],
                                      wout_ref[...], g2_ref[...], wr_ref[...], br_ref[...])
        x1_ref[0, rows, :] = x1
        xn2_ref[0, rows, :] = xn2_bf
        gates_ref[0, rows, :] = gates


def _layer_prompt(x, sinks, g1, win, convw, gattn, gconv, wout, g2, wr, br, wg32, wu32, wd32):
    nb, seq, _ = x.shape
    t = T_LAYER
    ts = T_LAYER // N_SUB
    n_i = seq // t
    assert nb * n_i == N_EXPERTS, "one expert's weights are cast per grid step"
    const = lambda b, i, s: (0, 0)
    tile = lambda b, i, s: (b, i, 0)
    per_seq = lambda b, i, s: (b, 0, 0)
    per_step = lambda b, i, s: (b * n_i + i, 0, 0)
    resident = pl.Buffered(1)
    grid_spec = pltpu.PrefetchScalarGridSpec(
        num_scalar_prefetch=1,
        grid=(nb, seq // t),
        in_specs=[
            pl.BlockSpec((1, t, D_MODEL), tile),
            pl.BlockSpec((1, D_MODEL), const),
            pl.BlockSpec((D_MODEL, IN_PROJ_DIM), const, pipeline_mode=resident),
            pl.BlockSpec((CONV_WIDTH, CONV_DIM), const),
            pl.BlockSpec((1, ATTN_DIM), const),
            pl.BlockSpec((1, CONV_DIM), const),
            pl.BlockSpec((D_MODEL, D_MODEL), const, pipeline_mode=resident),
            pl.BlockSpec((1, D_MODEL), const),
            pl.BlockSpec((D_MODEL, 2 * ROUTER_LANES), const, pipeline_mode=resident),
            pl.BlockSpec((ROUTER_ROWS, 1), const),
            pl.BlockSpec((1, D_MODEL, D_EXPERT), per_step),
            pl.BlockSpec((1, D_MODEL, D_EXPERT), per_step),
            pl.BlockSpec((1, D_EXPERT, D_MODEL), per_step),
        ],
        out_specs=[
            pl.BlockSpec((1, t, D_MODEL), tile),
            pl.BlockSpec((1, t, D_MODEL), tile),
            pl.BlockSpec((1, t, ROUTER_LANES), tile),
            pl.BlockSpec((1, KV_DIM, WINDOW), per_seq),
            pl.BlockSpec((1, KV_DIM, WINDOW), per_seq),
            pl.BlockSpec((1, CONV_WIDTH - 1, CONV_DIM), per_seq),
            pl.BlockSpec((1, D_MODEL, D_EXPERT), per_step),
            pl.BlockSpec((1, D_MODEL, D_EXPERT), per_step),
            pl.BlockSpec((1, D_EXPERT, D_MODEL), per_step),
        ],
        scratch_shapes=[
            pltpu.VMEM((N_SUB, ts, ATTN_DIM), BF16),
            pltpu.VMEM((N_SUB, ts + WINDOW, KV_DIM), BF16),
            pltpu.VMEM((N_SUB, ts + WINDOW, KV_DIM), BF16),
            pltpu.VMEM((N_SUB, ts, ATTN_DIM), F32),
            pltpu.VMEM((N_SUB, 8, CONV_DIM), F32),
        ],
    )
    return pl.pallas_call(
        _layer_prompt_body,
        grid_spec=grid_spec,
        out_shape=[
            jax.ShapeDtypeStruct((nb, seq, D_MODEL), F32),
            jax.ShapeDtypeStruct((nb, seq, D_MODEL), BF16),
            jax.ShapeDtypeStruct((nb, seq, ROUTER_LANES), F32),
            jax.ShapeDtypeStruct((nb, KV_DIM, WINDOW), F32),
            jax.ShapeDtypeStruct((nb, KV_DIM, WINDOW), F32),
            jax.ShapeDtypeStruct((nb, CONV_WIDTH - 1, CONV_DIM), F32),
            jax.ShapeDtypeStruct((N_EXPERTS, D_MODEL, D_EXPERT), BF16),
            jax.ShapeDtypeStruct((N_EXPERTS, D_MODEL, D_EXPERT), BF16),
            jax.ShapeDtypeStruct((N_EXPERTS, D_EXPERT, D_MODEL), BF16),
        ],
        compiler_params=pltpu.CompilerParams(
            dimension_semantics=("arbitrary", "arbitrary"), vmem_limit_bytes=MOE_VMEM_LIMIT),
        name="layer_prompt",
    )(sinks, x, g1, win, convw, gattn, gconv, wout, g2, wr, br, wg32, wu32, wd32)


def _moe_body(cap, xc, x1_ref, xn2_ref, gates_ref, wg_ref, wu_ref, wd_ref, gf_ref, y_ref,
              xs_s, gs_s, ys_s):
    t = x1_ref.shape[0] * TOKEN_CHUNK
    per_group = N_EXPERTS // N_GROUPS
    gates = gates_ref[...].reshape(t, ROUTER_LANES)
    gid_row = jnp.transpose(gates)[GID_LANE:GID_LANE + 1, :]
    grp = lax.broadcasted_iota(jnp.int32, (8, t), 0).astype(F32)
    onehot = jnp.where(grp == gid_row, 1.0, 0.0)

    r_idx = lax.broadcasted_iota(jnp.int32, (t, t), 0)
    c_idx = lax.broadcasted_iota(jnp.int32, (t, t), 1)
    upper = jnp.where(r_idx <= c_idx, 1.0, 0.0).astype(BF16)
    csum = _dot(onehot.astype(BF16), upper)
    rank_row = jnp.sum(onehot * (csum - 1.0), axis=0, keepdims=True)
    cnt = [csum[g, t - 1].astype(jnp.int32) for g in range(N_GROUPS)]
    off = [jnp.int32(0)]
    for g in range(1, N_GROUPS):
        off.append(off[-1] + cnt[g - 1])

    pos_row = rank_row
    for g in range(1, N_GROUPS):
        pos_row = pos_row + jnp.where(gid_row == float(g), off[g].astype(F32), 0.0)
    pos_col = jnp.transpose(jnp.broadcast_to(pos_row, (LANES, t)))
    pos_col = jnp.concatenate([pos_col] * (t // LANES), axis=1)
    perm = jnp.where(r_idx.astype(F32) == pos_row, 1.0, 0.0).astype(BF16)
    perm_t = jnp.where(c_idx.astype(F32) == pos_col, 1.0, 0.0).astype(BF16)

    p1 = gates.astype(BF16)
    p2 = (gates - p1.astype(F32)).astype(BF16)
    moved = _dot(perm, jnp.concatenate([xn2_ref[...].reshape(t, D_MODEL), p1, p2], axis=1))
    xs_s[0:t, :] = moved[:, :D_MODEL].astype(BF16)
    gs_s[0:t, :] = moved[:, D_MODEL:D_MODEL + ROUTER_LANES] + moved[:, D_MODEL + ROUTER_LANES:]
    pad = xs_s.shape[0] - t
    xs_s[t:, :] = jnp.zeros((pad, D_MODEL), BF16)
    gs_s[t:, :] = jnp.zeros((pad, ROUTER_LANES), F32)
    ys_s[...] = jnp.zeros(ys_s.shape, F32)

    def group_rows(g, r0, rows):
        xs = xs_s[pl.ds(r0, rows), :]
        gsc = gs_s[pl.ds(r0, rows), :]
        acts = []
        for k in range(per_group):
            e = g * per_group + k
            hg = _dot(xs, wg_ref[e])
            hu = _dot(xs, wu_ref[e])
            acts.append((hg / (1.0 + jnp.exp(-hg)) * hu * gsc[:, e:e + 1]).astype(BF16))
        ys_s[pl.ds(r0, rows), :] += _dot(jnp.concatenate(acts, axis=1), wd_ref[g])

    base = [(off[g] // ROW_ALIGN) * ROW_ALIGN for g in range(N_GROUPS)]
    for g in range(N_GROUPS):
        group_rows(g, pl.multiple_of(base[g], ROW_ALIGN), cap)
    for g in range(N_GROUPS):
        n_extra = jnp.maximum(0, (off[g] + cnt[g] - (base[g] + cap) + xc - 1) // xc)

        def extra(k, carry, g=g):
            group_rows(g, pl.multiple_of(base[g] + cap + k * xc, ROW_ALIGN), xc)
            return carry

        lax.fori_loop(0, n_extra, extra, 0)

    moe = _dot(perm_t, ys_s[0:t, :].astype(BF16))
    y = _rms(x1_ref[...].reshape(t, D_MODEL) + moe, gf_ref[...])
    y_ref[...] = y.reshape(y_ref.shape)


def _moe(x1, xn2, gates, wg, wu, wd, gf, t, cap, xc):
    n = x1.shape[0]
    n_tiles = n // t
    pieces = t // TOKEN_CHUNK
    view = lambda a: a.reshape(pieces, n_tiles, TOKEN_CHUNK, a.shape[-1])
    tile = lambda i: (0, i, 0, 0)
    whole = lambda i: (0, 0, 0)
    resident = pl.Buffered(1)
    rows = t + cap + xc
    hidden = (N_EXPERTS // N_GROUPS) * D_EXPERT
    y = pl.pallas_call(
        functools.partial(_moe_body, cap, xc),
        grid=(n_tiles,),
        in_specs=[
            pl.BlockSpec((pieces, 1, TOKEN_CHUNK, D_MODEL), tile),
            pl.BlockSpec((pieces, 1, TOKEN_CHUNK, D_MODEL), tile),
            pl.BlockSpec((pieces, 1, TOKEN_CHUNK, ROUTER_LANES), tile),
            pl.BlockSpec((N_EXPERTS, D_MODEL, D_EXPERT), whole, pipeline_mode=resident),
            pl.BlockSpec((N_EXPERTS, D_MODEL, D_EXPERT), whole, pipeline_mode=resident),
            pl.BlockSpec((N_GROUPS, hidden, D_MODEL), whole, pipeline_mode=resident),
            pl.BlockSpec((1, D_MODEL), lambda i: (0, 0)),
        ],
        out_specs=pl.BlockSpec((pieces, 1, TOKEN_CHUNK, D_MODEL), tile),
        out_shape=jax.ShapeDtypeStruct((pieces, n_tiles, TOKEN_CHUNK, D_MODEL), F32),
        scratch_shapes=[
            pltpu.VMEM((rows, D_MODEL), BF16),
            pltpu.VMEM((rows, ROUTER_LANES), F32),
            pltpu.VMEM((rows, D_MODEL), F32),
        ],
        compiler_params=pltpu.CompilerParams(
            dimension_semantics=("arbitrary",), vmem_limit_bytes=MOE_VMEM_LIMIT),
        name="moe",
    )(view(x1), view(xn2), view(gates), wg, wu, wd, gf)
    return y.reshape(n, D_MODEL)


def _proj_sample_body(x_ref, g1_ref, win_ref, proj_ref, kvt_ref):
    xn = _rms(x_ref[...], g1_ref[...]).astype(BF16)
    proj_ref[...] = _dot(xn, win_ref[...])
    kvt_ref[...] = lax.dot_general(win_ref[:, OFF_K:OFF_B], xn, (((0,), (1,)), ((), ())),
                                   preferred_element_type=F32)


def _proj_sample(x, g1, win):
    n = x.shape[0]
    return pl.pallas_call(
        _proj_sample_body,
        out_shape=[jax.ShapeDtypeStruct((n, IN_PROJ_DIM), F32),
                   jax.ShapeDtypeStruct((2 * KV_DIM, n), F32)],
        compiler_params=pltpu.CompilerParams(vmem_limit_bytes=VMEM_LIMIT),
        name="proj_sample",
    )(x, g1, win)


def _attn_sample_body(q8_ref, knew_ref, vnew_ref, kvt_ref, khist_ref, vhist_ref, sinks_ref,
                      out8_ref, kout_ref, vout_ref):
    chunk = q8_ref.shape[0]
    first = pl.program_id(0) * chunk
    q8 = q8_ref[...]
    kh = khist_ref[...]
    vh = vhist_ref[...]
    knew = knew_ref[...]
    vnew = vnew_ref[...]
    s = jnp.einsum('bhj,bjc->bhc', q8, kh.astype(BF16), preferred_element_type=F32)
    s_new = jnp.sum(q8.astype(F32) * knew, axis=-1, keepdims=True)
    sink = sinks_ref[...][None]
    mx = jnp.maximum(jnp.maximum(jnp.max(s, axis=-1, keepdims=True), s_new), sink)
    e = jnp.exp(s - mx)
    e_new = jnp.exp(s_new - mx)
    den = jnp.sum(e, axis=-1, keepdims=True) + e_new + jnp.exp(sink - mx)
    o = jnp.einsum('bhc,bjc->bhj', e.astype(BF16), vh.astype(BF16), preferred_element_type=F32)
    out8_ref[...] = (o + e_new * vnew) / den
    last = lax.broadcasted_iota(jnp.int32, (KV_DIM, WINDOW), 1) == WINDOW - 1
    knew_t = kvt_ref[0:KV_DIM, :]
    vnew_t = kvt_ref[KV_DIM:, :]
    for bb in range(chunk):
        shift = lax.rem(2 * WINDOW - 1 - (first + bb), WINDOW)
        kout_ref[bb] = jnp.where(last, pltpu.roll(knew_t, shift, axis=1), pltpu.roll(kh[bb], WINDOW - 1, axis=1))
        vout_ref[bb] = jnp.where(last, pltpu.roll(vnew_t, shift, axis=1), pltpu.roll(vh[bb], WINDOW - 1, axis=1))


def _attn_sample(q8, knew, vnew, kvt, khist, vhist, sinks_col):
    assert q8.shape[0] == WINDOW == LANES, "one lane per sequence in the channel-major projection"
    n = q8.shape[0]
    c = DEC_CHUNK
    blk3 = lambda i: (i, 0, 0)
    return pl.pallas_call(
        _attn_sample_body,
        grid=(n // c,),
        in_specs=[
            pl.BlockSpec((c, N_HEADS, LANES), blk3),
            pl.BlockSpec((c, 1, KV_DIM), blk3),
            pl.BlockSpec((c, 1, KV_DIM), blk3),
            pl.BlockSpec((2 * KV_DIM, n), lambda i: (0, 0)),
            pl.BlockSpec((c, KV_DIM, WINDOW), blk3),
            pl.BlockSpec((c, KV_DIM, WINDOW), blk3),
            pl.BlockSpec((N_HEADS, 1), lambda i: (0, 0)),
        ],
        out_specs=[
            pl.BlockSpec((c, N_HEADS, LANES), blk3),
            pl.BlockSpec((c, KV_DIM, WINDOW), blk3),
            pl.BlockSpec((c, KV_DIM, WINDOW), blk3),
        ],
        out_shape=[
            jax.ShapeDtypeStruct((n, N_HEADS, LANES), F32),
            jax.ShapeDtypeStruct((n, KV_DIM, WINDOW), F32),
            jax.ShapeDtypeStruct((n, KV_DIM, WINDOW), F32),
        ],
        compiler_params=pltpu.CompilerParams(
            dimension_semantics=("arbitrary",), vmem_limit_bytes=VMEM_LIMIT),
        name="attn_sample",
    )(q8, knew, vnew, kvt, khist, vhist, sinks_col)


def _tail_sample_body(x_ref, attn_ref, proj_ref, uprev2_ref, uprev1_ref, convw_ref, gattn_ref,
                      gconv_ref, wout_ref, g2_ref, wr_ref, br_ref,
                      x1_ref, xn2_ref, gates_ref, u_ref):
    gate_b = proj_ref[:, OFF_B:OFF_C]
    u = proj_ref[:, OFF_C:OFF_H] * proj_ref[:, OFF_H:IN_PROJ_DIM]
    cw = convw_ref[...]
    z = cw[0:1, :] * uprev2_ref[...] + cw[1:2, :] * uprev1_ref[...] + cw[2:3, :] * u
    u_ref[...] = u
    x1, xn2_bf, gates = _mix_tail(x_ref[...], attn_ref[...], gate_b * z, gattn_ref[...], gconv_ref[...],
                                  wout_ref[...], g2_ref[...], wr_ref[...], br_ref[...])
    x1_ref[...] = x1
    xn2_ref[...] = xn2_bf
    gates_ref[...] = gates


def _tail_sample(x, attn, proj, uprev2, uprev1, convw, gattn, gconv, wout, g2, wr, br):
    n = x.shape[0]
    return pl.pallas_call(
        _tail_sample_body,
        out_shape=[
            jax.ShapeDtypeStruct((n, D_MODEL), F32),
            jax.ShapeDtypeStruct((n, D_MODEL), BF16),
            jax.ShapeDtypeStruct((n, ROUTER_LANES), F32),
            jax.ShapeDtypeStruct((n, CONV_DIM), F32),
        ],
        compiler_params=pltpu.CompilerParams(vmem_limit_bytes=VMEM_LIMIT),
        name="tail_sample",
    )(x, attn, proj, uprev2, uprev1, convw, gattn, gconv, wout, g2, wr, br)


def _pair_heads(w, axis):
    shape = w.shape
    n_pair = N_HEADS // 2
    split = shape[:axis] + (2, n_pair, HEAD_DIM) + shape[axis + 1:]
    return jnp.swapaxes(w.reshape(split), axis, axis + 1).reshape(shape)


def kernel(x_prompt, x_sample, cache_k_win, cache_v_win, state_conv, norm1_g, w_in, conv_w, sinks,
           attn_out_g, conv_out_g, w_out, norm2_g, w_group, b_group, w_fine, b_fine, w_gate, w_up,
           w_down, final_g):
    nb, seq, _ = x_prompt.shape
    nd = x_sample.shape[0]

    w_in0 = w_in[0].astype(BF16)
    win = jnp.concatenate([_pair_heads(w_in0[:, :ATTN_DIM], 1), w_in0[:, ATTN_DIM:]], axis=1)
    w_out0 = w_out[0].astype(BF16)
    wout = jnp.concatenate([_pair_heads(w_out0[:ATTN_DIM], 0), w_out0[ATTN_DIM:]], axis=0)
    gattn = _pair_heads(attn_out_g[0], 0)[None]
    gconv = conv_out_g[0][None]
    g1 = norm1_g[0][None]
    g2 = norm2_g[0][None]
    gf = final_g[None]
    convw = conv_w[0]
    sinks0 = sinks[0]
    pad = ROUTER_LANES - N_EXPERTS - N_GROUPS
    wr32 = jnp.concatenate([w_fine[0], w_group[0], jnp.zeros((D_MODEL, pad), F32)], axis=1)
    wr_hi = wr32.astype(BF16)
    wr = jnp.concatenate([wr_hi, (wr32 - wr_hi.astype(F32)).astype(BF16)], axis=1)
    br = jnp.concatenate([b_fine[0], b_group[0], jnp.zeros((pad,), F32)])[:ROUTER_ROWS, None]

    x1p, xn2p, gatesp, kp, vp, up, wg, wu, wd = _layer_prompt(
        x_prompt, sinks0, g1, win, convw, gattn, gconv, wout, g2, wr, br, w_gate[0], w_up[0], w_down[0])
    wd = wd.reshape(N_GROUPS, (N_EXPERTS // N_GROUPS) * D_EXPERT, D_MODEL)
    n_p = nb * seq
    y_prompt = _moe(x1p.reshape(n_p, D_MODEL), xn2p.reshape(n_p, D_MODEL),
                    gatesp.reshape(n_p, ROUTER_LANES), wg, wu, wd, gf, T_MOE, MOE_CAP, MOE_XC).reshape(nb, seq, D_MODEL)

    xs = x_sample.reshape(nd, D_MODEL)
    proj, kvt = _proj_sample(xs, g1, win)
    lane = jnp.arange(LANES)
    qp = (proj[:, :ATTN_DIM] * (HEAD_DIM ** -0.5)).reshape(nd, N_HEADS // 2, LANES)
    q8 = jnp.concatenate([jnp.where(lane < HEAD_DIM, qp, 0.0), jnp.where(lane >= HEAD_DIM, qp, 0.0)],
                         axis=1).astype(BF16)
    knew = proj[:, OFF_K:OFF_V].reshape(nd, 1, KV_DIM)
    vnew = proj[:, OFF_V:OFF_B].reshape(nd, 1, KV_DIM)
    to_cm = lambda c: jnp.transpose(c[0], (0, 2, 3, 1)).reshape(nd, KV_DIM, WINDOW)
    from_cm = lambda c, n: jnp.transpose(c.reshape(n, KV_DIM // HEAD_DIM, HEAD_DIM, WINDOW), (0, 3, 1, 2))[None]
    out8, ks, vs = _attn_sample(q8, knew, vnew, kvt, to_cm(cache_k_win), to_cm(cache_v_win), sinks0[:, None])
    attn_s = jnp.where(lane < HEAD_DIM, out8[:, :N_HEADS // 2], out8[:, N_HEADS // 2:]).reshape(nd, ATTN_DIM)
    st = state_conv[0]
    x1s, xn2s, gatess, us = _tail_sample(xs, attn_s, proj, st[:, 0], st[:, 1], convw, gattn, gconv,
                                         wout, g2, wr, br)
    y_sample = _moe(x1s, xn2s, gatess, wg, wu, wd, gf, nd, MOE_CAP_DEC, MOE_XC_DEC).reshape(nd, 1, D_MODEL)

    return (y_prompt, y_sample,
            from_cm(kp, nb), from_cm(vp, nb),
            up[None],
            from_cm(ks, nd), from_cm(vs, nd),
            jnp.stack([st[:, 1], us], axis=1)[None])
```

```python
import functools

import jax
import jax.numpy as jnp
from jax import lax
from jax.experimental import pallas as pl
from jax.experimental.pallas import tpu as pltpu

D_MODEL = 1024
HEAD_DIM = 64
N_HEADS = 8
ATTN_DIM = 512
KV_DIM = 128
WINDOW = 128
CONV_DIM = 512
CONV_WIDTH = 3
OFF_K = 512
OFF_V = 640
OFF_B = 768
OFF_C = 1280
OFF_H = 1792
IN_PROJ_DIM = 2304
N_GROUPS = 4
N_EXPERTS = 16
D_EXPERT = 256
RMS_EPS = 1e-5
NEG_INF = -1e30

LANES = 128
ROW_ALIGN = 16
ROUTER_LANES = 128
ROUTER_ROWS = 32
COARSE_OFF = N_EXPERTS
T_LAYER = 1024
N_SUB = 2
GID_LANE = N_EXPERTS
T_MOE = 512
TOKEN_CHUNK = 16
MOE_CAP, MOE_XC = 160, 64
MOE_CAP_DEC, MOE_XC_DEC = 64, 32
DEC_CHUNK = 32
VMEM_LIMIT = 48 * 1024 * 1024
MOE_VMEM_LIMIT = 56 * 1024 * 1024

BF16 = jnp.bfloat16
F32 = jnp.float32


def _dot(a, b):
    return jnp.dot(a, b, preferred_element_type=F32)


def _rms(x, g):
    ms = jnp.mean(x * x, axis=-1, keepdims=True)
    return x * lax.rsqrt(ms + RMS_EPS) * g


def _group_norm64(y, g):
    rows, cols = y.shape
    lo = lax.broadcasted_iota(jnp.int32, (rows, LANES), 1) < HEAD_DIM
    outs = []
    for c in range(cols // LANES):
        blk = y[:, c * LANES:(c + 1) * LANES]
        sq = blk * blk
        s_lo = jnp.sum(jnp.where(lo, sq, 0.0), axis=-1, keepdims=True)
        s_hi = jnp.sum(jnp.where(lo, 0.0, sq), axis=-1, keepdims=True)
        ms = jnp.where(lo, s_lo, s_hi) * (1.0 / HEAD_DIM)
        outs.append(blk * lax.rsqrt(ms + RMS_EPS))
    return jnp.concatenate(outs, axis=-1) * g


def _router(xn2, wr, br):
    hi = xn2.astype(BF16)
    both = _dot(hi, wr)
    logits = jnp.transpose(both[:, :ROUTER_LANES] + both[:, ROUTER_LANES:])[:ROUTER_ROWS] + br
    n = logits.shape[1]
    row = lax.broadcasted_iota(jnp.int32, (ROUTER_ROWS, n), 0)
    rowf = row.astype(F32)
    big = float(ROUTER_ROWS)
    coarse = (row >= COARSE_OFF) & (row < COARSE_OFF + N_GROUPS)
    cm = jnp.max(jnp.where(coarse, logits, -jnp.inf), axis=0, keepdims=True)
    g_sel = jnp.min(jnp.where(coarse & (logits == cm), rowf - COARSE_OFF, big), axis=0, keepdims=True)
    zc = jnp.sum(jnp.where(coarse, jnp.exp(logits - cm), 0.0), axis=0, keepdims=True)
    p_sel = 1.0 / zc
    per_group = N_EXPERTS // N_GROUPS
    fine = (row < N_EXPERTS) & ((row // per_group).astype(F32) == g_sel)
    f1 = jnp.max(jnp.where(fine, logits, -jnp.inf), axis=0, keepdims=True)
    i1 = jnp.min(jnp.where(fine & (logits == f1), rowf, big), axis=0, keepdims=True)
    rest = fine & (rowf != i1)
    f2 = jnp.max(jnp.where(rest, logits, -jnp.inf), axis=0, keepdims=True)
    i2 = jnp.min(jnp.where(rest & (logits == f2), rowf, big), axis=0, keepdims=True)
    e2 = jnp.exp(f2 - f1)
    t1 = 1.0 / (1.0 + e2)
    t2 = e2 * t1
    gates_t = jnp.where(rowf == i1, p_sel * t1, jnp.where(rowf == i2, p_sel * t2, 0.0))
    gates_t = jnp.where(row == GID_LANE, g_sel, gates_t)
    gates_t = jnp.concatenate([gates_t, jnp.zeros((ROUTER_LANES - ROUTER_ROWS, n), F32)], axis=0)
    return hi, jnp.transpose(gates_t)


def _mix_tail(x, attn, conv_out, gattn, gconv, wout, g2, wr, br):
    mixed = jnp.concatenate([_group_norm64(attn, gattn), _group_norm64(conv_out, gconv)],
                            axis=-1).astype(BF16)
    x1 = x + _dot(mixed, wout)
    xn2_bf, gates = _router(_rms(x1, g2), wr, br)
    return x1, xn2_bf, gates


def _layer_prompt_body(sinks_ref, x_ref, g1_ref, win_ref, convw_ref, gattn_ref, gconv_ref,
                       wout_ref, g2_ref, wr_ref, br_ref, wg32_ref, wu32_ref, wd32_ref,
                       x1_ref, xn2_ref, gates_ref, knew_ref, vnew_ref, unew_ref,
                       wg16_ref, wu16_ref, wd16_ref,
                       q_s, kcat_s, vcat_s, attn_s, ucarry_s):
    i = pl.program_id(1)
    wg16_ref[...] = wg32_ref[...].astype(BF16)
    wu16_ref[...] = wu32_ref[...].astype(BF16)
    wd16_ref[...] = wd32_ref[...].astype(BF16)
    t = T_LAYER // N_SUB
    last = N_SUB - 1

    @pl.when(i == 0)
    def _():
        kcat_s[last, t:t + WINDOW, :] = jnp.zeros((WINDOW, KV_DIM), BF16)
        vcat_s[last, t:t + WINDOW, :] = jnp.zeros((WINDOW, KV_DIM), BF16)
        ucarry_s[last] = jnp.zeros((8, CONV_DIM), F32)

    a_idx = lax.broadcasted_iota(jnp.int32, (WINDOW, 2 * WINDOW), 0)
    c_idx = lax.broadcasted_iota(jnp.int32, (WINDOW, 2 * WINDOW), 1)
    lane_lo = lax.broadcasted_iota(jnp.int32, (WINDOW, LANES), 1) < HEAD_DIM
    row = lax.broadcasted_iota(jnp.int32, (t, CONV_DIM), 0)
    n_pair = N_HEADS // 2
    cw = convw_ref[...]

    for h in range(N_SUB):
        src = (h - 1) % N_SUB
        rows = pl.ds(h * t, t)
        x = x_ref[0, rows, :]
        xn = _rms(x, g1_ref[...]).astype(BF16)

        kcat_s[h, 0:WINDOW, :] = kcat_s[src, t:t + WINDOW, :]
        vcat_s[h, 0:WINDOW, :] = vcat_s[src, t:t + WINDOW, :]
        q_s[h] = (_dot(xn, win_ref[:, 0:OFF_K]) * (HEAD_DIM ** -0.5)).astype(BF16)
        kv = _dot(xn, win_ref[:, OFF_K:OFF_B])
        k = kv[:, :KV_DIM]
        v = kv[:, KV_DIM:]
        kcat_s[h, WINDOW:, :] = k.astype(BF16)
        vcat_s[h, WINDOW:, :] = v.astype(BF16)
        if h == last:
            knew_ref[0] = jnp.transpose(k[t - WINDOW:, :])
            vnew_ref[0] = jnp.transpose(v[t - WINDOW:, :])

        for j in range(t // WINDOW):
            r0 = j * WINDOW
            kk = kcat_s[h, r0:r0 + 2 * WINDOW, :]
            vv = vcat_s[h, r0:r0 + 2 * WINDOW, :]
            c_min = jnp.where(i == 0, WINDOW, 0) if (h == 0 and j == 0) else 0
            valid = (c_idx >= jnp.maximum(a_idx, c_min)) & (c_idx <= a_idx + WINDOW)
            qcols = [q_s[h, r0:r0 + WINDOW, m * LANES:(m + 1) * LANES] for m in range(n_pair)]
            outs = []
            for half in range(2):
                keep = lane_lo if half == 0 else jnp.logical_not(lane_lo)
                qg = jnp.concatenate([jnp.where(keep, qc, jnp.zeros_like(qc)) for qc in qcols], axis=0)
                s = lax.dot_general(qg, kk, (((1,), (1,)), ((), ())), preferred_element_type=F32)
                es, dens = [], []
                for m in range(n_pair):
                    sm = jnp.where(valid, s[m * WINDOW:(m + 1) * WINDOW], NEG_INF)
                    sink = sinks_ref[m + half * n_pair]
                    mx = jnp.maximum(jnp.max(sm, axis=-1, keepdims=True), sink)
                    e = jnp.exp(sm - mx)
                    dens.append(jnp.sum(e, axis=-1, keepdims=True) + jnp.exp(sink - mx))
                    es.append(e.astype(BF16))
                o = _dot(jnp.concatenate(es, axis=0), vv)
                outs.append([o[m * WINDOW:(m + 1) * WINDOW] / dens[m] for m in range(n_pair)])
            for m in range(n_pair):
                attn_s[h, r0:r0 + WINDOW, m * LANES:(m + 1) * LANES] = jnp.where(lane_lo, outs[0][m], outs[1][m])

        gate_b = _dot(xn, win_ref[:, OFF_B:OFF_C])
        u = _dot(xn, win_ref[:, OFF_C:OFF_H]) * _dot(xn, win_ref[:, OFF_H:IN_PROJ_DIM])
        prev = ucarry_s[src]
        u1 = jnp.where(row == 0, prev[7:8, :], pltpu.roll(u, 1, axis=0))
        u2 = jnp.where(row == 0, prev[6:7, :], jnp.where(row == 1, prev[7:8, :], pltpu.roll(u, 2, axis=0)))
        z = cw[0:1, :] * u2 + cw[1:2, :] * u1 + cw[2:3, :] * u
        ucarry_s[h] = u[t - 8:, :]
        if h == last:
            unew_ref[0] = u[t - (CONV_WIDTH - 1):, :]

        x1, xn2_bf, gates = _mix_tail(x, attn_s[h], gate_b * z, gattn_ref[...], gconv_ref[...],
                                      wout_ref[...], g2_ref[...], wr_ref[...], br_ref[...])
        x1_ref[0, rows, :] = x1
        xn2_ref[0, rows, :] = xn2_bf
        gates_ref[0, rows, :] = gates


def _layer_prompt(x, sinks, g1, win, convw, gattn, gconv, wout, g2, wr, br, wg32, wu32, wd32):
    nb, seq, _ = x.shape
    t = T_LAYER
    ts = T_LAYER // N_SUB
    n_i = seq // t
    assert nb * n_i == N_EXPERTS, "one expert's weights are cast per grid step"
    const = lambda b, i, s: (0, 0)
    tile = lambda b, i, s: (b, i, 0)
    per_seq = lambda b, i, s: (b, 0, 0)
    per_step = lambda b, i, s: (b * n_i + i, 0, 0)
    resident = pl.Buffered(1)
    grid_spec = pltpu.PrefetchScalarGridSpec(
        num_scalar_prefetch=1,
        grid=(nb, seq // t),
        in_specs=[
            pl.BlockSpec((1, t, D_MODEL), tile),
            pl.BlockSpec((1, D_MODEL), const),
            pl.BlockSpec((D_MODEL, IN_PROJ_DIM), const, pipeline_mode=resident),
            pl.BlockSpec((CONV_WIDTH, CONV_DIM), const),
            pl.BlockSpec((1, ATTN_DIM), const),
            pl.BlockSpec((1, CONV_DIM), const),
            pl.BlockSpec((D_MODEL, D_MODEL), const, pipeline_mode=resident),
            pl.BlockSpec((1, D_MODEL), const),
            pl.BlockSpec((D_MODEL, 2 * ROUTER_LANES), const, pipeline_mode=resident),
            pl.BlockSpec((ROUTER_ROWS, 1), const),
            pl.BlockSpec((1, D_MODEL, D_EXPERT), per_step),
            pl.BlockSpec((1, D_MODEL, D_EXPERT), per_step),
            pl.BlockSpec((1, D_EXPERT, D_MODEL), per_step),
        ],
        out_specs=[
            pl.BlockSpec((1, t, D_MODEL), tile),
            pl.BlockSpec((1, t, D_MODEL), tile),
            pl.BlockSpec((1, t, ROUTER_LANES), tile),
            pl.BlockSpec((1, KV_DIM, WINDOW), per_seq),
            pl.BlockSpec((1, KV_DIM, WINDOW), per_seq),
            pl.BlockSpec((1, CONV_WIDTH - 1, CONV_DIM), per_seq),
            pl.BlockSpec((1, D_MODEL, D_EXPERT), per_step),
            pl.BlockSpec((1, D_MODEL, D_EXPERT), per_step),
            pl.BlockSpec((1, D_EXPERT, D_MODEL), per_step),
        ],
        scratch_shapes=[
            pltpu.VMEM((N_SUB, ts, ATTN_DIM), BF16),
            pltpu.VMEM((N_SUB, ts + WINDOW, KV_DIM), BF16),
            pltpu.VMEM((N_SUB, ts + WINDOW, KV_DIM), BF16),
            pltpu.VMEM((N_SUB, ts, ATTN_DIM), F32),
            pltpu.VMEM((N_SUB, 8, CONV_DIM), F32),
        ],
    )
    return pl.pallas_call(
        _layer_prompt_body,
        grid_spec=grid_spec,
        out_shape=[
            jax.ShapeDtypeStruct((nb, seq, D_MODEL), F32),
            jax.ShapeDtypeStruct((nb, seq, D_MODEL), BF16),
            jax.ShapeDtypeStruct((nb, seq, ROUTER_LANES), F32),
            jax.ShapeDtypeStruct((nb, KV_DIM, WINDOW), F32),
            jax.ShapeDtypeStruct((nb, KV_DIM, WINDOW), F32),
            jax.ShapeDtypeStruct((nb, CONV_WIDTH - 1, CONV_DIM), F32),
            jax.ShapeDtypeStruct((N_EXPERTS, D_MODEL, D_EXPERT), BF16),
            jax.ShapeDtypeStruct((N_EXPERTS, D_MODEL, D_EXPERT), BF16),
            jax.ShapeDtypeStruct((N_EXPERTS, D_EXPERT, D_MODEL), BF16),
        ],
        compiler_params=pltpu.CompilerParams(
            dimension_semantics=("arbitrary", "arbitrary"), vmem_limit_bytes=MOE_VMEM_LIMIT),
        name="layer_prompt",
    )(sinks, x, g1, win, convw, gattn, gconv, wout, g2, wr, br, wg32, wu32, wd32)


def _weight_copies(g, wg_hbm, wu_hbm, wd_hbm, wg_ref, wu_ref, wd_ref, sems):
    per_group = N_EXPERTS // N_GROUPS
    experts = pl.ds(g * per_group, per_group)
    return (pltpu.make_async_copy(wg_hbm.at[experts], wg_ref.at[experts], sems.at[g, 0]),
            pltpu.make_async_copy(wu_hbm.at[experts], wu_ref.at[experts], sems.at[g, 1]),
            pltpu.make_async_copy(wd_hbm.at[g], wd_ref.at[g], sems.at[g, 2]))


def _moe_body(cap, xc, x1_ref, xn2_ref, gates_ref, wg_hbm, wu_hbm, wd_hbm, gf_ref, y_ref,
              xs_s, gs_s, ys_s, wg_ref, wu_ref, wd_ref, sems):
    i = pl.program_id(0)
    copies = [_weight_copies(g, wg_hbm, wu_hbm, wd_hbm, wg_ref, wu_ref, wd_ref, sems)
              for g in range(N_GROUPS)]
    tile_refs = (x1_ref, xn2_ref, gates_ref, wg_ref, wu_ref, wd_ref, gf_ref, y_ref, xs_s, gs_s, ys_s)

    @pl.when(i == 0)
    def _():
        for group_copies in copies:
            for c in group_copies:
                c.start()
        _moe_tile(cap, xc, copies, *tile_refs)

    @pl.when(i > 0)
    def _():
        _moe_tile(cap, xc, None, *tile_refs)


def _moe_tile(cap, xc, pending, x1_ref, xn2_ref, gates_ref, wg_ref, wu_ref, wd_ref, gf_ref, y_ref,
              xs_s, gs_s, ys_s):
    t = x1_ref.shape[0] * TOKEN_CHUNK
    per_group = N_EXPERTS // N_GROUPS
    gates = gates_ref[...].reshape(t, ROUTER_LANES)
    gid_row = jnp.transpose(gates)[GID_LANE:GID_LANE + 1, :]
    grp = lax.broadcasted_iota(jnp.int32, (8, t), 0).astype(F32)
    onehot = jnp.where(grp == gid_row, 1.0, 0.0)

    r_idx = lax.broadcasted_iota(jnp.int32, (t, t), 0)
    c_idx = lax.broadcasted_iota(jnp.int32, (t, t), 1)
    upper = jnp.where(r_idx <= c_idx, 1.0, 0.0).astype(BF16)
    csum = _dot(onehot.astype(BF16), upper)
    rank_row = jnp.sum(onehot * (csum - 1.0), axis=0, keepdims=True)
    cnt = [csum[g, t - 1].astype(jnp.int32) for g in range(N_GROUPS)]
    off = [jnp.int32(0)]
    for g in range(1, N_GROUPS):
        off.append(off[-1] + cnt[g - 1])

    pos_row = rank_row
    for g in range(1, N_GROUPS):
        pos_row = pos_row + jnp.where(gid_row == float(g), off[g].astype(F32), 0.0)
    pos_col = jnp.transpose(jnp.broadcast_to(pos_row, (LANES, t)))
    pos_col = jnp.concatenate([pos_col] * (t // LANES), axis=1)
    perm = jnp.where(r_idx.astype(F32) == pos_row, 1.0, 0.0).astype(BF16)
    perm_t = jnp.where(c_idx.astype(F32) == pos_col, 1.0, 0.0).astype(BF16)

    p1 = gates.astype(BF16)
    p2 = (gates - p1.astype(F32)).astype(BF16)
    moved = _dot(perm, jnp.concatenate([xn2_ref[...].reshape(t, D_MODEL), p1, p2], axis=1))
    xs_s[0:t, :] = moved[:, :D_MODEL].astype(BF16)
    gs_s[0:t, :] = moved[:, D_MODEL:D_MODEL + ROUTER_LANES] + moved[:, D_MODEL + ROUTER_LANES:]
    pad = xs_s.shape[0] - t
    xs_s[t:, :] = jnp.zeros((pad, D_MODEL), BF16)
    gs_s[t:, :] = jnp.zeros((pad, ROUTER_LANES), F32)
    ys_s[...] = jnp.zeros(ys_s.shape, F32)

    def group_rows(g, r0, rows):
        xs = xs_s[pl.ds(r0, rows), :]
        gsc = gs_s[pl.ds(r0, rows), :]
        acts = []
        for k in range(per_group):
            e = g * per_group + k
            hg = _dot(xs, wg_ref[e])
            hu = _dot(xs, wu_ref[e])
            acts.append((hg / (1.0 + jnp.exp(-hg)) * hu * gsc[:, e:e + 1]).astype(BF16))
        ys_s[pl.ds(r0, rows), :] += _dot(jnp.concatenate(acts, axis=1), wd_ref[g])

    base = [(off[g] // ROW_ALIGN) * ROW_ALIGN for g in range(N_GROUPS)]
    for g in range(N_GROUPS):
        if pending is not None:
            for c in pending[g]:
                c.wait()
        group_rows(g, pl.multiple_of(base[g], ROW_ALIGN), cap)
    for g in range(N_GROUPS):
        n_extra = jnp.maximum(0, (off[g] + cnt[g] - (base[g] + cap) + xc - 1) // xc)

        def extra(k, carry, g=g):
            group_rows(g, pl.multiple_of(base[g] + cap + k * xc, ROW_ALIGN), xc)
            return carry

        lax.fori_loop(0, n_extra, extra, 0)

    moe = _dot(perm_t, ys_s[0:t, :].astype(BF16))
    y = _rms(x1_ref[...].reshape(t, D_MODEL) + moe, gf_ref[...])
    y_ref[...] = y.reshape(y_ref.shape)


def _moe(x1, xn2, gates, wg, wu, wd, gf, t, cap, xc):
    n = x1.shape[0]
    n_tiles = n // t
    pieces = t // TOKEN_CHUNK
    view = lambda a: a.reshape(pieces, n_tiles, TOKEN_CHUNK, a.shape[-1])
    tile = lambda i: (0, i, 0, 0)
    in_hbm = pl.BlockSpec(memory_space=pl.ANY)
    rows = t + cap + xc
    hidden = (N_EXPERTS // N_GROUPS) * D_EXPERT
    y = pl.pallas_call(
        functools.partial(_moe_body, cap, xc),
        grid=(n_tiles,),
        in_specs=[
            pl.BlockSpec((pieces, 1, TOKEN_CHUNK, D_MODEL), tile),
            pl.BlockSpec((pieces, 1, TOKEN_CHUNK, D_MODEL), tile),
            pl.BlockSpec((pieces, 1, TOKEN_CHUNK, ROUTER_LANES), tile),
            in_hbm,
            in_hbm,
            in_hbm,
            pl.BlockSpec((1, D_MODEL), lambda i: (0, 0)),
        ],
        out_specs=pl.BlockSpec((pieces, 1, TOKEN_CHUNK, D_MODEL), tile),
        out_shape=jax.ShapeDtypeStruct((pieces, n_tiles, TOKEN_CHUNK, D_MODEL), F32),
        scratch_shapes=[
            pltpu.VMEM((rows, D_MODEL), BF16),
            pltpu.VMEM((rows, ROUTER_LANES), F32),
            pltpu.VMEM((rows, D_MODEL), F32),
            pltpu.VMEM((N_EXPERTS, D_MODEL, D_EXPERT), BF16),
            pltpu.VMEM((N_EXPERTS, D_MODEL, D_EXPERT), BF16),
            pltpu.VMEM((N_GROUPS, hidden, D_MODEL), BF16),
            pltpu.SemaphoreType.DMA((N_GROUPS, 3)),
        ],
        compiler_params=pltpu.CompilerParams(
            dimension_semantics=("arbitrary",), vmem_limit_bytes=MOE_VMEM_LIMIT),
        name="moe",
    )(view(x1), view(xn2), view(gates), wg, wu, wd, gf)
    return y.reshape(n, D_MODEL)


def _proj_sample_body(x_ref, g1_ref, win_ref, proj_ref, kvt_ref):
    xn = _rms(x_ref[...], g1_ref[...]).astype(BF16)
    proj_ref[...] = _dot(xn, win_ref[...])
    kvt_ref[...] = lax.dot_general(win_ref[:, OFF_K:OFF_B], xn, (((0,), (1,)), ((), ())),
                                   preferred_element_type=F32)


def _proj_sample(x, g1, win):
    n = x.shape[0]
    return pl.pallas_call(
        _proj_sample_body,
        out_shape=[jax.ShapeDtypeStruct((n, IN_PROJ_DIM), F32),
                   jax.ShapeDtypeStruct((2 * KV_DIM, n), F32)],
        compiler_params=pltpu.CompilerParams(vmem_limit_bytes=VMEM_LIMIT),
        name="proj_sample",
    )(x, g1, win)


def _attn_sample_body(q8_ref, knew_ref, vnew_ref, kvt_ref, khist_ref, vhist_ref, sinks_ref,
                      out8_ref, kout_ref, vout_ref):
    chunk = q8_ref.shape[0]
    first = pl.program_id(0) * chunk
    q8 = q8_ref[...]
    kh = khist_ref[...]
    vh = vhist_ref[...]
    knew = knew_ref[...]
    vnew = vnew_ref[...]
    s = jnp.einsum('bhj,bjc->bhc', q8, kh.astype(BF16), preferred_element_type=F32)
    s_new = jnp.sum(q8.astype(F32) * knew, axis=-1, keepdims=True)
    sink = sinks_ref[...][None]
    mx = jnp.maximum(jnp.maximum(jnp.max(s, axis=-1, keepdims=True), s_new), sink)
    e = jnp.exp(s - mx)
    e_new = jnp.exp(s_new - mx)
    den = jnp.sum(e, axis=-1, keepdims=True) + e_new + jnp.exp(sink - mx)
    o = jnp.einsum('bhc,bjc->bhj', e.astype(BF16), vh.astype(BF16), preferred_element_type=F32)
    out8_ref[...] = (o + e_new * vnew) / den
    last = lax.broadcasted_iota(jnp.int32, (KV_DIM, WINDOW), 1) == WINDOW - 1
    knew_t = kvt_ref[0:KV_DIM, :]
    vnew_t = kvt_ref[KV_DIM:, :]
    for bb in range(chunk):
        shift = lax.rem(2 * WINDOW - 1 - (first + bb), WINDOW)
        kout_ref[bb] = jnp.where(last, pltpu.roll(knew_t, shift, axis=1), pltpu.roll(kh[bb], WINDOW - 1, axis=1))
        vout_ref[bb] = jnp.where(last, pltpu.roll(vnew_t, shift, axis=1), pltpu.roll(vh[bb], WINDOW - 1, axis=1))


def _attn_sample(q8, knew, vnew, kvt, khist, vhist, sinks_col):
    assert q8.shape[0] == WINDOW == LANES, "one lane per sequence in the channel-major projection"
    n = q8.shape[0]
    c = DEC_CHUNK
    blk3 = lambda i: (i, 0, 0)
    return pl.pallas_call(
        _attn_sample_body,
        grid=(n // c,),
        in_specs=[
            pl.BlockSpec((c, N_HEADS, LANES), blk3),
            pl.BlockSpec((c, 1, KV_DIM), blk3),
            pl.BlockSpec((c, 1, KV_DIM), blk3),
            pl.BlockSpec((2 * KV_DIM, n), lambda i: (0, 0)),
            pl.BlockSpec((c, KV_DIM, WINDOW), blk3),
            pl.BlockSpec((c, KV_DIM, WINDOW), blk3),
            pl.BlockSpec((N_HEADS, 1), lambda i: (0, 0)),
        ],
        out_specs=[
            pl.BlockSpec((c, N_HEADS, LANES), blk3),
            pl.BlockSpec((c, KV_DIM, WINDOW), blk3),
            pl.BlockSpec((c, KV_DIM, WINDOW), blk3),
        ],
        out_shape=[
            jax.ShapeDtypeStruct((n, N_HEADS, LANES), F32),
            jax.ShapeDtypeStruct((n, KV_DIM, WINDOW), F32),
            jax.ShapeDtypeStruct((n, KV_DIM, WINDOW), F32),
        ],
        compiler_params=pltpu.CompilerParams(
            dimension_semantics=("arbitrary",), vmem_limit_bytes=VMEM_LIMIT),
        name="attn_sample",
    )(q8, knew, vnew, kvt, khist, vhist, sinks_col)


def _tail_sample_body(x_ref, attn_ref, proj_ref, uprev2_ref, uprev1_ref, convw_ref, gattn_ref,
                      gconv_ref, wout_ref, g2_ref, wr_ref, br_ref,
                      x1_ref, xn2_ref, gates_ref, u_ref):
    gate_b = proj_ref[:, OFF_B:OFF_C]
    u = proj_ref[:, OFF_C:OFF_H] * proj_ref[:, OFF_H:IN_PROJ_DIM]
    cw = convw_ref[...]
    z = cw[0:1, :] * uprev2_ref[...] + cw[1:2, :] * uprev1_ref[...] + cw[2:3, :] * u
    u_ref[...] = u
    x1, xn2_bf, gates = _mix_tail(x_ref[...], attn_ref[...], gate_b * z, gattn_ref[...], gconv_ref[...],
                                  wout_ref[...], g2_ref[...], wr_ref[...], br_ref[...])
    x1_ref[...] = x1
    xn2_ref[...] = xn2_bf
    gates_ref[...] = gates


def _tail_sample(x, attn, proj, uprev2, uprev1, convw, gattn, gconv, wout, g2, wr, br):
    n = x.shape[0]
    return pl.pallas_call(
        _tail_sample_body,
        out_shape=[
            jax.ShapeDtypeStruct((n, D_MODEL), F32),
            jax.ShapeDtypeStruct((n, D_MODEL), BF16),
            jax.ShapeDtypeStruct((n, ROUTER_LANES), F32),
            jax.ShapeDtypeStruct((n, CONV_DIM), F32),
        ],
        compiler_params=pltpu.CompilerParams(vmem_limit_bytes=VMEM_LIMIT),
        name="tail_sample",
    )(x, attn, proj, uprev2, uprev1, convw, gattn, gconv, wout, g2, wr, br)


def _pair_heads(w, axis):
    shape = w.shape
    n_pair = N_HEADS // 2
    split = shape[:axis] + (2, n_pair, HEAD_DIM) + shape[axis + 1:]
    return jnp.swapaxes(w.reshape(split), axis, axis + 1).reshape(shape)


def kernel(x_prompt, x_sample, cache_k_win, cache_v_win, state_conv, norm1_g, w_in, conv_w, sinks,
           attn_out_g, conv_out_g, w_out, norm2_g, w_group, b_group, w_fine, b_fine, w_gate, w_up,
           w_down, final_g):
    nb, seq, _ = x_prompt.shape
    nd = x_sample.shape[0]

    w_in0 = w_in[0].astype(BF16)
    win = jnp.concatenate([_pair_heads(w_in0[:, :ATTN_DIM], 1), w_in0[:, ATTN_DIM:]], axis=1)
    w_out0 = w_out[0].astype(BF16)
    wout = jnp.concatenate([_pair_heads(w_out0[:ATTN_DIM], 0), w_out0[ATTN_DIM:]], axis=0)
    gattn = _pair_heads(attn_out_g[0], 0)[None]
    gconv = conv_out_g[0][None]
    g1 = norm1_g[0][None]
    g2 = norm2_g[0][None]
    gf = final_g[None]
    convw = conv_w[0]
    sinks0 = sinks[0]
    pad = ROUTER_LANES - N_EXPERTS - N_GROUPS
    wr32 = jnp.concatenate([w_fine[0], w_group[0], jnp.zeros((D_MODEL, pad), F32)], axis=1)
    wr_hi = wr32.astype(BF16)
    wr = jnp.concatenate([wr_hi, (wr32 - wr_hi.astype(F32)).astype(BF16)], axis=1)
    br = jnp.concatenate([b_fine[0], b_group[0], jnp.zeros((pad,), F32)])[:ROUTER_ROWS, None]

    x1p, xn2p, gatesp, kp, vp, up, wg, wu, wd = _layer_prompt(
        x_prompt, sinks0, g1, win, convw, gattn, gconv, wout, g2, wr, br, w_gate[0], w_up[0], w_down[0])
    wd = wd.reshape(N_GROUPS, (N_EXPERTS // N_GROUPS) * D_EXPERT, D_MODEL)
    n_p = nb * seq
    y_prompt = _moe(x1p.reshape(n_p, D_MODEL), xn2p.reshape(n_p, D_MODEL),
                    gatesp.reshape(n_p, ROUTER_LANES), wg, wu, wd, gf, T_MOE, MOE_CAP, MOE_XC).reshape(nb, seq, D_MODEL)

    xs = x_sample.reshape(nd, D_MODEL)
    proj, kvt = _proj_sample(xs, g1, win)
    lane = jnp.arange(LANES)
    qp = (proj[:, :ATTN_DIM] * (HEAD_DIM ** -0.5)).reshape(nd, N_HEADS // 2, LANES)
    q8 = jnp.concatenate([jnp.where(lane < HEAD_DIM, qp, 0.0), jnp.where(lane >= HEAD_DIM, qp, 0.0)],
                         axis=1).astype(BF16)
    knew = proj[:, OFF_K:OFF_V].reshape(nd, 1, KV_DIM)
    vnew = proj[:, OFF_V:OFF_B].reshape(nd, 1, KV_DIM)
    to_cm = lambda c: jnp.transpose(c[0], (0, 2, 3, 1)).reshape(nd, KV_DIM, WINDOW)
    from_cm = lambda c, n: jnp.transpose(c.reshape(n, KV_DIM // HEAD_DIM, HEAD_DIM, WINDOW), (0, 3, 1, 2))[None]
    out8, ks, vs = _attn_sample(q8, knew, vnew, kvt, to_cm(cache_k_win), to_cm(cache_v_win), sinks0[:, None])
    attn_s = jnp.where(lane < HEAD_DIM, out8[:, :N_HEADS // 2], out8[:, N_HEADS // 2:]).reshape(nd, ATTN_DIM)
    st = state_conv[0]
    x1s, xn2s, gatess, us = _tail_sample(xs, attn_s, proj, st[:, 0], st[:, 1], convw, gattn, gconv,
                                         wout, g2, wr, br)
    y_sample = _moe(x1s, xn2s, gatess, wg, wu, wd, gf, nd, MOE_CAP_DEC, MOE_XC_DEC).reshape(nd, 1, D_MODEL)

    return (y_prompt, y_sample,
            from_cm(kp, nb), from_cm(vp, nb),
            up[None],
            from_cm(ks, nd), from_cm(vs, nd),
            jnp.stack([st[:, 1], us], axis=1)[None])
```

```python
import functools

import jax
import jax.numpy as jnp
from jax import lax
from jax.experimental import pallas as pl
from jax.experimental.pallas import tpu as pltpu

D_MODEL = 1024
HEAD_DIM = 64
N_HEADS = 8
ATTN_DIM = 512
KV_DIM = 128
WINDOW = 128
CONV_DIM = 512
CONV_WIDTH = 3
OFF_K = 512
OFF_V = 640
OFF_B = 768
OFF_C = 1280
OFF_H = 1792
IN_PROJ_DIM = 2304
N_GROUPS = 4
N_EXPERTS = 16
D_EXPERT = 256
RMS_EPS = 1e-5
NEG_INF = -1e30

LANES = 128
ROW_ALIGN = 16
ROUTER_LANES = 128
ROUTER_ROWS = 32
COARSE_OFF = N_EXPERTS
T_LAYER = 1024
N_SUB = 2
GID_LANE = N_EXPERTS
T_MOE = 512
TOKEN_CHUNK = 16
MOE_CAP, MOE_XC = 160, 64
MOE_CAP_DEC, MOE_XC_DEC = 64, 32
DEC_CHUNK = 32
VMEM_LIMIT = 48 * 1024 * 1024
MOE_VMEM_LIMIT = 56 * 1024 * 1024

BF16 = jnp.bfloat16
F32 = jnp.float32


def _dot(a, b):
    return jnp.dot(a, b, preferred_element_type=F32)


def _rms(x, g):
    ms = jnp.mean(x * x, axis=-1, keepdims=True)
    return x * lax.rsqrt(ms + RMS_EPS) * g


def _group_norm64(y, g):
    rows, cols = y.shape
    lo = lax.broadcasted_iota(jnp.int32, (rows, LANES), 1) < HEAD_DIM
    outs = []
    for c in range(cols // LANES):
        blk = y[:, c * LANES:(c + 1) * LANES]
        sq = blk * blk
        s_lo = jnp.sum(jnp.where(lo, sq, 0.0), axis=-1, keepdims=True)
        s_hi = jnp.sum(jnp.where(lo, 0.0, sq), axis=-1, keepdims=True)
        ms = jnp.where(lo, s_lo, s_hi) * (1.0 / HEAD_DIM)
        outs.append(blk * lax.rsqrt(ms + RMS_EPS))
    return jnp.concatenate(outs, axis=-1) * g


def _router(xn2, wr, br):
    hi = xn2.astype(BF16)
    both = _dot(hi, wr)
    logits = jnp.transpose(both[:, :ROUTER_LANES] + both[:, ROUTER_LANES:])[:ROUTER_ROWS] + br
    n = logits.shape[1]
    row = lax.broadcasted_iota(jnp.int32, (ROUTER_ROWS, n), 0)
    rowf = row.astype(F32)
    big = float(ROUTER_ROWS)
    coarse = (row >= COARSE_OFF) & (row < COARSE_OFF + N_GROUPS)
    cm = jnp.max(jnp.where(coarse, logits, -jnp.inf), axis=0, keepdims=True)
    g_sel = jnp.min(jnp.where(coarse & (logits == cm), rowf - COARSE_OFF, big), axis=0, keepdims=True)
    zc = jnp.sum(jnp.where(coarse, jnp.exp(logits - cm), 0.0), axis=0, keepdims=True)
    p_sel = 1.0 / zc
    per_group = N_EXPERTS // N_GROUPS
    fine = (row < N_EXPERTS) & ((row // per_group).astype(F32) == g_sel)
    f1 = jnp.max(jnp.where(fine, logits, -jnp.inf), axis=0, keepdims=True)
    i1 = jnp.min(jnp.where(fine & (logits == f1), rowf, big), axis=0, keepdims=True)
    rest = fine & (rowf != i1)
    f2 = jnp.max(jnp.where(rest, logits, -jnp.inf), axis=0, keepdims=True)
    i2 = jnp.min(jnp.where(rest & (logits == f2), rowf, big), axis=0, keepdims=True)
    e2 = jnp.exp(f2 - f1)
    t1 = 1.0 / (1.0 + e2)
    t2 = e2 * t1
    gates_t = jnp.where(rowf == i1, p_sel * t1, jnp.where(rowf == i2, p_sel * t2, 0.0))
    gates_t = jnp.where(row == GID_LANE, g_sel, gates_t)
    gates_t = jnp.concatenate([gates_t, jnp.zeros((ROUTER_LANES - ROUTER_ROWS, n), F32)], axis=0)
    return hi, jnp.transpose(gates_t)


def _mix_tail(x, attn, conv_out, gattn, gconv, wout, g2, wr, br):
    mixed = jnp.concatenate([_group_norm64(attn, gattn), _group_norm64(conv_out, gconv)],
                            axis=-1).astype(BF16)
    x1 = x + _dot(mixed, wout)
    xn2_bf, gates = _router(_rms(x1, g2), wr, br)
    return x1, xn2_bf, gates


def _layer_prompt_body(sinks_ref, x_ref, g1_ref, win_ref, convw_ref, gattn_ref, gconv_ref,
                       wout_ref, g2_ref, wr_ref, br_ref, wg32_ref, wu32_ref, wd32_ref,
                       x1_ref, xn2_ref, gates_ref, knew_ref, vnew_ref, unew_ref,
                       wg16_ref, wu16_ref, wd16_ref,
                       q_s, kcat_s, vcat_s, attn_s, ucarry_s):
    i = pl.program_id(1)
    wg16_ref[...] = wg32_ref[...].astype(BF16)
    wu16_ref[...] = wu32_ref[...].astype(BF16)
    wd16_ref[...] = wd32_ref[...].astype(BF16)
    t = T_LAYER // N_SUB
    last = N_SUB - 1

    @pl.when(i == 0)
    def _():
        kcat_s[last, t:t + WINDOW, :] = jnp.zeros((WINDOW, KV_DIM), BF16)
        vcat_s[last, t:t + WINDOW, :] = jnp.zeros((WINDOW, KV_DIM), BF16)
        ucarry_s[last] = jnp.zeros((8, CONV_DIM), F32)

    a_idx = lax.broadcasted_iota(jnp.int32, (WINDOW, 2 * WINDOW), 0)
    c_idx = lax.broadcasted_iota(jnp.int32, (WINDOW, 2 * WINDOW), 1)
    lane_lo = lax.broadcasted_iota(jnp.int32, (WINDOW, LANES), 1) < HEAD_DIM
    row = lax.broadcasted_iota(jnp.int32, (t, CONV_DIM), 0)
    n_pair = N_HEADS // 2
    cw = convw_ref[...]

    for h in range(N_SUB):
        src = (h - 1) % N_SUB
        rows = pl.ds(h * t, t)
        x = x_ref[0, rows, :]
        xn = _rms(x, g1_ref[...]).astype(BF16)

        kcat_s[h, 0:WINDOW, :] = kcat_s[src, t:t + WINDOW, :]
        vcat_s[h, 0:WINDOW, :] = vcat_s[src, t:t + WINDOW, :]
        q_s[h] = (_dot(xn, win_ref[:, 0:OFF_K]) * (HEAD_DIM ** -0.5)).astype(BF16)
        kv = _dot(xn, win_ref[:, OFF_K:OFF_B])
        k = kv[:, :KV_DIM]
        v = kv[:, KV_DIM:]
        kcat_s[h, WINDOW:, :] = k.astype(BF16)
        vcat_s[h, WINDOW:, :] = v.astype(BF16)
        if h == last:
            knew_ref[0] = jnp.transpose(k[t - WINDOW:, :])
            vnew_ref[0] = jnp.transpose(v[t - WINDOW:, :])

        for j in range(t // WINDOW):
            r0 = j * WINDOW
            kk = kcat_s[h, r0:r0 + 2 * WINDOW, :]
            vv = vcat_s[h, r0:r0 + 2 * WINDOW, :]
            c_min = jnp.where(i == 0, WINDOW, 0) if (h == 0 and j == 0) else 0
            valid = (c_idx >= jnp.maximum(a_idx, c_min)) & (c_idx <= a_idx + WINDOW)
            qcols = [q_s[h, r0:r0 + WINDOW, m * LANES:(m + 1) * LANES] for m in range(n_pair)]
            outs = []
            for half in range(2):
                keep = lane_lo if half == 0 else jnp.logical_not(lane_lo)
                qg = jnp.concatenate([jnp.where(keep, qc, jnp.zeros_like(qc)) for qc in qcols], axis=0)
                s = lax.dot_general(qg, kk, (((1,), (1,)), ((), ())), preferred_element_type=F32)
                es, dens = [], []
                for m in range(n_pair):
                    sm = jnp.where(valid, s[m * WINDOW:(m + 1) * WINDOW], NEG_INF)
                    sink = sinks_ref[m + half * n_pair]
                    mx = jnp.maximum(jnp.max(sm, axis=-1, keepdims=True), sink)
                    e = jnp.exp(sm - mx)
                    dens.append(jnp.sum(e, axis=-1, keepdims=True) + jnp.exp(sink - mx))
                    es.append(e.astype(BF16))
                o = _dot(jnp.concatenate(es, axis=0), vv)
                outs.append([o[m * WINDOW:(m + 1) * WINDOW] / dens[m] for m in range(n_pair)])
            for m in range(n_pair):
                attn_s[h, r0:r0 + WINDOW, m * LANES:(m + 1) * LANES] = jnp.where(lane_lo, outs[0][m], outs[1][m])

        gate_b = _dot(xn, win_ref[:, OFF_B:OFF_C])
        u = _dot(xn, win_ref[:, OFF_C:OFF_H]) * _dot(xn, win_ref[:, OFF_H:IN_PROJ_DIM])
        prev = ucarry_s[src]
        u1 = jnp.where(row == 0, prev[7:8, :], pltpu.roll(u, 1, axis=0))
        u2 = jnp.where(row == 0, prev[6:7, :], jnp.where(row == 1, prev[7:8, :], pltpu.roll(u, 2, axis=0)))
        z = cw[0:1, :] * u2 + cw[1:2, :] * u1 + cw[2:3, :] * u
        ucarry_s[h] = u[t - 8:, :]
        if h == last:
            unew_ref[0] = u[t - (CONV_WIDTH - 1):, :]

        x1, xn2_bf, gates = _mix_tail(x, attn_s[h], gate_b * z, gattn_ref[...], gconv_ref[...],
                                      wout_ref[...], g2_ref[...], wr_ref[...], br_ref[...])
        x1_ref[0, rows, :] = x1
        xn2_ref[0, rows, :] = xn2_bf
        gates_ref[0, rows, :] = gates


def _layer_prompt(x, sinks, g1, win, convw, gattn, gconv, wout, g2, wr, br, wg32, wu32, wd32):
    nb, seq, _ = x.shape
    t = T_LAYER
    ts = T_LAYER // N_SUB
    n_i = seq // t
    assert nb * n_i == N_EXPERTS, "one expert's weights are cast per grid step"
    const = lambda b, i, s: (0, 0)
    tile = lambda b, i, s: (b, i, 0)
    per_seq = lambda b, i, s: (b, 0, 0)
    per_step = lambda b, i, s: (b * n_i + i, 0, 0)
    resident = pl.Buffered(1)
    grid_spec = pltpu.PrefetchScalarGridSpec(
        num_scalar_prefetch=1,
        grid=(nb, seq // t),
        in_specs=[
            pl.BlockSpec((1, t, D_MODEL), tile),
            pl.BlockSpec((1, D_MODEL), const),
            pl.BlockSpec((D_MODEL, IN_PROJ_DIM), const, pipeline_mode=resident),
            pl.BlockSpec((CONV_WIDTH, CONV_DIM), const),
            pl.BlockSpec((1, ATTN_DIM), const),
            pl.BlockSpec((1, CONV_DIM), const),
            pl.BlockSpec((D_MODEL, D_MODEL), const, pipeline_mode=resident),
            pl.BlockSpec((1, D_MODEL), const),
            pl.BlockSpec((D_MODEL, 2 * ROUTER_LANES), const, pipeline_mode=resident),
            pl.BlockSpec((ROUTER_ROWS, 1), const),
            pl.BlockSpec((1, D_MODEL, D_EXPERT), per_step),
            pl.BlockSpec((1, D_MODEL, D_EXPERT), per_step),
            pl.BlockSpec((1, D_EXPERT, D_MODEL), per_step),
        ],
        out_specs=[
            pl.BlockSpec((1, t, D_MODEL), tile),
            pl.BlockSpec((1, t, D_MODEL), tile),
            pl.BlockSpec((1, t, ROUTER_LANES), tile),
            pl.BlockSpec((1, KV_DIM, WINDOW), per_seq),
            pl.BlockSpec((1, KV_DIM, WINDOW), per_seq),
            pl.BlockSpec((1, CONV_WIDTH - 1, CONV_DIM), per_seq),
            pl.BlockSpec((1, D_MODEL, D_EXPERT), per_step),
            pl.BlockSpec((1, D_MODEL, D_EXPERT), per_step),
            pl.BlockSpec((1, D_EXPERT, D_MODEL), per_step),
        ],
        scratch_shapes=[
            pltpu.VMEM((N_SUB, ts, ATTN_DIM), BF16),
            pltpu.VMEM((N_SUB, ts + WINDOW, KV_DIM), BF16),
            pltpu.VMEM((N_SUB, ts + WINDOW, KV_DIM), BF16),
            pltpu.VMEM((N_SUB, ts, ATTN_DIM), F32),
            pltpu.VMEM((N_SUB, 8, CONV_DIM), F32),
        ],
    )
    return pl.pallas_call(
        _layer_prompt_body,
        grid_spec=grid_spec,
        out_shape=[
            jax.ShapeDtypeStruct((nb, seq, D_MODEL), F32),
            jax.ShapeDtypeStruct((nb, seq, D_MODEL), BF16),
            jax.ShapeDtypeStruct((nb, seq, ROUTER_LANES), F32),
            jax.ShapeDtypeStruct((nb, KV_DIM, WINDOW), F32),
            jax.ShapeDtypeStruct((nb, KV_DIM, WINDOW), F32),
            jax.ShapeDtypeStruct((nb, CONV_WIDTH - 1, CONV_DIM), F32),
            jax.ShapeDtypeStruct((N_EXPERTS, D_MODEL, D_EXPERT), BF16),
            jax.ShapeDtypeStruct((N_EXPERTS, D_MODEL, D_EXPERT), BF16),
            jax.ShapeDtypeStruct((N_EXPERTS, D_EXPERT, D_MODEL), BF16),
        ],
        compiler_params=pltpu.CompilerParams(
            dimension_semantics=("arbitrary", "arbitrary"), vmem_limit_bytes=MOE_VMEM_LIMIT,
            allow_input_fusion=[False, False, False, True, False, True, False, True, False, True,
                                False, False, False, False]),
        name="layer_prompt",
    )(sinks, x, g1, win, convw, gattn, gconv, wout, g2, wr, br, wg32, wu32, wd32)


def _moe_body(cap, xc, x1_ref, xn2_ref, gates_ref, wg_ref, wu_ref, wd_ref, gf_ref, y_ref,
              xs_s, gs_s, ys_s):
    t = x1_ref.shape[0] * TOKEN_CHUNK
    per_group = N_EXPERTS // N_GROUPS
    gates = gates_ref[...].reshape(t, ROUTER_LANES)
    gid_row = jnp.transpose(gates)[GID_LANE:GID_LANE + 1, :]
    grp = lax.broadcasted_iota(jnp.int32, (8, t), 0).astype(F32)
    onehot = jnp.where(grp == gid_row, 1.0, 0.0)

    r_idx = lax.broadcasted_iota(jnp.int32, (t, t), 0)
    c_idx = lax.broadcasted_iota(jnp.int32, (t, t), 1)
    upper = jnp.where(r_idx <= c_idx, 1.0, 0.0).astype(BF16)
    csum = _dot(onehot.astype(BF16), upper)
    rank_row = jnp.sum(onehot * (csum - 1.0), axis=0, keepdims=True)
    cnt = [csum[g, t - 1].astype(jnp.int32) for g in range(N_GROUPS)]
    off = [jnp.int32(0)]
    for g in range(1, N_GROUPS):
        off.append(off[-1] + cnt[g - 1])

    pos_row = rank_row
    for g in range(1, N_GROUPS):
        pos_row = pos_row + jnp.where(gid_row == float(g), off[g].astype(F32), 0.0)
    pos_col = jnp.transpose(jnp.broadcast_to(pos_row, (LANES, t)))
    pos_col = jnp.concatenate([pos_col] * (t // LANES), axis=1)
    perm = jnp.where(r_idx.astype(F32) == pos_row, 1.0, 0.0).astype(BF16)
    perm_t = jnp.where(c_idx.astype(F32) == pos_col, 1.0, 0.0).astype(BF16)

    p1 = gates.astype(BF16)
    p2 = (gates - p1.astype(F32)).astype(BF16)
    moved = _dot(perm, jnp.concatenate([xn2_ref[...].reshape(t, D_MODEL), p1, p2], axis=1))
    xs_s[0:t, :] = moved[:, :D_MODEL].astype(BF16)
    gs_s[0:t, :] = moved[:, D_MODEL:D_MODEL + ROUTER_LANES] + moved[:, D_MODEL + ROUTER_LANES:]
    pad = xs_s.shape[0] - t
    xs_s[t:, :] = jnp.zeros((pad, D_MODEL), BF16)
    gs_s[t:, :] = jnp.zeros((pad, ROUTER_LANES), F32)
    ys_s[...] = jnp.zeros(ys_s.shape, F32)

    def group_rows(g, r0, rows):
        xs = xs_s[pl.ds(r0, rows), :]
        gsc = gs_s[pl.ds(r0, rows), :]
        acts = []
        for k in range(per_group):
            e = g * per_group + k
            hg = _dot(xs, wg_ref[e])
            hu = _dot(xs, wu_ref[e])
            acts.append((hg / (1.0 + jnp.exp(-hg)) * hu * gsc[:, e:e + 1]).astype(BF16))
        ys_s[pl.ds(r0, rows), :] += _dot(jnp.concatenate(acts, axis=1), wd_ref[g])

    base = [(off[g] // ROW_ALIGN) * ROW_ALIGN for g in range(N_GROUPS)]
    for g in range(N_GROUPS):
        group_rows(g, pl.multiple_of(base[g], ROW_ALIGN), cap)
    for g in range(N_GROUPS):
        n_extra = jnp.maximum(0, (off[g] + cnt[g] - (base[g] + cap) + xc - 1) // xc)

        def extra(k, carry, g=g):
            group_rows(g, pl.multiple_of(base[g] + cap + k * xc, ROW_ALIGN), xc)
            return carry

        lax.fori_loop(0, n_extra, extra, 0)

    moe = _dot(perm_t, ys_s[0:t, :].astype(BF16))
    y = _rms(x1_ref[...].reshape(t, D_MODEL) + moe, gf_ref[...])
    y_ref[...] = y.reshape(y_ref.shape)


def _moe(x1, xn2, gates, wg, wu, wd, gf, t, cap, xc):
    n = x1.shape[0]
    n_tiles = n // t
    pieces = t // TOKEN_CHUNK
    view = lambda a: a.reshape(pieces, n_tiles, TOKEN_CHUNK, a.shape[-1])
    tile = lambda i: (0, i, 0, 0)
    whole = lambda i: (0, 0, 0)
    resident = pl.Buffered(1)
    rows = t + cap + xc
    hidden = (N_EXPERTS // N_GROUPS) * D_EXPERT
    y = pl.pallas_call(
        functools.partial(_moe_body, cap, xc),
        grid=(n_tiles,),
        in_specs=[
            pl.BlockSpec((pieces, 1, TOKEN_CHUNK, D_MODEL), tile),
            pl.BlockSpec((pieces, 1, TOKEN_CHUNK, D_MODEL), tile),
            pl.BlockSpec((pieces, 1, TOKEN_CHUNK, ROUTER_LANES), tile),
            pl.BlockSpec((N_EXPERTS, D_MODEL, D_EXPERT), whole, pipeline_mode=resident),
            pl.BlockSpec((N_EXPERTS, D_MODEL, D_EXPERT), whole, pipeline_mode=resident),
            pl.BlockSpec((N_GROUPS, hidden, D_MODEL), whole, pipeline_mode=resident),
            pl.BlockSpec((1, D_MODEL), lambda i: (0, 0)),
        ],
        out_specs=pl.BlockSpec((pieces, 1, TOKEN_CHUNK, D_MODEL), tile),
        out_shape=jax.ShapeDtypeStruct((pieces, n_tiles, TOKEN_CHUNK, D_MODEL), F32),
        scratch_shapes=[
            pltpu.VMEM((rows, D_MODEL), BF16),
            pltpu.VMEM((rows, ROUTER_LANES), F32),
            pltpu.VMEM((rows, D_MODEL), F32),
        ],
        compiler_params=pltpu.CompilerParams(
            dimension_semantics=("arbitrary",), vmem_limit_bytes=MOE_VMEM_LIMIT),
        name="moe",
    )(view(x1), view(xn2), view(gates), wg, wu, wd, gf)
    return y.reshape(n, D_MODEL)


def _proj_sample_body(x_ref, g1_ref, win_ref, proj_ref, kvt_ref):
    xn = _rms(x_ref[...], g1_ref[...]).astype(BF16)
    proj_ref[...] = _dot(xn, win_ref[...])
    kvt_ref[...] = lax.dot_general(win_ref[:, OFF_K:OFF_B], xn, (((0,), (1,)), ((), ())),
                                   preferred_element_type=F32)


def _proj_sample(x, g1, win):
    n = x.shape[0]
    return pl.pallas_call(
        _proj_sample_body,
        out_shape=[jax.ShapeDtypeStruct((n, IN_PROJ_DIM), F32),
                   jax.ShapeDtypeStruct((2 * KV_DIM, n), F32)],
        compiler_params=pltpu.CompilerParams(vmem_limit_bytes=VMEM_LIMIT, allow_input_fusion=[True] * 3),
        name="proj_sample",
    )(x, g1, win)


def _attn_sample_body(q8_ref, knew_ref, vnew_ref, kvt_ref, khist_ref, vhist_ref, sinks_ref,
                      out8_ref, kout_ref, vout_ref):
    chunk = q8_ref.shape[0]
    first = pl.program_id(0) * chunk
    q8 = q8_ref[...]
    kh = khist_ref[...]
    vh = vhist_ref[...]
    knew = knew_ref[...]
    vnew = vnew_ref[...]
    s = jnp.einsum('bhj,bjc->bhc', q8, kh.astype(BF16), preferred_element_type=F32)
    s_new = jnp.sum(q8.astype(F32) * knew, axis=-1, keepdims=True)
    sink = sinks_ref[...][None]
    mx = jnp.maximum(jnp.maximum(jnp.max(s, axis=-1, keepdims=True), s_new), sink)
    e = jnp.exp(s - mx)
    e_new = jnp.exp(s_new - mx)
    den = jnp.sum(e, axis=-1, keepdims=True) + e_new + jnp.exp(sink - mx)
    o = jnp.einsum('bhc,bjc->bhj', e.astype(BF16), vh.astype(BF16), preferred_element_type=F32)
    out8_ref[...] = (o + e_new * vnew) / den
    last = lax.broadcasted_iota(jnp.int32, (KV_DIM, WINDOW), 1) == WINDOW - 1
    knew_t = kvt_ref[0:KV_DIM, :]
    vnew_t = kvt_ref[KV_DIM:, :]
    for bb in range(chunk):
        shift = lax.rem(2 * WINDOW - 1 - (first + bb), WINDOW)
        kout_ref[bb] = jnp.where(last, pltpu.roll(knew_t, shift, axis=1), pltpu.roll(kh[bb], WINDOW - 1, axis=1))
        vout_ref[bb] = jnp.where(last, pltpu.roll(vnew_t, shift, axis=1), pltpu.roll(vh[bb], WINDOW - 1, axis=1))


def _attn_sample(q8, knew, vnew, kvt, khist, vhist, sinks_col):
    assert q8.shape[0] == WINDOW == LANES, "one lane per sequence in the channel-major projection"
    n = q8.shape[0]
    c = DEC_CHUNK
    blk3 = lambda i: (i, 0, 0)
    return pl.pallas_call(
        _attn_sample_body,
        grid=(n // c,),
        in_specs=[
            pl.BlockSpec((c, N_HEADS, LANES), blk3),
            pl.BlockSpec((c, 1, KV_DIM), blk3),
            pl.BlockSpec((c, 1, KV_DIM), blk3),
            pl.BlockSpec((2 * KV_DIM, n), lambda i: (0, 0)),
            pl.BlockSpec((c, KV_DIM, WINDOW), blk3),
            pl.BlockSpec((c, KV_DIM, WINDOW), blk3),
            pl.BlockSpec((N_HEADS, 1), lambda i: (0, 0)),
        ],
        out_specs=[
            pl.BlockSpec((c, N_HEADS, LANES), blk3),
            pl.BlockSpec((c, KV_DIM, WINDOW), blk3),
            pl.BlockSpec((c, KV_DIM, WINDOW), blk3),
        ],
        out_shape=[
            jax.ShapeDtypeStruct((n, N_HEADS, LANES), F32),
            jax.ShapeDtypeStruct((n, KV_DIM, WINDOW), F32),
            jax.ShapeDtypeStruct((n, KV_DIM, WINDOW), F32),
        ],
        compiler_params=pltpu.CompilerParams(
            dimension_semantics=("arbitrary",), vmem_limit_bytes=VMEM_LIMIT, allow_input_fusion=[True] * 7),
        name="attn_sample",
    )(q8, knew, vnew, kvt, khist, vhist, sinks_col)


def _tail_sample_body(x_ref, attn_ref, proj_ref, uprev2_ref, uprev1_ref, convw_ref, gattn_ref,
                      gconv_ref, wout_ref, g2_ref, wr_ref, br_ref,
                      x1_ref, xn2_ref, gates_ref, u_ref):
    gate_b = proj_ref[:, OFF_B:OFF_C]
    u = proj_ref[:, OFF_C:OFF_H] * proj_ref[:, OFF_H:IN_PROJ_DIM]
    cw = convw_ref[...]
    z = cw[0:1, :] * uprev2_ref[...] + cw[1:2, :] * uprev1_ref[...] + cw[2:3, :] * u
    u_ref[...] = u
    x1, xn2_bf, gates = _mix_tail(x_ref[...], attn_ref[...], gate_b * z, gattn_ref[...], gconv_ref[...],
                                  wout_ref[...], g2_ref[...], wr_ref[...], br_ref[...])
    x1_ref[...] = x1
    xn2_ref[...] = xn2_bf
    gates_ref[...] = gates


def _tail_sample(x, attn, proj, uprev2, uprev1, convw, gattn, gconv, wout, g2, wr, br):
    n = x.shape[0]
    return pl.pallas_call(
        _tail_sample_body,
        out_shape=[
            jax.ShapeDtypeStruct((n, D_MODEL), F32),
            jax.ShapeDtypeStruct((n, D_MODEL), BF16),
            jax.ShapeDtypeStruct((n, ROUTER_LANES), F32),
            jax.ShapeDtypeStruct((n, CONV_DIM), F32),
        ],
        compiler_params=pltpu.CompilerParams(vmem_limit_bytes=VMEM_LIMIT, allow_input_fusion=[True] * 12),
        name="tail_sample",
    )(x, attn, proj, uprev2, uprev1, convw, gattn, gconv, wout, g2, wr, br)


def _pair_heads(w, axis):
    shape = w.shape
    n_pair = N_HEADS // 2
    split = shape[:axis] + (2, n_pair, HEAD_DIM) + shape[axis + 1:]
    return jnp.swapaxes(w.reshape(split), axis, axis + 1).reshape(shape)


def kernel(x_prompt, x_sample, cache_k_win, cache_v_win, state_conv, norm1_g, w_in, conv_w, sinks,
           attn_out_g, conv_out_g, w_out, norm2_g, w_group, b_group, w_fine, b_fine, w_gate, w_up,
           w_down, final_g):
    nb, seq, _ = x_prompt.shape
    nd = x_sample.shape[0]

    w_in0 = w_in[0].astype(BF16)
    win = jnp.concatenate([_pair_heads(w_in0[:, :ATTN_DIM], 1), w_in0[:, ATTN_DIM:]], axis=1)
    w_out0 = w_out[0].astype(BF16)
    wout = jnp.concatenate([_pair_heads(w_out0[:ATTN_DIM], 0), w_out0[ATTN_DIM:]], axis=0)
    gattn = _pair_heads(attn_out_g[0], 0)[None]
    gconv = conv_out_g[0][None]
    g1 = norm1_g[0][None]
    g2 = norm2_g[0][None]
    gf = final_g[None]
    convw = conv_w[0]
    sinks0 = sinks[0]
    pad = ROUTER_LANES - N_EXPERTS - N_GROUPS
    wr32 = jnp.concatenate([w_fine[0], w_group[0], jnp.zeros((D_MODEL, pad), F32)], axis=1)
    wr_hi = wr32.astype(BF16)
    wr = jnp.concatenate([wr_hi, (wr32 - wr_hi.astype(F32)).astype(BF16)], axis=1)
    br = jnp.concatenate([b_fine[0], b_group[0], jnp.zeros((pad,), F32)])[:ROUTER_ROWS, None]

    x1p, xn2p, gatesp, kp, vp, up, wg, wu, wd = _layer_prompt(
        x_prompt, sinks0, g1, win, convw, gattn, gconv, wout, g2, wr, br, w_gate[0], w_up[0], w_down[0])
    wd = wd.reshape(N_GROUPS, (N_EXPERTS // N_GROUPS) * D_EXPERT, D_MODEL)
    n_p = nb * seq
    y_prompt = _moe(x1p.reshape(n_p, D_MODEL), xn2p.reshape(n_p, D_MODEL),
                    gatesp.reshape(n_p, ROUTER_LANES), wg, wu, wd, gf, T_MOE, MOE_CAP, MOE_XC).reshape(nb, seq, D_MODEL)

    xs = x_sample.reshape(nd, D_MODEL)
    proj, kvt = _proj_sample(xs, g1, win)
    lane = jnp.arange(LANES)
    qp = (proj[:, :ATTN_DIM] * (HEAD_DIM ** -0.5)).reshape(nd, N_HEADS // 2, LANES)
    q8 = jnp.concatenate([jnp.where(lane < HEAD_DIM, qp, 0.0), jnp.where(lane >= HEAD_DIM, qp, 0.0)],
                         axis=1).astype(BF16)
    knew = proj[:, OFF_K:OFF_V].reshape(nd, 1, KV_DIM)
    vnew = proj[:, OFF_V:OFF_B].reshape(nd, 1, KV_DIM)
    to_cm = lambda c: jnp.transpose(c[0], (0, 2, 3, 1)).reshape(nd, KV_DIM, WINDOW)
    from_cm = lambda c, n: jnp.transpose(c.reshape(n, KV_DIM // HEAD_DIM, HEAD_DIM, WINDOW), (0, 3, 1, 2))[None]
    out8, ks, vs = _attn_sample(q8, knew, vnew, kvt, to_cm(cache_k_win), to_cm(cache_v_win), sinks0[:, None])
    attn_s = jnp.where(lane < HEAD_DIM, out8[:, :N_HEADS // 2], out8[:, N_HEADS // 2:]).reshape(nd, ATTN_DIM)
    st = state_conv[0]
    x1s, xn2s, gatess, us = _tail_sample(xs, attn_s, proj, st[:, 0], st[:, 1], convw, gattn, gconv,
                                         wout, g2, wr, br)
    y_sample = _moe(x1s, xn2s, gatess, wg, wu, wd, gf, nd, MOE_CAP_DEC, MOE_XC_DEC).reshape(nd, 1, D_MODEL)

    return (y_prompt, y_sample,
            from_cm(kp, nb), from_cm(vp, nb),
            up[None],
            from_cm(ks, nd), from_cm(vs, nd),
            jnp.stack([st[:, 1], us], axis=1)[None])
```

```python
import functools

import jax
import jax.numpy as jnp
from jax import lax
from jax.experimental import pallas as pl
from jax.experimental.pallas import tpu as pltpu

D_MODEL = 1024
HEAD_DIM = 64
N_HEADS = 8
ATTN_DIM = 512
KV_DIM = 128
WINDOW = 128
CONV_DIM = 512
CONV_WIDTH = 3
OFF_K = 512
OFF_V = 640
OFF_B = 768
OFF_C = 1280
OFF_H = 1792
IN_PROJ_DIM = 2304
N_GROUPS = 4
N_EXPERTS = 16
D_EXPERT = 256
RMS_EPS = 1e-5
NEG_INF = -1e30

LANES = 128
ROW_ALIGN = 16
ROUTER_LANES = 128
ROUTER_ROWS = 32
COARSE_OFF = N_EXPERTS
T_LAYER = 1024
N_SUB = 2
GID_LANE = N_EXPERTS
T_MOE = 512
TOKEN_CHUNK = 16
MOE_CAP, MOE_XC = 160, 64
MOE_CAP_DEC, MOE_XC_DEC = 64, 32
DEC_CHUNK = 32
VMEM_LIMIT = 48 * 1024 * 1024
MOE_VMEM_LIMIT = 56 * 1024 * 1024

BF16 = jnp.bfloat16
F32 = jnp.float32


def _dot(a, b):
    return jnp.dot(a, b, preferred_element_type=F32)


def _rms(x, g):
    ms = jnp.mean(x * x, axis=-1, keepdims=True)
    return x * lax.rsqrt(ms + RMS_EPS) * g


def _group_norm64(y, g):
    rows, cols = y.shape
    lo = lax.broadcasted_iota(jnp.int32, (rows, LANES), 1) < HEAD_DIM
    outs = []
    for c in range(cols // LANES):
        blk = y[:, c * LANES:(c + 1) * LANES]
        sq = blk * blk
        s_lo = jnp.sum(jnp.where(lo, sq, 0.0), axis=-1, keepdims=True)
        s_hi = jnp.sum(jnp.where(lo, 0.0, sq), axis=-1, keepdims=True)
        ms = jnp.where(lo, s_lo, s_hi) * (1.0 / HEAD_DIM)
        outs.append(blk * lax.rsqrt(ms + RMS_EPS))
    return jnp.concatenate(outs, axis=-1) * g


def _router(xn2, wr, br):
    hi = xn2.astype(BF16)
    both = _dot(hi, wr)
    logits = jnp.transpose(both[:, :ROUTER_LANES] + both[:, ROUTER_LANES:])[:ROUTER_ROWS] + br
    n = logits.shape[1]
    row = lax.broadcasted_iota(jnp.int32, (ROUTER_ROWS, n), 0)
    rowf = row.astype(F32)
    big = float(ROUTER_ROWS)
    coarse = (row >= COARSE_OFF) & (row < COARSE_OFF + N_GROUPS)
    cm = jnp.max(jnp.where(coarse, logits, -jnp.inf), axis=0, keepdims=True)
    g_sel = jnp.min(jnp.where(coarse & (logits == cm), rowf - COARSE_OFF, big), axis=0, keepdims=True)
    zc = jnp.sum(jnp.where(coarse, jnp.exp(logits - cm), 0.0), axis=0, keepdims=True)
    p_sel = 1.0 / zc
    per_group = N_EXPERTS // N_GROUPS
    fine = (row < N_EXPERTS) & ((row // per_group).astype(F32) == g_sel)
    f1 = jnp.max(jnp.where(fine, logits, -jnp.inf), axis=0, keepdims=True)
    i1 = jnp.min(jnp.where(fine & (logits == f1), rowf, big), axis=0, keepdims=True)
    rest = fine & (rowf != i1)
    f2 = jnp.max(jnp.where(rest, logits, -jnp.inf), axis=0, keepdims=True)
    i2 = jnp.min(jnp.where(rest & (logits == f2), rowf, big), axis=0, keepdims=True)
    e2 = jnp.exp(f2 - f1)
    t1 = 1.0 / (1.0 + e2)
    t2 = e2 * t1
    gates_t = jnp.where(rowf == i1, p_sel * t1, jnp.where(rowf == i2, p_sel * t2, 0.0))
    gates_t = jnp.where(row == GID_LANE, g_sel, gates_t)
    gates_t = jnp.concatenate([gates_t, jnp.zeros((ROUTER_LANES - ROUTER_ROWS, n), F32)], axis=0)
    return hi, jnp.transpose(gates_t)


def _mix_tail(x, attn, conv_out, gattn, gconv, wout, g2, wr, br):
    mixed = jnp.concatenate([_group_norm64(attn, gattn), _group_norm64(conv_out, gconv)],
                            axis=-1).astype(BF16)
    x1 = x + _dot(mixed, wout)
    xn2_bf, gates = _router(_rms(x1, g2), wr, br)
    return x1, xn2_bf, gates


def _layer_prompt_body(sinks_ref, x_ref, g1_ref, win_ref, convw_ref, gattn_ref, gconv_ref,
                       wout_ref, g2_ref, wr_ref, br_ref, wg32_ref, wu32_ref, wd32_ref,
                       x1_ref, xn2_ref, gates_ref, knew_ref, vnew_ref, unew_ref,
                       wg16_ref, wu16_ref, wd16_ref,
                       q_s, kcat_s, vcat_s, attn_s, ucarry_s):
    i = pl.program_id(1)
    wg16_ref[...] = wg32_ref[...].astype(BF16)
    wu16_ref[...] = wu32_ref[...].astype(BF16)
    wd16_ref[...] = wd32_ref[...].astype(BF16)
    t = T_LAYER // N_SUB
    last = N_SUB - 1

    @pl.when(i == 0)
    def _():
        kcat_s[last, t:t + WINDOW, :] = jnp.zeros((WINDOW, KV_DIM), BF16)
        vcat_s[last, t:t + WINDOW, :] = jnp.zeros((WINDOW, KV_DIM), BF16)
        ucarry_s[last] = jnp.zeros((8, CONV_DIM), F32)

    a_idx = lax.broadcasted_iota(jnp.int32, (WINDOW, 2 * WINDOW), 0)
    c_idx = lax.broadcasted_iota(jnp.int32, (WINDOW, 2 * WINDOW), 1)
    lane_lo = lax.broadcasted_iota(jnp.int32, (WINDOW, LANES), 1) < HEAD_DIM
    row = lax.broadcasted_iota(jnp.int32, (t, CONV_DIM), 0)
    n_pair = N_HEADS // 2
    cw = convw_ref[...]

    for h in range(N_SUB):
        src = (h - 1) % N_SUB
        rows = pl.ds(h * t, t)
        x = x_ref[0, rows, :]
        xn = _rms(x, g1_ref[...]).astype(BF16)

        kcat_s[h, 0:WINDOW, :] = kcat_s[src, t:t + WINDOW, :]
        vcat_s[h, 0:WINDOW, :] = vcat_s[src, t:t + WINDOW, :]
        q_s[h] = (_dot(xn, win_ref[:, 0:OFF_K]) * (HEAD_DIM ** -0.5)).astype(BF16)
        kv = _dot(xn, win_ref[:, OFF_K:OFF_B])
        k = kv[:, :KV_DIM]
        v = kv[:, KV_DIM:]
        kcat_s[h, WINDOW:, :] = k.astype(BF16)
        vcat_s[h, WINDOW:, :] = v.astype(BF16)
        if h == last:
            knew_ref[0] = jnp.transpose(k[t - WINDOW:, :])
            vnew_ref[0] = jnp.transpose(v[t - WINDOW:, :])

        for j in range(t // WINDOW):
            r0 = j * WINDOW
            kk = kcat_s[h, r0:r0 + 2 * WINDOW, :]
            vv = vcat_s[h, r0:r0 + 2 * WINDOW, :]
            c_min = jnp.where(i == 0, WINDOW, 0) if (h == 0 and j == 0) else 0
            valid = (c_idx >= jnp.maximum(a_idx, c_min)) & (c_idx <= a_idx + WINDOW)
            qcols = [q_s[h, r0:r0 + WINDOW, m * LANES:(m + 1) * LANES] for m in range(n_pair)]
            outs = []
            for half in range(2):
                keep = lane_lo if half == 0 else jnp.logical_not(lane_lo)
                qg = jnp.concatenate([jnp.where(keep, qc, jnp.zeros_like(qc)) for qc in qcols], axis=0)
                s = lax.dot_general(qg, kk, (((1,), (1,)), ((), ())), preferred_element_type=F32)
                es, dens = [], []
                for m in range(n_pair):
                    sm = jnp.where(valid, s[m * WINDOW:(m + 1) * WINDOW], NEG_INF)
                    sink = sinks_ref[m + half * n_pair]
                    mx = jnp.maximum(jnp.max(sm, axis=-1, keepdims=True), sink)
                    e = jnp.exp(sm - mx)
                    dens.append(jnp.sum(e, axis=-1, keepdims=True) + jnp.exp(sink - mx))
                    es.append(e.astype(BF16))
                o = _dot(jnp.concatenate(es, axis=0), vv)
                outs.append([o[m * WINDOW:(m + 1) * WINDOW] / dens[m] for m in range(n_pair)])
            for m in range(n_pair):
                attn_s[h, r0:r0 + WINDOW, m * LANES:(m + 1) * LANES] = jnp.where(lane_lo, outs[0][m], outs[1][m])

        gate_b = _dot(xn, win_ref[:, OFF_B:OFF_C])
        u = _dot(xn, win_ref[:, OFF_C:OFF_H]) * _dot(xn, win_ref[:, OFF_H:IN_PROJ_DIM])
        prev = ucarry_s[src]
        u1 = jnp.where(row == 0, prev[7:8, :], pltpu.roll(u, 1, axis=0))
        u2 = jnp.where(row == 0, prev[6:7, :], jnp.where(row == 1, prev[7:8, :], pltpu.roll(u, 2, axis=0)))
        z = cw[0:1, :] * u2 + cw[1:2, :] * u1 + cw[2:3, :] * u
        ucarry_s[h] = u[t - 8:, :]
        if h == last:
            unew_ref[0] = u[t - (CONV_WIDTH - 1):, :]

        x1, xn2_bf, gates = _mix_tail(x, attn_s[h], gate_b * z, gattn_ref[...], gconv_ref[...],
                                      wout_ref[...], g2_ref[...], wr_ref[...], br_ref[...])
        x1_ref[0, rows, :] = x1
        xn2_ref[0, rows, :] = xn2_bf
        gates_ref[0, rows, :] = gates


def _layer_prompt(x, sinks, g1, win, convw, gattn, gconv, wout, g2, wr, br, wg32, wu32, wd32):
    nb, seq, _ = x.shape
    t = T_LAYER
    ts = T_LAYER // N_SUB
    n_i = seq // t
    assert nb * n_i == N_EXPERTS, "one expert's weights are cast per grid step"
    const = lambda b, i, s: (0, 0)
    tile = lambda b, i, s: (b, i, 0)
    per_seq = lambda b, i, s: (b, 0, 0)
    per_step = lambda b, i, s: (b * n_i + i, 0, 0)
    resident = pl.Buffered(1)
    grid_spec = pltpu.PrefetchScalarGridSpec(
        num_scalar_prefetch=1,
        grid=(nb, seq // t),
        in_specs=[
            pl.BlockSpec((1, t, D_MODEL), tile),
            pl.BlockSpec((1, D_MODEL), const),
            pl.BlockSpec((D_MODEL, IN_PROJ_DIM), const, pipeline_mode=resident),
            pl.BlockSpec((CONV_WIDTH, CONV_DIM), const),
            pl.BlockSpec((1, ATTN_DIM), const),
            pl.BlockSpec((1, CONV_DIM), const),
            pl.BlockSpec((D_MODEL, D_MODEL), const, pipeline_mode=resident),
            pl.BlockSpec((1, D_MODEL), const),
            pl.BlockSpec((D_MODEL, 2 * ROUTER_LANES), const, pipeline_mode=resident),
            pl.BlockSpec((ROUTER_ROWS, 1), const),
            pl.BlockSpec((1, D_MODEL, D_EXPERT), per_step),
            pl.BlockSpec((1, D_MODEL, D_EXPERT), per_step),
            pl.BlockSpec((1, D_EXPERT, D_MODEL), per_step),
        ],
        out_specs=[
            pl.BlockSpec((1, t, D_MODEL), tile),
            pl.BlockSpec((1, t, D_MODEL), tile),
            pl.BlockSpec((1, t, ROUTER_LANES), tile),
            pl.BlockSpec((1, KV_DIM, WINDOW), per_seq),
            pl.BlockSpec((1, KV_DIM, WINDOW), per_seq),
            pl.BlockSpec((1, CONV_WIDTH - 1, CONV_DIM), per_seq),
            pl.BlockSpec((1, D_MODEL, D_EXPERT), per_step),
            pl.BlockSpec((1, D_MODEL, D_EXPERT), per_step),
            pl.BlockSpec((1, D_EXPERT, D_MODEL), per_step),
        ],
        scratch_shapes=[
            pltpu.VMEM((N_SUB, ts, ATTN_DIM), BF16),
            pltpu.VMEM((N_SUB, ts + WINDOW, KV_DIM), BF16),
            pltpu.VMEM((N_SUB, ts + WINDOW, KV_DIM), BF16),
            pltpu.VMEM((N_SUB, ts, ATTN_DIM), F32),
            pltpu.VMEM((N_SUB, 8, CONV_DIM), F32),
        ],
    )
    return pl.pallas_call(
        _layer_prompt_body,
        grid_spec=grid_spec,
        out_shape=[
            jax.ShapeDtypeStruct((nb, seq, D_MODEL), F32),
            jax.ShapeDtypeStruct((nb, seq, D_MODEL), BF16),
            jax.ShapeDtypeStruct((nb, seq, ROUTER_LANES), F32),
            jax.ShapeDtypeStruct((nb, KV_DIM, WINDOW), F32),
            jax.ShapeDtypeStruct((nb, KV_DIM, WINDOW), F32),
            jax.ShapeDtypeStruct((nb, CONV_WIDTH - 1, CONV_DIM), F32),
            jax.ShapeDtypeStruct((N_EXPERTS, D_MODEL, D_EXPERT), BF16),
            jax.ShapeDtypeStruct((N_EXPERTS, D_MODEL, D_EXPERT), BF16),
            jax.ShapeDtypeStruct((N_EXPERTS, D_EXPERT, D_MODEL), BF16),
        ],
        compiler_params=pltpu.CompilerParams(
            dimension_semantics=("arbitrary", "arbitrary"), vmem_limit_bytes=MOE_VMEM_LIMIT),
        name="layer_prompt",
    )(sinks, x, g1, win, convw, gattn, gconv, wout, g2, wr, br, wg32, wu32, wd32)


def _moe_body(cap, xc, x1_ref, xn2_ref, gates_ref, wg_ref, wu_ref, wd_ref, gf_ref, y_ref,
              xs_s, gs_s, ys_s):
    t = x1_ref.shape[0] * TOKEN_CHUNK
    per_group = N_EXPERTS // N_GROUPS
    gates = gates_ref[...].reshape(t, ROUTER_LANES)
    gid_row = jnp.transpose(gates)[GID_LANE:GID_LANE + 1, :]
    grp = lax.broadcasted_iota(jnp.int32, (8, t), 0).astype(F32)
    onehot = jnp.where(grp == gid_row, 1.0, 0.0)

    r_idx = lax.broadcasted_iota(jnp.int32, (t, t), 0)
    c_idx = lax.broadcasted_iota(jnp.int32, (t, t), 1)
    upper = jnp.where(r_idx <= c_idx, 1.0, 0.0).astype(BF16)
    csum = _dot(onehot.astype(BF16), upper)
    rank_row = jnp.sum(onehot * (csum - 1.0), axis=0, keepdims=True)
    cnt = [csum[g, t - 1].astype(jnp.int32) for g in range(N_GROUPS)]
    off = [jnp.int32(0)]
    for g in range(1, N_GROUPS):
        off.append(off[-1] + cnt[g - 1])

    pos_row = rank_row
    for g in range(1, N_GROUPS):
        pos_row = pos_row + jnp.where(gid_row == float(g), off[g].astype(F32), 0.0)
    pos_col = jnp.transpose(jnp.broadcast_to(pos_row, (LANES, t)))
    pos_col = jnp.concatenate([pos_col] * (t // LANES), axis=1)
    perm = jnp.where(r_idx.astype(F32) == pos_row, 1.0, 0.0).astype(BF16)
    perm_t = jnp.where(c_idx.astype(F32) == pos_col, 1.0, 0.0).astype(BF16)

    p1 = gates.astype(BF16)
    p2 = (gates - p1.astype(F32)).astype(BF16)
    moved = _dot(perm, jnp.concatenate([xn2_ref[...].reshape(t, D_MODEL), p1, p2], axis=1))
    xs_s[0:t, :] = moved[:, :D_MODEL].astype(BF16)
    gs_s[0:t, :] = moved[:, D_MODEL:D_MODEL + ROUTER_LANES] + moved[:, D_MODEL + ROUTER_LANES:]
    pad = xs_s.shape[0] - t
    xs_s[t:, :] = jnp.zeros((pad, D_MODEL), BF16)
    gs_s[t:, :] = jnp.zeros((pad, ROUTER_LANES), F32)
    ys_s[...] = jnp.zeros(ys_s.shape, BF16)

    def group_rows(g, r0, rows):
        xs = xs_s[pl.ds(r0, rows), :]
        gsc = gs_s[pl.ds(r0, rows), :]
        acts = []
        for k in range(per_group):
            e = g * per_group + k
            hg = _dot(xs, wg_ref[e])
            hu = _dot(xs, wu_ref[e])
            acts.append((hg / (1.0 + jnp.exp(-hg)) * hu * gsc[:, e:e + 1]).astype(BF16))
        ys_s[pl.ds(r0, rows), :] += _dot(jnp.concatenate(acts, axis=1), wd_ref[g]).astype(BF16)

    base = [(off[g] // ROW_ALIGN) * ROW_ALIGN for g in range(N_GROUPS)]
    for g in range(N_GROUPS):
        group_rows(g, pl.multiple_of(base[g], ROW_ALIGN), cap)
    for g in range(N_GROUPS):
        n_extra = jnp.maximum(0, (off[g] + cnt[g] - (base[g] + cap) + xc - 1) // xc)

        def extra(k, carry, g=g):
            group_rows(g, pl.multiple_of(base[g] + cap + k * xc, ROW_ALIGN), xc)
            return carry

        lax.fori_loop(0, n_extra, extra, 0)

    moe = _dot(perm_t, ys_s[0:t, :])
    y = _rms(x1_ref[...].reshape(t, D_MODEL) + moe, gf_ref[...])
    y_ref[...] = y.reshape(y_ref.shape)


def _moe(x1, xn2, gates, wg, wu, wd, gf, t, cap, xc):
    n = x1.shape[0]
    n_tiles = n // t
    pieces = t // TOKEN_CHUNK
    view = lambda a: a.reshape(pieces, n_tiles, TOKEN_CHUNK, a.shape[-1])
    tile = lambda i: (0, i, 0, 0)
    whole = lambda i: (0, 0, 0)
    resident = pl.Buffered(1)
    rows = t + cap + xc
    hidden = (N_EXPERTS // N_GROUPS) * D_EXPERT
    y = pl.pallas_call(
        functools.partial(_moe_body, cap, xc),
        grid=(n_tiles,),
        in_specs=[
            pl.BlockSpec((pieces, 1, TOKEN_CHUNK, D_MODEL), tile),
            pl.BlockSpec((pieces, 1, TOKEN_CHUNK, D_MODEL), tile),
            pl.BlockSpec((pieces, 1, TOKEN_CHUNK, ROUTER_LANES), tile),
            pl.BlockSpec((N_EXPERTS, D_MODEL, D_EXPERT), whole, pipeline_mode=resident),
            pl.BlockSpec((N_EXPERTS, D_MODEL, D_EXPERT), whole, pipeline_mode=resident),
            pl.BlockSpec((N_GROUPS, hidden, D_MODEL), whole, pipeline_mode=resident),
            pl.BlockSpec((1, D_MODEL), lambda i: (0, 0)),
        ],
        out_specs=pl.BlockSpec((pieces, 1, TOKEN_CHUNK, D_MODEL), tile),
        out_shape=jax.ShapeDtypeStruct((pieces, n_tiles, TOKEN_CHUNK, D_MODEL), F32),
        scratch_shapes=[
            pltpu.VMEM((rows, D_MODEL), BF16),
            pltpu.VMEM((rows, ROUTER_LANES), F32),
            pltpu.VMEM((rows, D_MODEL), BF16),
        ],
        compiler_params=pltpu.CompilerParams(
            dimension_semantics=("arbitrary",), vmem_limit_bytes=MOE_VMEM_LIMIT),
        name="moe",
    )(view(x1), view(xn2), view(gates), wg, wu, wd, gf)
    return y.reshape(n, D_MODEL)


def _proj_sample_body(x_ref, g1_ref, win_ref, proj_ref, kvt_ref):
    xn = _rms(x_ref[...], g1_ref[...]).astype(BF16)
    proj_ref[...] = _dot(xn, win_ref[...])
    kvt_ref[...] = lax.dot_general(win_ref[:, OFF_K:OFF_B], xn, (((0,), (1,)), ((), ())),
                                   preferred_element_type=F32)


def _proj_sample(x, g1, win):
    n = x.shape[0]
    return pl.pallas_call(
        _proj_sample_body,
        out_shape=[jax.ShapeDtypeStruct((n, IN_PROJ_DIM), F32),
                   jax.ShapeDtypeStruct((2 * KV_DIM, n), F32)],
        compiler_params=pltpu.CompilerParams(vmem_limit_bytes=VMEM_LIMIT),
        name="proj_sample",
    )(x, g1, win)


def _attn_sample_body(q8_ref, knew_ref, vnew_ref, kvt_ref, khist_ref, vhist_ref, sinks_ref,
                      out8_ref, kout_ref, vout_ref):
    chunk = q8_ref.shape[0]
    first = pl.program_id(0) * chunk
    q8 = q8_ref[...]
    kh = khist_ref[...]
    vh = vhist_ref[...]
    knew = knew_ref[...]
    vnew = vnew_ref[...]
    s = jnp.einsum('bhj,bjc->bhc', q8, kh.astype(BF16), preferred_element_type=F32)
    s_new = jnp.sum(q8.astype(F32) * knew, axis=-1, keepdims=True)
    sink = sinks_ref[...][None]
    mx = jnp.maximum(jnp.maximum(jnp.max(s, axis=-1, keepdims=True), s_new), sink)
    e = jnp.exp(s - mx)
    e_new = jnp.exp(s_new - mx)
    den = jnp.sum(e, axis=-1, keepdims=True) + e_new + jnp.exp(sink - mx)
    o = jnp.einsum('bhc,bjc->bhj', e.astype(BF16), vh.astype(BF16), preferred_element_type=F32)
    out8_ref[...] = (o + e_new * vnew) / den
    last = lax.broadcasted_iota(jnp.int32, (KV_DIM, WINDOW), 1) == WINDOW - 1
    knew_t = kvt_ref[0:KV_DIM, :]
    vnew_t = kvt_ref[KV_DIM:, :]
    for bb in range(chunk):
        shift = lax.rem(2 * WINDOW - 1 - (first + bb), WINDOW)
        kout_ref[bb] = jnp.where(last, pltpu.roll(knew_t, shift, axis=1), pltpu.roll(kh[bb], WINDOW - 1, axis=1))
        vout_ref[bb] = jnp.where(last, pltpu.roll(vnew_t, shift, axis=1), pltpu.roll(vh[bb], WINDOW - 1, axis=1))


def _attn_sample(q8, knew, vnew, kvt, khist, vhist, sinks_col):
    assert q8.shape[0] == WINDOW == LANES, "one lane per sequence in the channel-major projection"
    n = q8.shape[0]
    c = DEC_CHUNK
    blk3 = lambda i: (i, 0, 0)
    return pl.pallas_call(
        _attn_sample_body,
        grid=(n // c,),
        in_specs=[
            pl.BlockSpec((c, N_HEADS, LANES), blk3),
            pl.BlockSpec((c, 1, KV_DIM), blk3),
            pl.BlockSpec((c, 1, KV_DIM), blk3),
            pl.BlockSpec((2 * KV_DIM, n), lambda i: (0, 0)),
            pl.BlockSpec((c, KV_DIM, WINDOW), blk3),
            pl.BlockSpec((c, KV_DIM, WINDOW), blk3),
            pl.BlockSpec((N_HEADS, 1), lambda i: (0, 0)),
        ],
        out_specs=[
            pl.BlockSpec((c, N_HEADS, LANES), blk3),
            pl.BlockSpec((c, KV_DIM, WINDOW), blk3),
            pl.BlockSpec((c, KV_DIM, WINDOW), blk3),
        ],
        out_shape=[
            jax.ShapeDtypeStruct((n, N_HEADS, LANES), F32),
            jax.ShapeDtypeStruct((n, KV_DIM, WINDOW), F32),
            jax.ShapeDtypeStruct((n, KV_DIM, WINDOW), F32),
        ],
        compiler_params=pltpu.CompilerParams(
            dimension_semantics=("arbitrary",), vmem_limit_bytes=VMEM_LIMIT),
        name="attn_sample",
    )(q8, knew, vnew, kvt, khist, vhist, sinks_col)


def _tail_sample_body(x_ref, attn_ref, proj_ref, uprev2_ref, uprev1_ref, convw_ref, gattn_ref,
                      gconv_ref, wout_ref, g2_ref, wr_ref, br_ref,
                      x1_ref, xn2_ref, gates_ref, u_ref):
    gate_b = proj_ref[:, OFF_B:OFF_C]
    u = proj_ref[:, OFF_C:OFF_H] * proj_ref[:, OFF_H:IN_PROJ_DIM]
    cw = convw_ref[...]
    z = cw[0:1, :] * uprev2_ref[...] + cw[1:2, :] * uprev1_ref[...] + cw[2:3, :] * u
    u_ref[...] = u
    x1, xn2_bf, gates = _mix_tail(x_ref[...], attn_ref[...], gate_b * z, gattn_ref[...], gconv_ref[...],
                                  wout_ref[...], g2_ref[...], wr_ref[...], br_ref[...])
    x1_ref[...] = x1
    xn2_ref[...] = xn2_bf
    gates_ref[...] = gates


def _tail_sample(x, attn, proj, uprev2, uprev1, convw, gattn, gconv, wout, g2, wr, br):
    n = x.shape[0]
    return pl.pallas_call(
        _tail_sample_body,
        out_shape=[
            jax.ShapeDtypeStruct((n, D_MODEL), F32),
            jax.ShapeDtypeStruct((n, D_MODEL), BF16),
            jax.ShapeDtypeStruct((n, ROUTER_LANES), F32),
            jax.ShapeDtypeStruct((n, CONV_DIM), F32),
        ],
        compiler_params=pltpu.CompilerParams(vmem_limit_bytes=VMEM_LIMIT),
        name="tail_sample",
    )(x, attn, proj, uprev2, uprev1, convw, gattn, gconv, wout, g2, wr, br)


def _pair_heads(w, axis):
    shape = w.shape
    n_pair = N_HEADS // 2
    split = shape[:axis] + (2, n_pair, HEAD_DIM) + shape[axis + 1:]
    return jnp.swapaxes(w.reshape(split), axis, axis + 1).reshape(shape)


def kernel(x_prompt, x_sample, cache_k_win, cache_v_win, state_conv, norm1_g, w_in, conv_w, sinks,
           attn_out_g, conv_out_g, w_out, norm2_g, w_group, b_group, w_fine, b_fine, w_gate, w_up,
           w_down, final_g):
    nb, seq, _ = x_prompt.shape
    nd = x_sample.shape[0]

    w_in0 = w_in[0].astype(BF16)
    win = jnp.concatenate([_pair_heads(w_in0[:, :ATTN_DIM], 1), w_in0[:, ATTN_DIM:]], axis=1)
    w_out0 = w_out[0].astype(BF16)
    wout = jnp.concatenate([_pair_heads(w_out0[:ATTN_DIM], 0), w_out0[ATTN_DIM:]], axis=0)
    gattn = _pair_heads(attn_out_g[0], 0)[None]
    gconv = conv_out_g[0][None]
    g1 = norm1_g[0][None]
    g2 = norm2_g[0][None]
    gf = final_g[None]
    convw = conv_w[0]
    sinks0 = sinks[0]
    pad = ROUTER_LANES - N_EXPERTS - N_GROUPS
    wr32 = jnp.concatenate([w_fine[0], w_group[0], jnp.zeros((D_MODEL, pad), F32)], axis=1)
    wr_hi = wr32.astype(BF16)
    wr = jnp.concatenate([wr_hi, (wr32 - wr_hi.astype(F32)).astype(BF16)], axis=1)
    br = jnp.concatenate([b_fine[0], b_group[0], jnp.zeros((pad,), F32)])[:ROUTER_ROWS, None]

    x1p, xn2p, gatesp, kp, vp, up, wg, wu, wd = _layer_prompt(
        x_prompt, sinks0, g1, win, convw, gattn, gconv, wout, g2, wr, br, w_gate[0], w_up[0], w_down[0])
    wd = wd.reshape(N_GROUPS, (N_EXPERTS // N_GROUPS) * D_EXPERT, D_MODEL)
    n_p = nb * seq
    y_prompt = _moe(x1p.reshape(n_p, D_MODEL), xn2p.reshape(n_p, D_MODEL),
                    gatesp.reshape(n_p, ROUTER_LANES), wg, wu, wd, gf, T_MOE, MOE_CAP, MOE_XC).reshape(nb, seq, D_MODEL)

    xs = x_sample.reshape(nd, D_MODEL)
    proj, kvt = _proj_sample(xs, g1, win)
    lane = jnp.arange(LANES)
    qp = (proj[:, :ATTN_DIM] * (HEAD_DIM ** -0.5)).reshape(nd, N_HEADS // 2, LANES)
    q8 = jnp.concatenate([jnp.where(lane < HEAD_DIM, qp, 0.0), jnp.where(lane >= HEAD_DIM, qp, 0.0)],
                         axis=1).astype(BF16)
    knew = proj[:, OFF_K:OFF_V].reshape(nd, 1, KV_DIM)
    vnew = proj[:, OFF_V:OFF_B].reshape(nd, 1, KV_DIM)
    to_cm = lambda c: jnp.transpose(c[0], (0, 2, 3, 1)).reshape(nd, KV_DIM, WINDOW)
    from_cm = lambda c, n: jnp.transpose(c.reshape(n, KV_DIM // HEAD_DIM, HEAD_DIM, WINDOW), (0, 3, 1, 2))[None]
    out8, ks, vs = _attn_sample(q8, knew, vnew, kvt, to_cm(cache_k_win), to_cm(cache_v_win), sinks0[:, None])
    attn_s = jnp.where(lane < HEAD_DIM, out8[:, :N_HEADS // 2], out8[:, N_HEADS // 2:]).reshape(nd, ATTN_DIM)
    st = state_conv[0]
    x1s, xn2s, gatess, us = _tail_sample(xs, attn_s, proj, st[:, 0], st[:, 1], convw, gattn, gconv,
                                         wout, g2, wr, br)
    y_sample = _moe(x1s, xn2s, gatess, wg, wu, wd, gf, nd, MOE_CAP_DEC, MOE_XC_DEC).reshape(nd, 1, D_MODEL)

    return (y_prompt, y_sample,
            from_cm(kp, nb), from_cm(vp, nb),
            up[None],
            from_cm(ks, nd), from_cm(vs, nd),
            jnp.stack([st[:, 1], us], axis=1)[None])
```

```python
import functools

import jax
import jax.numpy as jnp
from jax import lax
from jax.experimental import pallas as pl
from jax.experimental.pallas import tpu as pltpu

D_MODEL = 1024
HEAD_DIM = 64
N_HEADS = 8
ATTN_DIM = 512
KV_DIM = 128
WINDOW = 128
CONV_DIM = 512
CONV_WIDTH = 3
OFF_K = 512
OFF_V = 640
OFF_B = 768
OFF_C = 1280
OFF_H = 1792
IN_PROJ_DIM = 2304
N_GROUPS = 4
N_EXPERTS = 16
D_EXPERT = 256
RMS_EPS = 1e-5
NEG_INF = -1e30

LANES = 128
ROW_ALIGN = 16
ROUTER_LANES = 128
ROUTER_ROWS = 32
COARSE_OFF = N_EXPERTS
T_LAYER = 1024
N_SUB = 2
GID_LANE = N_EXPERTS
T_MOE = 512
TOKEN_CHUNK = 16
MOE_CAP, MOE_XC = 160, 64
MOE_CAP_DEC, MOE_XC_DEC = 64, 32
DEC_CHUNK = 32
VMEM_LIMIT = 48 * 1024 * 1024
MOE_VMEM_LIMIT = 56 * 1024 * 1024

BF16 = jnp.bfloat16
F32 = jnp.float32


def _dot(a, b):
    return jnp.dot(a, b, preferred_element_type=F32)


def _rms(x, g):
    ms = jnp.mean(x * x, axis=-1, keepdims=True)
    return x * lax.rsqrt(ms + RMS_EPS) * g


def _group_norm64(y, g):
    rows, cols = y.shape
    lo = lax.broadcasted_iota(jnp.int32, (rows, LANES), 1) < HEAD_DIM
    outs = []
    for c in range(cols // LANES):
        blk = y[:, c * LANES:(c + 1) * LANES]
        sq = blk * blk
        s_lo = jnp.sum(jnp.where(lo, sq, 0.0), axis=-1, keepdims=True)
        s_hi = jnp.sum(jnp.where(lo, 0.0, sq), axis=-1, keepdims=True)
        ms = jnp.where(lo, s_lo, s_hi) * (1.0 / HEAD_DIM)
        outs.append(blk * lax.rsqrt(ms + RMS_EPS))
    return jnp.concatenate(outs, axis=-1) * g


def _router(xn2, wr, br):
    hi = xn2.astype(BF16)
    both = _dot(hi, wr)
    logits = jnp.transpose(both[:, :ROUTER_LANES] + both[:, ROUTER_LANES:])[:ROUTER_ROWS] + br
    n = logits.shape[1]
    row = lax.broadcasted_iota(jnp.int32, (ROUTER_ROWS, n), 0)
    rowf = row.astype(F32)
    big = float(ROUTER_ROWS)
    coarse = (row >= COARSE_OFF) & (row < COARSE_OFF + N_GROUPS)
    cm = jnp.max(jnp.where(coarse, logits, -jnp.inf), axis=0, keepdims=True)
    g_sel = jnp.min(jnp.where(coarse & (logits == cm), rowf - COARSE_OFF, big), axis=0, keepdims=True)
    zc = jnp.sum(jnp.where(coarse, jnp.exp(logits - cm), 0.0), axis=0, keepdims=True)
    p_sel = 1.0 / zc
    per_group = N_EXPERTS // N_GROUPS
    fine = (row < N_EXPERTS) & ((row // per_group).astype(F32) == g_sel)
    f1 = jnp.max(jnp.where(fine, logits, -jnp.inf), axis=0, keepdims=True)
    i1 = jnp.min(jnp.where(fine & (logits == f1), rowf, big), axis=0, keepdims=True)
    rest = fine & (rowf != i1)
    f2 = jnp.max(jnp.where(rest, logits, -jnp.inf), axis=0, keepdims=True)
    i2 = jnp.min(jnp.where(rest & (logits == f2), rowf, big), axis=0, keepdims=True)
    e2 = jnp.exp(f2 - f1)
    t1 = 1.0 / (1.0 + e2)
    t2 = e2 * t1
    gates_t = jnp.where(rowf == i1, p_sel * t1, jnp.where(rowf == i2, p_sel * t2, 0.0))
    gates_t = jnp.where(row == GID_LANE, g_sel, gates_t)
    gates_t = jnp.concatenate([gates_t, jnp.zeros((ROUTER_LANES - ROUTER_ROWS, n), F32)], axis=0)
    return hi, jnp.transpose(gates_t)


def _mix_tail(x, attn, conv_out, gattn, gconv, wout, g2, wr, br):
    mixed = jnp.concatenate([_group_norm64(attn, gattn), _group_norm64(conv_out, gconv)],
                            axis=-1).astype(BF16)
    x1 = x + _dot(mixed, wout)
    xn2_bf, gates = _router(_rms(x1, g2), wr, br)
    return x1, xn2_bf, gates


def _layer_prompt_body(sinks_ref, x_ref, g1_ref, win_ref, convw_ref, gattn_ref, gconv_ref,
                       wout_ref, g2_ref, wr_ref, br_ref, wg32_ref, wu32_ref, wd32_ref,
                       x1_ref, xn2_ref, gates_ref, knew_ref, vnew_ref, unew_ref,
                       wg16_ref, wu16_ref, wd16_ref,
                       q_s, kcat_s, vcat_s, attn_s, ucarry_s):
    i = pl.program_id(1)
    wg16_ref[...] = wg32_ref[...].astype(BF16)
    wu16_ref[...] = wu32_ref[...].astype(BF16)
    wd16_ref[...] = wd32_ref[...].astype(BF16)
    t = T_LAYER // N_SUB
    last = N_SUB - 1

    @pl.when(i == 0)
    def _():
        kcat_s[last, t:t + WINDOW, :] = jnp.zeros((WINDOW, KV_DIM), BF16)
        vcat_s[last, t:t + WINDOW, :] = jnp.zeros((WINDOW, KV_DIM), BF16)
        ucarry_s[last] = jnp.zeros((8, CONV_DIM), F32)

    a_idx = lax.broadcasted_iota(jnp.int32, (WINDOW, 2 * WINDOW), 0)
    c_idx = lax.broadcasted_iota(jnp.int32, (WINDOW, 2 * WINDOW), 1)
    lane_lo = lax.broadcasted_iota(jnp.int32, (WINDOW, LANES), 1) < HEAD_DIM
    row = lax.broadcasted_iota(jnp.int32, (t, CONV_DIM), 0)
    n_pair = N_HEADS // 2
    cw = convw_ref[...]

    for h in range(N_SUB):
        src = (h - 1) % N_SUB
        rows = pl.ds(h * t, t)
        x = x_ref[0, rows, :]
        xn = _rms(x, g1_ref[...]).astype(BF16)

        kcat_s[h, 0:WINDOW, :] = kcat_s[src, t:t + WINDOW, :]
        vcat_s[h, 0:WINDOW, :] = vcat_s[src, t:t + WINDOW, :]
        q_s[h] = (_dot(xn, win_ref[:, 0:OFF_K]) * (HEAD_DIM ** -0.5)).astype(BF16)
        kv = _dot(xn, win_ref[:, OFF_K:OFF_B])
        k = kv[:, :KV_DIM]
        v = kv[:, KV_DIM:]
        kcat_s[h, WINDOW:, :] = k.astype(BF16)
        vcat_s[h, WINDOW:, :] = v.astype(BF16)
        if h == last:
            knew_ref[0] = jnp.transpose(k[t - WINDOW:, :])
            vnew_ref[0] = jnp.transpose(v[t - WINDOW:, :])

        for j in range(t // WINDOW):
            r0 = j * WINDOW
            kk = kcat_s[h, r0:r0 + 2 * WINDOW, :]
            vv = vcat_s[h, r0:r0 + 2 * WINDOW, :]
            c_min = jnp.where(i == 0, WINDOW, 0) if (h == 0 and j == 0) else 0
            valid = (c_idx >= jnp.maximum(a_idx, c_min)) & (c_idx <= a_idx + WINDOW)
            qcols = [q_s[h, r0:r0 + WINDOW, m * LANES:(m + 1) * LANES] for m in range(n_pair)]
            outs = []
            for half in range(2):
                keep = lane_lo if half == 0 else jnp.logical_not(lane_lo)
                qg = jnp.concatenate([jnp.where(keep, qc, jnp.zeros_like(qc)) for qc in qcols], axis=0)
                s = lax.dot_general(qg, kk, (((1,), (1,)), ((), ())), preferred_element_type=F32)
                es, dens = [], []
                for m in range(n_pair):
                    sm = jnp.where(valid, s[m * WINDOW:(m + 1) * WINDOW], NEG_INF)
                    sink = sinks_ref[m + half * n_pair]
                    mx = jnp.maximum(jnp.max(sm, axis=-1, keepdims=True), sink)
                    e = jnp.exp(sm - mx)
                    dens.append(jnp.sum(e, axis=-1, keepdims=True) + jnp.exp(sink - mx))
                    es.append(e.astype(BF16))
                o = _dot(jnp.concatenate(es, axis=0), vv)
                outs.append([o[m * WINDOW:(m + 1) * WINDOW] / dens[m] for m in range(n_pair)])
            for m in range(n_pair):
                attn_s[h, r0:r0 + WINDOW, m * LANES:(m + 1) * LANES] = jnp.where(lane_lo, outs[0][m], outs[1][m])

        gate_b = _dot(xn, win_ref[:, OFF_B:OFF_C])
        u = _dot(xn, win_ref[:, OFF_C:OFF_H]) * _dot(xn, win_ref[:, OFF_H:IN_PROJ_DIM])
        prev = ucarry_s[src]
        u1 = jnp.where(row == 0, prev[7:8, :], pltpu.roll(u, 1, axis=0))
        u2 = jnp.where(row == 0, prev[6:7, :], jnp.where(row == 1, prev[7:8, :], pltpu.roll(u, 2, axis=0)))
        z = cw[0:1, :] * u2 + cw[1:2, :] * u1 + cw[2:3, :] * u
        ucarry_s[h] = u[t - 8:, :]
        if h == last:
            unew_ref[0] = u[t - (CONV_WIDTH - 1):, :]

        x1, xn2_bf, gates = _mix_tail(x, attn_s[h], gate_b * z, gattn_ref[...], gconv_ref[...],
                                      wout_ref[...], g2_ref[...], wr_ref[...], br_ref[...])
        x1_ref[0, rows, :] = x1
        xn2_ref[0, rows, :] = xn2_bf
        gates_ref[0, rows, :] = gates


def _layer_prompt(x, sinks, g1, win, convw, gattn, gconv, wout, g2, wr, br, wg32, wu32, wd32):
    nb, seq, _ = x.shape
    t = T_LAYER
    ts = T_LAYER // N_SUB
    n_i = seq // t
    assert nb * n_i == N_EXPERTS, "one expert's weights are cast per grid step"
    const = lambda b, i, s: (0, 0)
    tile = lambda b, i, s: (b, i, 0)
    per_seq = lambda b, i, s: (b, 0, 0)
    per_step = lambda b, i, s: (b * n_i + i, 0, 0)
    resident = pl.Buffered(1)
    grid_spec = pltpu.PrefetchScalarGridSpec(
        num_scalar_prefetch=1,
        grid=(nb, seq // t),
        in_specs=[
            pl.BlockSpec((1, t, D_MODEL), tile),
            pl.BlockSpec((1, D_MODEL), const),
            pl.BlockSpec((D_MODEL, IN_PROJ_DIM), const, pipeline_mode=resident),
            pl.BlockSpec((CONV_WIDTH, CONV_DIM), const),
            pl.BlockSpec((1, ATTN_DIM), const),
            pl.BlockSpec((1, CONV_DIM), const),
            pl.BlockSpec((D_MODEL, D_MODEL), const, pipeline_mode=resident),
            pl.BlockSpec((1, D_MODEL), const),
            pl.BlockSpec((D_MODEL, 2 * ROUTER_LANES), const, pipeline_mode=resident),
            pl.BlockSpec((ROUTER_ROWS, 1), const),
            pl.BlockSpec((1, D_MODEL, D_EXPERT), per_step),
            pl.BlockSpec((1, D_MODEL, D_EXPERT), per_step),
            pl.BlockSpec((1, D_EXPERT, D_MODEL), per_step),
        ],
        out_specs=[
            pl.BlockSpec((1, t, D_MODEL), tile),
            pl.BlockSpec((1, t, D_MODEL), tile),
            pl.BlockSpec((1, t, ROUTER_LANES), tile),
            pl.BlockSpec((1, KV_DIM, WINDOW), per_seq),
            pl.BlockSpec((1, KV_DIM, WINDOW), per_seq),
            pl.BlockSpec((1, CONV_WIDTH - 1, CONV_DIM), per_seq),
            pl.BlockSpec((1, D_MODEL, D_EXPERT), per_step),
            pl.BlockSpec((1, D_MODEL, D_EXPERT), per_step),
            pl.BlockSpec((1, D_EXPERT, D_MODEL), per_step),
        ],
        scratch_shapes=[
            pltpu.VMEM((N_SUB, ts, ATTN_DIM), BF16),
            pltpu.VMEM((N_SUB, ts + WINDOW, KV_DIM), BF16),
            pltpu.VMEM((N_SUB, ts + WINDOW, KV_DIM), BF16),
            pltpu.VMEM((N_SUB, ts, ATTN_DIM), F32),
            pltpu.VMEM((N_SUB, 8, CONV_DIM), F32),
        ],
    )
    return pl.pallas_call(
        _layer_prompt_body,
        grid_spec=grid_spec,
        out_shape=[
            jax.ShapeDtypeStruct((nb, seq, D_MODEL), F32),
            jax.ShapeDtypeStruct((nb, seq, D_MODEL), BF16),
            jax.ShapeDtypeStruct((nb, seq, ROUTER_LANES), F32),
            jax.ShapeDtypeStruct((nb, KV_DIM, WINDOW), F32),
            jax.ShapeDtypeStruct((nb, KV_DIM, WINDOW), F32),
            jax.ShapeDtypeStruct((nb, CONV_WIDTH - 1, CONV_DIM), F32),
            jax.ShapeDtypeStruct((N_EXPERTS, D_MODEL, D_EXPERT), BF16),
            jax.ShapeDtypeStruct((N_EXPERTS, D_MODEL, D_EXPERT), BF16),
            jax.ShapeDtypeStruct((N_EXPERTS, D_EXPERT, D_MODEL), BF16),
        ],
        compiler_params=pltpu.CompilerParams(
            dimension_semantics=("arbitrary", "arbitrary"), vmem_limit_bytes=MOE_VMEM_LIMIT),
        name="layer_prompt",
    )(sinks, x, g1, win, convw, gattn, gconv, wout, g2, wr, br, wg32, wu32, wd32)


def _moe_body(cap, xc, x1_ref, xn2_ref, gates_ref, wg_ref, wu_ref, wd_ref, gf_ref, y_ref,
              xs_s, gs_s, ys_s):
    t = x1_ref.shape[0] * TOKEN_CHUNK
    per_group = N_EXPERTS // N_GROUPS
    gates = gates_ref[...].reshape(t, ROUTER_LANES)
    gid_row = jnp.transpose(gates)[GID_LANE:GID_LANE + 1, :]
    grp = lax.broadcasted_iota(jnp.int32, (8, t), 0).astype(F32)
    onehot = jnp.where(grp == gid_row, 1.0, 0.0)

    r_idx = lax.broadcasted_iota(jnp.int32, (t, t), 0)
    c_idx = lax.broadcasted_iota(jnp.int32, (t, t), 1)
    upper = jnp.where(r_idx <= c_idx, 1.0, 0.0).astype(BF16)
    csum = _dot(onehot.astype(BF16), upper)
    rank_row = jnp.sum(onehot * (csum - 1.0), axis=0, keepdims=True)
    cnt = [csum[g, t - 1].astype(jnp.int32) for g in range(N_GROUPS)]
    off = [jnp.int32(0)]
    for g in range(1, N_GROUPS):
        off.append(off[-1] + cnt[g - 1])

    pos_row = rank_row
    for g in range(1, N_GROUPS):
        pos_row = pos_row + jnp.where(gid_row == float(g), off[g].astype(F32), 0.0)
    pos_col = jnp.transpose(jnp.broadcast_to(pos_row, (LANES, t)))
    pos_col = jnp.concatenate([pos_col] * (t // LANES), axis=1)
    perm = jnp.where(r_idx.astype(F32) == pos_row, 1.0, 0.0).astype(BF16)
    perm_t = jnp.where(c_idx.astype(F32) == pos_col, 1.0, 0.0).astype(BF16)

    p1 = gates.astype(BF16)
    p2 = (gates - p1.astype(F32)).astype(BF16)
    moved = _dot(perm, jnp.concatenate([xn2_ref[...].reshape(t, D_MODEL), p1, p2], axis=1))
    xs_s[0:t, :] = moved[:, :D_MODEL].astype(BF16)
    gs_s[0:t, :] = moved[:, D_MODEL:D_MODEL + ROUTER_LANES] + moved[:, D_MODEL + ROUTER_LANES:]
    pad = xs_s.shape[0] - t
    xs_s[t:, :] = jnp.zeros((pad, D_MODEL), BF16)
    gs_s[t:, :] = jnp.zeros((pad, ROUTER_LANES), F32)
    ys_s[...] = jnp.zeros(ys_s.shape, BF16)

    def group_rows(g, r0, rows):
        xs = xs_s[pl.ds(r0, rows), :]
        gsc = gs_s[pl.ds(r0, rows), :]
        acts = []
        for k in range(per_group):
            e = g * per_group + k
            hg = _dot(xs, wg_ref[e])
            hu = _dot(xs, wu_ref[e])
            acts.append((hg / (1.0 + jnp.exp(-hg)) * hu * gsc[:, e:e + 1]).astype(BF16))
        ys_s[pl.ds(r0, rows), :] += _dot(jnp.concatenate(acts, axis=1), wd_ref[g]).astype(BF16)

    base = [(off[g] // ROW_ALIGN) * ROW_ALIGN for g in range(N_GROUPS)]
    for g in range(N_GROUPS):
        group_rows(g, pl.multiple_of(base[g], ROW_ALIGN), cap)
    for g in range(N_GROUPS):
        n_extra = jnp.maximum(0, (off[g] + cnt[g] - (base[g] + cap) + xc - 1) // xc)

        def extra(k, carry, g=g):
            group_rows(g, pl.multiple_of(base[g] + cap + k * xc, ROW_ALIGN), xc)
            return carry

        lax.fori_loop(0, n_extra, extra, 0)

    moe = _dot(perm_t, ys_s[0:t, :])
    y = _rms(x1_ref[...].reshape(t, D_MODEL) + moe, gf_ref[...])
    y_ref[...] = y.reshape(y_ref.shape)


def _moe(x1, xn2, gates, wg, wu, wd, gf, t, cap, xc):
    n = x1.shape[0]
    n_tiles = n // t
    pieces = t // TOKEN_CHUNK
    view = lambda a: a.reshape(pieces, n_tiles, TOKEN_CHUNK, a.shape[-1])
    tile = lambda i: (0, i, 0, 0)
    whole = lambda i: (0, 0, 0)
    resident = pl.Buffered(1)
    rows = t + cap + xc
    hidden = (N_EXPERTS // N_GROUPS) * D_EXPERT
    y = pl.pallas_call(
        functools.partial(_moe_body, cap, xc),
        grid=(n_tiles,),
        in_specs=[
            pl.BlockSpec((pieces, 1, TOKEN_CHUNK, D_MODEL), tile),
            pl.BlockSpec((pieces, 1, TOKEN_CHUNK, D_MODEL), tile),
            pl.BlockSpec((pieces, 1, TOKEN_CHUNK, ROUTER_LANES), tile),
            pl.BlockSpec((N_EXPERTS, D_MODEL, D_EXPERT), whole, pipeline_mode=resident),
            pl.BlockSpec((N_EXPERTS, D_MODEL, D_EXPERT), whole, pipeline_mode=resident),
            pl.BlockSpec((N_GROUPS, hidden, D_MODEL), whole, pipeline_mode=resident),
            pl.BlockSpec((1, D_MODEL), lambda i: (0, 0)),
        ],
        out_specs=pl.BlockSpec((pieces, 1, TOKEN_CHUNK, D_MODEL), tile),
        out_shape=jax.ShapeDtypeStruct((pieces, n_tiles, TOKEN_CHUNK, D_MODEL), F32),
        scratch_shapes=[
            pltpu.VMEM((rows, D_MODEL), BF16),
            pltpu.VMEM((rows, ROUTER_LANES), F32),
            pltpu.VMEM((rows, D_MODEL), BF16),
        ],
        compiler_params=pltpu.CompilerParams(
            dimension_semantics=("parallel",), vmem_limit_bytes=MOE_VMEM_LIMIT),
        name="moe",
    )(view(x1), view(xn2), view(gates), wg, wu, wd, gf)
    return y.reshape(n, D_MODEL)


def _proj_sample_body(x_ref, g1_ref, win_ref, proj_ref, kvt_ref):
    xn = _rms(x_ref[...], g1_ref[...]).astype(BF16)
    proj_ref[...] = _dot(xn, win_ref[...])
    kvt_ref[...] = lax.dot_general(win_ref[:, OFF_K:OFF_B], xn, (((0,), (1,)), ((), ())),
                                   preferred_element_type=F32)


def _proj_sample(x, g1, win):
    n = x.shape[0]
    return pl.pallas_call(
        _proj_sample_body,
        out_shape=[jax.ShapeDtypeStruct((n, IN_PROJ_DIM), F32),
                   jax.ShapeDtypeStruct((2 * KV_DIM, n), F32)],
        compiler_params=pltpu.CompilerParams(vmem_limit_bytes=VMEM_LIMIT),
        name="proj_sample",
    )(x, g1, win)


def _attn_sample_body(q8_ref, knew_ref, vnew_ref, kvt_ref, khist_ref, vhist_ref, sinks_ref,
                      out8_ref, kout_ref, vout_ref):
    chunk = q8_ref.shape[0]
    first = pl.program_id(0) * chunk
    q8 = q8_ref[...]
    kh = khist_ref[...]
    vh = vhist_ref[...]
    knew = knew_ref[...]
    vnew = vnew_ref[...]
    s = jnp.einsum('bhj,bjc->bhc', q8, kh.astype(BF16), preferred_element_type=F32)
    s_new = jnp.sum(q8.astype(F32) * knew, axis=-1, keepdims=True)
    sink = sinks_ref[...][None]
    mx = jnp.maximum(jnp.maximum(jnp.max(s, axis=-1, keepdims=True), s_new), sink)
    e = jnp.exp(s - mx)
    e_new = jnp.exp(s_new - mx)
    den = jnp.sum(e, axis=-1, keepdims=True) + e_new + jnp.exp(sink - mx)
    o = jnp.einsum('bhc,bjc->bhj', e.astype(BF16), vh.astype(BF16), preferred_element_type=F32)
    out8_ref[...] = (o + e_new * vnew) / den
    last = lax.broadcasted_iota(jnp.int32, (KV_DIM, WINDOW), 1) == WINDOW - 1
    knew_t = kvt_ref[0:KV_DIM, :]
    vnew_t = kvt_ref[KV_DIM:, :]
    for bb in range(chunk):
        shift = lax.rem(2 * WINDOW - 1 - (first + bb), WINDOW)
        kout_ref[bb] = jnp.where(last, pltpu.roll(knew_t, shift, axis=1), pltpu.roll(kh[bb], WINDOW - 1, axis=1))
        vout_ref[bb] = jnp.where(last, pltpu.roll(vnew_t, shift, axis=1), pltpu.roll(vh[bb], WINDOW - 1, axis=1))


def _attn_sample(q8, knew, vnew, kvt, khist, vhist, sinks_col):
    assert q8.shape[0] == WINDOW == LANES, "one lane per sequence in the channel-major projection"
    n = q8.shape[0]
    c = DEC_CHUNK
    blk3 = lambda i: (i, 0, 0)
    return pl.pallas_call(
        _attn_sample_body,
        grid=(n // c,),
        in_specs=[
            pl.BlockSpec((c, N_HEADS, LANES), blk3),
            pl.BlockSpec((c, 1, KV_DIM), blk3),
            pl.BlockSpec((c, 1, KV_DIM), blk3),
            pl.BlockSpec((2 * KV_DIM, n), lambda i: (0, 0)),
            pl.BlockSpec((c, KV_DIM, WINDOW), blk3),
            pl.BlockSpec((c, KV_DIM, WINDOW), blk3),
            pl.BlockSpec((N_HEADS, 1), lambda i: (0, 0)),
        ],
        out_specs=[
            pl.BlockSpec((c, N_HEADS, LANES), blk3),
            pl.BlockSpec((c, KV_DIM, WINDOW), blk3),
            pl.BlockSpec((c, KV_DIM, WINDOW), blk3),
        ],
        out_shape=[
            jax.ShapeDtypeStruct((n, N_HEADS, LANES), F32),
            jax.ShapeDtypeStruct((n, KV_DIM, WINDOW), F32),
            jax.ShapeDtypeStruct((n, KV_DIM, WINDOW), F32),
        ],
        compiler_params=pltpu.CompilerParams(
            dimension_semantics=("arbitrary",), vmem_limit_bytes=VMEM_LIMIT),
        name="attn_sample",
    )(q8, knew, vnew, kvt, khist, vhist, sinks_col)


def _tail_sample_body(x_ref, attn_ref, proj_ref, uprev2_ref, uprev1_ref, convw_ref, gattn_ref,
                      gconv_ref, wout_ref, g2_ref, wr_ref, br_ref,
                      x1_ref, xn2_ref, gates_ref, u_ref):
    gate_b = proj_ref[:, OFF_B:OFF_C]
    u = proj_ref[:, OFF_C:OFF_H] * proj_ref[:, OFF_H:IN_PROJ_DIM]
    cw = convw_ref[...]
    z = cw[0:1, :] * uprev2_ref[...] + cw[1:2, :] * uprev1_ref[...] + cw[2:3, :] * u
    u_ref[...] = u
    x1, xn2_bf, gates = _mix_tail(x_ref[...], attn_ref[...], gate_b * z, gattn_ref[...], gconv_ref[...],
                                  wout_ref[...], g2_ref[...], wr_ref[...], br_ref[...])
    x1_ref[...] = x1
    xn2_ref[...] = xn2_bf
    gates_ref[...] = gates


def _tail_sample(x, attn, proj, uprev2, uprev1, convw, gattn, gconv, wout, g2, wr, br):
    n = x.shape[0]
    return pl.pallas_call(
        _tail_sample_body,
        out_shape=[
            jax.ShapeDtypeStruct((n, D_MODEL), F32),
            jax.ShapeDtypeStruct((n, D_MODEL), BF16),
            jax.ShapeDtypeStruct((n, ROUTER_LANES), F32),
            jax.ShapeDtypeStruct((n, CONV_DIM), F32),
        ],
        compiler_params=pltpu.CompilerParams(vmem_limit_bytes=VMEM_LIMIT),
        name="tail_sample",
    )(x, attn, proj, uprev2, uprev1, convw, gattn, gconv, wout, g2, wr, br)


def _pair_heads(w, axis):
    shape = w.shape
    n_pair = N_HEADS // 2
    split = shape[:axis] + (2, n_pair, HEAD_DIM) + shape[axis + 1:]
    return jnp.swapaxes(w.reshape(split), axis, axis + 1).reshape(shape)


def kernel(x_prompt, x_sample, cache_k_win, cache_v_win, state_conv, norm1_g, w_in, conv_w, sinks,
           attn_out_g, conv_out_g, w_out, norm2_g, w_group, b_group, w_fine, b_fine, w_gate, w_up,
           w_down, final_g):
    nb, seq, _ = x_prompt.shape
    nd = x_sample.shape[0]

    w_in0 = w_in[0].astype(BF16)
    win = jnp.concatenate([_pair_heads(w_in0[:, :ATTN_DIM], 1), w_in0[:, ATTN_DIM:]], axis=1)
    w_out0 = w_out[0].astype(BF16)
    wout = jnp.concatenate([_pair_heads(w_out0[:ATTN_DIM], 0), w_out0[ATTN_DIM:]], axis=0)
    gattn = _pair_heads(attn_out_g[0], 0)[None]
    gconv = conv_out_g[0][None]
    g1 = norm1_g[0][None]
    g2 = norm2_g[0][None]
    gf = final_g[None]
    convw = conv_w[0]
    sinks0 = sinks[0]
    pad = ROUTER_LANES - N_EXPERTS - N_GROUPS
    wr32 = jnp.concatenate([w_fine[0], w_group[0], jnp.zeros((D_MODEL, pad), F32)], axis=1)
    wr_hi = wr32.astype(BF16)
    wr = jnp.concatenate([wr_hi, (wr32 - wr_hi.astype(F32)).astype(BF16)], axis=1)
    br = jnp.concatenate([b_fine[0], b_group[0], jnp.zeros((pad,), F32)])[:ROUTER_ROWS, None]

    x1p, xn2p, gatesp, kp, vp, up, wg, wu, wd = _layer_prompt(
        x_prompt, sinks0, g1, win, convw, gattn, gconv, wout, g2, wr, br, w_gate[0], w_up[0], w_down[0])
    wd = wd.reshape(N_GROUPS, (N_EXPERTS // N_GROUPS) * D_EXPERT, D_MODEL)
    n_p = nb * seq
    y_prompt = _moe(x1p.reshape(n_p, D_MODEL), xn2p.reshape(n_p, D_MODEL),
                    gatesp.reshape(n_p, ROUTER_LANES), wg, wu, wd, gf, T_MOE, MOE_CAP, MOE_XC).reshape(nb, seq, D_MODEL)

    xs = x_sample.reshape(nd, D_MODEL)
    proj, kvt = _proj_sample(xs, g1, win)
    lane = jnp.arange(LANES)
    qp = (proj[:, :ATTN_DIM] * (HEAD_DIM ** -0.5)).reshape(nd, N_HEADS // 2, LANES)
    q8 = jnp.concatenate([jnp.where(lane < HEAD_DIM, qp, 0.0), jnp.where(lane >= HEAD_DIM, qp, 0.0)],
                         axis=1).astype(BF16)
    knew = proj[:, OFF_K:OFF_V].reshape(nd, 1, KV_DIM)
    vnew = proj[:, OFF_V:OFF_B].reshape(nd, 1, KV_DIM)
    to_cm = lambda c: jnp.transpose(c[0], (0, 2, 3, 1)).reshape(nd, KV_DIM, WINDOW)
    from_cm = lambda c, n: jnp.transpose(c.reshape(n, KV_DIM // HEAD_DIM, HEAD_DIM, WINDOW), (0, 3, 1, 2))[None]
    out8, ks, vs = _attn_sample(q8, knew, vnew, kvt, to_cm(cache_k_win), to_cm(cache_v_win), sinks0[:, None])
    attn_s = jnp.where(lane < HEAD_DIM, out8[:, :N_HEADS // 2], out8[:, N_HEADS // 2:]).reshape(nd, ATTN_DIM)
    st = state_conv[0]
    x1s, xn2s, gatess, us = _tail_sample(xs, attn_s, proj, st[:, 0], st[:, 1], convw, gattn, gconv,
                                         wout, g2, wr, br)
    y_sample = _moe(x1s, xn2s, gatess, wg, wu, wd, gf, nd, MOE_CAP_DEC, MOE_XC_DEC).reshape(nd, 1, D_MODEL)

    return (y_prompt, y_sample,
            from_cm(kp, nb), from_cm(vp, nb),
            up[None],
            from_cm(ks, nd), from_cm(vs, nd),
            jnp.stack([st[:, 1], us], axis=1)[None])
```

```python
import functools

import jax
import jax.numpy as jnp
from jax import lax
from jax.experimental import pallas as pl
from jax.experimental.pallas import tpu as pltpu

D_MODEL = 1024
HEAD_DIM = 64
N_HEADS = 8
ATTN_DIM = 512
KV_DIM = 128
WINDOW = 128
CONV_DIM = 512
CONV_WIDTH = 3
OFF_K = 512
OFF_V = 640
OFF_B = 768
OFF_C = 1280
OFF_H = 1792
IN_PROJ_DIM = 2304
N_GROUPS = 4
N_EXPERTS = 16
D_EXPERT = 256
RMS_EPS = 1e-5
NEG_INF = -1e30

LANES = 128
ROW_ALIGN = 16
ROUTER_LANES = 128
ROUTER_ROWS = 32
COARSE_OFF = N_EXPERTS
T_LAYER = 1024
N_SUB = 2
GID_LANE = N_EXPERTS
T_MOE = 512
TOKEN_CHUNK = 16
MOE_CAP, MOE_XC = 160, 64
MOE_CAP_DEC, MOE_XC_DEC = 64, 32
DEC_CHUNK = 32
VMEM_LIMIT = 48 * 1024 * 1024
MOE_VMEM_LIMIT = 56 * 1024 * 1024

BF16 = jnp.bfloat16
F32 = jnp.float32


def _dot(a, b):
    return jnp.dot(a, b, preferred_element_type=F32)


def _rms(x, g):
    ms = jnp.mean(x * x, axis=-1, keepdims=True)
    return x * lax.rsqrt(ms + RMS_EPS) * g


def _group_norm64(y, g):
    rows, cols = y.shape
    lo = lax.broadcasted_iota(jnp.int32, (rows, LANES), 1) < HEAD_DIM
    outs = []
    for c in range(cols // LANES):
        blk = y[:, c * LANES:(c + 1) * LANES]
        sq = blk * blk
        s_lo = jnp.sum(jnp.where(lo, sq, 0.0), axis=-1, keepdims=True)
        s_hi = jnp.sum(jnp.where(lo, 0.0, sq), axis=-1, keepdims=True)
        ms = jnp.where(lo, s_lo, s_hi) * (1.0 / HEAD_DIM)
        outs.append(blk * lax.rsqrt(ms + RMS_EPS))
    return jnp.concatenate(outs, axis=-1) * g


def _router(xn2, wr, br):
    hi = xn2.astype(BF16)
    both = _dot(hi, wr)
    logits = jnp.transpose(both[:, :ROUTER_LANES] + both[:, ROUTER_LANES:])[:ROUTER_ROWS] + br
    n = logits.shape[1]
    row = lax.broadcasted_iota(jnp.int32, (ROUTER_ROWS, n), 0)
    rowf = row.astype(F32)
    big = float(ROUTER_ROWS)
    coarse = (row >= COARSE_OFF) & (row < COARSE_OFF + N_GROUPS)
    cm = jnp.max(jnp.where(coarse, logits, -jnp.inf), axis=0, keepdims=True)
    g_sel = jnp.min(jnp.where(coarse & (logits == cm), rowf - COARSE_OFF, big), axis=0, keepdims=True)
    zc = jnp.sum(jnp.where(coarse, jnp.exp(logits - cm), 0.0), axis=0, keepdims=True)
    p_sel = 1.0 / zc
    per_group = N_EXPERTS // N_GROUPS
    fine = (row < N_EXPERTS) & ((row // per_group).astype(F32) == g_sel)
    f1 = jnp.max(jnp.where(fine, logits, -jnp.inf), axis=0, keepdims=True)
    i1 = jnp.min(jnp.where(fine & (logits == f1), rowf, big), axis=0, keepdims=True)
    rest = fine & (rowf != i1)
    f2 = jnp.max(jnp.where(rest, logits, -jnp.inf), axis=0, keepdims=True)
    i2 = jnp.min(jnp.where(rest & (logits == f2), rowf, big), axis=0, keepdims=True)
    e2 = jnp.exp(f2 - f1)
    t1 = 1.0 / (1.0 + e2)
    t2 = e2 * t1
    gates_t = jnp.where(rowf == i1, p_sel * t1, jnp.where(rowf == i2, p_sel * t2, 0.0))
    gates_t = jnp.where(row == GID_LANE, g_sel, gates_t)
    gates_t = jnp.concatenate([gates_t, jnp.zeros((ROUTER_LANES - ROUTER_ROWS, n), F32)], axis=0)
    return hi, jnp.transpose(gates_t)


def _mix_tail(x, attn, conv_out, gattn, gconv, wout, g2, wr, br):
    mixed = jnp.concatenate([_group_norm64(attn, gattn), _group_norm64(conv_out, gconv)],
                            axis=-1).astype(BF16)
    x1 = x + _dot(mixed, wout)
    xn2_bf, gates = _router(_rms(x1, g2), wr, br)
    return x1, xn2_bf, gates


def _layer_prompt_body(sinks_ref, x_ref, g1_ref, win_ref, convw_ref, gattn_ref, gconv_ref,
                       wout_ref, g2_ref, wr_ref, br_ref, wg32_ref, wu32_ref, wd32_ref,
                       x1_ref, xn2_ref, gates_ref, knew_ref, vnew_ref, unew_ref,
                       wg16_ref, wu16_ref, wd16_ref,
                       q_s, kcat_s, vcat_s, attn_s, ucarry_s):
    i = pl.program_id(1)
    wg16_ref[...] = wg32_ref[...].astype(BF16)
    wu16_ref[...] = wu32_ref[...].astype(BF16)
    wd16_ref[...] = wd32_ref[...].astype(BF16)
    t = T_LAYER // N_SUB
    last = N_SUB - 1

    @pl.when(i == 0)
    def _():
        kcat_s[last, t:t + WINDOW, :] = jnp.zeros((WINDOW, KV_DIM), BF16)
        vcat_s[last, t:t + WINDOW, :] = jnp.zeros((WINDOW, KV_DIM), BF16)
        ucarry_s[last] = jnp.zeros((8, CONV_DIM), F32)

    a_idx = lax.broadcasted_iota(jnp.int32, (WINDOW, 2 * WINDOW), 0)
    c_idx = lax.broadcasted_iota(jnp.int32, (WINDOW, 2 * WINDOW), 1)
    lane_lo = lax.broadcasted_iota(jnp.int32, (WINDOW, LANES), 1) < HEAD_DIM
    row = lax.broadcasted_iota(jnp.int32, (t, CONV_DIM), 0)
    n_pair = N_HEADS // 2
    cw = convw_ref[...]

    for h in range(N_SUB):
        src = (h - 1) % N_SUB
        rows = pl.ds(h * t, t)
        x = x_ref[0, rows, :]
        xn = _rms(x, g1_ref[...]).astype(BF16)

        kcat_s[h, 0:WINDOW, :] = kcat_s[src, t:t + WINDOW, :]
        vcat_s[h, 0:WINDOW, :] = vcat_s[src, t:t + WINDOW, :]
        q_s[h] = (_dot(xn, win_ref[:, 0:OFF_K]) * (HEAD_DIM ** -0.5)).astype(BF16)
        kv = _dot(xn, win_ref[:, OFF_K:OFF_B])
        k = kv[:, :KV_DIM]
        v = kv[:, KV_DIM:]
        kcat_s[h, WINDOW:, :] = k.astype(BF16)
        vcat_s[h, WINDOW:, :] = v.astype(BF16)
        if h == last:
            knew_ref[0] = jnp.transpose(k[t - WINDOW:, :])
            vnew_ref[0] = jnp.transpose(v[t - WINDOW:, :])

        for j in range(t // WINDOW):
            r0 = j * WINDOW
            kk = kcat_s[h, r0:r0 + 2 * WINDOW, :]
            vv = vcat_s[h, r0:r0 + 2 * WINDOW, :]
            c_min = jnp.where(i == 0, WINDOW, 0) if (h == 0 and j == 0) else 0
            valid = (c_idx >= jnp.maximum(a_idx, c_min)) & (c_idx <= a_idx + WINDOW)
            qcols = [q_s[h, r0:r0 + WINDOW, m * LANES:(m + 1) * LANES] for m in range(n_pair)]
            outs = []
            for half in range(2):
                keep = lane_lo if half == 0 else jnp.logical_not(lane_lo)
                qg = jnp.concatenate([jnp.where(keep, qc, jnp.zeros_like(qc)) for qc in qcols], axis=0)
                s = lax.dot_general(qg, kk, (((1,), (1,)), ((), ())), preferred_element_type=F32)
                es, dens = [], []
                for m in range(n_pair):
                    sm = jnp.where(valid, s[m * WINDOW:(m + 1) * WINDOW], NEG_INF)
                    sink = sinks_ref[m + half * n_pair]
                    mx = jnp.maximum(jnp.max(sm, axis=-1, keepdims=True), sink)
                    e = jnp.exp(sm - mx)
                    dens.append(jnp.sum(e, axis=-1, keepdims=True) + jnp.exp(sink - mx))
                    es.append(e.astype(BF16))
                o = _dot(jnp.concatenate(es, axis=0), vv)
                outs.append([o[m * WINDOW:(m + 1) * WINDOW] / dens[m] for m in range(n_pair)])
            for m in range(n_pair):
                attn_s[h, r0:r0 + WINDOW, m * LANES:(m + 1) * LANES] = jnp.where(lane_lo, outs[0][m], outs[1][m])

        gate_b = _dot(xn, win_ref[:, OFF_B:OFF_C])
        u = _dot(xn, win_ref[:, OFF_C:OFF_H]) * _dot(xn, win_ref[:, OFF_H:IN_PROJ_DIM])
        prev = ucarry_s[src]
        u1 = jnp.where(row == 0, prev[7:8, :], pltpu.roll(u, 1, axis=0))
        u2 = jnp.where(row == 0, prev[6:7, :], jnp.where(row == 1, prev[7:8, :], pltpu.roll(u, 2, axis=0)))
        z = cw[0:1, :] * u2 + cw[1:2, :] * u1 + cw[2:3, :] * u
        ucarry_s[h] = u[t - 8:, :]
        if h == last:
            unew_ref[0] = u[t - (CONV_WIDTH - 1):, :]

        x1, xn2_bf, gates = _mix_tail(x, attn_s[h], gate_b * z, gattn_ref[...], gconv_ref[...],
                                      wout_ref[...], g2_ref[...], wr_ref[...], br_ref[...])
        x1_ref[0, rows, :] = x1
        xn2_ref[0, rows, :] = xn2_bf
        gates_ref[0, rows, :] = gates


def _layer_prompt(x, sinks, g1, win, convw, gattn, gconv, wout, g2, wr, br, wg32, wu32, wd32):
    nb, seq, _ = x.shape
    t = T_LAYER
    ts = T_LAYER // N_SUB
    n_i = seq // t
    assert nb * n_i == N_EXPERTS, "one expert's weights are cast per grid step"
    const = lambda b, i, s: (0, 0)
    tile = lambda b, i, s: (b, i, 0)
    per_seq = lambda b, i, s: (b, 0, 0)
    per_step = lambda b, i, s: (b * n_i + i, 0, 0)
    resident = pl.Buffered(1)
    grid_spec = pltpu.PrefetchScalarGridSpec(
        num_scalar_prefetch=1,
        grid=(nb, seq // t),
        in_specs=[
            pl.BlockSpec((1, t, D_MODEL), tile),
            pl.BlockSpec((1, D_MODEL), const),
            pl.BlockSpec((D_MODEL, IN_PROJ_DIM), const, pipeline_mode=resident),
            pl.BlockSpec((CONV_WIDTH, CONV_DIM), const),
            pl.BlockSpec((1, ATTN_DIM), const),
            pl.BlockSpec((1, CONV_DIM), const),
            pl.BlockSpec((D_MODEL, D_MODEL), const, pipeline_mode=resident),
            pl.BlockSpec((1, D_MODEL), const),
            pl.BlockSpec((D_MODEL, 2 * ROUTER_LANES), const, pipeline_mode=resident),
            pl.BlockSpec((ROUTER_ROWS, 1), const),
            pl.BlockSpec((1, D_MODEL, D_EXPERT), per_step),
            pl.BlockSpec((1, D_MODEL, D_EXPERT), per_step),
            pl.BlockSpec((1, D_EXPERT, D_MODEL), per_step),
        ],
        out_specs=[
            pl.BlockSpec((1, t, D_MODEL), tile),
            pl.BlockSpec((1, t, D_MODEL), tile),
            pl.BlockSpec((1, t, ROUTER_LANES), tile),
            pl.BlockSpec((1, KV_DIM, WINDOW), per_seq),
            pl.BlockSpec((1, KV_DIM, WINDOW), per_seq),
            pl.BlockSpec((1, CONV_WIDTH - 1, CONV_DIM), per_seq),
            pl.BlockSpec((1, D_MODEL, D_EXPERT), per_step),
            pl.BlockSpec((1, D_MODEL, D_EXPERT), per_step),
            pl.BlockSpec((1, D_EXPERT, D_MODEL), per_step),
        ],
        scratch_shapes=[
            pltpu.VMEM((N_SUB, ts, ATTN_DIM), BF16),
            pltpu.VMEM((N_SUB, ts + WINDOW, KV_DIM), BF16),
            pltpu.VMEM((N_SUB, ts + WINDOW, KV_DIM), BF16),
            pltpu.VMEM((N_SUB, ts, ATTN_DIM), F32),
            pltpu.VMEM((N_SUB, 8, CONV_DIM), F32),
        ],
    )
    return pl.pallas_call(
        _layer_prompt_body,
        grid_spec=grid_spec,
        out_shape=[
            jax.ShapeDtypeStruct((nb, seq, D_MODEL), F32),
            jax.ShapeDtypeStruct((nb, seq, D_MODEL), BF16),
            jax.ShapeDtypeStruct((nb, seq, ROUTER_LANES), F32),
            jax.ShapeDtypeStruct((nb, KV_DIM, WINDOW), F32),
            jax.ShapeDtypeStruct((nb, KV_DIM, WINDOW), F32),
            jax.ShapeDtypeStruct((nb, CONV_WIDTH - 1, CONV_DIM), F32),
            jax.ShapeDtypeStruct((N_EXPERTS, D_MODEL, D_EXPERT), BF16),
            jax.ShapeDtypeStruct((N_EXPERTS, D_MODEL, D_EXPERT), BF16),
            jax.ShapeDtypeStruct((N_EXPERTS, D_EXPERT, D_MODEL), BF16),
        ],
        compiler_params=pltpu.CompilerParams(
            dimension_semantics=("parallel", "arbitrary"), vmem_limit_bytes=MOE_VMEM_LIMIT),
        name="layer_prompt",
    )(sinks, x, g1, win, convw, gattn, gconv, wout, g2, wr, br, wg32, wu32, wd32)


def _moe_body(cap, xc, x1_ref, xn2_ref, gates_ref, wg_ref, wu_ref, wd_ref, gf_ref, y_ref,
              xs_s, gs_s, ys_s):
    t = x1_ref.shape[0] * TOKEN_CHUNK
    per_group = N_EXPERTS // N_GROUPS
    gates = gates_ref[...].reshape(t, ROUTER_LANES)
    gid_row = jnp.transpose(gates)[GID_LANE:GID_LANE + 1, :]
    grp = lax.broadcasted_iota(jnp.int32, (8, t), 0).astype(F32)
    onehot = jnp.where(grp == gid_row, 1.0, 0.0)

    r_idx = lax.broadcasted_iota(jnp.int32, (t, t), 0)
    c_idx = lax.broadcasted_iota(jnp.int32, (t, t), 1)
    upper = jnp.where(r_idx <= c_idx, 1.0, 0.0).astype(BF16)
    csum = _dot(onehot.astype(BF16), upper)
    rank_row = jnp.sum(onehot * (csum - 1.0), axis=0, keepdims=True)
    cnt = [csum[g, t - 1].astype(jnp.int32) for g in range(N_GROUPS)]
    off = [jnp.int32(0)]
    for g in range(1, N_GROUPS):
        off.append(off[-1] + cnt[g - 1])

    pos_row = rank_row
    for g in range(1, N_GROUPS):
        pos_row = pos_row + jnp.where(gid_row == float(g), off[g].astype(F32), 0.0)
    pos_col = jnp.transpose(jnp.broadcast_to(pos_row, (LANES, t)))
    pos_col = jnp.concatenate([pos_col] * (t // LANES), axis=1)
    perm = jnp.where(r_idx.astype(F32) == pos_row, 1.0, 0.0).astype(BF16)
    perm_t = jnp.where(c_idx.astype(F32) == pos_col, 1.0, 0.0).astype(BF16)

    p1 = gates.astype(BF16)
    p2 = (gates - p1.astype(F32)).astype(BF16)
    moved = _dot(perm, jnp.concatenate([xn2_ref[...].reshape(t, D_MODEL), p1, p2], axis=1))
    xs_s[0:t, :] = moved[:, :D_MODEL].astype(BF16)
    gs_s[0:t, :] = moved[:, D_MODEL:D_MODEL + ROUTER_LANES] + moved[:, D_MODEL + ROUTER_LANES:]
    pad = xs_s.shape[0] - t
    xs_s[t:, :] = jnp.zeros((pad, D_MODEL), BF16)
    gs_s[t:, :] = jnp.zeros((pad, ROUTER_LANES), F32)
    ys_s[...] = jnp.zeros(ys_s.shape, BF16)

    def group_rows(g, r0, rows):
        xs = xs_s[pl.ds(r0, rows), :]
        gsc = gs_s[pl.ds(r0, rows), :]
        acts = []
        for k in range(per_group):
            e = g * per_group + k
            hg = _dot(xs, wg_ref[e])
            hu = _dot(xs, wu_ref[e])
            acts.append((hg / (1.0 + jnp.exp(-hg)) * hu * gsc[:, e:e + 1]).astype(BF16))
        ys_s[pl.ds(r0, rows), :] += _dot(jnp.concatenate(acts, axis=1), wd_ref[g]).astype(BF16)

    base = [(off[g] // ROW_ALIGN) * ROW_ALIGN for g in range(N_GROUPS)]
    for g in range(N_GROUPS):
        group_rows(g, pl.multiple_of(base[g], ROW_ALIGN), cap)
    for g in range(N_GROUPS):
        n_extra = jnp.maximum(0, (off[g] + cnt[g] - (base[g] + cap) + xc - 1) // xc)

        def extra(k, carry, g=g):
            group_rows(g, pl.multiple_of(base[g] + cap + k * xc, ROW_ALIGN), xc)
            return carry

        lax.fori_loop(0, n_extra, extra, 0)

    moe = _dot(perm_t, ys_s[0:t, :])
    y = _rms(x1_ref[...].reshape(t, D_MODEL) + moe, gf_ref[...])
    y_ref[...] = y.reshape(y_ref.shape)


def _moe(x1, xn2, gates, wg, wu, wd, gf, t, cap, xc):
    n = x1.shape[0]
    n_tiles = n // t
    pieces = t // TOKEN_CHUNK
    view = lambda a: a.reshape(pieces, n_tiles, TOKEN_CHUNK, a.shape[-1])
    tile = lambda i: (0, i, 0, 0)
    whole = lambda i: (0, 0, 0)
    resident = pl.Buffered(1)
    rows = t + cap + xc
    hidden = (N_EXPERTS // N_GROUPS) * D_EXPERT
    y = pl.pallas_call(
        functools.partial(_moe_body, cap, xc),
        grid=(n_tiles,),
        in_specs=[
            pl.BlockSpec((pieces, 1, TOKEN_CHUNK, D_MODEL), tile),
            pl.BlockSpec((pieces, 1, TOKEN_CHUNK, D_MODEL), tile),
            pl.BlockSpec((pieces, 1, TOKEN_CHUNK, ROUTER_LANES), tile),
            pl.BlockSpec((N_EXPERTS, D_MODEL, D_EXPERT), whole, pipeline_mode=resident),
            pl.BlockSpec((N_EXPERTS, D_MODEL, D_EXPERT), whole, pipeline_mode=resident),
            pl.BlockSpec((N_GROUPS, hidden, D_MODEL), whole, pipeline_mode=resident),
            pl.BlockSpec((1, D_MODEL), lambda i: (0, 0)),
        ],
        out_specs=pl.BlockSpec((pieces, 1, TOKEN_CHUNK, D_MODEL), tile),
        out_shape=jax.ShapeDtypeStruct((pieces, n_tiles, TOKEN_CHUNK, D_MODEL), F32),
        scratch_shapes=[
            pltpu.VMEM((rows, D_MODEL), BF16),
            pltpu.VMEM((rows, ROUTER_LANES), F32),
            pltpu.VMEM((rows, D_MODEL), BF16),
        ],
        compiler_params=pltpu.CompilerParams(
            dimension_semantics=("parallel",), vmem_limit_bytes=MOE_VMEM_LIMIT),
        name="moe",
    )(view(x1), view(xn2), view(gates), wg, wu, wd, gf)
    return y.reshape(n, D_MODEL)


def _proj_sample_body(x_ref, g1_ref, win_ref, proj_ref, kvt_ref):
    xn = _rms(x_ref[...], g1_ref[...]).astype(BF16)
    proj_ref[...] = _dot(xn, win_ref[...])
    kvt_ref[...] = lax.dot_general(win_ref[:, OFF_K:OFF_B], xn, (((0,), (1,)), ((), ())),
                                   preferred_element_type=F32)


def _proj_sample(x, g1, win):
    n = x.shape[0]
    return pl.pallas_call(
        _proj_sample_body,
        out_shape=[jax.ShapeDtypeStruct((n, IN_PROJ_DIM), F32),
                   jax.ShapeDtypeStruct((2 * KV_DIM, n), F32)],
        compiler_params=pltpu.CompilerParams(vmem_limit_bytes=VMEM_LIMIT),
        name="proj_sample",
    )(x, g1, win)


def _attn_sample_body(q8_ref, knew_ref, vnew_ref, kvt_ref, khist_ref, vhist_ref, sinks_ref,
                      out8_ref, kout_ref, vout_ref):
    chunk = q8_ref.shape[0]
    first = pl.program_id(0) * chunk
    q8 = q8_ref[...]
    kh = khist_ref[...]
    vh = vhist_ref[...]
    knew = knew_ref[...]
    vnew = vnew_ref[...]
    s = jnp.einsum('bhj,bjc->bhc', q8, kh.astype(BF16), preferred_element_type=F32)
    s_new = jnp.sum(q8.astype(F32) * knew, axis=-1, keepdims=True)
    sink = sinks_ref[...][None]
    mx = jnp.maximum(jnp.maximum(jnp.max(s, axis=-1, keepdims=True), s_new), sink)
    e = jnp.exp(s - mx)
    e_new = jnp.exp(s_new - mx)
    den = jnp.sum(e, axis=-1, keepdims=True) + e_new + jnp.exp(sink - mx)
    o = jnp.einsum('bhc,bjc->bhj', e.astype(BF16), vh.astype(BF16), preferred_element_type=F32)
    out8_ref[...] = (o + e_new * vnew) / den
    last = lax.broadcasted_iota(jnp.int32, (KV_DIM, WINDOW), 1) == WINDOW - 1
    knew_t = kvt_ref[0:KV_DIM, :]
    vnew_t = kvt_ref[KV_DIM:, :]
    for bb in range(chunk):
        shift = lax.rem(2 * WINDOW - 1 - (first + bb), WINDOW)
        kout_ref[bb] = jnp.where(last, pltpu.roll(knew_t, shift, axis=1), pltpu.roll(kh[bb], WINDOW - 1, axis=1))
        vout_ref[bb] = jnp.where(last, pltpu.roll(vnew_t, shift, axis=1), pltpu.roll(vh[bb], WINDOW - 1, axis=1))


def _attn_sample(q8, knew, vnew, kvt, khist, vhist, sinks_col):
    assert q8.shape[0] == WINDOW == LANES, "one lane per sequence in the channel-major projection"
    n = q8.shape[0]
    c = DEC_CHUNK
    blk3 = lambda i: (i, 0, 0)
    return pl.pallas_call(
        _attn_sample_body,
        grid=(n // c,),
        in_specs=[
            pl.BlockSpec((c, N_HEADS, LANES), blk3),
            pl.BlockSpec((c, 1, KV_DIM), blk3),
            pl.BlockSpec((c, 1, KV_DIM), blk3),
            pl.BlockSpec((2 * KV_DIM, n), lambda i: (0, 0)),
            pl.BlockSpec((c, KV_DIM, WINDOW), blk3),
            pl.BlockSpec((c, KV_DIM, WINDOW), blk3),
            pl.BlockSpec((N_HEADS, 1), lambda i: (0, 0)),
        ],
        out_specs=[
            pl.BlockSpec((c, N_HEADS, LANES), blk3),
            pl.BlockSpec((c, KV_DIM, WINDOW), blk3),
            pl.BlockSpec((c, KV_DIM, WINDOW), blk3),
        ],
        out_shape=[
            jax.ShapeDtypeStruct((n, N_HEADS, LANES), F32),
            jax.ShapeDtypeStruct((n, KV_DIM, WINDOW), F32),
            jax.ShapeDtypeStruct((n, KV_DIM, WINDOW), F32),
        ],
        compiler_params=pltpu.CompilerParams(
            dimension_semantics=("arbitrary",), vmem_limit_bytes=VMEM_LIMIT),
        name="attn_sample",
    )(q8, knew, vnew, kvt, khist, vhist, sinks_col)


def _tail_sample_body(x_ref, attn_ref, proj_ref, uprev2_ref, uprev1_ref, convw_ref, gattn_ref,
                      gconv_ref, wout_ref, g2_ref, wr_ref, br_ref,
                      x1_ref, xn2_ref, gates_ref, u_ref):
    gate_b = proj_ref[:, OFF_B:OFF_C]
    u = proj_ref[:, OFF_C:OFF_H] * proj_ref[:, OFF_H:IN_PROJ_DIM]
    cw = convw_ref[...]
    z = cw[0:1, :] * uprev2_ref[...] + cw[1:2, :] * uprev1_ref[...] + cw[2:3, :] * u
    u_ref[...] = u
    x1, xn2_bf, gates = _mix_tail(x_ref[...], attn_ref[...], gate_b * z, gattn_ref[...], gconv_ref[...],
                                  wout_ref[...], g2_ref[...], wr_ref[...], br_ref[...])
    x1_ref[...] = x1
    xn2_ref[...] = xn2_bf
    gates_ref[...] = gates


def _tail_sample(x, attn, proj, uprev2, uprev1, convw, gattn, gconv, wout, g2, wr, br):
    n = x.shape[0]
    return pl.pallas_call(
        _tail_sample_body,
        out_shape=[
            jax.ShapeDtypeStruct((n, D_MODEL), F32),
            jax.ShapeDtypeStruct((n, D_MODEL), BF16),
            jax.ShapeDtypeStruct((n, ROUTER_LANES), F32),
            jax.ShapeDtypeStruct((n, CONV_DIM), F32),
        ],
        compiler_params=pltpu.CompilerParams(vmem_limit_bytes=VMEM_LIMIT),
        name="tail_sample",
    )(x, attn, proj, uprev2, uprev1, convw, gattn, gconv, wout, g2, wr, br)


def _pair_heads(w, axis):
    shape = w.shape
    n_pair = N_HEADS // 2
    split = shape[:axis] + (2, n_pair, HEAD_DIM) + shape[axis + 1:]
    return jnp.swapaxes(w.reshape(split), axis, axis + 1).reshape(shape)


def kernel(x_prompt, x_sample, cache_k_win, cache_v_win, state_conv, norm1_g, w_in, conv_w, sinks,
           attn_out_g, conv_out_g, w_out, norm2_g, w_group, b_group, w_fine, b_fine, w_gate, w_up,
           w_down, final_g):
    nb, seq, _ = x_prompt.shape
    nd = x_sample.shape[0]

    w_in0 = w_in[0].astype(BF16)
    win = jnp.concatenate([_pair_heads(w_in0[:, :ATTN_DIM], 1), w_in0[:, ATTN_DIM:]], axis=1)
    w_out0 = w_out[0].astype(BF16)
    wout = jnp.concatenate([_pair_heads(w_out0[:ATTN_DIM], 0), w_out0[ATTN_DIM:]], axis=0)
    gattn = _pair_heads(attn_out_g[0], 0)[None]
    gconv = conv_out_g[0][None]
    g1 = norm1_g[0][None]
    g2 = norm2_g[0][None]
    gf = final_g[None]
    convw = conv_w[0]
    sinks0 = sinks[0]
    pad = ROUTER_LANES - N_EXPERTS - N_GROUPS
    wr32 = jnp.concatenate([w_fine[0], w_group[0], jnp.zeros((D_MODEL, pad), F32)], axis=1)
    wr_hi = wr32.astype(BF16)
    wr = jnp.concatenate([wr_hi, (wr32 - wr_hi.astype(F32)).astype(BF16)], axis=1)
    br = jnp.concatenate([b_fine[0], b_group[0], jnp.zeros((pad,), F32)])[:ROUTER_ROWS, None]

    x1p, xn2p, gatesp, kp, vp, up, wg, wu, wd = _layer_prompt(
        x_prompt, sinks0, g1, win, convw, gattn, gconv, wout, g2, wr, br, w_gate[0], w_up[0], w_down[0])
    wd = wd.reshape(N_GROUPS, (N_EXPERTS // N_GROUPS) * D_EXPERT, D_MODEL)
    n_p = nb * seq
    y_prompt = _moe(x1p.reshape(n_p, D_MODEL), xn2p.reshape(n_p, D_MODEL),
                    gatesp.reshape(n_p, ROUTER_LANES), wg, wu, wd, gf, T_MOE, MOE_CAP, MOE_XC).reshape(nb, seq, D_MODEL)

    xs = x_sample.reshape(nd, D_MODEL)
    proj, kvt = _proj_sample(xs, g1, win)
    lane = jnp.arange(LANES)
    qp = (proj[:, :ATTN_DIM] * (HEAD_DIM ** -0.5)).reshape(nd, N_HEADS // 2, LANES)
    q8 = jnp.concatenate([jnp.where(lane < HEAD_DIM, qp, 0.0), jnp.where(lane >= HEAD_DIM, qp, 0.0)],
                         axis=1).astype(BF16)
    knew = proj[:, OFF_K:OFF_V].reshape(nd, 1, KV_DIM)
    vnew = proj[:, OFF_V:OFF_B].reshape(nd, 1, KV_DIM)
    to_cm = lambda c: jnp.transpose(c[0], (0, 2, 3, 1)).reshape(nd, KV_DIM, WINDOW)
    from_cm = lambda c, n: jnp.transpose(c.reshape(n, KV_DIM // HEAD_DIM, HEAD_DIM, WINDOW), (0, 3, 1, 2))[None]
    out8, ks, vs = _attn_sample(q8, knew, vnew, kvt, to_cm(cache_k_win), to_cm(cache_v_win), sinks0[:, None])
    attn_s = jnp.where(lane < HEAD_DIM, out8[:, :N_HEADS // 2], out8[:, N_HEADS // 2:]).reshape(nd, ATTN_DIM)
    st = state_conv[0]
    x1s, xn2s, gatess, us = _tail_sample(xs, attn_s, proj, st[:, 0], st[:, 1], convw, gattn, gconv,
                                         wout, g2, wr, br)
    y_sample = _moe(x1s, xn2s, gatess, wg, wu, wd, gf, nd, MOE_CAP_DEC, MOE_XC_DEC).reshape(nd, 1, D_MODEL)

    return (y_prompt, y_sample,
            from_cm(kp, nb), from_cm(vp, nb),
            up[None],
            from_cm(ks, nd), from_cm(vs, nd),
            jnp.stack([st[:, 1], us], axis=1)[None])
```
